```python
import math
import jax, jax.numpy as jnp
from jax import lax
import numpy as np

D_MODEL = 4096
BATCH = 8
SEQ = 4096
DEPTH = 2

EXPAND = 2
MIX_WIDTH = EXPAND * D_MODEL
S5_WIDTH = MIX_WIDTH // 4
S5_GROUP = 16
S5_GROUPS = S5_WIDTH // S5_GROUP
S5_STATE = 64
S5_EIG_CLIP = -1e-4
SSD_WIDTH = MIX_WIDTH - S5_WIDTH
SSD_HEAD_DIM = 64
SSD_HEADS = SSD_WIDTH // SSD_HEAD_DIM
SSD_GROUPS = 8
SSD_STATE = 128
SSD_CONV = 4
SSD_CHUNK = 128
SSD_XBC = SSD_WIDTH + 2 * SSD_GROUPS * SSD_STATE
FOX_HEAD_DIM = 128
FOX_HEADS = D_MODEL // FOX_HEAD_DIM
FOX_WIDTH = FOX_HEADS * FOX_HEAD_DIM
FOX_BLOCK = 128
NORM_EPS = 1e-5

EVEN_IN = 2 * S5_WIDTH + SSD_WIDTH + SSD_XBC + SSD_HEADS
ODD_IN = 4 * FOX_WIDTH + FOX_HEADS

kernel_name = "hybrid_s5_ssd_fox_trunk"

F32 = jnp.float32


def rms_norm(x, w):
    xf = x.astype(F32)
    y = xf * lax.rsqrt(jnp.mean(xf * xf, axis=-1, keepdims=True) + NORM_EPS)
    return (y * w.astype(F32)).astype(x.dtype)


def causal_depthwise_conv(x, w, b):
    k_width, ch = w.shape
    y = lax.conv_general_dilated(
        x, w.astype(F32)[:, None, :], window_strides=(1,),
        padding=((k_width - 1, 0),), dimension_numbers=("NWC", "WIO", "NWC"),
        feature_group_count=ch)
    return y + b.astype(F32)


def s5_mixer(u, lam_re, lam_im, log_step, b_re, b_im, c_re, c_im, d, w_glu, b_glu):
    bsz, seqlen, _ = u.shape
    u = u.reshape(bsz, seqlen, S5_GROUPS, S5_GROUP)
    lr = jnp.minimum(lam_re.astype(F32), S5_EIG_CLIP)
    li = lam_im.astype(F32)
    step = jnp.exp(log_step.astype(F32))[:, None]
    mag = jnp.exp(lr * step)
    ab_re = mag * jnp.cos(li * step)
    ab_im = mag * jnp.sin(li * step)
    denom = lr * lr + li * li
    nr = ab_re - 1.0
    ni = ab_im
    coef_re = (nr * lr + ni * li) / denom
    coef_im = (ni * lr - nr * li) / denom
    br = b_re.astype(F32)
    bi = b_im.astype(F32)
    bb_re = coef_re[..., None] * br - coef_im[..., None] * bi
    bb_im = coef_re[..., None] * bi + coef_im[..., None] * br
    bu_re = jnp.einsum('blgh,gph->blgp', u, bb_re)
    bu_im = jnp.einsum('blgh,gph->blgp', u, bb_im)
    a_re = jnp.broadcast_to(ab_re, bu_re.shape)
    a_im = jnp.broadcast_to(ab_im, bu_im.shape)

    def combine(e_i, e_j):
        ar_i, ai_i, br_i, bi_i = e_i
        ar_j, ai_j, br_j, bi_j = e_j
        return (ar_j * ar_i - ai_j * ai_i,
                ar_j * ai_i + ai_j * ar_i,
                ar_j * br_i - ai_j * bi_i + br_j,
                ar_j * bi_i + ai_j * br_i + bi_j)

    _, _, s_re, s_im = lax.associative_scan(combine, (a_re, a_im, bu_re, bu_im), axis=1)
    y = (jnp.einsum('blgp,ghp->blgh', s_re, c_re.astype(F32))
         - jnp.einsum('blgp,ghp->blgh', s_im, c_im.astype(F32))
         + d.astype(F32) * u)
    y = y.reshape(bsz, seqlen, S5_WIDTH)
    g = jax.nn.gelu(y)
    return g * jax.nn.sigmoid(g @ w_glu.astype(F32) + b_glu.astype(F32))


def ssd_chunked(x, dt, a_head, bm, cm):
    bsz, seqlen, nh, hd = x.shape
    ng, ns = bm.shape[2], bm.shape[3]
    r = nh // ng
    q = SSD_CHUNK
    nc = seqlen // q
    xc = (x * dt[..., None]).reshape(bsz, nc, q, ng, r, hd)
    la = (dt * a_head).reshape(bsz, nc, q, ng, r).transpose(0, 3, 4, 1, 2)
    la_cum = jnp.cumsum(la, axis=-1)
    bc = bm.reshape(bsz, nc, q, ng, ns)
    cc = cm.reshape(bsz, nc, q, ng, ns)
    causal = jnp.tril(jnp.ones((q, q), dtype=bool))
    seg = la_cum[..., :, None] - la_cum[..., None, :]
    decay_in = jnp.exp(jnp.where(causal, seg, -jnp.inf))
    scores = jnp.einsum('bcqgn,bckgn->bgcqk', cc, bc)
    w_in = scores[:, :, None] * decay_in
    y_diag = jnp.einsum('bgrcqk,bckgrp->bcqgrp', w_in, xc)
    decay_end = jnp.exp(la_cum[..., -1:] - la_cum).transpose(0, 3, 4, 1, 2)
    states = jnp.einsum('bckgn,bckgrp->bcgrpn', bc, xc * decay_end[..., None])
    chunk_decay = la_cum[..., -1]
    cs = jnp.cumsum(jnp.pad(chunk_decay, ((0, 0), (0, 0), (0, 0), (1, 0))), axis=-1)
    seg_c = cs[..., :, None] - cs[..., None, :]
    mask_c = jnp.tril(jnp.ones((nc + 1, nc + 1), dtype=bool))
    decay_c = jnp.exp(jnp.where(mask_c, seg_c, -jnp.inf))
    states_cat = jnp.concatenate([jnp.zeros_like(states[:, :1]), states], axis=1)
    states_in = jnp.einsum('bgrzc,bcgrpn->bzgrpn', decay_c[..., :nc, :], states_cat)
    decay_out = jnp.exp(la_cum).transpose(0, 3, 4, 1, 2)
    y_off = jnp.einsum('bcqgn,bcgrpn->bcqgrp', cc, states_in) * decay_out[..., None]
    return (y_diag + y_off).reshape(bsz, seqlen, nh, hd)


def ssd_mixer(z, xbc, dt_raw, conv_w, conv_b, dt_bias, a_log, d, norm_w):
    bsz, seqlen, _ = z.shape
    xbc = jax.nn.silu(causal_depthwise_conv(xbc, conv_w, conv_b))
    xs, bm, cm = jnp.split(xbc, [SSD_WIDTH, SSD_WIDTH + SSD_GROUPS * SSD_STATE], axis=-1)
    xs = xs.reshape(bsz, seqlen, SSD_HEADS, SSD_HEAD_DIM)
    bm = bm.reshape(bsz, seqlen, SSD_GROUPS, SSD_STATE)
    cm = cm.reshape(bsz, seqlen, SSD_GROUPS, SSD_STATE)
    dt = jax.nn.softplus(dt_raw + dt_bias.astype(F32))
    a_head = -jnp.exp(a_log.astype(F32))
    y = ssd_chunked(xs, dt, a_head, bm, cm) + d.astype(F32)[:, None] * xs
    y = y.reshape(bsz, seqlen, SSD_WIDTH) * jax.nn.silu(z)
    yg = y.reshape(bsz, seqlen, SSD_GROUPS, SSD_WIDTH // SSD_GROUPS)
    yg = yg * lax.rsqrt(jnp.mean(yg * yg, axis=-1, keepdims=True) + NORM_EPS)
    return yg.reshape(bsz, seqlen, SSD_WIDTH) * norm_w.astype(F32)


def fox_attention(q, k, v, f_logit, b_f):
    bsz, seqlen, nh, hd = q.shape
    log_f = jax.nn.log_sigmoid(f_logit + b_f.astype(F32))
    c = jnp.cumsum(log_f, axis=1).transpose(0, 2, 1)
    nb = seqlen // FOX_BLOCK
    qb = q.reshape(bsz, nb, FOX_BLOCK, nh, hd).transpose(1, 0, 2, 3, 4)
    cb = c.reshape(bsz, nh, nb, FOX_BLOCK).transpose(2, 0, 1, 3)
    kpos = jnp.arange(seqlen)
    scale = 1.0 / math.sqrt(FOX_HEAD_DIM)

    def block(args):
        qi, ci, i = args
        s = jnp.einsum('bqhd,bkhd->bhqk', qi, k) * scale + (ci[..., :, None] - c[:, :, None, :])
        qpos = i * FOX_BLOCK + jnp.arange(FOX_BLOCK)
        s = jnp.where(kpos[None, :] <= qpos[:, None], s, -jnp.inf)
        p = jax.nn.softmax(s, axis=-1)
        return jnp.einsum('bhqk,bkhd->bqhd', p, v)

    out = lax.map(block, (qb, cb, jnp.arange(nb)))
    return out.transpose(1, 0, 2, 3, 4).reshape(bsz, seqlen, nh * hd)


def ssm_layer(x, norm_w, w_in, lam_re, lam_im, log_step, b_re, b_im, c_re, c_im, s5_d,
              w_glu, b_glu, conv_w, conv_b, dt_bias, a_log, ssd_d, ssd_norm_w, w_out):
    h = rms_norm(x, norm_w)
    proj = (h @ w_in).astype(F32)
    s5_u, s5_gate, ssd_z, ssd_xbc, ssd_dt = jnp.split(
        proj, [S5_WIDTH, 2 * S5_WIDTH, 2 * S5_WIDTH + SSD_WIDTH,
               2 * S5_WIDTH + SSD_WIDTH + SSD_XBC], axis=-1)
    s5_out = s5_mixer(s5_u, lam_re, lam_im, log_step, b_re, b_im, c_re, c_im, s5_d,
                      w_glu, b_glu) * jax.nn.silu(s5_gate)
    ssd_out = ssd_mixer(ssd_z, ssd_xbc, ssd_dt, conv_w, conv_b, dt_bias, a_log, ssd_d, ssd_norm_w)
    mixed = jnp.concatenate([s5_out, ssd_out], axis=-1).astype(x.dtype)
    return mixed @ w_out


def fox_layer(x, norm_w, w_in, b_f, w_out):
    bsz, seqlen, _ = x.shape
    h = rms_norm(x, norm_w)
    proj = (h @ w_in).astype(F32)
    q, k, v, gate, f_logit = jnp.split(
        proj, [FOX_WIDTH, 2 * FOX_WIDTH, 3 * FOX_WIDTH, 4 * FOX_WIDTH], axis=-1)
    shp = (bsz, seqlen, FOX_HEADS, FOX_HEAD_DIM)
    att = fox_attention(q.reshape(shp), k.reshape(shp), v.reshape(shp), f_logit, b_f)
    out = (att * jax.nn.silu(gate)).astype(x.dtype)
    return out @ w_out


def _fwd_setup_inputs(seed: int = 0) -> dict:
    key = jax.random.key(seed)
    ks = jax.random.split(key, 32)
    nrm = lambda k, shp, s: jax.random.normal(k, shp, F32) * s
    x = nrm(ks[0], (BATCH, SEQ, D_MODEL), 1.0)
    l0_norm_w = 1.0 + nrm(ks[1], (D_MODEL,), 0.02)
    l0_w_in = nrm(ks[2], (D_MODEL, EVEN_IN), D_MODEL ** -0.5)
    l0_s5_lambda_re = -0.5 + nrm(ks[3], (S5_GROUPS, S5_STATE), 0.01)
    l0_s5_lambda_im = (jnp.pi * jnp.broadcast_to(jnp.arange(S5_STATE, dtype=F32), (S5_GROUPS, S5_STATE))
                       + nrm(ks[4], (S5_GROUPS, S5_STATE), 0.01))
    l0_s5_log_step = jax.random.uniform(ks[5], (S5_GROUPS,), F32, math.log(1e-3), math.log(1e-1))
    l0_s5_b_re = nrm(ks[6], (S5_GROUPS, S5_STATE, S5_GROUP), (2 * S5_GROUP) ** -0.5)
    l0_s5_b_im = nrm(ks[7], (S5_GROUPS, S5_STATE, S5_GROUP), (2 * S5_GROUP) ** -0.5)
    l0_s5_c_re = nrm(ks[8], (S5_GROUPS, S5_GROUP, S5_STATE), S5_STATE ** -0.5)
    l0_s5_c_im = nrm(ks[9], (S5_GROUPS, S5_GROUP, S5_STATE), S5_STATE ** -0.5)
    l0_s5_d = nrm(ks[10], (S5_GROUPS, S5_GROUP), 1.0)
    l0_s5_w_glu = nrm(ks[11], (S5_WIDTH, S5_WIDTH), S5_WIDTH ** -0.5)
    l0_s5_b_glu = nrm(ks[12], (S5_WIDTH,), 0.01)
    l0_ssd_conv_w = nrm(ks[13], (SSD_CONV, SSD_XBC), SSD_CONV ** -0.5)
    l0_ssd_conv_b = nrm(ks[14], (SSD_XBC,), 0.01)
    dt0 = jnp.exp(jax.random.uniform(ks[15], (SSD_HEADS,), F32, math.log(1e-3), math.log(1e-1)))
    l0_ssd_dt_bias = dt0 + jnp.log(-jnp.expm1(-dt0))
    l0_ssd_a_log = jnp.log(jax.random.uniform(ks[16], (SSD_HEADS,), F32, 1.0, 16.0))
    l0_ssd_d = 1.0 + nrm(ks[17], (SSD_HEADS,), 0.01)
    l0_ssd_norm_w = 1.0 + nrm(ks[18], (SSD_WIDTH,), 0.02)
    l0_w_out = nrm(ks[19], (MIX_WIDTH, D_MODEL), MIX_WIDTH ** -0.5)
    l1_norm_w = 1.0 + nrm(ks[20], (D_MODEL,), 0.02)
    l1_w_in = nrm(ks[21], (D_MODEL, ODD_IN), D_MODEL ** -0.5)
    l1_fox_b_f = jnp.log(jnp.exp(jax.random.uniform(ks[22], (FOX_HEADS,), F32, math.log(8.0), math.log(2048.0))))
    l1_w_out = nrm(ks[23], (FOX_WIDTH, D_MODEL), FOX_WIDTH ** -0.5)
    final_norm_w = 1.0 + nrm(ks[24], (D_MODEL,), 0.02)
    return {
        "x": x,
        "l0_norm_w": l0_norm_w, "l0_w_in": l0_w_in,
        "l0_s5_lambda_re": l0_s5_lambda_re, "l0_s5_lambda_im": l0_s5_lambda_im,
        "l0_s5_log_step": l0_s5_log_step,
        "l0_s5_b_re": l0_s5_b_re, "l0_s5_b_im": l0_s5_b_im,
        "l0_s5_c_re": l0_s5_c_re, "l0_s5_c_im": l0_s5_c_im,
        "l0_s5_d": l0_s5_d, "l0_s5_w_glu": l0_s5_w_glu, "l0_s5_b_glu": l0_s5_b_glu,
        "l0_ssd_conv_w": l0_ssd_conv_w, "l0_ssd_conv_b": l0_ssd_conv_b,
        "l0_ssd_dt_bias": l0_ssd_dt_bias, "l0_ssd_a_log": l0_ssd_a_log,
        "l0_ssd_d": l0_ssd_d, "l0_ssd_norm_w": l0_ssd_norm_w,
        "l0_w_out": l0_w_out,
        "l1_norm_w": l1_norm_w, "l1_w_in": l1_w_in, "l1_fox_b_f": l1_fox_b_f,
        "l1_w_out": l1_w_out,
        "final_norm_w": final_norm_w,
    }


def _fwd_reference(x, l0_norm_w, l0_w_in, l0_s5_lambda_re, l0_s5_lambda_im, l0_s5_log_step,
              l0_s5_b_re, l0_s5_b_im, l0_s5_c_re, l0_s5_c_im, l0_s5_d, l0_s5_w_glu,
              l0_s5_b_glu, l0_ssd_conv_w, l0_ssd_conv_b, l0_ssd_dt_bias, l0_ssd_a_log,
              l0_ssd_d, l0_ssd_norm_w, l0_w_out, l1_norm_w, l1_w_in, l1_fox_b_f, l1_w_out,
              final_norm_w):
    layers = [
        (l0_norm_w, l0_w_in, l0_s5_lambda_re, l0_s5_lambda_im, l0_s5_log_step,
         l0_s5_b_re, l0_s5_b_im, l0_s5_c_re, l0_s5_c_im, l0_s5_d, l0_s5_w_glu,
         l0_s5_b_glu, l0_ssd_conv_w, l0_ssd_conv_b, l0_ssd_dt_bias, l0_ssd_a_log,
         l0_ssd_d, l0_ssd_norm_w, l0_w_out),
        (l1_norm_w, l1_w_in, l1_fox_b_f, l1_w_out),
    ]
    for layer in range(DEPTH):
        if layer % 2 == 0:
            x = x + ssm_layer(x, *layers[layer])
        else:
            x = x + fox_layer(x, *layers[layer])
    return rms_norm(x, final_norm_w)


import jax as _jax
import jax.numpy as _jnp

TWIN_FORMAT = 'train_step'
FWD_PARAMS = ['x', 'l0_norm_w', 'l0_w_in', 'l0_s5_lambda_re', 'l0_s5_lambda_im', 'l0_s5_log_step', 'l0_s5_b_re', 'l0_s5_b_im', 'l0_s5_c_re', 'l0_s5_c_im', 'l0_s5_d', 'l0_s5_w_glu', 'l0_s5_b_glu', 'l0_ssd_conv_w', 'l0_ssd_conv_b', 'l0_ssd_dt_bias', 'l0_ssd_a_log', 'l0_ssd_d', 'l0_ssd_norm_w', 'l0_w_out', 'l1_norm_w', 'l1_w_in', 'l1_fox_b_f', 'l1_w_out', 'final_norm_w']
TWIN_WEIGHTS = ['l0_norm_w', 'l0_w_in', 'l0_s5_lambda_re', 'l0_s5_lambda_im', 'l0_s5_log_step', 'l0_s5_b_re', 'l0_s5_b_im', 'l0_s5_c_re', 'l0_s5_c_im', 'l0_s5_d', 'l0_s5_w_glu', 'l0_s5_b_glu', 'l0_ssd_conv_w', 'l0_ssd_conv_b', 'l0_ssd_dt_bias', 'l0_ssd_a_log', 'l0_ssd_d', 'l0_ssd_norm_w', 'l0_w_out', 'l1_norm_w', 'l1_w_in', 'l1_fox_b_f', 'l1_w_out', 'final_norm_w']
TWIN_DIFF_INPUT = 'x'
TWIN_INPUTS = ['x', 'l0_norm_w', 'l0_w_in', 'l0_s5_lambda_re', 'l0_s5_lambda_im', 'l0_s5_log_step', 'l0_s5_b_re', 'l0_s5_b_im', 'l0_s5_c_re', 'l0_s5_c_im', 'l0_s5_d', 'l0_s5_w_glu', 'l0_s5_b_glu', 'l0_ssd_conv_w', 'l0_ssd_conv_b', 'l0_ssd_dt_bias', 'l0_ssd_a_log', 'l0_ssd_d', 'l0_ssd_norm_w', 'l0_w_out', 'l1_norm_w', 'l1_w_in', 'l1_fox_b_f', 'l1_w_out', 'final_norm_w', 'loss_target', 'm_l0_norm_w', 'm_l0_w_in', 'm_l0_s5_lambda_re', 'm_l0_s5_lambda_im', 'm_l0_s5_log_step', 'm_l0_s5_b_re', 'm_l0_s5_b_im', 'm_l0_s5_c_re', 'm_l0_s5_c_im', 'm_l0_s5_d', 'm_l0_s5_w_glu', 'm_l0_s5_b_glu', 'm_l0_ssd_conv_w', 'm_l0_ssd_conv_b', 'm_l0_ssd_dt_bias', 'm_l0_ssd_a_log', 'm_l0_ssd_d', 'm_l0_ssd_norm_w', 'm_l0_w_out', 'm_l1_norm_w', 'm_l1_w_in', 'm_l1_fox_b_f', 'm_l1_w_out', 'm_final_norm_w', 'v_l0_norm_w', 'v_l0_w_in', 'v_l0_s5_lambda_re', 'v_l0_s5_lambda_im', 'v_l0_s5_log_step', 'v_l0_s5_b_re', 'v_l0_s5_b_im', 'v_l0_s5_c_re', 'v_l0_s5_c_im', 'v_l0_s5_d', 'v_l0_s5_w_glu', 'v_l0_s5_b_glu', 'v_l0_ssd_conv_w', 'v_l0_ssd_conv_b', 'v_l0_ssd_dt_bias', 'v_l0_ssd_a_log', 'v_l0_ssd_d', 'v_l0_ssd_norm_w', 'v_l0_w_out', 'v_l1_norm_w', 'v_l1_w_in', 'v_l1_fox_b_f', 'v_l1_w_out', 'v_final_norm_w']
TWIN_OUTPUTS = ['loss', 'grad_x', 'grad_l0_norm_w', 'grad_l0_w_in', 'grad_l0_s5_lambda_re', 'grad_l0_s5_lambda_im', 'grad_l0_s5_log_step', 'grad_l0_s5_b_re', 'grad_l0_s5_b_im', 'grad_l0_s5_c_re', 'grad_l0_s5_c_im', 'grad_l0_s5_d', 'grad_l0_s5_w_glu', 'grad_l0_s5_b_glu', 'grad_l0_ssd_conv_w', 'grad_l0_ssd_conv_b', 'grad_l0_ssd_dt_bias', 'grad_l0_ssd_a_log', 'grad_l0_ssd_d', 'grad_l0_ssd_norm_w', 'grad_l0_w_out', 'grad_l1_norm_w', 'grad_l1_w_in', 'grad_l1_fox_b_f', 'grad_l1_w_out', 'grad_final_norm_w', 'delta_l0_norm_w', 'delta_l0_w_in', 'delta_l0_s5_lambda_re', 'delta_l0_s5_lambda_im', 'delta_l0_s5_log_step', 'delta_l0_s5_b_re', 'delta_l0_s5_b_im', 'delta_l0_s5_c_re', 'delta_l0_s5_c_im', 'delta_l0_s5_d', 'delta_l0_s5_w_glu', 'delta_l0_s5_b_glu', 'delta_l0_ssd_conv_w', 'delta_l0_ssd_conv_b', 'delta_l0_ssd_dt_bias', 'delta_l0_ssd_a_log', 'delta_l0_ssd_d', 'delta_l0_ssd_norm_w', 'delta_l0_w_out', 'delta_l1_norm_w', 'delta_l1_w_in', 'delta_l1_fox_b_f', 'delta_l1_w_out', 'delta_final_norm_w', 'new_m_l0_norm_w', 'new_m_l0_w_in', 'new_m_l0_s5_lambda_re', 'new_m_l0_s5_lambda_im', 'new_m_l0_s5_log_step', 'new_m_l0_s5_b_re', 'new_m_l0_s5_b_im', 'new_m_l0_s5_c_re', 'new_m_l0_s5_c_im', 'new_m_l0_s5_d', 'new_m_l0_s5_w_glu', 'new_m_l0_s5_b_glu', 'new_m_l0_ssd_conv_w', 'new_m_l0_ssd_conv_b', 'new_m_l0_ssd_dt_bias', 'new_m_l0_ssd_a_log', 'new_m_l0_ssd_d', 'new_m_l0_ssd_norm_w', 'new_m_l0_w_out', 'new_m_l1_norm_w', 'new_m_l1_w_in', 'new_m_l1_fox_b_f', 'new_m_l1_w_out', 'new_m_final_norm_w', 'new_v_l0_norm_w', 'new_v_l0_w_in', 'new_v_l0_s5_lambda_re', 'new_v_l0_s5_lambda_im', 'new_v_l0_s5_log_step', 'new_v_l0_s5_b_re', 'new_v_l0_s5_b_im', 'new_v_l0_s5_c_re', 'new_v_l0_s5_c_im', 'new_v_l0_s5_d', 'new_v_l0_s5_w_glu', 'new_v_l0_s5_b_glu', 'new_v_l0_ssd_conv_w', 'new_v_l0_ssd_conv_b', 'new_v_l0_ssd_dt_bias', 'new_v_l0_ssd_a_log', 'new_v_l0_ssd_d', 'new_v_l0_ssd_norm_w', 'new_v_l0_w_out', 'new_v_l1_norm_w', 'new_v_l1_w_in', 'new_v_l1_fox_b_f', 'new_v_l1_w_out', 'new_v_final_norm_w']
TWIN_LEAF_KINDS = {'loss': 'loss', 'grad_x': 'grad_x', 'grad_l0_norm_w': 'grad_w', 'grad_l0_w_in': 'grad_w', 'grad_l0_s5_lambda_re': 'grad_w', 'grad_l0_s5_lambda_im': 'grad_w', 'grad_l0_s5_log_step': 'grad_w', 'grad_l0_s5_b_re': 'grad_w', 'grad_l0_s5_b_im': 'grad_w', 'grad_l0_s5_c_re': 'grad_w', 'grad_l0_s5_c_im': 'grad_w', 'grad_l0_s5_d': 'grad_w', 'grad_l0_s5_w_glu': 'grad_w', 'grad_l0_s5_b_glu': 'grad_w', 'grad_l0_ssd_conv_w': 'grad_w', 'grad_l0_ssd_conv_b': 'grad_w', 'grad_l0_ssd_dt_bias': 'grad_w', 'grad_l0_ssd_a_log': 'grad_w', 'grad_l0_ssd_d': 'grad_w', 'grad_l0_ssd_norm_w': 'grad_w', 'grad_l0_w_out': 'grad_w', 'grad_l1_norm_w': 'grad_w', 'grad_l1_w_in': 'grad_w', 'grad_l1_fox_b_f': 'grad_w', 'grad_l1_w_out': 'grad_w', 'grad_final_norm_w': 'grad_w', 'delta_l0_norm_w': 'delta_w', 'delta_l0_w_in': 'delta_w', 'delta_l0_s5_lambda_re': 'delta_w', 'delta_l0_s5_lambda_im': 'delta_w', 'delta_l0_s5_log_step': 'delta_w', 'delta_l0_s5_b_re': 'delta_w', 'delta_l0_s5_b_im': 'delta_w', 'delta_l0_s5_c_re': 'delta_w', 'delta_l0_s5_c_im': 'delta_w', 'delta_l0_s5_d': 'delta_w', 'delta_l0_s5_w_glu': 'delta_w', 'delta_l0_s5_b_glu': 'delta_w', 'delta_l0_ssd_conv_w': 'delta_w', 'delta_l0_ssd_conv_b': 'delta_w', 'delta_l0_ssd_dt_bias': 'delta_w', 'delta_l0_ssd_a_log': 'delta_w', 'delta_l0_ssd_d': 'delta_w', 'delta_l0_ssd_norm_w': 'delta_w', 'delta_l0_w_out': 'delta_w', 'delta_l1_norm_w': 'delta_w', 'delta_l1_w_in': 'delta_w', 'delta_l1_fox_b_f': 'delta_w', 'delta_l1_w_out': 'delta_w', 'delta_final_norm_w': 'delta_w', 'new_m_l0_norm_w': 'new_m', 'new_m_l0_w_in': 'new_m', 'new_m_l0_s5_lambda_re': 'new_m', 'new_m_l0_s5_lambda_im': 'new_m', 'new_m_l0_s5_log_step': 'new_m', 'new_m_l0_s5_b_re': 'new_m', 'new_m_l0_s5_b_im': 'new_m', 'new_m_l0_s5_c_re': 'new_m', 'new_m_l0_s5_c_im': 'new_m', 'new_m_l0_s5_d': 'new_m', 'new_m_l0_s5_w_glu': 'new_m', 'new_m_l0_s5_b_glu': 'new_m', 'new_m_l0_ssd_conv_w': 'new_m', 'new_m_l0_ssd_conv_b': 'new_m', 'new_m_l0_ssd_dt_bias': 'new_m', 'new_m_l0_ssd_a_log': 'new_m', 'new_m_l0_ssd_d': 'new_m', 'new_m_l0_ssd_norm_w': 'new_m', 'new_m_l0_w_out': 'new_m', 'new_m_l1_norm_w': 'new_m', 'new_m_l1_w_in': 'new_m', 'new_m_l1_fox_b_f': 'new_m', 'new_m_l1_w_out': 'new_m', 'new_m_final_norm_w': 'new_m', 'new_v_l0_norm_w': 'new_v', 'new_v_l0_w_in': 'new_v', 'new_v_l0_s5_lambda_re': 'new_v', 'new_v_l0_s5_lambda_im': 'new_v', 'new_v_l0_s5_log_step': 'new_v', 'new_v_l0_s5_b_re': 'new_v', 'new_v_l0_s5_b_im': 'new_v', 'new_v_l0_s5_c_re': 'new_v', 'new_v_l0_s5_c_im': 'new_v', 'new_v_l0_s5_d': 'new_v', 'new_v_l0_s5_w_glu': 'new_v', 'new_v_l0_s5_b_glu': 'new_v', 'new_v_l0_ssd_conv_w': 'new_v', 'new_v_l0_ssd_conv_b': 'new_v', 'new_v_l0_ssd_dt_bias': 'new_v', 'new_v_l0_ssd_a_log': 'new_v', 'new_v_l0_ssd_d': 'new_v', 'new_v_l0_ssd_norm_w': 'new_v', 'new_v_l0_w_out': 'new_v', 'new_v_l1_norm_w': 'new_v', 'new_v_l1_w_in': 'new_v', 'new_v_l1_fox_b_f': 'new_v', 'new_v_l1_w_out': 'new_v', 'new_v_final_norm_w': 'new_v'}


def _forward(args):
    return _fwd_reference(*[args[k] for k in FWD_PARAMS])


def _output_shape():
    out = _jax.eval_shape(lambda: _forward(_fwd_setup_inputs(0)))
    return out.shape, out.dtype

N_MICROBATCH = 1
ADAM_LR = 0.001
ADAM_B1 = 0.9
ADAM_B2 = 0.999
ADAM_EPS = 1e-08
ADAM_WD = 0.01
ADAM_STEP = 10
PER_EXAMPLE_BATCH_AXIS = {'x': 0, 'loss_target': 0}
SHARED_INPUTS = []
_WEIGHT_DTYPES = {'l0_norm_w': _jnp.float32, 'l0_w_in': _jnp.float32, 'l0_s5_lambda_re': _jnp.float32, 'l0_s5_lambda_im': _jnp.float32, 'l0_s5_log_step': _jnp.float32, 'l0_s5_b_re': _jnp.float32, 'l0_s5_b_im': _jnp.float32, 'l0_s5_c_re': _jnp.float32, 'l0_s5_c_im': _jnp.float32, 'l0_s5_d': _jnp.float32, 'l0_s5_w_glu': _jnp.float32, 'l0_s5_b_glu': _jnp.float32, 'l0_ssd_conv_w': _jnp.float32, 'l0_ssd_conv_b': _jnp.float32, 'l0_ssd_dt_bias': _jnp.float32, 'l0_ssd_a_log': _jnp.float32, 'l0_ssd_d': _jnp.float32, 'l0_ssd_norm_w': _jnp.float32, 'l0_w_out': _jnp.float32, 'l1_norm_w': _jnp.float32, 'l1_w_in': _jnp.float32, 'l1_fox_b_f': _jnp.float32, 'l1_w_out': _jnp.float32, 'final_norm_w': _jnp.float32}
MOMENT_SCALE = {'l0_norm_w': 4.439655e-02, 'l0_w_in': 2.085712e-02, 'l0_s5_lambda_re': 3.870776e-04, 'l0_s5_lambda_im': 4.083226e-04, 'l0_s5_log_step': 2.359733e-01, 'l0_s5_b_re': 2.494462e-04, 'l0_s5_b_im': 2.490425e-04, 'l0_s5_c_re': 3.489535e-04, 'l0_s5_c_im': 3.552781e-04, 'l0_s5_d': 5.656204e-03, 'l0_s5_w_glu': 1.508729e-03, 'l0_s5_b_glu': 2.392377e-03, 'l0_ssd_conv_w': 2.193208e-02, 'l0_ssd_conv_b': 2.930162e-02, 'l0_ssd_dt_bias': 5.396244e-02, 'l0_ssd_a_log': 7.194254e-02, 'l0_ssd_d': 1.390367e-01, 'l0_ssd_norm_w': 2.413211e-02, 'l0_w_out': 2.971570e-02, 'l1_norm_w': 1.017247e-02, 'l1_w_in': 5.120948e-03, 'l1_fox_b_f': 2.687681e-02, 'l1_w_out': 5.752018e-03, 'final_norm_w': 7.991635e+00}


def _to_microbatches(a, axis):
    t = _jnp.moveaxis(a, axis, 0)
    t = t.reshape((N_MICROBATCH, t.shape[0] // N_MICROBATCH) + t.shape[1:])
    return _jnp.moveaxis(t, 1, axis + 1)


def setup_inputs(seed: int = 0) -> dict:
    inp = _fwd_setup_inputs(seed)
    key = _jax.random.fold_in(_jax.random.key(seed), 7919)
    shape, _ = _output_shape()
    out = dict(inp)
    out["loss_target"] = _jax.random.normal(_jax.random.fold_in(key, 0), shape, _jnp.float32)
    for i, name in enumerate(TWIN_WEIGHTS):
        w = inp[name].astype(_jnp.float32)
        if MOMENT_SCALE is None:
            s = _jnp.sqrt(_jnp.mean(_jnp.square(w)) + 1e-30)
        else:
            s = MOMENT_SCALE[name]
        km, kv = _jax.random.split(_jax.random.fold_in(key, i + 1))
        out[name] = w
        out["m_" + name] = s * _jax.random.normal(km, w.shape, _jnp.float32)
        out["v_" + name] = (s * s) * _jax.random.uniform(kv, w.shape, _jnp.float32, 0.5, 1.5)
    if N_MICROBATCH > 1:
        for name, axis in PER_EXAMPLE_BATCH_AXIS.items():
            out[name] = _to_microbatches(out[name], axis)
    return {'x': out['x'], 'l0_norm_w': out['l0_norm_w'], 'l0_w_in': out['l0_w_in'], 'l0_s5_lambda_re': out['l0_s5_lambda_re'], 'l0_s5_lambda_im': out['l0_s5_lambda_im'], 'l0_s5_log_step': out['l0_s5_log_step'], 'l0_s5_b_re': out['l0_s5_b_re'], 'l0_s5_b_im': out['l0_s5_b_im'], 'l0_s5_c_re': out['l0_s5_c_re'], 'l0_s5_c_im': out['l0_s5_c_im'], 'l0_s5_d': out['l0_s5_d'], 'l0_s5_w_glu': out['l0_s5_w_glu'], 'l0_s5_b_glu': out['l0_s5_b_glu'], 'l0_ssd_conv_w': out['l0_ssd_conv_w'], 'l0_ssd_conv_b': out['l0_ssd_conv_b'], 'l0_ssd_dt_bias': out['l0_ssd_dt_bias'], 'l0_ssd_a_log': out['l0_ssd_a_log'], 'l0_ssd_d': out['l0_ssd_d'], 'l0_ssd_norm_w': out['l0_ssd_norm_w'], 'l0_w_out': out['l0_w_out'], 'l1_norm_w': out['l1_norm_w'], 'l1_w_in': out['l1_w_in'], 'l1_fox_b_f': out['l1_fox_b_f'], 'l1_w_out': out['l1_w_out'], 'final_norm_w': out['final_norm_w'], 'loss_target': out['loss_target'], 'm_l0_norm_w': out['m_l0_norm_w'], 'm_l0_w_in': out['m_l0_w_in'], 'm_l0_s5_lambda_re': out['m_l0_s5_lambda_re'], 'm_l0_s5_lambda_im': out['m_l0_s5_lambda_im'], 'm_l0_s5_log_step': out['m_l0_s5_log_step'], 'm_l0_s5_b_re': out['m_l0_s5_b_re'], 'm_l0_s5_b_im': out['m_l0_s5_b_im'], 'm_l0_s5_c_re': out['m_l0_s5_c_re'], 'm_l0_s5_c_im': out['m_l0_s5_c_im'], 'm_l0_s5_d': out['m_l0_s5_d'], 'm_l0_s5_w_glu': out['m_l0_s5_w_glu'], 'm_l0_s5_b_glu': out['m_l0_s5_b_glu'], 'm_l0_ssd_conv_w': out['m_l0_ssd_conv_w'], 'm_l0_ssd_conv_b': out['m_l0_ssd_conv_b'], 'm_l0_ssd_dt_bias': out['m_l0_ssd_dt_bias'], 'm_l0_ssd_a_log': out['m_l0_ssd_a_log'], 'm_l0_ssd_d': out['m_l0_ssd_d'], 'm_l0_ssd_norm_w': out['m_l0_ssd_norm_w'], 'm_l0_w_out': out['m_l0_w_out'], 'm_l1_norm_w': out['m_l1_norm_w'], 'm_l1_w_in': out['m_l1_w_in'], 'm_l1_fox_b_f': out['m_l1_fox_b_f'], 'm_l1_w_out': out['m_l1_w_out'], 'm_final_norm_w': out['m_final_norm_w'], 'v_l0_norm_w': out['v_l0_norm_w'], 'v_l0_w_in': out['v_l0_w_in'], 'v_l0_s5_lambda_re': out['v_l0_s5_lambda_re'], 'v_l0_s5_lambda_im': out['v_l0_s5_lambda_im'], 'v_l0_s5_log_step': out['v_l0_s5_log_step'], 'v_l0_s5_b_re': out['v_l0_s5_b_re'], 'v_l0_s5_b_im': out['v_l0_s5_b_im'], 'v_l0_s5_c_re': out['v_l0_s5_c_re'], 'v_l0_s5_c_im': out['v_l0_s5_c_im'], 'v_l0_s5_d': out['v_l0_s5_d'], 'v_l0_s5_w_glu': out['v_l0_s5_w_glu'], 'v_l0_s5_b_glu': out['v_l0_s5_b_glu'], 'v_l0_ssd_conv_w': out['v_l0_ssd_conv_w'], 'v_l0_ssd_conv_b': out['v_l0_ssd_conv_b'], 'v_l0_ssd_dt_bias': out['v_l0_ssd_dt_bias'], 'v_l0_ssd_a_log': out['v_l0_ssd_a_log'], 'v_l0_ssd_d': out['v_l0_ssd_d'], 'v_l0_ssd_norm_w': out['v_l0_ssd_norm_w'], 'v_l0_w_out': out['v_l0_w_out'], 'v_l1_norm_w': out['v_l1_norm_w'], 'v_l1_w_in': out['v_l1_w_in'], 'v_l1_fox_b_f': out['v_l1_fox_b_f'], 'v_l1_w_out': out['v_l1_w_out'], 'v_final_norm_w': out['v_final_norm_w']}


def _loss(weights, diff, rest, loss_target):
    with _jax.named_scope("forward"):
        args = {**rest, TWIN_DIFF_INPUT: diff, **{k: w.astype(_WEIGHT_DTYPES[k]) for k, w in weights.items()}}
        y = _forward(args)
    with _jax.named_scope("loss_head"):
        err = _jnp.square(y.astype(_jnp.float32) - loss_target)
        return 0.5 * _jnp.sum(_jnp.mean(err, axis=-1)) if err.ndim else 0.5 * err


def _adamw(w, g, m, v):
    m = ADAM_B1 * m + (1.0 - ADAM_B1) * g
    v = ADAM_B2 * v + (1.0 - ADAM_B2) * _jnp.square(g)
    m_hat = m / (1.0 - ADAM_B1 ** ADAM_STEP)
    v_hat = v / (1.0 - ADAM_B2 ** ADAM_STEP)
    delta = -ADAM_LR * (m_hat / (_jnp.sqrt(v_hat) + ADAM_EPS) + ADAM_WD * w)
    return delta, m, v


def reference(x, l0_norm_w, l0_w_in, l0_s5_lambda_re, l0_s5_lambda_im, l0_s5_log_step, l0_s5_b_re, l0_s5_b_im, l0_s5_c_re, l0_s5_c_im, l0_s5_d, l0_s5_w_glu, l0_s5_b_glu, l0_ssd_conv_w, l0_ssd_conv_b, l0_ssd_dt_bias, l0_ssd_a_log, l0_ssd_d, l0_ssd_norm_w, l0_w_out, l1_norm_w, l1_w_in, l1_fox_b_f, l1_w_out, final_norm_w, loss_target, m_l0_norm_w, m_l0_w_in, m_l0_s5_lambda_re, m_l0_s5_lambda_im, m_l0_s5_log_step, m_l0_s5_b_re, m_l0_s5_b_im, m_l0_s5_c_re, m_l0_s5_c_im, m_l0_s5_d, m_l0_s5_w_glu, m_l0_s5_b_glu, m_l0_ssd_conv_w, m_l0_ssd_conv_b, m_l0_ssd_dt_bias, m_l0_ssd_a_log, m_l0_ssd_d, m_l0_ssd_norm_w, m_l0_w_out, m_l1_norm_w, m_l1_w_in, m_l1_fox_b_f, m_l1_w_out, m_final_norm_w, v_l0_norm_w, v_l0_w_in, v_l0_s5_lambda_re, v_l0_s5_lambda_im, v_l0_s5_log_step, v_l0_s5_b_re, v_l0_s5_b_im, v_l0_s5_c_re, v_l0_s5_c_im, v_l0_s5_d, v_l0_s5_w_glu, v_l0_s5_b_glu, v_l0_ssd_conv_w, v_l0_ssd_conv_b, v_l0_ssd_dt_bias, v_l0_ssd_a_log, v_l0_ssd_d, v_l0_ssd_norm_w, v_l0_w_out, v_l1_norm_w, v_l1_w_in, v_l1_fox_b_f, v_l1_w_out, v_final_norm_w):
    given = dict(x=x, l0_norm_w=l0_norm_w, l0_w_in=l0_w_in, l0_s5_lambda_re=l0_s5_lambda_re, l0_s5_lambda_im=l0_s5_lambda_im, l0_s5_log_step=l0_s5_log_step, l0_s5_b_re=l0_s5_b_re, l0_s5_b_im=l0_s5_b_im, l0_s5_c_re=l0_s5_c_re, l0_s5_c_im=l0_s5_c_im, l0_s5_d=l0_s5_d, l0_s5_w_glu=l0_s5_w_glu, l0_s5_b_glu=l0_s5_b_glu, l0_ssd_conv_w=l0_ssd_conv_w, l0_ssd_conv_b=l0_ssd_conv_b, l0_ssd_dt_bias=l0_ssd_dt_bias, l0_ssd_a_log=l0_ssd_a_log, l0_ssd_d=l0_ssd_d, l0_ssd_norm_w=l0_ssd_norm_w, l0_w_out=l0_w_out, l1_norm_w=l1_norm_w, l1_w_in=l1_w_in, l1_fox_b_f=l1_fox_b_f, l1_w_out=l1_w_out, final_norm_w=final_norm_w, loss_target=loss_target, m_l0_norm_w=m_l0_norm_w, m_l0_w_in=m_l0_w_in, m_l0_s5_lambda_re=m_l0_s5_lambda_re, m_l0_s5_lambda_im=m_l0_s5_lambda_im, m_l0_s5_log_step=m_l0_s5_log_step, m_l0_s5_b_re=m_l0_s5_b_re, m_l0_s5_b_im=m_l0_s5_b_im, m_l0_s5_c_re=m_l0_s5_c_re, m_l0_s5_c_im=m_l0_s5_c_im, m_l0_s5_d=m_l0_s5_d, m_l0_s5_w_glu=m_l0_s5_w_glu, m_l0_s5_b_glu=m_l0_s5_b_glu, m_l0_ssd_conv_w=m_l0_ssd_conv_w, m_l0_ssd_conv_b=m_l0_ssd_conv_b, m_l0_ssd_dt_bias=m_l0_ssd_dt_bias, m_l0_ssd_a_log=m_l0_ssd_a_log, m_l0_ssd_d=m_l0_ssd_d, m_l0_ssd_norm_w=m_l0_ssd_norm_w, m_l0_w_out=m_l0_w_out, m_l1_norm_w=m_l1_norm_w, m_l1_w_in=m_l1_w_in, m_l1_fox_b_f=m_l1_fox_b_f, m_l1_w_out=m_l1_w_out, m_final_norm_w=m_final_norm_w, v_l0_norm_w=v_l0_norm_w, v_l0_w_in=v_l0_w_in, v_l0_s5_lambda_re=v_l0_s5_lambda_re, v_l0_s5_lambda_im=v_l0_s5_lambda_im, v_l0_s5_log_step=v_l0_s5_log_step, v_l0_s5_b_re=v_l0_s5_b_re, v_l0_s5_b_im=v_l0_s5_b_im, v_l0_s5_c_re=v_l0_s5_c_re, v_l0_s5_c_im=v_l0_s5_c_im, v_l0_s5_d=v_l0_s5_d, v_l0_s5_w_glu=v_l0_s5_w_glu, v_l0_s5_b_glu=v_l0_s5_b_glu, v_l0_ssd_conv_w=v_l0_ssd_conv_w, v_l0_ssd_conv_b=v_l0_ssd_conv_b, v_l0_ssd_dt_bias=v_l0_ssd_dt_bias, v_l0_ssd_a_log=v_l0_ssd_a_log, v_l0_ssd_d=v_l0_ssd_d, v_l0_ssd_norm_w=v_l0_ssd_norm_w, v_l0_w_out=v_l0_w_out, v_l1_norm_w=v_l1_norm_w, v_l1_w_in=v_l1_w_in, v_l1_fox_b_f=v_l1_fox_b_f, v_l1_w_out=v_l1_w_out, v_final_norm_w=v_final_norm_w)
    weights = {n: given[n] for n in TWIN_WEIGHTS}
    shared = {n: given[n] for n in SHARED_INPUTS}
    per_example = {n: given[n] for n in ['x']}
    grad_fn = _jax.value_and_grad(_loss, argnums=(0, 1))

    def one_microbatch(ex, loss_target):
        ex = dict(ex)
        diff = ex.pop(TWIN_DIFF_INPUT)
        return grad_fn(weights, diff, {**shared, **ex}, loss_target)

    if N_MICROBATCH == 1:
        loss, (grad_w, grad_x) = one_microbatch(per_example, given["loss_target"])
    else:
        def body(carry, xs):
            loss_sum, grad_sum = carry
            l_k, (gw_k, gx_k) = one_microbatch(xs[0], xs[1])
            with _jax.named_scope("update"):
                return (loss_sum + l_k, _jax.tree.map(_jnp.add, grad_sum, gw_k)), gx_k

        init = (_jnp.zeros((), _jnp.float32), _jax.tree.map(_jnp.zeros_like, weights))
        (loss, grad_w), grad_x = _jax.lax.scan(body, init, (per_example, given["loss_target"]))
    with _jax.named_scope("update"):
        delta_w, new_m, new_v = {}, {}, {}
        for n in TWIN_WEIGHTS:
            delta_w[n], new_m[n], new_v[n] = _adamw(weights[n], grad_w[n], given["m_" + n], given["v_" + n])
    return (loss, grad_x, *[grad_w[n] for n in TWIN_WEIGHTS], *[delta_w[n] for n in TWIN_WEIGHTS],
            *[new_m[n] for n in TWIN_WEIGHTS], *[new_v[n] for n in TWIN_WEIGHTS])
```

```python
import functools
import math

import jax
import jax.numpy as jnp
from jax import lax
from jax.experimental import pallas as pl
from jax.experimental.pallas import tpu as pltpu

F32 = jnp.float32
BF16 = jnp.bfloat16
MXU_DT = BF16

D_MODEL = 4096
S5_WIDTH = 2048
S5_GROUP = 16
S5_GROUPS = 128
S5_STATE = 64
S5_EIG_CLIP = -1e-4
S5_BLK = 16
SSD_WIDTH = 6144
SSD_HEAD_DIM = 64
SSD_HEADS = 96
SSD_GROUPS = 8
SSD_STATE = 128
SSD_CONV = 4
SSD_CHUNK = 128
SSD_XBC = 8192
SSD_HPG = SSD_HEADS // SSD_GROUPS
FOX_HEAD_DIM = 128
FOX_HEADS = 32
FOX_WIDTH = 4096
NORM_EPS = 1e-5
EVEN_IN = 18528
ODD_IN = 16416
EVEN_PAD = 18560
ODD_PAD = 16512
LANES = 128
N_DEV = 8

ADAM_LR = 0.001
ADAM_B1 = 0.9
ADAM_B2 = 0.999
ADAM_EPS = 1e-08
ADAM_WD = 0.01
ADAM_STEP = 10

VMEM_LIMIT_BYTES = 48 * 1024 * 1024


def _cparams(*sem):
    return pltpu.CompilerParams(dimension_semantics=sem, vmem_limit_bytes=VMEM_LIMIT_BYTES)


def _pick(n, target, quantum=LANES):
    if n <= target:
        return n
    t = (target // quantum) * quantum
    while t >= quantum:
        if n % t == 0:
            return t
        t -= quantum
    raise ValueError((n, target, quantum))


_MM_DIMS = {"nn": ((1,), (0,)), "nt": ((1,), (1,)), "tn": ((0,), (0,))}


def _mm(a, b, mode, out_dtype, name, res=None, tm_t=1024, tn_t=1024, tk_t=512):
    if mode == "nn":
        (m, k), (k2, n) = a.shape, b.shape
    elif mode == "nt":
        (m, k), (n, k2) = a.shape, b.shape
    else:
        (k, m), (k2, n) = a.shape, b.shape
    assert k == k2, (a.shape, b.shape, mode)
    tm, tn, tk = _pick(m, tm_t), _pick(n, tn_t), _pick(k, tk_t)
    nk = k // tk
    dims = (_MM_DIMS[mode], ((), ()))
    has_res = res is not None

    def body(*refs):
        if has_res:
            a_ref, b_ref, r_ref, o_ref, acc = refs
        else:
            a_ref, b_ref, o_ref, acc = refs
        kk = pl.program_id(2)

        @pl.when(kk == 0)
        def _():
            acc[...] = jnp.zeros_like(acc)

        acc[...] += lax.dot_general(a_ref[...].astype(MXU_DT), b_ref[...].astype(MXU_DT), dims,
                                    preferred_element_type=F32)

        @pl.when(kk == nk - 1)
        def _():
            r = acc[...]
            if has_res:
                r = r + r_ref[...]
            o_ref[...] = r.astype(out_dtype)

    a_spec = (pl.BlockSpec((tk, tm), lambda i, j, kk: (kk, i)) if mode == "tn"
              else pl.BlockSpec((tm, tk), lambda i, j, kk: (i, kk)))
    b_spec = (pl.BlockSpec((tn, tk), lambda i, j, kk: (j, kk)) if mode == "nt"
              else pl.BlockSpec((tk, tn), lambda i, j, kk: (kk, j)))
    o_spec = pl.BlockSpec((tm, tn), lambda i, j, kk: (i, j))
    in_specs = [a_spec, b_spec] + ([o_spec] if has_res else [])
    args = (a, b) + ((res,) if has_res else ())
    return pl.pallas_call(
        body, name=name, grid=(m // tm, n // tn, nk), in_specs=in_specs, out_specs=o_spec,
        out_shape=jax.ShapeDtypeStruct((m, n), out_dtype),
        scratch_shapes=[pltpu.VMEM((tm, tn), F32)],
        compiler_params=_cparams("parallel", "parallel", "arbitrary"),
    )(*args)


def _tiles(name, fn, tiled, rows, out_tiled, out_acc, tr, tc=None):
    tiled = [t if isinstance(t, tuple) else (t, 0) for t in tiled]
    length = tiled[0][0].shape[0]
    width = out_tiled[0][0] if out_tiled else rows[0].shape[1]
    tc = width if tc is None else tc
    tr = min(tr, length)
    n_in = len(tiled) + len(rows)
    n_ot = len(out_tiled)

    def body(*refs):
        outs = fn(*[r[...] for r in refs[:n_in]])
        outs_t, outs_a = outs[:n_ot], outs[n_ot:]
        for r, v in zip(refs[n_in:n_in + n_ot], outs_t):
            r[...] = v.astype(r.dtype)
        i = pl.program_id(1)
        for r, v in zip(refs[n_in + n_ot:], outs_a):
            @pl.when(i == 0)
            def _(r=r):
                r[...] = jnp.zeros_like(r)

            r[...] += jnp.broadcast_to(v, r.shape)

    def tspec(off):
        return pl.BlockSpec((tr, tc), lambda j, i: (i, j + off))

    in_specs = [tspec(off) for _, off in tiled] + [pl.BlockSpec((1, tc), lambda j, i: (0, j)) for _ in rows]
    out_specs = [tspec(0) for _ in out_tiled] + [pl.BlockSpec((8, tc), lambda j, i: (0, j)) for _ in range(out_acc)]
    out_shape = ([jax.ShapeDtypeStruct((length, w), dt) for w, dt in out_tiled]
                 + [jax.ShapeDtypeStruct((8, width), F32) for _ in range(out_acc)])
    return pl.pallas_call(
        body, name=name, grid=(width // tc, length // tr), in_specs=in_specs, out_specs=out_specs,
        out_shape=out_shape, compiler_params=_cparams("parallel", "arbitrary"),
    )(*[t for t, _ in tiled], *rows)


def _rms(x, w):
    return x * lax.rsqrt(jnp.mean(x * x, axis=-1, keepdims=True) + NORM_EPS) * w


def _colsum(v):
    return jnp.sum(v, axis=0, keepdims=True)


def _rmsnorm_fwd(x, w, name):
    def fn(xb, wb):
        return (_rms(xb, wb),)
    return _tiles(name, fn, [x], [w.reshape(1, -1)], [(x.shape[1], MXU_DT)], 0, tr=256)[0]


def _rmsnorm_bwd(x, w, dh, dres, name):
    def fn(xb, db, rb, wb):
        _, vjp = jax.vjp(_rms, xb, wb)
        dx, dw = vjp(db)
        return dx + rb, dx + rb, dw
    dx, dxb, dw = _tiles(name, fn, [x, dh, dres], [w.reshape(1, -1)],
                         [(x.shape[1], F32), (x.shape[1], MXU_DT)], 1, tr=256)
    return dx, dxb, dw[0]


def _final_loss(x, w, target, name):
    def fn(xb, tb, wb):
        def f(xv, wv):
            e = _rms(xv, wv) - tb
            return 0.5 * jnp.sum(jnp.mean(e * e, axis=-1, keepdims=True), axis=0, keepdims=True)
        lv, vjp = jax.vjp(f, xb, wb)
        dx, dw = vjp(jnp.ones_like(lv))
        return dx, dx, dw, jnp.broadcast_to(lv, (1, xb.shape[1]))
    dx, dxb, dw, lv = _tiles(name, fn, [x, target], [w.reshape(1, -1)],
                             [(x.shape[1], F32), (x.shape[1], MXU_DT)], 2, tr=256)
    return lv[0, 0], dx, dxb, dw[0]


def _dg(a, b, mode):
    return lax.dot_general(a.astype(MXU_DT), b.astype(MXU_DT), (_MM_DIMS[mode], ((), ())),
                           preferred_element_type=F32)


@jax.custom_vjp
def _dot_nn(a, b):
    return _dg(a, b, "nn")


@jax.custom_vjp
def _dot_nt(a, b):
    return _dg(a, b, "nt")


@jax.custom_vjp
def _dot_tn(a, b):
    return _dg(a, b, "tn")


_dot_nn.defvjp(lambda a, b: (_dg(a, b, "nn"), (a, b)),
               lambda r, g: (_dg(g, r[1], "nt"), _dg(r[0], g, "tn")))
_dot_nt.defvjp(lambda a, b: (_dg(a, b, "nt"), (a, b)),
               lambda r, g: (_dg(g, r[1], "nn"), _dg(g, r[0], "tn")))
_dot_tn.defvjp(lambda a, b: (_dg(a, b, "tn"), (a, b)),
               lambda r, g: (_dg(r[1], g, "nt"), _dg(r[0], g, "nn")))


def _dot_exact(a, b):
    return jnp.dot(a, b, precision=lax.Precision.HIGHEST, preferred_element_type=F32)


S5_NS = S5_GROUPS * S5_STATE
S5_BS = S5_NS // S5_BLK
S5_BC = S5_WIDTH // S5_BLK


def _s5_disc(lr_raw, li, ls, br, bi):
    lr = jnp.minimum(lr_raw, S5_EIG_CLIP)
    step = jnp.exp(ls)
    mag = jnp.exp(lr * step)
    ab_re = mag * jnp.cos(li * step)
    ab_im = mag * jnp.sin(li * step)
    denom = lr * lr + li * li
    nr = ab_re - 1.0
    ni = ab_im
    coef_re = (nr * lr + ni * li) / denom
    coef_im = (ni * lr - nr * li) / denom
    return ab_re, ab_im, coef_re * br - coef_im * bi, coef_re * bi + coef_im * br


def _s5_prep(lr_raw, li, ls, br, bi):
    shapes = [jax.ShapeDtypeStruct((1, S5_NS), F32)] * 2 + [jax.ShapeDtypeStruct((S5_GROUP, S5_NS), F32)] * 2

    def body(a, b, c, d, e, o1, o2, o3, o4):
        for r, v in zip((o1, o2, o3, o4), _s5_disc(a[...], b[...], c[...], d[...], e[...])):
            r[...] = v

    return pl.pallas_call(body, name="s5_prep", out_shape=shapes)(lr_raw, li, ls, br, bi)


def _s5_prep_bwd(lr_raw, li, ls, br, bi, d_are, d_aim, d_bbre, d_bbim):
    shapes = [jax.ShapeDtypeStruct((1, S5_NS), F32)] * 3 + [jax.ShapeDtypeStruct((S5_GROUP, S5_NS), F32)] * 2

    def body(a, b, c, d, e, g1, g2, g3, g4, o1, o2, o3, o4, o5):
        _, vjp = jax.vjp(_s5_disc, a[...], b[...], c[...], d[...], e[...])
        for r, v in zip((o1, o2, o3, o4, o5), vjp((g1[...], g2[...], g3[...], g4[...]))):
            r[...] = v

    return pl.pallas_call(body, name="s5_prep_bwd", out_shape=shapes)(
        lr_raw, li, ls, br, bi, d_are, d_aim, d_bbre, d_bbim)


def _cmul(ar, ai, br, bi):
    return ar * br - ai * bi, ar * bi + ai * br


def _s5_powers(ar, ai, n):
    pr, pi_ = [ar], [ai]
    for _ in range(n - 1):
        r, i = _cmul(pr[-1], pi_[-1], pr[-1], pi_[-1])
        pr.append(r)
        pi_.append(i)
    return pr, pi_


def _s5_scan_tile(xr, xi, pr, pi_, reverse):
    t = xr.shape[0]
    row = lax.broadcasted_iota(jnp.int32, xr.shape, 0)
    sr, si = xr, xi
    for k in range(len(pr)):
        d = 1 << k
        shift = (t - d) if reverse else d
        keep = (row < t - d) if reverse else (row >= d)
        qr = jnp.where(keep, pltpu.roll(sr, shift, 0), 0.0)
        qi = jnp.where(keep, pltpu.roll(si, shift, 0), 0.0)
        mr, mi = _cmul(pr[k], pi_[k], qr, qi)
        sr, si = sr + mr, si + mi
    return sr, si


def _s5_setup(ar, ai, t, reverse, apr, api, tabr, tabi):
    n = int(math.log2(t))
    pr, pi_ = _s5_powers(ar, ai, n)
    for k in range(n):
        apr[k:k + 1, :] = pr[k]
        api[k:k + 1, :] = pi_[k]
    row = lax.broadcasted_iota(jnp.int32, (t, ar.shape[1]), 0)
    first = (row == t - 1) if reverse else (row == 0)
    xr = jnp.where(first, ar, 0.0)
    xi = jnp.where(first, ai, 0.0)
    sr, si = _s5_scan_tile(xr, xi, pr, pi_, reverse)
    tabr[...] = sr
    tabi[...] = si


def _s5_fwd(p_s5, bb_re, bb_im, ct_re, ct_im, a_re, a_im, d_row, t_tile=256):
    length = p_s5.shape[0]
    t = min(t_tile, length)
    nt = length // t
    n = int(math.log2(t))

    def body(u_ref, bbr, bbi, ctr, cti, ar_ref, ai_ref, d_ref, y_ref, sr_ref, si_ref,
             apr, api, tabr, tabi, cr, ci):
        i = pl.program_id(1)

        @pl.when(i == 0)
        def _():
            _s5_setup(ar_ref[...], ai_ref[...], t, False, apr, api, tabr, tabi)
            cr[...] = jnp.zeros_like(cr)
            ci[...] = jnp.zeros_like(ci)

        u = u_ref[...]
        pr = [apr[k:k + 1, :] for k in range(n)]
        pi_ = [api[k:k + 1, :] for k in range(n)]
        sr, si = _s5_scan_tile(_dg(u, bbr[...], "nn"), _dg(u, bbi[...], "nn"), pr, pi_, False)
        mr, mi = _cmul(tabr[...], tabi[...], cr[0:1, :], ci[0:1, :])
        sr, si = sr + mr, si + mi
        cr[0:1, :] = sr[t - 1:t, :]
        ci[0:1, :] = si[t - 1:t, :]
        sr_ref[...] = sr
        si_ref[...] = si
        y_ref[...] = _dg(sr, ctr[...], "nn") - _dg(si, cti[...], "nn") + d_ref[...] * u

    blk3 = lambda a, b: pl.BlockSpec((None, a, b), lambda j, i: (j, 0, 0))
    return pl.pallas_call(
        body, name="s5_fwd", grid=(S5_BLK, nt),
        in_specs=[pl.BlockSpec((t, S5_BC), lambda j, i: (i, j)),
                  blk3(S5_BC, S5_BS), blk3(S5_BC, S5_BS), blk3(S5_BS, S5_BC), blk3(S5_BS, S5_BC),
                  blk3(1, S5_BS), blk3(1, S5_BS), blk3(1, S5_BC)],
        out_specs=[pl.BlockSpec((t, S5_BC), lambda j, i: (i, j)),
                   pl.BlockSpec((t, S5_BS), lambda j, i: (i, j)),
                   pl.BlockSpec((t, S5_BS), lambda j, i: (i, j))],
        out_shape=[jax.ShapeDtypeStruct((length, S5_WIDTH), F32),
                   jax.ShapeDtypeStruct((length, S5_NS), F32),
                   jax.ShapeDtypeStruct((length, S5_NS), F32)],
        scratch_shapes=[pltpu.VMEM((8, S5_BS), F32), pltpu.VMEM((8, S5_BS), F32),
                        pltpu.VMEM((t, S5_BS), F32), pltpu.VMEM((t, S5_BS), F32),
                        pltpu.VMEM((8, S5_BS), F32), pltpu.VMEM((8, S5_BS), F32)],
        compiler_params=_cparams("parallel", "arbitrary"),
    )(p_s5, bb_re, bb_im, ct_re, ct_im, a_re, a_im, d_row)


def _s5_bwd(dy, p_s5, s_re, s_im, bb_re, bb_im, ct_re, ct_im, a_re, a_im, d_row, t_tile=256):
    length = p_s5.shape[0]
    t = min(t_tile, length)
    nt = length // t
    n = int(math.log2(t))

    def body(dy_ref, u_ref, sr_ref, si_ref, pr_ref, pi_ref, bbr, bbi, ctr, cti, ar_ref, ai_ref, d_ref,
             du_ref, dbbr, dbbi, dctr, dcti, dar, dai, dd_ref, apr, api, tabr, tabi, cr, ci):
        i = pl.program_id(1)

        @pl.when(i == 0)
        def _():
            _s5_setup(ar_ref[...], -ai_ref[...], t, True, apr, api, tabr, tabi)
            for r in (cr, ci, dbbr, dbbi, dctr, dcti, dar, dai, dd_ref):
                r[...] = jnp.zeros_like(r)

        dyv = dy_ref[...]
        u = u_ref[...]
        pr = [apr[k:k + 1, :] for k in range(n)]
        pi_ = [api[k:k + 1, :] for k in range(n)]
        gr = _dg(dyv, ctr[...], "nt")
        gi = -_dg(dyv, cti[...], "nt")
        lr, li = _s5_scan_tile(gr, gi, pr, pi_, True)
        mr, mi = _cmul(tabr[...], tabi[...], cr[0:1, :], ci[0:1, :])
        lr, li = lr + mr, li + mi
        cr[0:1, :] = lr[0:1, :]
        ci[0:1, :] = li[0:1, :]
        du_ref[...] = (_dg(lr, bbr[...], "nt") + _dg(li, bbi[...], "nt") + d_ref[...] * dyv).astype(du_ref.dtype)
        dbbr[...] += _dg(u, lr, "tn")
        dbbi[...] += _dg(u, li, "tn")
        sr = sr_ref[...]
        si = si_ref[...]
        dctr[...] += _dg(sr, dyv, "tn")
        dcti[...] -= _dg(si, dyv, "tn")
        dd_ref[...] += jnp.broadcast_to(_colsum(dyv * u), dd_ref.shape)
        row = lax.broadcasted_iota(jnp.int32, sr.shape, 0)
        has_prev = (i < nt - 1).astype(F32)
        ssr = jnp.where(row == 0, pr_ref[7:8, :] * has_prev, pltpu.roll(sr, 1, 0))
        ssi = jnp.where(row == 0, pi_ref[7:8, :] * has_prev, pltpu.roll(si, 1, 0))
        dar[...] += jnp.broadcast_to(_colsum(lr * ssr + li * ssi), dar.shape)
        dai[...] += jnp.broadcast_to(_colsum(li * ssr - lr * ssi), dai.shape)

    rev = lambda j, i: (nt - 1 - i, j)
    prev = lambda j, i: (jnp.maximum((nt - 1 - i) * (t // 8) - 1, 0), j)
    blk3 = lambda a, b: pl.BlockSpec((None, a, b), lambda j, i: (j, 0, 0))
    return pl.pallas_call(
        body, name="s5_bwd", grid=(S5_BLK, nt),
        in_specs=[pl.BlockSpec((t, S5_BC), rev), pl.BlockSpec((t, S5_BC), rev),
                  pl.BlockSpec((t, S5_BS), rev), pl.BlockSpec((t, S5_BS), rev),
                  pl.BlockSpec((8, S5_BS), prev), pl.BlockSpec((8, S5_BS), prev),
                  blk3(S5_BC, S5_BS), blk3(S5_BC, S5_BS), blk3(S5_BS, S5_BC), blk3(S5_BS, S5_BC),
                  blk3(1, S5_BS), blk3(1, S5_BS), blk3(1, S5_BC)],
        out_specs=[pl.BlockSpec((t, S5_BC), rev),
                   blk3(S5_BC, S5_BS), blk3(S5_BC, S5_BS), blk3(S5_BS, S5_BC), blk3(S5_BS, S5_BC),
                   blk3(8, S5_BS), blk3(8, S5_BS), blk3(8, S5_BC)],
        out_shape=[jax.ShapeDtypeStruct((length, S5_WIDTH), MXU_DT),
                   jax.ShapeDtypeStruct((S5_BLK, S5_BC, S5_BS), F32), jax.ShapeDtypeStruct((S5_BLK, S5_BC, S5_BS), F32),
                   jax.ShapeDtypeStruct((S5_BLK, S5_BS, S5_BC), F32), jax.ShapeDtypeStruct((S5_BLK, S5_BS, S5_BC), F32),
                   jax.ShapeDtypeStruct((S5_BLK, 8, S5_BS), F32), jax.ShapeDtypeStruct((S5_BLK, 8, S5_BS), F32),
                   jax.ShapeDtypeStruct((S5_BLK, 8, S5_BC), F32)],
        scratch_shapes=[pltpu.VMEM((8, S5_BS), F32), pltpu.VMEM((8, S5_BS), F32),
                        pltpu.VMEM((t, S5_BS), F32), pltpu.VMEM((t, S5_BS), F32),
                        pltpu.VMEM((8, S5_BS), F32), pltpu.VMEM((8, S5_BS), F32)],
        compiler_params=_cparams("parallel", "arbitrary"),
    )(dy, p_s5, s_re, s_im, s_re, s_im, bb_re, bb_im, ct_re, ct_im, a_re, a_im, d_row)


def _blockdiag(m, rows, cols):
    m = m.reshape(S5_BLK, 8, rows, 1, cols)
    on_diag = jnp.eye(8, dtype=bool)[None, :, None, :, None]
    return jnp.where(on_diag, m, 0).reshape(S5_BLK, 8 * rows, 8 * cols)


def _blockdiag_t(m, rows, cols):
    m = m.reshape(S5_BLK, 8, rows, 8, cols)
    on_diag = jnp.eye(8, dtype=bool)[None, :, None, :, None]
    return jnp.sum(jnp.where(on_diag, m, 0), axis=3).reshape(S5_GROUPS, rows, cols)


def _gelu(y):
    return jax.nn.gelu(y)


def _s5_out_fn(y, zg, gate, b):
    return _gelu(y) * jax.nn.sigmoid(zg + b) * jax.nn.silu(gate)


HALO = 8


def _conv_fwd(xbc, w, b, tr=256, tc=1024):
    length, width = xbc.shape
    tr = min(tr, length)

    def body(x_ref, h_ref, w_ref, b_ref, cv_ref, act_ref):
        i = pl.program_id(1)
        x = x_ref[...]
        xx = jnp.concatenate([h_ref[...] * (i > 0).astype(F32), x], axis=0)
        acc = b_ref[...] + w_ref[3:4, :] * x
        for k in range(SSD_CONV - 1):
            acc = acc + w_ref[k:k + 1, :] * pltpu.roll(xx, SSD_CONV - 1 - k, 0)[HALO:, :]
        cv_ref[...] = acc
        act_ref[...] = jax.nn.silu(acc)

    main = pl.BlockSpec((tr, tc), lambda j, i: (i, j))
    before = pl.BlockSpec((HALO, tc), lambda j, i: (jnp.maximum(i * (tr // HALO) - 1, 0), j))
    return pl.pallas_call(
        body, name="ssd_conv_fwd", grid=(width // tc, length // tr),
        in_specs=[main, before, pl.BlockSpec((SSD_CONV, tc), lambda j, i: (0, j)),
                  pl.BlockSpec((1, tc), lambda j, i: (0, j))],
        out_specs=[main, main],
        out_shape=[jax.ShapeDtypeStruct((length, width), F32)] * 2,
        compiler_params=_cparams("parallel", "arbitrary"),
    )(xbc, xbc, w, b.reshape(1, -1))


def _conv_bwd(dact, cv, xbc, w, tr=256, tc=1024):
    length, width = xbc.shape
    tr = min(tr, length)
    nr = length // tr
    n = tr + HALO

    def dsilu(d, c):
        sg = jax.nn.sigmoid(c)
        return d * (sg * (1.0 + c * (1.0 - sg)))

    def body(da_ref, dan_ref, cv_ref, cvn_ref, x_ref, xp_ref, w_ref, dx_ref, dw_ref, db_ref):
        i = pl.program_id(1)

        @pl.when(i == 0)
        def _():
            dw_ref[...] = jnp.zeros_like(dw_ref)
            db_ref[...] = jnp.zeros_like(db_ref)

        dc = dsilu(da_ref[...], cv_ref[...])
        dcn = dsilu(dan_ref[...], cvn_ref[...]) * (i < nr - 1).astype(F32)
        dd = jnp.concatenate([dc, dcn], axis=0)
        x = x_ref[...]
        xx = jnp.concatenate([xp_ref[...] * (i > 0).astype(F32), x], axis=0)
        dx = w_ref[3:4, :] * dc
        dw_ref[3:4, :] += _colsum(dc * x)
        for k in range(SSD_CONV - 1):
            j = SSD_CONV - 1 - k
            dx = dx + w_ref[k:k + 1, :] * pltpu.roll(dd, n - j, 0)[:tr, :]
            dw_ref[k:k + 1, :] += _colsum(dc * pltpu.roll(xx, j, 0)[HALO:, :])
        dx_ref[...] = dx.astype(dx_ref.dtype)
        db_ref[...] += jnp.broadcast_to(_colsum(dc), db_ref.shape)

    main = pl.BlockSpec((tr, tc), lambda j, i: (i, j))
    before = pl.BlockSpec((HALO, tc), lambda j, i: (jnp.maximum(i * (tr // HALO) - 1, 0), j))
    after = pl.BlockSpec((HALO, tc), lambda j, i: (jnp.minimum((i + 1) * (tr // HALO), length // HALO - 1), j))
    acc = pl.BlockSpec((8, tc), lambda j, i: (0, j))
    return pl.pallas_call(
        body, name="ssd_conv_bwd", grid=(width // tc, nr),
        in_specs=[main, after, main, after, main, before, pl.BlockSpec((SSD_CONV, tc), lambda j, i: (0, j))],
        out_specs=[main, acc, acc],
        out_shape=[jax.ShapeDtypeStruct((length, width), MXU_DT),
                   jax.ShapeDtypeStruct((8, width), F32), jax.ShapeDtypeStruct((8, width), F32)],
        compiler_params=_cparams("parallel", "arbitrary"),
    )(dact, dact, cv, cv, xbc, xbc, w)


def _tri(lower):
    r = lax.broadcasted_iota(jnp.int32, (SSD_CHUNK, SSD_CHUNK), 0)
    c = lax.broadcasted_iota(jnp.int32, (SSD_CHUNK, SSD_CHUNK), 1)
    return ((r >= c) if lower else (r <= c)).astype(F32)


def _dt_fwd(raw, bias, a_log):
    length = raw.shape[0]
    nc = length // SSD_CHUNK

    def body(r_ref, b_ref, a_ref, dt_ref, cum_ref, cumt_ref):
        dt = jax.nn.softplus(r_ref[...] + b_ref[...])
        cum = _dot_exact(_tri(True), dt * (-jnp.exp(a_ref[...])))
        dt_ref[...] = dt
        cum_ref[...] = cum
        cumt_ref[...] = cum.T

    blk = pl.BlockSpec((SSD_CHUNK, LANES), lambda c: (c, 0))
    row = pl.BlockSpec((1, LANES), lambda c: (0, 0))
    return pl.pallas_call(
        body, name="ssd_dt_fwd", grid=(nc,), in_specs=[blk, row, row],
        out_specs=[blk, blk, pl.BlockSpec((None, LANES, SSD_CHUNK), lambda c: (c, 0, 0))],
        out_shape=[jax.ShapeDtypeStruct((length, LANES), F32)] * 2
        + [jax.ShapeDtypeStruct((nc, LANES, SSD_CHUNK), F32)],
        compiler_params=_cparams("parallel"),
    )(raw, bias, a_log)


def _dt_bwd(raw, bias, a_log, ddt, dcum):
    length = raw.shape[0]
    nc = length // SSD_CHUNK

    def body(r_ref, b_ref, a_ref, ddt_ref, dcum_ref, dr_ref, db_ref, da_ref):
        @pl.when(pl.program_id(0) == 0)
        def _():
            db_ref[...] = jnp.zeros_like(db_ref)
            da_ref[...] = jnp.zeros_like(da_ref)

        z = r_ref[...] + b_ref[...]
        a = -jnp.exp(a_ref[...])
        dla = _dot_exact(_tri(False), dcum_ref[...])
        draw = (ddt_ref[...] + dla * a) * jax.nn.sigmoid(z)
        dr_ref[...] = draw.astype(dr_ref.dtype)
        db_ref[...] += jnp.broadcast_to(_colsum(draw), db_ref.shape)
        da_ref[...] += jnp.broadcast_to(_colsum(dla * jax.nn.softplus(z)) * a, da_ref.shape)

    blk = pl.BlockSpec((SSD_CHUNK, LANES), lambda c: (c, 0))
    row = pl.BlockSpec((1, LANES), lambda c: (0, 0))
    acc = pl.BlockSpec((8, LANES), lambda c: (0, 0))
    return pl.pallas_call(
        body, name="ssd_dt_bwd", grid=(nc,), in_specs=[blk, row, row, blk, blk],
        out_specs=[blk, acc, acc],
        out_shape=[jax.ShapeDtypeStruct((length, LANES), MXU_DT),
                   jax.ShapeDtypeStruct((8, LANES), F32), jax.ShapeDtypeStruct((8, LANES), F32)],
        compiler_params=_cparams("arbitrary"),
    )(raw, bias, a_log, ddt, dcum)


def _ssd_head(x, dtc, cumc, cumr, cl, g, bm, cm, sp):
    xdt = x * dtc
    q = lax.broadcasted_iota(jnp.int32, g.shape, 0)
    k = lax.broadcasted_iota(jnp.int32, g.shape, 1)
    w = g * jnp.exp(jnp.where(q >= k, cumc - cumr, -1e30))
    y = _dot_nn(w, xdt) + _dot_nt(cm, sp) * jnp.exp(cumc)
    s_new = jnp.exp(cl) * sp + _dot_tn(xdt * jnp.exp(cl - cumc), bm)
    return y, s_new


SSD_GW = SSD_HPG * SSD_HEAD_DIM
SSD_B_OFF = SSD_WIDTH // SSD_STATE
SSD_C_OFF = SSD_B_OFF + SSD_GROUPS


def _ssd_core_specs(nc, rev):
    ch = (lambda c: nc - 1 - c) if rev else (lambda c: c)
    xs = pl.BlockSpec((SSD_CHUNK, SSD_GW), lambda g, c: (ch(c), g))
    bspec = pl.BlockSpec((SSD_CHUNK, SSD_STATE), lambda g, c: (ch(c), SSD_B_OFF + g))
    cspec = pl.BlockSpec((SSD_CHUNK, SSD_STATE), lambda g, c: (ch(c), SSD_C_OFF + g))
    lane = pl.BlockSpec((None, SSD_CHUNK, LANES), lambda g, c: (g, ch(c), 0))
    rows = pl.BlockSpec((None, None, 16, SSD_CHUNK), lambda g, c: (g, ch(c), 0, 0))
    st = pl.BlockSpec((None, None, SSD_HPG, SSD_HEAD_DIM, SSD_STATE), lambda g, c: (g, ch(c), 0, 0, 0))
    return xs, bspec, cspec, lane, rows, st


def _ssd_core_fwd(act, dtg, cumg, cumtg):
    length = act.shape[0]
    nc = length // SSD_CHUNK

    def body(x_ref, b_ref, c_ref, dt_ref, cum_ref, cumt_ref, y_ref, st_ref, s_scr):
        @pl.when(pl.program_id(1) == 0)
        def _():
            s_scr[...] = jnp.zeros_like(s_scr)

        bm = b_ref[...]
        cm = c_ref[...]
        g = _dot_nt(cm, bm)
        for r in range(SSD_HPG):
            cols = slice(r * SSD_HEAD_DIM, (r + 1) * SSD_HEAD_DIM)
            sp = s_scr[r]
            st_ref[r] = sp
            y, s_new = _ssd_head(x_ref[:, cols], dt_ref[:, r:r + 1], cum_ref[:, r:r + 1], cumt_ref[r:r + 1, :],
                                 cum_ref[SSD_CHUNK - 1:SSD_CHUNK, r:r + 1], g, bm, cm, sp)
            y_ref[:, cols] = y
            s_scr[r] = s_new

    xs, bspec, cspec, lane, rows, st = _ssd_core_specs(nc, False)
    return pl.pallas_call(
        body, name="ssd_core_fwd", grid=(SSD_GROUPS, nc),
        in_specs=[xs, bspec, cspec, lane, lane, rows], out_specs=[xs, st],
        out_shape=[jax.ShapeDtypeStruct((length, SSD_WIDTH), F32),
                   jax.ShapeDtypeStruct((SSD_GROUPS, nc, SSD_HPG, SSD_HEAD_DIM, SSD_STATE), F32)],
        scratch_shapes=[pltpu.VMEM((SSD_HPG, SSD_HEAD_DIM, SSD_STATE), F32)],
        compiler_params=_cparams("parallel", "arbitrary"),
    )(act, act, act, dtg, cumg, cumtg)


def _ssd_core_bwd(dy, dxs_add, act, dtg, cumg, cumtg, states):
    length = act.shape[0]
    nc = length // SSD_CHUNK

    def body(dy_ref, add_ref, x_ref, b_ref, c_ref, dt_ref, cum_ref, cumt_ref, st_ref,
             dx_ref, db_ref, dc_ref, ddt_ref, dcum_ref, dcumt_ref, dcl_ref, ds_scr):
        @pl.when(pl.program_id(1) == 0)
        def _():
            ds_scr[...] = jnp.zeros_like(ds_scr)

        for r_ in (ddt_ref, dcum_ref, dcumt_ref, dcl_ref):
            r_[...] = jnp.zeros_like(r_)
        bm = b_ref[...]
        cm = c_ref[...]
        g, g_vjp = jax.vjp(_dot_nt, cm, bm)
        dg = jnp.zeros_like(g)
        dbm = jnp.zeros_like(bm)
        dcm = jnp.zeros_like(cm)
        for r in range(SSD_HPG):
            cols = slice(r * SSD_HEAD_DIM, (r + 1) * SSD_HEAD_DIM)
            _, vjp = jax.vjp(_ssd_head, x_ref[:, cols], dt_ref[:, r:r + 1], cum_ref[:, r:r + 1],
                             cumt_ref[r:r + 1, :], cum_ref[SSD_CHUNK - 1:SSD_CHUNK, r:r + 1], g, bm, cm, st_ref[r])
            dx, ddtc, dcumc, dcumr, dcl, dg_r, dbm_r, dcm_r, dsp = vjp((dy_ref[:, cols], ds_scr[r]))
            dx_ref[:, cols] = dx + add_ref[:, cols]
            ddt_ref[:, r:r + 1] = ddtc
            dcum_ref[:, r:r + 1] = dcumc
            dcumt_ref[r:r + 1, :] = dcumr
            dcl_ref[0:1, r:r + 1] = dcl
            ds_scr[r] = dsp
            dg, dbm, dcm = dg + dg_r, dbm + dbm_r, dcm + dcm_r
        dcm_g, dbm_g = g_vjp(dg)
        db_ref[...] = dbm + dbm_g
        dc_ref[...] = dcm + dcm_g

    xs, bspec, cspec, lane, rows, st = _ssd_core_specs(nc, True)
    bc_out = pl.BlockSpec((SSD_CHUNK, SSD_STATE), lambda g, c: (nc - 1 - c, g))
    last = pl.BlockSpec((None, None, 8, LANES), lambda g, c: (g, nc - 1 - c, 0, 0))
    return pl.pallas_call(
        body, name="ssd_core_bwd", grid=(SSD_GROUPS, nc),
        in_specs=[xs, xs, xs, bspec, cspec, lane, lane, rows, st],
        out_specs=[xs, bc_out, bc_out, lane, lane, rows, last],
        out_shape=[jax.ShapeDtypeStruct((length, SSD_WIDTH), F32),
                   jax.ShapeDtypeStruct((length, SSD_GROUPS * SSD_STATE), F32),
                   jax.ShapeDtypeStruct((length, SSD_GROUPS * SSD_STATE), F32),
                   jax.ShapeDtypeStruct((SSD_GROUPS, length, LANES), F32),
                   jax.ShapeDtypeStruct((SSD_GROUPS, length, LANES), F32),
                   jax.ShapeDtypeStruct((SSD_GROUPS, nc, 16, SSD_CHUNK), F32),
                   jax.ShapeDtypeStruct((SSD_GROUPS, nc, 8, LANES), F32)],
        scratch_shapes=[pltpu.VMEM((SSD_HPG, SSD_HEAD_DIM, SSD_STATE), F32)],
        compiler_params=_cparams("parallel", "arbitrary"),
    )(dy, dxs_add, act, act, act, dtg, cumg, cumtg, states)


def _ssd_post_fn(yc, xs, z, dch, nw):
    y = (yc + dch * xs) * jax.nn.silu(z)
    return y * lax.rsqrt(jnp.mean(y * y, axis=-1, keepdims=True) + NORM_EPS) * nw


FOX_SCALE = 1.0 / math.sqrt(FOX_HEAD_DIM)
MASKED = -1e30


def _fgate_fwd(f_raw, b_f):
    length = f_raw.shape[0]
    nb = length // SSD_CHUNK

    def body(f_ref, b_ref, c_ref, carry):
        @pl.when(pl.program_id(0) == 0)
        def _():
            carry[...] = jnp.zeros_like(carry)

        c = _dot_exact(_tri(True), jax.nn.log_sigmoid(f_ref[...] + b_ref[...])) + carry[0:1, :]
        c_ref[...] = c
        carry[0:1, :] = c[SSD_CHUNK - 1:SSD_CHUNK, :]

    blk = pl.BlockSpec((SSD_CHUNK, LANES), lambda i: (i, 0))
    return pl.pallas_call(
        body, name="fox_fgate_fwd", grid=(nb,), in_specs=[blk, pl.BlockSpec((1, LANES), lambda i: (0, 0))],
        out_specs=blk, out_shape=jax.ShapeDtypeStruct((length, LANES), F32),
        scratch_shapes=[pltpu.VMEM((8, LANES), F32)], compiler_params=_cparams("arbitrary"),
    )(f_raw, b_f)


def _fgate_bwd(f_raw, b_f, dc):
    length = f_raw.shape[0]
    nb = length // SSD_CHUNK

    def body(f_ref, b_ref, dc_ref, df_ref, db_ref, carry):
        @pl.when(pl.program_id(0) == 0)
        def _():
            carry[...] = jnp.zeros_like(carry)
            db_ref[...] = jnp.zeros_like(db_ref)

        dcv = dc_ref[...]
        dlog = _dot_exact(_tri(False), dcv) + carry[0:1, :]
        carry[0:1, :] += _colsum(dcv)
        df = dlog * jax.nn.sigmoid(-(f_ref[...] + b_ref[...]))
        df_ref[...] = df.astype(df_ref.dtype)
        db_ref[...] += jnp.broadcast_to(_colsum(df), db_ref.shape)

    blk = pl.BlockSpec((SSD_CHUNK, LANES), lambda i: (nb - 1 - i, 0))
    return pl.pallas_call(
        body, name="fox_fgate_bwd", grid=(nb,),
        in_specs=[blk, pl.BlockSpec((1, LANES), lambda i: (0, 0)), blk],
        out_specs=[blk, pl.BlockSpec((8, LANES), lambda i: (0, 0))],
        out_shape=[jax.ShapeDtypeStruct((length, LANES), MXU_DT), jax.ShapeDtypeStruct((8, LANES), F32)],
        scratch_shapes=[pltpu.VMEM((8, LANES), F32)], compiler_params=_cparams("arbitrary"),
    )(f_raw, b_f, dc)


def _fox_scores(q, k, cq, ck, q0, k0):
    s = _dg(q, k, "nt") * FOX_SCALE + (cq - ck)
    qpos = q0 + lax.broadcasted_iota(jnp.int32, s.shape, 0)
    kpos = k0 + lax.broadcasted_iota(jnp.int32, s.shape, 1)
    return jnp.where(kpos <= qpos, s, MASKED)


def _fox_fwd(qkv, c_rep, c_t, tile=256):
    length = qkv.shape[0]
    t = min(tile, length)
    nq = length // t

    def body(q_ref, k_ref, v_ref, cq_ref, ct_ref, o_ref, lse_ref):
        i = pl.program_id(1)
        q = q_ref[...]
        cq = cq_ref[:, 0:1]

        def step(j, carry):
            m, l, acc = carry
            k0 = pl.multiple_of(j * t, t)
            s = _fox_scores(q, k_ref[pl.ds(k0, t), :], cq, ct_ref[:, pl.ds(k0, t)], i * t, k0)
            m_new = jnp.maximum(m, jnp.max(s, axis=1, keepdims=True))
            p = jnp.exp(s - m_new)
            alpha = jnp.exp(m - m_new)
            return (m_new, alpha * l + jnp.sum(p, axis=1, keepdims=True),
                    alpha * acc + _dg(p, v_ref[pl.ds(k0, t), :], "nn"))

        init = (jnp.full((t, 1), MASKED, F32), jnp.zeros((t, 1), F32), jnp.zeros((t, FOX_HEAD_DIM), F32))
        m, l, acc = lax.fori_loop(0, i + 1, step, init)
        o_ref[...] = acc / l
        lse_ref[...] = jnp.broadcast_to(m + jnp.log(l), lse_ref.shape)

    qt = pl.BlockSpec((t, FOX_HEAD_DIM), lambda h, i: (i, h))
    return pl.pallas_call(
        body, name="fox_attn_fwd", grid=(FOX_HEADS, nq),
        in_specs=[qt,
                  pl.BlockSpec((length, FOX_HEAD_DIM), lambda h, i: (0, FOX_HEADS + h)),
                  pl.BlockSpec((length, FOX_HEAD_DIM), lambda h, i: (0, 2 * FOX_HEADS + h)),
                  qt, pl.BlockSpec((None, 1, length), lambda h, i: (h, 0, 0))],
        out_specs=[qt, qt],
        out_shape=[jax.ShapeDtypeStruct((length, FOX_WIDTH), F32)] * 2,
        compiler_params=_cparams("parallel", "arbitrary"),
    )(qkv, qkv, qkv, c_rep, c_t)


def _fox_bwd(qkv, d_att, lse, delta, c_rep, c_t, tile=256):
    length = qkv.shape[0]
    t = min(tile, length)
    nk = length // t

    def body(q_ref, k_ref, v_ref, do_ref, lse_ref, dl_ref, cq_ref, ct_ref,
             dq_ref, dk_ref, dv_ref, dcq_ref, dck_ref, dq_acc):
        j = pl.program_id(1)

        @pl.when(j == 0)
        def _():
            dq_acc[...] = jnp.zeros_like(dq_acc)
            dcq_ref[...] = jnp.zeros_like(dcq_ref)

        k = k_ref[...]
        v = v_ref[...]
        ck = ct_ref[...]

        def step(i, carry):
            dk, dv, dck = carry
            q0 = pl.multiple_of(i * t, t)
            rows = pl.ds(q0, t)
            q = q_ref[rows, :]
            do = do_ref[rows, :]
            s = _fox_scores(q, k, cq_ref[rows, 0:1], ck, q0, j * t)
            p = jnp.exp(s - lse_ref[rows, 0:1])
            dv = dv + _dg(p, do, "tn")
            ds = p * (_dg(do, v, "nt") - dl_ref[rows, 0:1])
            dk = dk + _dg(ds, q, "tn") * FOX_SCALE
            dq_acc[rows, :] += _dg(ds, k, "nn") * FOX_SCALE
            dcq_ref[rows, :] += jnp.broadcast_to(jnp.sum(ds, axis=1, keepdims=True), (t, FOX_HEAD_DIM))
            return dk, dv, dck + _colsum(ds)

        init = (jnp.zeros((t, FOX_HEAD_DIM), F32), jnp.zeros((t, FOX_HEAD_DIM), F32), jnp.zeros((1, t), F32))
        dk, dv, dck = lax.fori_loop(j, nk, step, init)
        dk_ref[...] = dk.astype(dk_ref.dtype)
        dv_ref[...] = dv.astype(dv_ref.dtype)
        dck_ref[...] = -dck

        @pl.when(j == nk - 1)
        def _():
            dq_ref[...] = dq_acc[...].astype(dq_ref.dtype)

    full = lambda off: pl.BlockSpec((length, FOX_HEAD_DIM), lambda h, j: (0, off + h))
    kt = lambda off: pl.BlockSpec((t, FOX_HEAD_DIM), lambda h, j: (j, off + h))
    ck_spec = pl.BlockSpec((None, 1, t), lambda h, j: (h, 0, j))
    return pl.pallas_call(
        body, name="fox_attn_bwd", grid=(FOX_HEADS, nk),
        in_specs=[full(0), kt(FOX_HEADS), kt(2 * FOX_HEADS), full(0), full(0), full(0), full(0), ck_spec],
        out_specs=[full(0), kt(0), kt(0), full(0), ck_spec],
        out_shape=[jax.ShapeDtypeStruct((length, FOX_WIDTH), MXU_DT)] * 3
        + [jax.ShapeDtypeStruct((length, FOX_WIDTH), F32), jax.ShapeDtypeStruct((FOX_HEADS, 1, length), F32)],
        scratch_shapes=[pltpu.VMEM((length, FOX_HEAD_DIM), F32)],
        compiler_params=_cparams("parallel", "arbitrary"),
    )(qkv, qkv, qkv, d_att, lse, delta, c_rep, c_t)


def _fox_gate_fn(att, gate):
    return att * jax.nn.silu(gate)


def _exchange(x, gather, name):
    out_shape = (N_DEV,) + x.shape if gather else x.shape

    def body(x_ref, o_ref, send_sems, recv_sems, local_sem):
        mx, my, mc = lax.axis_index("x"), lax.axis_index("y"), lax.axis_index("c")
        me = 4 * mx + 2 * my + mc

        def src(p):
            return x_ref if gather else x_ref.at[p]

        local = pltpu.make_async_copy(src(me), o_ref.at[me], local_sem)
        local.start()
        sends, recvs = [], []
        for d in range(1, N_DEV):
            px = 1 - mx if d & 4 else mx
            py = 1 - my if d & 2 else my
            pc = 1 - mc if d & 1 else mc
            peer = 4 * px + 2 * py + pc

            def copy(src_ref, slot, d=d, dev=(px, py, pc)):
                return pltpu.make_async_remote_copy(
                    src_ref=src_ref, dst_ref=o_ref.at[slot], send_sem=send_sems.at[d - 1],
                    recv_sem=recv_sems.at[d - 1], device_id=dev, device_id_type=pl.DeviceIdType.MESH)

            sends.append(copy(src(peer), me))
            recvs.append(copy(src(peer), peer))
        for cp in sends:
            cp.start()
        for cp in recvs:
            cp.wait_recv()
        for cp in sends:
            cp.wait_send()
        local.wait()

    anyspace = pl.BlockSpec(memory_space=pl.ANY)
    return pl.pallas_call(
        body, name=name, in_specs=[anyspace], out_specs=anyspace,
        out_shape=jax.ShapeDtypeStruct(out_shape, x.dtype),
        scratch_shapes=[pltpu.SemaphoreType.DMA((N_DEV - 1,)), pltpu.SemaphoreType.DMA((N_DEV - 1,)),
                        pltpu.SemaphoreType.DMA],
        compiler_params=pltpu.CompilerParams(has_side_effects=True),
    )(x)


ADAM_TILE_ELEMS = 128 * 1024


def _adamw(w, parts, m, v, name):
    rows, cols = w.shape
    tr = rows if rows * cols <= ADAM_TILE_ELEMS else _pick(rows, max(8, ADAM_TILE_ELEMS // cols), 8)

    def body(w_ref, p_ref, m_ref, v_ref, g_ref, d_ref, nm_ref, nv_ref):
        g = p_ref[0].astype(F32)
        for p in range(1, N_DEV):
            g = g + p_ref[p].astype(F32)
        mm = ADAM_B1 * m_ref[...] + (1.0 - ADAM_B1) * g
        vv = ADAM_B2 * v_ref[...] + (1.0 - ADAM_B2) * jnp.square(g)
        m_hat = mm / (1.0 - ADAM_B1 ** ADAM_STEP)
        v_hat = vv / (1.0 - ADAM_B2 ** ADAM_STEP)
        g_ref[...] = g
        d_ref[...] = -ADAM_LR * (m_hat / (jnp.sqrt(v_hat) + ADAM_EPS) + ADAM_WD * w_ref[...])
        nm_ref[...] = mm
        nv_ref[...] = vv

    blk = pl.BlockSpec((tr, cols), lambda i: (i, 0))
    return pl.pallas_call(
        body, name=name, grid=(rows // tr,),
        in_specs=[blk, pl.BlockSpec((N_DEV, tr, cols), lambda i: (0, i, 0)), blk, blk],
        out_specs=[blk] * 4, out_shape=[jax.ShapeDtypeStruct((rows, cols), F32)] * 4,
        compiler_params=_cparams("parallel"),
    )(w, parts, m, v)


WEIGHTS = ("l0_norm_w", "l0_w_in", "l0_s5_lambda_re", "l0_s5_lambda_im", "l0_s5_log_step", "l0_s5_b_re",
           "l0_s5_b_im", "l0_s5_c_re", "l0_s5_c_im", "l0_s5_d", "l0_s5_w_glu", "l0_s5_b_glu", "l0_ssd_conv_w",
           "l0_ssd_conv_b", "l0_ssd_dt_bias", "l0_ssd_a_log", "l0_ssd_d", "l0_ssd_norm_w", "l0_w_out",
           "l1_norm_w", "l1_w_in", "l1_fox_b_f", "l1_w_out", "final_norm_w")
SHARDED = ("l0_w_in", "l0_s5_w_glu", "l0_ssd_conv_w", "l0_w_out", "l1_w_in", "l1_w_out")
COLUMN_SHARDED = ("l0_w_in", "l0_ssd_conv_w", "l1_w_in")


def _pad_lanes(a, width=LANES):
    return jnp.pad(a, [(0, 0)] * (a.ndim - 1) + [(0, width - a.shape[-1])])


def _to_groups(a):
    length = a.shape[0]
    return _pad_lanes(a[:, :SSD_HEADS].reshape(length, SSD_GROUPS, SSD_HPG).transpose(1, 0, 2))


def _from_groups(a):
    length = a.shape[1]
    return a[:, :, :SSD_HPG].transpose(1, 0, 2).reshape(length, SSD_HEADS)


def _local_step(x, target, w):
    length = x.shape[0]
    nc = length // SSD_CHUNK
    g = {}

    h0 = _rmsnorm_fwd(x, w["l0_norm_w"], "l0_norm")
    w0 = w["l0_w_in"]
    w0_s5, w0_z, w0_xbc = w0[:, :2 * S5_WIDTH], w0[:, 2 * S5_WIDTH:2 * S5_WIDTH + SSD_WIDTH], w0[:, 10240:18432]
    w0_dt = _pad_lanes(w0[:, 18432:])
    p_s5 = _mm(h0, w0_s5, "nn", F32, "l0_in_s5")
    p_z = _mm(h0, w0_z, "nn", F32, "l0_in_z")
    p_xbc = _mm(h0, w0_xbc, "nn", F32, "l0_in_xbc")
    p_dt = _mm(h0, w0_dt, "nn", F32, "l0_in_dt")

    row = lambda a: a.reshape(1, S5_NS)
    b_rows = lambda a: a.transpose(2, 0, 1).reshape(S5_GROUP, S5_NS)
    prep_in = (row(w["l0_s5_lambda_re"]), row(w["l0_s5_lambda_im"]),
               row(jnp.repeat(w["l0_s5_log_step"], S5_STATE)), b_rows(w["l0_s5_b_re"]), b_rows(w["l0_s5_b_im"]))
    ab_re, ab_im, bbr, bbi = _s5_prep(*prep_in)
    to_bb = lambda a: _blockdiag(a.reshape(S5_GROUP, S5_GROUPS, S5_STATE).transpose(1, 0, 2),
                                 S5_GROUP, S5_STATE).astype(MXU_DT)
    to_ct = lambda a: _blockdiag(a.transpose(0, 2, 1), S5_STATE, S5_GROUP).astype(MXU_DT)
    bb_re, bb_im = to_bb(bbr), to_bb(bbi)
    ct_re, ct_im = to_ct(w["l0_s5_c_re"]), to_ct(w["l0_s5_c_im"])
    a_re3, a_im3 = ab_re.reshape(S5_BLK, 1, S5_BS), ab_im.reshape(S5_BLK, 1, S5_BS)
    d3 = w["l0_s5_d"].reshape(S5_BLK, 1, S5_BC)
    y5, s_re, s_im = _s5_fwd(p_s5, bb_re, bb_im, ct_re, ct_im, a_re3, a_im3, d3)
    g_bf = _tiles("s5_gelu", lambda yb: (_gelu(yb),), [y5], [], [(S5_WIDTH, MXU_DT)], 0, tr=256)[0]
    zg = _mm(g_bf, w["l0_s5_w_glu"], "nn", F32, "s5_glu")
    b_glu = w["l0_s5_b_glu"].reshape(1, -1)
    s5_out = _tiles("s5_out", lambda yb, zb, gb, bb: (_s5_out_fn(yb, zb, gb, bb),),
                    [y5, zg, (p_s5, 1)], [b_glu], [(S5_WIDTH, MXU_DT)], 0, tr=256)[0]

    conv_w = w["l0_ssd_conv_w"]
    cv, act = _conv_fwd(p_xbc, conv_w, w["l0_ssd_conv_b"])
    bias_row = _pad_lanes(w["l0_ssd_dt_bias"].reshape(1, -1))
    alog_row = _pad_lanes(w["l0_ssd_a_log"].reshape(1, -1))
    dt, cum, cum_t = _dt_fwd(p_dt, bias_row, alog_row)
    dtg, cumg = _to_groups(dt), _to_groups(cum)
    cumtg = cum_t[:, :SSD_HEADS].reshape(nc, SSD_GROUPS, SSD_HPG, SSD_CHUNK).transpose(1, 0, 2, 3)
    cumtg = jnp.pad(cumtg, ((0, 0), (0, 0), (0, 16 - SSD_HPG), (0, 0)))
    ycore, states = _ssd_core_fwd(act, dtg, cumg, cumtg)
    dchan = jnp.repeat(w["l0_ssd_d"], SSD_HEAD_DIM).reshape(1, -1)
    nw_row = w["l0_ssd_norm_w"].reshape(1, -1)
    ssd_out = _tiles("ssd_post", lambda a, b, c, d, e: (_ssd_post_fn(a, b, c, d, e),),
                     [ycore, act, p_z], [dchan, nw_row], [(SSD_WIDTH, MXU_DT)], 0, tr=256, tc=SSD_GW)[0]
    mixed = jnp.concatenate([s5_out, ssd_out], axis=1)
    x1 = _mm(mixed, w["l0_w_out"], "nn", F32, "l0_out", res=x)

    h1 = _rmsnorm_fwd(x1, w["l1_norm_w"], "l1_norm")
    w1 = w["l1_w_in"]
    w1_qkv, w1_gate = w1[:, :3 * FOX_WIDTH], w1[:, 3 * FOX_WIDTH:4 * FOX_WIDTH]
    w1_f = _pad_lanes(w1[:, 4 * FOX_WIDTH:])
    qkv = _mm(h1, w1_qkv, "nn", MXU_DT, "l1_in_qkv")
    gate1 = _mm(h1, w1_gate, "nn", F32, "l1_in_gate")
    f_raw = _mm(h1, w1_f, "nn", F32, "l1_in_f")
    bf_row = _pad_lanes(w["l1_fox_b_f"].reshape(1, -1))
    c = _fgate_fwd(f_raw, bf_row)
    c_rep = jnp.repeat(c[:, :FOX_HEADS], FOX_HEAD_DIM, axis=1)
    c_t = c[:, :FOX_HEADS].T.reshape(FOX_HEADS, 1, length)
    att, lse = _fox_fwd(qkv, c_rep, c_t)
    out1 = _tiles("fox_gate", lambda a, b: (_fox_gate_fn(a, b),), [att, gate1], [],
                  [(FOX_WIDTH, MXU_DT)], 0, tr=256)[0]
    x2 = _mm(out1, w["l1_w_out"], "nn", F32, "l1_out", res=x1)

    loss_part, dx2, dx2b, g["final_norm_w"] = _final_loss(x2, w["final_norm_w"], target, "final_loss")

    d_out1 = _mm(dx2b, w["l1_w_out"], "nt", F32, "l1_out_dx")
    g["l1_w_out"] = _mm(out1, dx2b, "tn", F32, "l1_out_dw")

    def gate_bwd(a, gt, d):
        _, vjp = jax.vjp(_fox_gate_fn, a, gt)
        da, dgt = vjp(d)
        return da, dgt, jnp.broadcast_to(jnp.sum(da * a, axis=1, keepdims=True), a.shape)

    d_att, d_gate1, delta = _tiles("fox_gate_bwd", gate_bwd, [att, gate1, d_out1], [],
                                   [(FOX_WIDTH, MXU_DT), (FOX_WIDTH, MXU_DT), (FOX_WIDTH, F32)], 0,
                                   tr=512, tc=FOX_HEAD_DIM)
    dq, dk, dv, dcq, dck = _fox_bwd(qkv, d_att, lse, delta, c_rep, c_t)
    dc = dcq.reshape(length, FOX_HEADS, FOX_HEAD_DIM)[:, :, 0] + dck.reshape(FOX_HEADS, length).T
    df, dbf = _fgate_bwd(f_raw, bf_row, _pad_lanes(dc))
    g["l1_fox_b_f"] = dbf[0, :FOX_HEADS]
    dqkv = jnp.concatenate([dq, dk, dv], axis=1)
    dh1 = _mm(dqkv, w1_qkv, "nt", F32, "l1_in_dx_qkv")
    dh1 = _mm(d_gate1, w1_gate, "nt", F32, "l1_in_dx_gate", res=dh1)
    dh1 = _mm(df, w1_f, "nt", F32, "l1_in_dx_f", res=dh1)
    g["l1_w_in"] = jnp.concatenate([
        _mm(h1, dqkv, "tn", F32, "l1_in_dw_qkv"), _mm(h1, d_gate1, "tn", F32, "l1_in_dw_gate"),
        _mm(h1, df, "tn", F32, "l1_in_dw_f")[:, :FOX_HEADS]], axis=1)
    dx1, dx1b, g["l1_norm_w"] = _rmsnorm_bwd(x1, w["l1_norm_w"], dh1, dx2, "l1_norm_bwd")

    wout0 = w["l0_w_out"]
    d_s5 = _mm(dx1b, wout0[:S5_WIDTH], "nt", F32, "l0_out_dx_s5")
    d_ssd = _mm(dx1b, wout0[S5_WIDTH:], "nt", F32, "l0_out_dx_ssd")
    g["l0_w_out"] = _mm(mixed, dx1b, "tn", F32, "l0_out_dw")

    def post_bwd(a, b, c_, d, dch, nw):
        _, vjp = jax.vjp(_ssd_post_fn, a, b, c_, dch, nw)
        return vjp(d)

    dycore, dxs_post, dz, ddch, dnw = _tiles(
        "ssd_post_bwd", post_bwd, [ycore, act, p_z, d_ssd], [dchan, nw_row],
        [(SSD_WIDTH, F32), (SSD_WIDTH, F32), (SSD_WIDTH, MXU_DT)], 2, tr=256, tc=SSD_GW)
    g["l0_ssd_d"] = ddch[0].reshape(SSD_HEADS, SSD_HEAD_DIM).sum(axis=1)
    g["l0_ssd_norm_w"] = dnw[0]
    dxs, d_b, d_c, ddtg, dcumg, dcumtg, dclg = _ssd_core_bwd(dycore, dxs_post, act, dtg, cumg, cumtg, states)
    dact = jnp.concatenate([dxs, d_b, d_c], axis=1)
    dxbc, dconvw, dconvb = _conv_bwd(dact, cv, p_xbc, conv_w)
    g["l0_ssd_conv_w"] = dconvw[:SSD_CONV]
    g["l0_ssd_conv_b"] = dconvb[0]
    dcum = _from_groups(dcumg)
    dcum = dcum + dcumtg[:, :, :SSD_HPG].transpose(1, 3, 0, 2).reshape(length, SSD_HEADS)
    dcl = dclg[:, :, 0, :SSD_HPG].transpose(1, 0, 2).reshape(nc, SSD_HEADS)
    dcum = dcum.reshape(nc, SSD_CHUNK, SSD_HEADS).at[:, SSD_CHUNK - 1, :].add(dcl).reshape(length, SSD_HEADS)
    ddt_raw, dbias, dalog = _dt_bwd(p_dt, bias_row, alog_row, _pad_lanes(_from_groups(ddtg)), _pad_lanes(dcum))
    g["l0_ssd_dt_bias"] = dbias[0, :SSD_HEADS]
    g["l0_ssd_a_log"] = dalog[0, :SSD_HEADS]

    def s5_out_bwd(yb, zb, gb, d, bb):
        _, vjp = jax.vjp(_s5_out_fn, yb, zb, gb, bb)
        return vjp(d)

    dy_direct, dzg, dgate0, dbglu = _tiles(
        "s5_out_bwd", s5_out_bwd, [y5, zg, (p_s5, 1), d_s5], [b_glu],
        [(S5_WIDTH, F32), (S5_WIDTH, MXU_DT), (S5_WIDTH, MXU_DT)], 1, tr=256)
    g["l0_s5_b_glu"] = dbglu[0]
    g["l0_s5_w_glu"] = _mm(g_bf, dzg, "tn", F32, "s5_glu_dw")
    dg2 = _mm(dzg, w["l0_s5_w_glu"], "nt", F32, "s5_glu_dx")

    def gelu_bwd(yb, d, direct):
        _, vjp = jax.vjp(_gelu, yb)
        return (vjp(d)[0] + direct,)

    dy5 = _tiles("s5_gelu_bwd", gelu_bwd, [y5, dg2, dy_direct], [], [(S5_WIDTH, F32)], 0, tr=256)[0]
    du, dbbr3, dbbi3, dctr3, dcti3, dar, dai, dd5 = _s5_bwd(dy5, p_s5, s_re, s_im, bb_re, bb_im, ct_re, ct_im,
                                                           a_re3, a_im3, d3)
    from_bb = lambda a: _blockdiag_t(a, S5_GROUP, S5_STATE).transpose(1, 0, 2).reshape(S5_GROUP, S5_NS)
    from_ct = lambda a: _blockdiag_t(a, S5_STATE, S5_GROUP).transpose(0, 2, 1)
    g["l0_s5_c_re"], g["l0_s5_c_im"] = from_ct(dctr3), from_ct(dcti3)
    g["l0_s5_d"] = dd5[:, 0, :].reshape(S5_GROUPS, S5_GROUP)
    dlr, dli, dls, dbr, dbi = _s5_prep_bwd(*prep_in, dar[:, 0, :].reshape(1, S5_NS), dai[:, 0, :].reshape(1, S5_NS),
                                           from_bb(dbbr3), from_bb(dbbi3))
    g["l0_s5_lambda_re"] = dlr.reshape(S5_GROUPS, S5_STATE)
    g["l0_s5_lambda_im"] = dli.reshape(S5_GROUPS, S5_STATE)
    g["l0_s5_log_step"] = dls.reshape(S5_GROUPS, S5_STATE).sum(axis=1)
    from_rows = lambda a: a.reshape(S5_GROUP, S5_GROUPS, S5_STATE).transpose(1, 2, 0)
    g["l0_s5_b_re"], g["l0_s5_b_im"] = from_rows(dbr), from_rows(dbi)

    dus = jnp.concatenate([du, dgate0], axis=1)
    dh0 = _mm(dus, w0_s5, "nt", F32, "l0_in_dx_s5")
    dh0 = _mm(dz, w0_z, "nt", F32, "l0_in_dx_z", res=dh0)
    dh0 = _mm(dxbc, w0_xbc, "nt", F32, "l0_in_dx_xbc", res=dh0)
    dh0 = _mm(ddt_raw, w0_dt, "nt", F32, "l0_in_dx_dt", res=dh0)
    g["l0_w_in"] = jnp.concatenate([
        _mm(h0, dus, "tn", F32, "l0_in_dw_s5"), _mm(h0, dz, "tn", F32, "l0_in_dw_z"),
        _mm(h0, dxbc, "tn", F32, "l0_in_dw_xbc"),
        _mm(h0, ddt_raw, "tn", F32, "l0_in_dw_dt")[:, :SSD_HEADS]], axis=1)
    grad_x, _, g["l0_norm_w"] = _rmsnorm_bwd(x, w["l0_norm_w"], dh0, dx1, "l0_norm_bwd")
    return loss_part, grad_x, g


def _gather_weight(name, shard):
    if name == "l0_ssd_conv_w":
        full = _exchange(shard, True, "gather_" + name)
    else:
        full = _exchange(shard.astype(MXU_DT), True, "gather_" + name)
    if name in COLUMN_SHARDED:
        return full.transpose(1, 0, 2).reshape(shard.shape[0], N_DEV * shard.shape[1])
    return full.reshape(N_DEV * shard.shape[0], shard.shape[1])


def _scatter_grad(name, grad, shard_shape):
    rows, cols = shard_shape
    if name in COLUMN_SHARDED:
        parts = grad.reshape(rows, N_DEV, cols).transpose(1, 0, 2)
    else:
        parts = grad.reshape(N_DEV, rows, cols)
    if name != "l0_ssd_conv_w":
        parts = parts.astype(MXU_DT)
    return _exchange(parts, False, "scatter_" + name)


SMALL_ROWS_QUANTUM = 8 * LANES


def _step(args):
    x = args["x"][0]
    target = args["loss_target"][0]
    full = {n: args[n] for n in WEIGHTS if n not in SHARDED}
    for n in SHARDED:
        full[n] = _gather_weight(n, args[n])
    loss_part, grad_x, g = _local_step(x, target, full)

    out_g, out_d, out_m, out_v = {}, {}, {}, {}
    for n in SHARDED:
        parts = _scatter_grad(n, g[n], args[n].shape)
        out_g[n], out_d[n], out_m[n], out_v[n] = _adamw(args[n], parts, args["m_" + n], args["v_" + n], "adamw_" + n)

    small = [n for n in WEIGHTS if n not in SHARDED]
    sizes = [int(math.prod(args[n].shape)) for n in small]
    total = sum(sizes) + 1
    padded = -(-total // SMALL_ROWS_QUANTUM) * SMALL_ROWS_QUANTUM

    def pack(pieces, extra):
        flat = jnp.concatenate([p.reshape(-1).astype(F32) for p in pieces] + [extra.reshape(1)])
        return jnp.pad(flat, (0, padded - total)).reshape(padded // LANES, LANES)

    zero = jnp.zeros((), F32)
    parts = _exchange(pack([g[n] for n in small], loss_part), True, "gather_small_grads")
    sg, sd, sm, sv = _adamw(pack([args[n] for n in small], zero), parts,
                            pack([args["m_" + n] for n in small], zero),
                            pack([args["v_" + n] for n in small], zero), "adamw_small")
    off = 0
    for n, sz in zip(small, sizes):
        cut = lambda a: a.reshape(-1)[off:off + sz].reshape(args[n].shape)
        out_g[n], out_d[n], out_m[n], out_v[n] = cut(sg), cut(sd), cut(sm), cut(sv)
        off += sz
    loss = sg.reshape(-1)[total - 1]
    return (loss, grad_x[None], *[out_g[n] for n in WEIGHTS], *[out_d[n] for n in WEIGHTS],
            *[out_m[n] for n in WEIGHTS], *[out_v[n] for n in WEIGHTS])


def kernel(x, l0_norm_w, l0_w_in, l0_s5_lambda_re, l0_s5_lambda_im, l0_s5_log_step, l0_s5_b_re, l0_s5_b_im, l0_s5_c_re, l0_s5_c_im, l0_s5_d, l0_s5_w_glu, l0_s5_b_glu, l0_ssd_conv_w, l0_ssd_conv_b, l0_ssd_dt_bias, l0_ssd_a_log, l0_ssd_d, l0_ssd_norm_w, l0_w_out, l1_norm_w, l1_w_in, l1_fox_b_f, l1_w_out, final_norm_w, loss_target, m_l0_norm_w, m_l0_w_in, m_l0_s5_lambda_re, m_l0_s5_lambda_im, m_l0_s5_log_step, m_l0_s5_b_re, m_l0_s5_b_im, m_l0_s5_c_re, m_l0_s5_c_im, m_l0_s5_d, m_l0_s5_w_glu, m_l0_s5_b_glu, m_l0_ssd_conv_w, m_l0_ssd_conv_b, m_l0_ssd_dt_bias, m_l0_ssd_a_log, m_l0_ssd_d, m_l0_ssd_norm_w, m_l0_w_out, m_l1_norm_w, m_l1_w_in, m_l1_fox_b_f, m_l1_w_out, m_final_norm_w, v_l0_norm_w, v_l0_w_in, v_l0_s5_lambda_re, v_l0_s5_lambda_im, v_l0_s5_log_step, v_l0_s5_b_re, v_l0_s5_b_im, v_l0_s5_c_re, v_l0_s5_c_im, v_l0_s5_d, v_l0_s5_w_glu, v_l0_s5_b_glu, v_l0_ssd_conv_w, v_l0_ssd_conv_b, v_l0_ssd_dt_bias, v_l0_ssd_a_log, v_l0_ssd_d, v_l0_ssd_norm_w, v_l0_w_out, v_l1_norm_w, v_l1_w_in, v_l1_fox_b_f, v_l1_w_out, v_final_norm_w):
    return _step(dict(locals()))
```

```python
import functools
import math

import jax
import jax.numpy as jnp
from jax import lax
from jax.experimental import pallas as pl
from jax.experimental.pallas import tpu as pltpu

F32 = jnp.float32
BF16 = jnp.bfloat16
MXU_DT = BF16

D_MODEL = 4096
S5_WIDTH = 2048
S5_GROUP = 16
S5_GROUPS = 128
S5_STATE = 64
S5_EIG_CLIP = -1e-4
S5_BLK = 16
SSD_WIDTH = 6144
SSD_HEAD_DIM = 64
SSD_HEADS = 96
SSD_GROUPS = 8
SSD_STATE = 128
SSD_CONV = 4
SSD_CHUNK = 128
SSD_XBC = 8192
SSD_HPG = SSD_HEADS // SSD_GROUPS
FOX_HEAD_DIM = 128
FOX_HEADS = 32
FOX_WIDTH = 4096
NORM_EPS = 1e-5
EVEN_IN = 18528
ODD_IN = 16416
EVEN_PAD = 18560
ODD_PAD = 16512
LANES = 128
N_DEV = 8

ADAM_LR = 0.001
ADAM_B1 = 0.9
ADAM_B2 = 0.999
ADAM_EPS = 1e-08
ADAM_WD = 0.01
ADAM_STEP = 10

VMEM_LIMIT_BYTES = 48 * 1024 * 1024


def _cparams(*sem):
    return pltpu.CompilerParams(dimension_semantics=sem, vmem_limit_bytes=VMEM_LIMIT_BYTES)


def _pick(n, target, quantum=LANES):
    if n <= target:
        return n
    t = (target // quantum) * quantum
    while t >= quantum:
        if n % t == 0:
            return t
        t -= quantum
    raise ValueError((n, target, quantum))


_MM_DIMS = {"nn": ((1,), (0,)), "nt": ((1,), (1,)), "tn": ((0,), (0,))}


MM_VMEM_BUDGET = 38 * 1024 * 1024


def _mm_tk(k, tm, tn, out_bytes, has_res, tk_t):
    fixed = 2 * tm * tn * out_bytes + (2 * tm * tn * 4 if has_res else 0)
    tk = min(k, tk_t)
    while True:
        if k % tk == 0 and (tk == k or tk % LANES == 0):
            need = fixed + 2 * (tm + tn) * tk * 2 + (tm * tn * 4 if tk < k else 0)
            if need <= MM_VMEM_BUDGET or tk <= LANES:
                return tk
        tk -= LANES if tk % LANES == 0 else tk % LANES


def _mm(a, b, mode, out_dtype, name, res=None, tm_t=1024, tn_t=512, tk_t=8192):
    if mode == "nn":
        (m, k), (k2, n) = a.shape, b.shape
    elif mode == "nt":
        (m, k), (n, k2) = a.shape, b.shape
    else:
        (k, m), (k2, n) = a.shape, b.shape
    assert k == k2, (a.shape, b.shape, mode)
    tm, tn = _pick(m, tm_t), _pick(n, tn_t)
    has_res = res is not None
    tk = _mm_tk(k, tm, tn, jnp.dtype(out_dtype).itemsize, has_res, tk_t)
    nk = k // tk
    dims = (_MM_DIMS[mode], ((), ()))

    def body(*refs):
        a_ref, b_ref = refs[:2]
        r_ref = refs[2] if has_res else None
        o_ref = refs[3] if has_res else refs[2]
        part = lax.dot_general(a_ref[...].astype(MXU_DT), b_ref[...].astype(MXU_DT), dims,
                               preferred_element_type=F32)

        def finish(r):
            if has_res:
                r = r + r_ref[...]
            o_ref[...] = r.astype(out_dtype)

        if nk == 1:
            finish(part)
            return
        acc = refs[-1]
        kk = pl.program_id(2)

        @pl.when(kk == 0)
        def _():
            acc[...] = part

        @pl.when(jnp.logical_and(kk > 0, kk < nk - 1))
        def _():
            acc[...] += part

        @pl.when(kk == nk - 1)
        def _():
            finish(acc[...] + part)

    a_spec = (pl.BlockSpec((tk, tm), lambda i, j, kk: (kk, i)) if mode == "tn"
              else pl.BlockSpec((tm, tk), lambda i, j, kk: (i, kk)))
    b_spec = (pl.BlockSpec((tn, tk), lambda i, j, kk: (j, kk)) if mode == "nt"
              else pl.BlockSpec((tk, tn), lambda i, j, kk: (kk, j)))
    o_spec = pl.BlockSpec((tm, tn), lambda i, j, kk: (i, j))
    in_specs = [a_spec, b_spec] + ([o_spec] if has_res else [])
    args = (a, b) + ((res,) if has_res else ())
    return pl.pallas_call(
        body, name=name, grid=(m // tm, n // tn, nk), in_specs=in_specs, out_specs=o_spec,
        out_shape=jax.ShapeDtypeStruct((m, n), out_dtype),
        scratch_shapes=[pltpu.VMEM((tm, tn), F32)] if nk > 1 else [],
        compiler_params=_cparams("parallel", "parallel", "arbitrary"),
    )(*args)


def _tiles(name, fn, tiled, rows, out_tiled, out_acc, tr, tc=None):
    tiled = [t if isinstance(t, tuple) else (t, 0) for t in tiled]
    length = tiled[0][0].shape[0]
    width = out_tiled[0][0] if out_tiled else rows[0].shape[1]
    tc = width if tc is None else tc
    tr = min(tr, length)
    n_in = len(tiled) + len(rows)
    n_ot = len(out_tiled)

    def body(*refs):
        outs = fn(*[r[...] for r in refs[:n_in]])
        outs_t, outs_a = outs[:n_ot], outs[n_ot:]
        for r, v in zip(refs[n_in:n_in + n_ot], outs_t):
            r[...] = v.astype(r.dtype)
        i = pl.program_id(1)
        for r, v in zip(refs[n_in + n_ot:], outs_a):
            @pl.when(i == 0)
            def _(r=r):
                r[...] = jnp.zeros_like(r)

            r[...] += jnp.broadcast_to(v, r.shape)

    def tspec(off):
        return pl.BlockSpec((tr, tc), lambda j, i: (i, j + off))

    in_specs = [tspec(off) for _, off in tiled] + [pl.BlockSpec((1, tc), lambda j, i: (0, j)) for _ in rows]
    out_specs = [tspec(0) for _ in out_tiled] + [pl.BlockSpec((8, tc), lambda j, i: (0, j)) for _ in range(out_acc)]
    out_shape = ([jax.ShapeDtypeStruct((length, w), dt) for w, dt in out_tiled]
                 + [jax.ShapeDtypeStruct((8, width), F32) for _ in range(out_acc)])
    return pl.pallas_call(
        body, name=name, grid=(width // tc, length // tr), in_specs=in_specs, out_specs=out_specs,
        out_shape=out_shape, compiler_params=_cparams("parallel", "arbitrary"),
    )(*[t for t, _ in tiled], *rows)


def _rms(x, w):
    return x * lax.rsqrt(jnp.mean(x * x, axis=-1, keepdims=True) + NORM_EPS) * w


def _colsum(v):
    return jnp.sum(v, axis=0, keepdims=True)


def _rmsnorm_fwd(x, w, name):
    def fn(xb, wb):
        return (_rms(xb, wb),)
    return _tiles(name, fn, [x], [w.reshape(1, -1)], [(x.shape[1], MXU_DT)], 0, tr=256)[0]


def _rmsnorm_bwd(x, w, dh, dres, name):
    def fn(xb, db, rb, wb):
        _, vjp = jax.vjp(_rms, xb, wb)
        dx, dw = vjp(db)
        return dx + rb, dx + rb, dw
    dx, dxb, dw = _tiles(name, fn, [x, dh, dres], [w.reshape(1, -1)],
                         [(x.shape[1], F32), (x.shape[1], MXU_DT)], 1, tr=256)
    return dx, dxb, dw[0]


def _final_loss(x, w, target, name):
    def fn(xb, tb, wb):
        def f(xv, wv):
            e = _rms(xv, wv) - tb
            return 0.5 * jnp.sum(jnp.mean(e * e, axis=-1, keepdims=True), axis=0, keepdims=True)
        lv, vjp = jax.vjp(f, xb, wb)
        dx, dw = vjp(jnp.ones_like(lv))
        return dx, dx, dw, jnp.broadcast_to(lv, (1, xb.shape[1]))
    dx, dxb, dw, lv = _tiles(name, fn, [x, target], [w.reshape(1, -1)],
                             [(x.shape[1], F32), (x.shape[1], MXU_DT)], 2, tr=256)
    return lv[0, 0], dx, dxb, dw[0]


def _dg(a, b, mode):
    return lax.dot_general(a.astype(MXU_DT), b.astype(MXU_DT), (_MM_DIMS[mode], ((), ())),
                           preferred_element_type=F32)


@jax.custom_vjp
def _dot_nn(a, b):
    return _dg(a, b, "nn")


@jax.custom_vjp
def _dot_nt(a, b):
    return _dg(a, b, "nt")


@jax.custom_vjp
def _dot_tn(a, b):
    return _dg(a, b, "tn")


_dot_nn.defvjp(lambda a, b: (_dg(a, b, "nn"), (a, b)),
               lambda r, g: (_dg(g, r[1], "nt"), _dg(r[0], g, "tn")))
_dot_nt.defvjp(lambda a, b: (_dg(a, b, "nt"), (a, b)),
               lambda r, g: (_dg(g, r[1], "nn"), _dg(g, r[0], "tn")))
_dot_tn.defvjp(lambda a, b: (_dg(a, b, "tn"), (a, b)),
               lambda r, g: (_dg(r[1], g, "nt"), _dg(r[0], g, "nn")))


def _dot_exact(a, b):
    return jnp.dot(a, b, precision=lax.Precision.HIGHEST, preferred_element_type=F32)


S5_NS = S5_GROUPS * S5_STATE
S5_BS = S5_NS // S5_BLK
S5_BC = S5_WIDTH // S5_BLK


def _s5_disc(lr_raw, li, ls, br, bi):
    lr = jnp.minimum(lr_raw, S5_EIG_CLIP)
    step = jnp.exp(ls)
    mag = jnp.exp(lr * step)
    ab_re = mag * jnp.cos(li * step)
    ab_im = mag * jnp.sin(li * step)
    denom = lr * lr + li * li
    nr = ab_re - 1.0
    ni = ab_im
    coef_re = (nr * lr + ni * li) / denom
    coef_im = (ni * lr - nr * li) / denom
    return ab_re, ab_im, coef_re * br - coef_im * bi, coef_re * bi + coef_im * br


def _s5_prep(lr_raw, li, ls, br, bi):
    shapes = [jax.ShapeDtypeStruct((1, S5_NS), F32)] * 2 + [jax.ShapeDtypeStruct((S5_GROUP, S5_NS), F32)] * 2

    def body(a, b, c, d, e, o1, o2, o3, o4):
        for r, v in zip((o1, o2, o3, o4), _s5_disc(a[...], b[...], c[...], d[...], e[...])):
            r[...] = v

    return pl.pallas_call(body, name="s5_prep", out_shape=shapes)(lr_raw, li, ls, br, bi)


def _s5_prep_bwd(lr_raw, li, ls, br, bi, d_are, d_aim, d_bbre, d_bbim):
    shapes = [jax.ShapeDtypeStruct((1, S5_NS), F32)] * 3 + [jax.ShapeDtypeStruct((S5_GROUP, S5_NS), F32)] * 2

    def body(a, b, c, d, e, g1, g2, g3, g4, o1, o2, o3, o4, o5):
        _, vjp = jax.vjp(_s5_disc, a[...], b[...], c[...], d[...], e[...])
        for r, v in zip((o1, o2, o3, o4, o5), vjp((g1[...], g2[...], g3[...], g4[...]))):
            r[...] = v

    return pl.pallas_call(body, name="s5_prep_bwd", out_shape=shapes)(
        lr_raw, li, ls, br, bi, d_are, d_aim, d_bbre, d_bbim)


def _cmul(ar, ai, br, bi):
    return ar * br - ai * bi, ar * bi + ai * br


def _s5_powers(ar, ai, n):
    pr, pi_ = [ar], [ai]
    for _ in range(n - 1):
        r, i = _cmul(pr[-1], pi_[-1], pr[-1], pi_[-1])
        pr.append(r)
        pi_.append(i)
    return pr, pi_


def _s5_scan_tile(xr, xi, pr, pi_, reverse):
    t = xr.shape[0]
    row = lax.broadcasted_iota(jnp.int32, xr.shape, 0)
    sr, si = xr, xi
    for k in range(len(pr)):
        d = 1 << k
        shift = (t - d) if reverse else d
        keep = (row < t - d) if reverse else (row >= d)
        qr = jnp.where(keep, pltpu.roll(sr, shift, 0), 0.0)
        qi = jnp.where(keep, pltpu.roll(si, shift, 0), 0.0)
        mr, mi = _cmul(pr[k], pi_[k], qr, qi)
        sr, si = sr + mr, si + mi
    return sr, si


def _s5_setup(ar, ai, t, reverse, apr, api, tabr, tabi):
    n = int(math.log2(t))
    pr, pi_ = _s5_powers(ar, ai, n)
    for k in range(n):
        apr[k:k + 1, :] = pr[k]
        api[k:k + 1, :] = pi_[k]
    row = lax.broadcasted_iota(jnp.int32, (t, ar.shape[1]), 0)
    first = (row == t - 1) if reverse else (row == 0)
    xr = jnp.where(first, ar, 0.0)
    xi = jnp.where(first, ai, 0.0)
    sr, si = _s5_scan_tile(xr, xi, pr, pi_, reverse)
    tabr[...] = sr
    tabi[...] = si


def _s5_fwd(p_s5, bb_re, bb_im, ct_re, ct_im, a_re, a_im, d_row, t_tile=256):
    length = p_s5.shape[0]
    t = min(t_tile, length)
    nt = length // t
    n = int(math.log2(t))

    def body(u_ref, bbr, bbi, ctr, cti, ar_ref, ai_ref, d_ref, y_ref, sr_ref, si_ref,
             apr, api, tabr, tabi, cr, ci):
        i = pl.program_id(1)

        @pl.when(i == 0)
        def _():
            _s5_setup(ar_ref[...], ai_ref[...], t, False, apr, api, tabr, tabi)
            cr[...] = jnp.zeros_like(cr)
            ci[...] = jnp.zeros_like(ci)

        u = u_ref[...]
        pr = [apr[k:k + 1, :] for k in range(n)]
        pi_ = [api[k:k + 1, :] for k in range(n)]
        sr, si = _s5_scan_tile(_dg(u, bbr[...], "nn"), _dg(u, bbi[...], "nn"), pr, pi_, False)
        mr, mi = _cmul(tabr[...], tabi[...], cr[0:1, :], ci[0:1, :])
        sr, si = sr + mr, si + mi
        cr[0:1, :] = sr[t - 1:t, :]
        ci[0:1, :] = si[t - 1:t, :]
        sr_ref[...] = sr
        si_ref[...] = si
        y_ref[...] = _dg(sr, ctr[...], "nn") - _dg(si, cti[...], "nn") + d_ref[...] * u

    blk3 = lambda a, b: pl.BlockSpec((None, a, b), lambda j, i: (j, 0, 0))
    return pl.pallas_call(
        body, name="s5_fwd", grid=(S5_BLK, nt),
        in_specs=[pl.BlockSpec((t, S5_BC), lambda j, i: (i, j)),
                  blk3(S5_BC, S5_BS), blk3(S5_BC, S5_BS), blk3(S5_BS, S5_BC), blk3(S5_BS, S5_BC),
                  blk3(1, S5_BS), blk3(1, S5_BS), blk3(1, S5_BC)],
        out_specs=[pl.BlockSpec((t, S5_BC), lambda j, i: (i, j)),
                   pl.BlockSpec((t, S5_BS), lambda j, i: (i, j)),
                   pl.BlockSpec((t, S5_BS), lambda j, i: (i, j))],
        out_shape=[jax.ShapeDtypeStruct((length, S5_WIDTH), F32),
                   jax.ShapeDtypeStruct((length, S5_NS), F32),
                   jax.ShapeDtypeStruct((length, S5_NS), F32)],
        scratch_shapes=[pltpu.VMEM((8, S5_BS), F32), pltpu.VMEM((8, S5_BS), F32),
                        pltpu.VMEM((t, S5_BS), F32), pltpu.VMEM((t, S5_BS), F32),
                        pltpu.VMEM((8, S5_BS), F32), pltpu.VMEM((8, S5_BS), F32)],
        compiler_params=_cparams("parallel", "arbitrary"),
    )(p_s5, bb_re, bb_im, ct_re, ct_im, a_re, a_im, d_row)


def _s5_bwd(dy, p_s5, s_re, s_im, bb_re, bb_im, ct_re, ct_im, a_re, a_im, d_row, t_tile=256):
    length = p_s5.shape[0]
    t = min(t_tile, length)
    nt = length // t
    n = int(math.log2(t))

    def body(dy_ref, u_ref, sr_ref, si_ref, pr_ref, pi_ref, bbr, bbi, ctr, cti, ar_ref, ai_ref, d_ref,
             du_ref, dbbr, dbbi, dctr, dcti, dar, dai, dd_ref, apr, api, tabr, tabi, cr, ci):
        i = pl.program_id(1)

        @pl.when(i == 0)
        def _():
            _s5_setup(ar_ref[...], -ai_ref[...], t, True, apr, api, tabr, tabi)
            for r in (cr, ci, dbbr, dbbi, dctr, dcti, dar, dai, dd_ref):
                r[...] = jnp.zeros_like(r)

        dyv = dy_ref[...]
        u = u_ref[...]
        pr = [apr[k:k + 1, :] for k in range(n)]
        pi_ = [api[k:k + 1, :] for k in range(n)]
        gr = _dg(dyv, ctr[...], "nt")
        gi = -_dg(dyv, cti[...], "nt")
        lr, li = _s5_scan_tile(gr, gi, pr, pi_, True)
        mr, mi = _cmul(tabr[...], tabi[...], cr[0:1, :], ci[0:1, :])
        lr, li = lr + mr, li + mi
        cr[0:1, :] = lr[0:1, :]
        ci[0:1, :] = li[0:1, :]
        du_ref[...] = (_dg(lr, bbr[...], "nt") + _dg(li, bbi[...], "nt") + d_ref[...] * dyv).astype(du_ref.dtype)
        dbbr[...] += _dg(u, lr, "tn")
        dbbi[...] += _dg(u, li, "tn")
        sr = sr_ref[...]
        si = si_ref[...]
        dctr[...] += _dg(sr, dyv, "tn")
        dcti[...] -= _dg(si, dyv, "tn")
        dd_ref[...] += jnp.broadcast_to(_colsum(dyv * u), dd_ref.shape)
        row = lax.broadcasted_iota(jnp.int32, sr.shape, 0)
        has_prev = (i < nt - 1).astype(F32)
        ssr = jnp.where(row == 0, pr_ref[7:8, :] * has_prev, pltpu.roll(sr, 1, 0))
        ssi = jnp.where(row == 0, pi_ref[7:8, :] * has_prev, pltpu.roll(si, 1, 0))
        dar[...] += jnp.broadcast_to(_colsum(lr * ssr + li * ssi), dar.shape)
        dai[...] += jnp.broadcast_to(_colsum(li * ssr - lr * ssi), dai.shape)

    rev = lambda j, i: (nt - 1 - i, j)
    prev = lambda j, i: (jnp.maximum((nt - 1 - i) * (t // 8) - 1, 0), j)
    blk3 = lambda a, b: pl.BlockSpec((None, a, b), lambda j, i: (j, 0, 0))
    return pl.pallas_call(
        body, name="s5_bwd", grid=(S5_BLK, nt),
        in_specs=[pl.BlockSpec((t, S5_BC), rev), pl.BlockSpec((t, S5_BC), rev),
                  pl.BlockSpec((t, S5_BS), rev), pl.BlockSpec((t, S5_BS), rev),
                  pl.BlockSpec((8, S5_BS), prev), pl.BlockSpec((8, S5_BS), prev),
                  blk3(S5_BC, S5_BS), blk3(S5_BC, S5_BS), blk3(S5_BS, S5_BC), blk3(S5_BS, S5_BC),
                  blk3(1, S5_BS), blk3(1, S5_BS), blk3(1, S5_BC)],
        out_specs=[pl.BlockSpec((t, S5_BC), rev),
                   blk3(S5_BC, S5_BS), blk3(S5_BC, S5_BS), blk3(S5_BS, S5_BC), blk3(S5_BS, S5_BC),
                   blk3(8, S5_BS), blk3(8, S5_BS), blk3(8, S5_BC)],
        out_shape=[jax.ShapeDtypeStruct((length, S5_WIDTH), MXU_DT),
                   jax.ShapeDtypeStruct((S5_BLK, S5_BC, S5_BS), F32), jax.ShapeDtypeStruct((S5_BLK, S5_BC, S5_BS), F32),
                   jax.ShapeDtypeStruct((S5_BLK, S5_BS, S5_BC), F32), jax.ShapeDtypeStruct((S5_BLK, S5_BS, S5_BC), F32),
                   jax.ShapeDtypeStruct((S5_BLK, 8, S5_BS), F32), jax.ShapeDtypeStruct((S5_BLK, 8, S5_BS), F32),
                   jax.ShapeDtypeStruct((S5_BLK, 8, S5_BC), F32)],
        scratch_shapes=[pltpu.VMEM((8, S5_BS), F32), pltpu.VMEM((8, S5_BS), F32),
                        pltpu.VMEM((t, S5_BS), F32), pltpu.VMEM((t, S5_BS), F32),
                        pltpu.VMEM((8, S5_BS), F32), pltpu.VMEM((8, S5_BS), F32)],
        compiler_params=_cparams("parallel", "arbitrary"),
    )(dy, p_s5, s_re, s_im, s_re, s_im, bb_re, bb_im, ct_re, ct_im, a_re, a_im, d_row)


def _blockdiag(m, rows, cols):
    m = m.reshape(S5_BLK, 8, rows, 1, cols)
    on_diag = jnp.eye(8, dtype=bool)[None, :, None, :, None]
    return jnp.where(on_diag, m, 0).reshape(S5_BLK, 8 * rows, 8 * cols)


def _blockdiag_t(m, rows, cols):
    m = m.reshape(S5_BLK, 8, rows, 8, cols)
    on_diag = jnp.eye(8, dtype=bool)[None, :, None, :, None]
    return jnp.sum(jnp.where(on_diag, m, 0), axis=3).reshape(S5_GROUPS, rows, cols)


def _gelu(y):
    return jax.nn.gelu(y)


def _s5_out_fn(y, zg, gate, b):
    return _gelu(y) * jax.nn.sigmoid(zg + b) * jax.nn.silu(gate)


HALO = 8


def _conv_fwd(xbc, w, b, tr=256, tc=1024):
    length, width = xbc.shape
    tr = min(tr, length)

    def body(x_ref, h_ref, w_ref, b_ref, cv_ref, act_ref):
        i = pl.program_id(1)
        x = x_ref[...]
        xx = jnp.concatenate([h_ref[...] * (i > 0).astype(F32), x], axis=0)
        acc = b_ref[...] + w_ref[3:4, :] * x
        for k in range(SSD_CONV - 1):
            acc = acc + w_ref[k:k + 1, :] * pltpu.roll(xx, SSD_CONV - 1 - k, 0)[HALO:, :]
        cv_ref[...] = acc
        act_ref[...] = jax.nn.silu(acc)

    main = pl.BlockSpec((tr, tc), lambda j, i: (i, j))
    before = pl.BlockSpec((HALO, tc), lambda j, i: (jnp.maximum(i * (tr // HALO) - 1, 0), j))
    return pl.pallas_call(
        body, name="ssd_conv_fwd", grid=(width // tc, length // tr),
        in_specs=[main, before, pl.BlockSpec((SSD_CONV, tc), lambda j, i: (0, j)),
                  pl.BlockSpec((1, tc), lambda j, i: (0, j))],
        out_specs=[main, main],
        out_shape=[jax.ShapeDtypeStruct((length, width), F32)] * 2,
        compiler_params=_cparams("parallel", "arbitrary"),
    )(xbc, xbc, w, b.reshape(1, -1))


def _conv_bwd(dact, cv, xbc, w, tr=256, tc=1024):
    length, width = xbc.shape
    tr = min(tr, length)
    nr = length // tr
    n = tr + HALO

    def dsilu(d, c):
        sg = jax.nn.sigmoid(c)
        return d * (sg * (1.0 + c * (1.0 - sg)))

    def body(da_ref, dan_ref, cv_ref, cvn_ref, x_ref, xp_ref, w_ref, dx_ref, dw_ref, db_ref):
        i = pl.program_id(1)

        @pl.when(i == 0)
        def _():
            dw_ref[...] = jnp.zeros_like(dw_ref)
            db_ref[...] = jnp.zeros_like(db_ref)

        dc = dsilu(da_ref[...], cv_ref[...])
        dcn = dsilu(dan_ref[...], cvn_ref[...]) * (i < nr - 1).astype(F32)
        dd = jnp.concatenate([dc, dcn], axis=0)
        x = x_ref[...]
        xx = jnp.concatenate([xp_ref[...] * (i > 0).astype(F32), x], axis=0)
        dx = w_ref[3:4, :] * dc
        dw_ref[3:4, :] += _colsum(dc * x)
        for k in range(SSD_CONV - 1):
            j = SSD_CONV - 1 - k
            dx = dx + w_ref[k:k + 1, :] * pltpu.roll(dd, n - j, 0)[:tr, :]
            dw_ref[k:k + 1, :] += _colsum(dc * pltpu.roll(xx, j, 0)[HALO:, :])
        dx_ref[...] = dx.astype(dx_ref.dtype)
        db_ref[...] += jnp.broadcast_to(_colsum(dc), db_ref.shape)

    main = pl.BlockSpec((tr, tc), lambda j, i: (i, j))
    before = pl.BlockSpec((HALO, tc), lambda j, i: (jnp.maximum(i * (tr // HALO) - 1, 0), j))
    after = pl.BlockSpec((HALO, tc), lambda j, i: (jnp.minimum((i + 1) * (tr // HALO), length // HALO - 1), j))
    acc = pl.BlockSpec((8, tc), lambda j, i: (0, j))
    return pl.pallas_call(
        body, name="ssd_conv_bwd", grid=(width // tc, nr),
        in_specs=[main, after, main, after, main, before, pl.BlockSpec((SSD_CONV, tc), lambda j, i: (0, j))],
        out_specs=[main, acc, acc],
        out_shape=[jax.ShapeDtypeStruct((length, width), MXU_DT),
                   jax.ShapeDtypeStruct((8, width), F32), jax.ShapeDtypeStruct((8, width), F32)],
        compiler_params=_cparams("parallel", "arbitrary"),
    )(dact, dact, cv, cv, xbc, xbc, w)


def _tri(lower):
    r = lax.broadcasted_iota(jnp.int32, (SSD_CHUNK, SSD_CHUNK), 0)
    c = lax.broadcasted_iota(jnp.int32, (SSD_CHUNK, SSD_CHUNK), 1)
    return ((r >= c) if lower else (r <= c)).astype(F32)


def _dt_fwd(raw, bias, a_log):
    length = raw.shape[0]
    nc = length // SSD_CHUNK

    def body(r_ref, b_ref, a_ref, dt_ref, cum_ref, cumt_ref):
        dt = jax.nn.softplus(r_ref[...] + b_ref[...])
        cum = _dot_exact(_tri(True), dt * (-jnp.exp(a_ref[...])))
        dt_ref[...] = dt
        cum_ref[...] = cum
        cumt_ref[...] = cum.T

    blk = pl.BlockSpec((SSD_CHUNK, LANES), lambda c: (c, 0))
    row = pl.BlockSpec((1, LANES), lambda c: (0, 0))
    return pl.pallas_call(
        body, name="ssd_dt_fwd", grid=(nc,), in_specs=[blk, row, row],
        out_specs=[blk, blk, pl.BlockSpec((None, LANES, SSD_CHUNK), lambda c: (c, 0, 0))],
        out_shape=[jax.ShapeDtypeStruct((length, LANES), F32)] * 2
        + [jax.ShapeDtypeStruct((nc, LANES, SSD_CHUNK), F32)],
        compiler_params=_cparams("parallel"),
    )(raw, bias, a_log)


def _dt_bwd(raw, bias, a_log, ddt, dcum):
    length = raw.shape[0]
    nc = length // SSD_CHUNK

    def body(r_ref, b_ref, a_ref, ddt_ref, dcum_ref, dr_ref, db_ref, da_ref):
        @pl.when(pl.program_id(0) == 0)
        def _():
            db_ref[...] = jnp.zeros_like(db_ref)
            da_ref[...] = jnp.zeros_like(da_ref)

        z = r_ref[...] + b_ref[...]
        a = -jnp.exp(a_ref[...])
        dla = _dot_exact(_tri(False), dcum_ref[...])
        draw = (ddt_ref[...] + dla * a) * jax.nn.sigmoid(z)
        dr_ref[...] = draw.astype(dr_ref.dtype)
        db_ref[...] += jnp.broadcast_to(_colsum(draw), db_ref.shape)
        da_ref[...] += jnp.broadcast_to(_colsum(dla * jax.nn.softplus(z)) * a, da_ref.shape)

    blk = pl.BlockSpec((SSD_CHUNK, LANES), lambda c: (c, 0))
    row = pl.BlockSpec((1, LANES), lambda c: (0, 0))
    acc = pl.BlockSpec((8, LANES), lambda c: (0, 0))
    return pl.pallas_call(
        body, name="ssd_dt_bwd", grid=(nc,), in_specs=[blk, row, row, blk, blk],
        out_specs=[blk, acc, acc],
        out_shape=[jax.ShapeDtypeStruct((length, LANES), MXU_DT),
                   jax.ShapeDtypeStruct((8, LANES), F32), jax.ShapeDtypeStruct((8, LANES), F32)],
        compiler_params=_cparams("arbitrary"),
    )(raw, bias, a_log, ddt, dcum)


def _ssd_head(x, dtc, cumc, cumr, cl, g, bm, cm, sp):
    xdt = x * dtc
    q = lax.broadcasted_iota(jnp.int32, g.shape, 0)
    k = lax.broadcasted_iota(jnp.int32, g.shape, 1)
    w = g * jnp.exp(jnp.where(q >= k, cumc - cumr, -1e30))
    y = _dot_nn(w, xdt) + _dot_nt(cm, sp) * jnp.exp(cumc)
    s_new = jnp.exp(cl) * sp + _dot_tn(xdt * jnp.exp(cl - cumc), bm)
    return y, s_new


SSD_GW = SSD_HPG * SSD_HEAD_DIM
SSD_B_OFF = SSD_WIDTH // SSD_STATE
SSD_C_OFF = SSD_B_OFF + SSD_GROUPS


def _ssd_core_specs(nc, rev):
    ch = (lambda c: nc - 1 - c) if rev else (lambda c: c)
    xs = pl.BlockSpec((SSD_CHUNK, SSD_GW), lambda g, c: (ch(c), g))
    bspec = pl.BlockSpec((SSD_CHUNK, SSD_STATE), lambda g, c: (ch(c), SSD_B_OFF + g))
    cspec = pl.BlockSpec((SSD_CHUNK, SSD_STATE), lambda g, c: (ch(c), SSD_C_OFF + g))
    lane = pl.BlockSpec((None, SSD_CHUNK, LANES), lambda g, c: (g, ch(c), 0))
    rows = pl.BlockSpec((None, None, 16, SSD_CHUNK), lambda g, c: (g, ch(c), 0, 0))
    st = pl.BlockSpec((None, None, SSD_HPG, SSD_HEAD_DIM, SSD_STATE), lambda g, c: (g, ch(c), 0, 0, 0))
    return xs, bspec, cspec, lane, rows, st


def _ssd_core_fwd(act, dtg, cumg, cumtg):
    length = act.shape[0]
    nc = length // SSD_CHUNK

    def body(x_ref, b_ref, c_ref, dt_ref, cum_ref, cumt_ref, y_ref, st_ref, s_scr):
        @pl.when(pl.program_id(1) == 0)
        def _():
            s_scr[...] = jnp.zeros_like(s_scr)

        bm = b_ref[...]
        cm = c_ref[...]
        g = _dot_nt(cm, bm)
        for r in range(SSD_HPG):
            cols = slice(r * SSD_HEAD_DIM, (r + 1) * SSD_HEAD_DIM)
            sp = s_scr[r]
            st_ref[r] = sp
            y, s_new = _ssd_head(x_ref[:, cols], dt_ref[:, r:r + 1], cum_ref[:, r:r + 1], cumt_ref[r:r + 1, :],
                                 cum_ref[SSD_CHUNK - 1:SSD_CHUNK, r:r + 1], g, bm, cm, sp)
            y_ref[:, cols] = y
            s_scr[r] = s_new

    xs, bspec, cspec, lane, rows, st = _ssd_core_specs(nc, False)
    return pl.pallas_call(
        body, name="ssd_core_fwd", grid=(SSD_GROUPS, nc),
        in_specs=[xs, bspec, cspec, lane, lane, rows], out_specs=[xs, st],
        out_shape=[jax.ShapeDtypeStruct((length, SSD_WIDTH), F32),
                   jax.ShapeDtypeStruct((SSD_GROUPS, nc, SSD_HPG, SSD_HEAD_DIM, SSD_STATE), F32)],
        scratch_shapes=[pltpu.VMEM((SSD_HPG, SSD_HEAD_DIM, SSD_STATE), F32)],
        compiler_params=_cparams("parallel", "arbitrary"),
    )(act, act, act, dtg, cumg, cumtg)


def _ssd_core_bwd(dy, dxs_add, act, dtg, cumg, cumtg, states):
    length = act.shape[0]
    nc = length // SSD_CHUNK

    def body(dy_ref, add_ref, x_ref, b_ref, c_ref, dt_ref, cum_ref, cumt_ref, st_ref,
             dx_ref, db_ref, dc_ref, ddt_ref, dcum_ref, dcumt_ref, dcl_ref, ds_scr):
        @pl.when(pl.program_id(1) == 0)
        def _():
            ds_scr[...] = jnp.zeros_like(ds_scr)

        for r_ in (ddt_ref, dcum_ref, dcumt_ref, dcl_ref):
            r_[...] = jnp.zeros_like(r_)
        bm = b_ref[...]
        cm = c_ref[...]
        g, g_vjp = jax.vjp(_dot_nt, cm, bm)
        dg = jnp.zeros_like(g)
        dbm = jnp.zeros_like(bm)
        dcm = jnp.zeros_like(cm)
        for r in range(SSD_HPG):
            cols = slice(r * SSD_HEAD_DIM, (r + 1) * SSD_HEAD_DIM)
            _, vjp = jax.vjp(_ssd_head, x_ref[:, cols], dt_ref[:, r:r + 1], cum_ref[:, r:r + 1],
                             cumt_ref[r:r + 1, :], cum_ref[SSD_CHUNK - 1:SSD_CHUNK, r:r + 1], g, bm, cm, st_ref[r])
            dx, ddtc, dcumc, dcumr, dcl, dg_r, dbm_r, dcm_r, dsp = vjp((dy_ref[:, cols], ds_scr[r]))
            dx_ref[:, cols] = dx + add_ref[:, cols]
            ddt_ref[:, r:r + 1] = ddtc
            dcum_ref[:, r:r + 1] = dcumc
            dcumt_ref[r:r + 1, :] = dcumr
            dcl_ref[0:1, r:r + 1] = dcl
            ds_scr[r] = dsp
            dg, dbm, dcm = dg + dg_r, dbm + dbm_r, dcm + dcm_r
        dcm_g, dbm_g = g_vjp(dg)
        db_ref[...] = dbm + dbm_g
        dc_ref[...] = dcm + dcm_g

    xs, bspec, cspec, lane, rows, st = _ssd_core_specs(nc, True)
    bc_out = pl.BlockSpec((SSD_CHUNK, SSD_STATE), lambda g, c: (nc - 1 - c, g))
    last = pl.BlockSpec((None, None, 8, LANES), lambda g, c: (g, nc - 1 - c, 0, 0))
    return pl.pallas_call(
        body, name="ssd_core_bwd", grid=(SSD_GROUPS, nc),
        in_specs=[xs, xs, xs, bspec, cspec, lane, lane, rows, st],
        out_specs=[xs, bc_out, bc_out, lane, lane, rows, last],
        out_shape=[jax.ShapeDtypeStruct((length, SSD_WIDTH), F32),
                   jax.ShapeDtypeStruct((length, SSD_GROUPS * SSD_STATE), F32),
                   jax.ShapeDtypeStruct((length, SSD_GROUPS * SSD_STATE), F32),
                   jax.ShapeDtypeStruct((SSD_GROUPS, length, LANES), F32),
                   jax.ShapeDtypeStruct((SSD_GROUPS, length, LANES), F32),
                   jax.ShapeDtypeStruct((SSD_GROUPS, nc, 16, SSD_CHUNK), F32),
                   jax.ShapeDtypeStruct((SSD_GROUPS, nc, 8, LANES), F32)],
        scratch_shapes=[pltpu.VMEM((SSD_HPG, SSD_HEAD_DIM, SSD_STATE), F32)],
        compiler_params=_cparams("parallel", "arbitrary"),
    )(dy, dxs_add, act, act, act, dtg, cumg, cumtg, states)


def _ssd_post_fn(yc, xs, z, dch, nw):
    y = (yc + dch * xs) * jax.nn.silu(z)
    return y * lax.rsqrt(jnp.mean(y * y, axis=-1, keepdims=True) + NORM_EPS) * nw


FOX_SCALE = 1.0 / math.sqrt(FOX_HEAD_DIM)
MASKED = -1e30


def _fgate_fwd(f_raw, b_f):
    length = f_raw.shape[0]
    nb = length // SSD_CHUNK

    def body(f_ref, b_ref, c_ref, carry):
        @pl.when(pl.program_id(0) == 0)
        def _():
            carry[...] = jnp.zeros_like(carry)

        c = _dot_exact(_tri(True), jax.nn.log_sigmoid(f_ref[...] + b_ref[...])) + carry[0:1, :]
        c_ref[...] = c
        carry[0:1, :] = c[SSD_CHUNK - 1:SSD_CHUNK, :]

    blk = pl.BlockSpec((SSD_CHUNK, LANES), lambda i: (i, 0))
    return pl.pallas_call(
        body, name="fox_fgate_fwd", grid=(nb,), in_specs=[blk, pl.BlockSpec((1, LANES), lambda i: (0, 0))],
        out_specs=blk, out_shape=jax.ShapeDtypeStruct((length, LANES), F32),
        scratch_shapes=[pltpu.VMEM((8, LANES), F32)], compiler_params=_cparams("arbitrary"),
    )(f_raw, b_f)


def _fgate_bwd(f_raw, b_f, dc):
    length = f_raw.shape[0]
    nb = length // SSD_CHUNK

    def body(f_ref, b_ref, dc_ref, df_ref, db_ref, carry):
        @pl.when(pl.program_id(0) == 0)
        def _():
            carry[...] = jnp.zeros_like(carry)
            db_ref[...] = jnp.zeros_like(db_ref)

        dcv = dc_ref[...]
        dlog = _dot_exact(_tri(False), dcv) + carry[0:1, :]
        carry[0:1, :] += _colsum(dcv)
        df = dlog * jax.nn.sigmoid(-(f_ref[...] + b_ref[...]))
        df_ref[...] = df.astype(df_ref.dtype)
        db_ref[...] += jnp.broadcast_to(_colsum(df), db_ref.shape)

    blk = pl.BlockSpec((SSD_CHUNK, LANES), lambda i: (nb - 1 - i, 0))
    return pl.pallas_call(
        body, name="fox_fgate_bwd", grid=(nb,),
        in_specs=[blk, pl.BlockSpec((1, LANES), lambda i: (0, 0)), blk],
        out_specs=[blk, pl.BlockSpec((8, LANES), lambda i: (0, 0))],
        out_shape=[jax.ShapeDtypeStruct((length, LANES), MXU_DT), jax.ShapeDtypeStruct((8, LANES), F32)],
        scratch_shapes=[pltpu.VMEM((8, LANES), F32)], compiler_params=_cparams("arbitrary"),
    )(f_raw, b_f, dc)


def _fox_scores(q, k, bias, diagonal):
    s = _dg(q, k, "nt") * FOX_SCALE + bias
    if diagonal:
        row = lax.broadcasted_iota(jnp.int32, s.shape, 0)
        col = lax.broadcasted_iota(jnp.int32, s.shape, 1)
        s = jnp.where(col <= row, s, MASKED)
    return s


FOX_TILE = 512


def _fox_c0(c, t):
    return jnp.repeat(jnp.repeat(c[::t, :FOX_HEADS], FOX_HEAD_DIM, axis=1), 8, axis=0)


def _fox_fwd(qkv, c0_rep, c_t, tile=FOX_TILE):
    length = qkv.shape[0]
    t = min(tile, length)
    nq = length // t

    def body(q_ref, k_ref, v_ref, c0_ref, ct_ref, o_ref, lse_ref):
        i = pl.program_id(1)
        q = q_ref[...]
        c0 = c0_ref[0:1, 0:1]

        def tile_step(k0, carry, diagonal):
            m, l, acc = carry
            s = _fox_scores(q, k_ref[pl.ds(k0, t), :], c0 - ct_ref[:, pl.ds(k0, t)], diagonal)
            m_new = jnp.maximum(m, jnp.max(s, axis=1, keepdims=True))
            p = jnp.exp(s - m_new)
            alpha = jnp.exp(m - m_new)
            return (m_new, alpha * l + jnp.sum(p, axis=1, keepdims=True),
                    alpha * acc + _dg(p, v_ref[pl.ds(k0, t), :], "nn"))

        init = (jnp.full((t, 1), MASKED, F32), jnp.zeros((t, 1), F32), jnp.zeros((t, FOX_HEAD_DIM), F32))
        carry = lax.fori_loop(0, i, lambda j, c: tile_step(pl.multiple_of(j * t, t), c, False), init)
        m, l, acc = tile_step(pl.multiple_of(i * t, t), carry, True)
        o_ref[...] = acc / l
        lse_ref[...] = jnp.broadcast_to(m + jnp.log(l), lse_ref.shape)

    qt = pl.BlockSpec((t, FOX_HEAD_DIM), lambda h, i: (i, h))
    return pl.pallas_call(
        body, name="fox_attn_fwd", grid=(FOX_HEADS, nq),
        in_specs=[qt,
                  pl.BlockSpec((length, FOX_HEAD_DIM), lambda h, i: (0, FOX_HEADS + h)),
                  pl.BlockSpec((length, FOX_HEAD_DIM), lambda h, i: (0, 2 * FOX_HEADS + h)),
                  pl.BlockSpec((8, FOX_HEAD_DIM), lambda h, i: (i, h)),
                  pl.BlockSpec((None, 1, length), lambda h, i: (h, 0, 0))],
        out_specs=[qt, qt],
        out_shape=[jax.ShapeDtypeStruct((length, FOX_WIDTH), F32)] * 2,
        compiler_params=_cparams("parallel", "arbitrary"),
    )(qkv, qkv, qkv, c0_rep, c_t)


def _fox_bwd(qkv, d_att, lse, delta, c0_rep, c_t, tile=FOX_TILE):
    length = qkv.shape[0]
    t = min(tile, length)
    nk = length // t

    def body(q_ref, k_ref, v_ref, do_ref, lse_ref, dl_ref, c0_ref, ct_ref,
             dq_ref, dk_ref, dv_ref, dcq_ref, dck_ref, dq_acc):
        j = pl.program_id(1)

        @pl.when(j == 0)
        def _():
            dq_acc[...] = jnp.zeros_like(dq_acc)
            dcq_ref[...] = jnp.zeros_like(dcq_ref)

        k = k_ref[...]
        v = v_ref[...]
        ck = ct_ref[...]

        def tile_step(i, carry, diagonal):
            dk, dv, dck = carry
            rows = pl.ds(pl.multiple_of(i * t, t), t)
            q = q_ref[rows, :]
            do = do_ref[rows, :]
            c0 = c0_ref[pl.ds(pl.multiple_of(i * 8, 8), 8), :][0:1, 0:1]
            s = _fox_scores(q, k, c0 - ck, diagonal)
            p = jnp.exp(s - lse_ref[rows, 0:1])
            dv = dv + _dg(p, do, "tn")
            ds = p * (_dg(do, v, "nt") - dl_ref[rows, 0:1])
            dk = dk + _dg(ds, q, "tn") * FOX_SCALE
            dq_acc[rows, :] += _dg(ds, k, "nn") * FOX_SCALE
            dcq_ref[rows, :] += jnp.broadcast_to(jnp.sum(ds, axis=1, keepdims=True), (t, FOX_HEAD_DIM))
            return dk, dv, dck + _colsum(ds)

        init = (jnp.zeros((t, FOX_HEAD_DIM), F32), jnp.zeros((t, FOX_HEAD_DIM), F32), jnp.zeros((1, t), F32))
        carry = tile_step(j, init, True)
        dk, dv, dck = lax.fori_loop(j + 1, nk, lambda i, c: tile_step(i, c, False), carry)
        dk_ref[...] = dk.astype(dk_ref.dtype)
        dv_ref[...] = dv.astype(dv_ref.dtype)
        dck_ref[...] = -dck

        @pl.when(j == nk - 1)
        def _():
            dq_ref[...] = dq_acc[...].astype(dq_ref.dtype)

    full = lambda off: pl.BlockSpec((length, FOX_HEAD_DIM), lambda h, j: (0, off + h))
    kt = lambda off: pl.BlockSpec((t, FOX_HEAD_DIM), lambda h, j: (j, off + h))
    ck_spec = pl.BlockSpec((None, 1, t), lambda h, j: (h, 0, j))
    return pl.pallas_call(
        body, name="fox_attn_bwd", grid=(FOX_HEADS, nk),
        in_specs=[full(0), kt(FOX_HEADS), kt(2 * FOX_HEADS), full(0), full(0), full(0),
                  pl.BlockSpec((8 * nk, FOX_HEAD_DIM), lambda h, j: (0, h)), ck_spec],
        out_specs=[full(0), kt(0), kt(0), full(0), ck_spec],
        out_shape=[jax.ShapeDtypeStruct((length, FOX_WIDTH), MXU_DT)] * 3
        + [jax.ShapeDtypeStruct((length, FOX_WIDTH), F32), jax.ShapeDtypeStruct((FOX_HEADS, 1, length), F32)],
        scratch_shapes=[pltpu.VMEM((length, FOX_HEAD_DIM), F32)],
        compiler_params=_cparams("parallel", "arbitrary"),
    )(qkv, qkv, qkv, d_att, lse, delta, c0_rep, c_t)


def _fox_gate_fn(att, gate):
    return att * jax.nn.silu(gate)


N_CHIP = 4
ANYSPACE = pl.BlockSpec(memory_space=pl.ANY)


def _other_chips(mx, my):
    return [(1 - mx, my), (mx, 1 - my), (1 - mx, 1 - my)]


def _gather(x, name):
    def body(x_ref, o_ref, send_sems, recv_sems, local_sem):
        mx, my, mc = lax.axis_index("x"), lax.axis_index("y"), lax.axis_index("c")
        me, sibling = (mx, my, mc), (mx, my, 1 - mc)
        chips = _other_chips(mx, my)

        def slot(px, py, pc):
            return o_ref.at[4 * px + 2 * py + pc]

        def copy(k, block, to, src=None):
            return pltpu.make_async_remote_copy(
                src_ref=slot(*block) if src is None else src, dst_ref=slot(*block),
                send_sem=send_sems.at[k], recv_sem=recv_sems.at[k],
                device_id=to, device_id_type=pl.DeviceIdType.MESH)

        mine = pltpu.make_async_copy(x_ref, slot(*me), local_sem)
        mine.start()
        first = [copy(0, me, sibling, src=x_ref)]
        first += [copy(1 + j, me, (*chip, mc), src=x_ref) for j, chip in enumerate(chips)]
        for cp in first:
            cp.start()
        passed = [copy(4 + j, (*chip, mc), sibling) for j, chip in enumerate(chips)]
        for j, chip in enumerate(chips):
            copy(1 + j, (*chip, mc), me).wait_recv()
            passed[j].start()
        copy(0, sibling, me).wait_recv()
        for j, chip in enumerate(chips):
            copy(4 + j, (*chip, 1 - mc), me).wait_recv()
        for cp in first + passed:
            cp.wait_send()
        mine.wait()

    return pl.pallas_call(
        body, name=name, in_specs=[ANYSPACE], out_specs=ANYSPACE,
        out_shape=jax.ShapeDtypeStruct((N_DEV,) + x.shape, x.dtype),
        scratch_shapes=[pltpu.SemaphoreType.DMA((N_DEV - 1,)), pltpu.SemaphoreType.DMA((N_DEV - 1,)),
                        pltpu.SemaphoreType.DMA],
        compiler_params=pltpu.CompilerParams(has_side_effects=True),
    )(x)


def _pair_send(parts, name):
    def body(p_ref, o_ref, send_sems, recv_sems):
        mx, my, mc = lax.axis_index("x"), lax.axis_index("y"), lax.axis_index("c")
        copies = [pltpu.make_async_remote_copy(
            src_ref=p_ref.at[2 * chip + (1 - mc)], dst_ref=o_ref.at[chip],
            send_sem=send_sems.at[chip], recv_sem=recv_sems.at[chip],
            device_id=(mx, my, 1 - mc), device_id_type=pl.DeviceIdType.MESH) for chip in range(N_CHIP)]
        for cp in copies:
            cp.start()
        for cp in copies:
            cp.wait_recv()
        for cp in copies:
            cp.wait_send()

    return pl.pallas_call(
        body, name=name, in_specs=[ANYSPACE], out_specs=ANYSPACE,
        out_shape=jax.ShapeDtypeStruct((N_CHIP,) + parts.shape[1:], parts.dtype),
        scratch_shapes=[pltpu.SemaphoreType.DMA((N_CHIP,)), pltpu.SemaphoreType.DMA((N_CHIP,))],
        compiler_params=pltpu.CompilerParams(has_side_effects=True),
    )(parts)


def _pair_sum(parts, recv, out_dtype, name):
    _, rows, cols = parts.shape
    tr, tc = _tile2d(rows, cols)

    def body(c_ref, p_ref, r_ref, o_ref):
        o_ref[...] = (p_ref[...] + r_ref[...]).astype(o_ref.dtype)

    return pl.pallas_call(
        body, name=name,
        grid_spec=pltpu.PrefetchScalarGridSpec(
            num_scalar_prefetch=1, grid=(N_CHIP, rows // tr, cols // tc),
            in_specs=[pl.BlockSpec((None, tr, tc), lambda k, i, j, c: (2 * k + c[0], i, j)),
                      pl.BlockSpec((None, tr, tc), lambda k, i, j, c: (k, i, j))],
            out_specs=pl.BlockSpec((None, tr, tc), lambda k, i, j, c: (k, i, j))),
        out_shape=jax.ShapeDtypeStruct((N_CHIP, rows, cols), out_dtype),
        compiler_params=_cparams("parallel", "parallel", "parallel"),
    )(lax.axis_index("c").astype(jnp.int32).reshape(1), parts, recv)


def _chip_exchange(sums, name):
    def body(s_ref, o_ref, send_sems, recv_sems, local_sem):
        mx, my, mc = lax.axis_index("x"), lax.axis_index("y"), lax.axis_index("c")
        my_chip = 2 * mx + my
        local = pltpu.make_async_copy(s_ref.at[my_chip], o_ref.at[my_chip], local_sem)
        local.start()
        sends, recvs = [], []
        for k, (px, py) in enumerate(_other_chips(mx, my)):
            peer = 2 * px + py

            def copy(src_slot, dst_slot, k=k, dev=(px, py, mc)):
                return pltpu.make_async_remote_copy(
                    src_ref=s_ref.at[src_slot], dst_ref=o_ref.at[dst_slot], send_sem=send_sems.at[k],
                    recv_sem=recv_sems.at[k], device_id=dev, device_id_type=pl.DeviceIdType.MESH)

            sends.append(copy(peer, my_chip))
            recvs.append(copy(peer, peer))
        for cp in sends:
            cp.start()
        for cp in recvs:
            cp.wait_recv()
        for cp in sends:
            cp.wait_send()
        local.wait()

    return pl.pallas_call(
        body, name=name, in_specs=[ANYSPACE], out_specs=ANYSPACE,
        out_shape=jax.ShapeDtypeStruct(sums.shape, sums.dtype),
        scratch_shapes=[pltpu.SemaphoreType.DMA((N_CHIP - 1,)), pltpu.SemaphoreType.DMA((N_CHIP - 1,)),
                        pltpu.SemaphoreType.DMA],
        compiler_params=pltpu.CompilerParams(has_side_effects=True),
    )(sums)


ADAM_TILE_ELEMS = 128 * 1024


def _tile2d(rows, cols):
    if rows * cols <= ADAM_TILE_ELEMS:
        return rows, cols
    if rows % 8 == 0:
        return _pick(rows, max(8, ADAM_TILE_ELEMS // cols), 8), cols
    return rows, _pick(cols, max(LANES, ADAM_TILE_ELEMS // rows))


def _adamw(w, parts, m, v, name):
    rows, cols = w.shape
    n_parts = parts.shape[0]
    tr, tc = _tile2d(rows, cols)

    def body(w_ref, p_ref, m_ref, v_ref, g_ref, d_ref, nm_ref, nv_ref):
        g = p_ref[0].astype(F32)
        for p in range(1, n_parts):
            g = g + p_ref[p].astype(F32)
        mm = ADAM_B1 * m_ref[...] + (1.0 - ADAM_B1) * g
        vv = ADAM_B2 * v_ref[...] + (1.0 - ADAM_B2) * jnp.square(g)
        m_hat = mm / (1.0 - ADAM_B1 ** ADAM_STEP)
        v_hat = vv / (1.0 - ADAM_B2 ** ADAM_STEP)
        g_ref[...] = g
        d_ref[...] = -ADAM_LR * (m_hat / (jnp.sqrt(v_hat) + ADAM_EPS) + ADAM_WD * w_ref[...])
        nm_ref[...] = mm
        nv_ref[...] = vv

    blk = pl.BlockSpec((tr, tc), lambda i, j: (i, j))
    return pl.pallas_call(
        body, name=name, grid=(rows // tr, cols // tc),
        in_specs=[blk, pl.BlockSpec((n_parts, tr, tc), lambda i, j: (0, i, j)), blk, blk],
        out_specs=[blk] * 4, out_shape=[jax.ShapeDtypeStruct((rows, cols), F32)] * 4,
        compiler_params=_cparams("parallel", "parallel"),
    )(w, parts, m, v)


WEIGHTS = ("l0_norm_w", "l0_w_in", "l0_s5_lambda_re", "l0_s5_lambda_im", "l0_s5_log_step", "l0_s5_b_re",
           "l0_s5_b_im", "l0_s5_c_re", "l0_s5_c_im", "l0_s5_d", "l0_s5_w_glu", "l0_s5_b_glu", "l0_ssd_conv_w",
           "l0_ssd_conv_b", "l0_ssd_dt_bias", "l0_ssd_a_log", "l0_ssd_d", "l0_ssd_norm_w", "l0_w_out",
           "l1_norm_w", "l1_w_in", "l1_fox_b_f", "l1_w_out", "final_norm_w")
SHARDED = ("l0_w_in", "l0_s5_w_glu", "l0_ssd_conv_w", "l0_w_out", "l1_w_in", "l1_w_out")


def _pad_lanes(a, width=LANES):
    return jnp.pad(a, [(0, 0)] * (a.ndim - 1) + [(0, width - a.shape[-1])])


def _pad_rows(a, height=LANES):
    return jnp.pad(a, [(0, height - a.shape[0])] + [(0, 0)] * (a.ndim - 1))


def _to_groups(a):
    length = a.shape[0]
    return _pad_lanes(a[:, :SSD_HEADS].reshape(length, SSD_GROUPS, SSD_HPG).transpose(1, 0, 2))


def _from_groups(a):
    length = a.shape[1]
    return a[:, :, :SSD_HPG].transpose(1, 0, 2).reshape(length, SSD_HEADS)


def _local_step(x, target, w):
    length = x.shape[0]
    nc = length // SSD_CHUNK
    g = {}

    h0 = _rmsnorm_fwd(x, w["l0_norm_w"], "l0_norm")
    w0 = w["l0_w_in"]
    w0_s5, w0_z, w0_xbc = w0[:2 * S5_WIDTH], w0[2 * S5_WIDTH:2 * S5_WIDTH + SSD_WIDTH], w0[10240:18432]
    w0_dt = _pad_rows(w0[18432:])
    p_s5 = _mm(h0, w0_s5, "nt", F32, "l0_in_s5")
    p_z = _mm(h0, w0_z, "nt", F32, "l0_in_z")
    p_xbc = _mm(h0, w0_xbc, "nt", F32, "l0_in_xbc")
    p_dt = _mm(h0, w0_dt, "nt", F32, "l0_in_dt")

    row = lambda a: a.reshape(1, S5_NS)
    b_rows = lambda a: a.transpose(2, 0, 1).reshape(S5_GROUP, S5_NS)
    prep_in = (row(w["l0_s5_lambda_re"]), row(w["l0_s5_lambda_im"]),
               row(jnp.repeat(w["l0_s5_log_step"], S5_STATE)), b_rows(w["l0_s5_b_re"]), b_rows(w["l0_s5_b_im"]))
    ab_re, ab_im, bbr, bbi = _s5_prep(*prep_in)
    to_bb = lambda a: _blockdiag(a.reshape(S5_GROUP, S5_GROUPS, S5_STATE).transpose(1, 0, 2),
                                 S5_GROUP, S5_STATE).astype(MXU_DT)
    to_ct = lambda a: _blockdiag(a.transpose(0, 2, 1), S5_STATE, S5_GROUP).astype(MXU_DT)
    bb_re, bb_im = to_bb(bbr), to_bb(bbi)
    ct_re, ct_im = to_ct(w["l0_s5_c_re"]), to_ct(w["l0_s5_c_im"])
    a_re3, a_im3 = ab_re.reshape(S5_BLK, 1, S5_BS), ab_im.reshape(S5_BLK, 1, S5_BS)
    d3 = w["l0_s5_d"].reshape(S5_BLK, 1, S5_BC)
    y5, s_re, s_im = _s5_fwd(p_s5, bb_re, bb_im, ct_re, ct_im, a_re3, a_im3, d3)
    g_bf = _tiles("s5_gelu", lambda yb: (_gelu(yb),), [y5], [], [(S5_WIDTH, MXU_DT)], 0, tr=256)[0]
    zg = _mm(g_bf, w["l0_s5_w_glu"], "nn", F32, "s5_glu")
    b_glu = w["l0_s5_b_glu"].reshape(1, -1)
    s5_out = _tiles("s5_out", lambda yb, zb, gb, bb: (_s5_out_fn(yb, zb, gb, bb),),
                    [y5, zg, (p_s5, 1)], [b_glu], [(S5_WIDTH, MXU_DT)], 0, tr=256)[0]

    conv_w = w["l0_ssd_conv_w"]
    cv, act = _conv_fwd(p_xbc, conv_w, w["l0_ssd_conv_b"])
    bias_row = _pad_lanes(w["l0_ssd_dt_bias"].reshape(1, -1))
    alog_row = _pad_lanes(w["l0_ssd_a_log"].reshape(1, -1))
    dt, cum, cum_t = _dt_fwd(p_dt, bias_row, alog_row)
    dtg, cumg = _to_groups(dt), _to_groups(cum)
    cumtg = cum_t[:, :SSD_HEADS].reshape(nc, SSD_GROUPS, SSD_HPG, SSD_CHUNK).transpose(1, 0, 2, 3)
    cumtg = jnp.pad(cumtg, ((0, 0), (0, 0), (0, 16 - SSD_HPG), (0, 0)))
    ycore, states = _ssd_core_fwd(act, dtg, cumg, cumtg)
    dchan = jnp.repeat(w["l0_ssd_d"], SSD_HEAD_DIM).reshape(1, -1)
    nw_row = w["l0_ssd_norm_w"].reshape(1, -1)
    ssd_out = _tiles("ssd_post", lambda a, b, c, d, e: (_ssd_post_fn(a, b, c, d, e),),
                     [ycore, act, p_z], [dchan, nw_row], [(SSD_WIDTH, MXU_DT)], 0, tr=256, tc=SSD_GW)[0]
    mixed = jnp.concatenate([s5_out, ssd_out], axis=1)
    x1 = _mm(mixed, w["l0_w_out"], "nn", F32, "l0_out", res=x)

    h1 = _rmsnorm_fwd(x1, w["l1_norm_w"], "l1_norm")
    w1 = w["l1_w_in"]
    w1_qkv, w1_gate = w1[:3 * FOX_WIDTH], w1[3 * FOX_WIDTH:4 * FOX_WIDTH]
    w1_f = _pad_rows(w1[4 * FOX_WIDTH:])
    qkv = _mm(h1, w1_qkv, "nt", MXU_DT, "l1_in_qkv")
    gate1 = _mm(h1, w1_gate, "nt", F32, "l1_in_gate")
    f_raw = _mm(h1, w1_f, "nt", F32, "l1_in_f")
    bf_row = _pad_lanes(w["l1_fox_b_f"].reshape(1, -1))
    c = _fgate_fwd(f_raw, bf_row)
    c0_rep = _fox_c0(c, min(FOX_TILE, length))
    c_t = c[:, :FOX_HEADS].T.reshape(FOX_HEADS, 1, length)
    att, lse = _fox_fwd(qkv, c0_rep, c_t)
    out1 = _tiles("fox_gate", lambda a, b: (_fox_gate_fn(a, b),), [att, gate1], [],
                  [(FOX_WIDTH, MXU_DT)], 0, tr=256)[0]
    x2 = _mm(out1, w["l1_w_out"], "nn", F32, "l1_out", res=x1)

    loss_part, dx2, dx2b, g["final_norm_w"] = _final_loss(x2, w["final_norm_w"], target, "final_loss")

    d_out1 = _mm(dx2b, w["l1_w_out"], "nt", F32, "l1_out_dx")
    g["l1_w_out"] = _mm(out1, dx2b, "tn", F32, "l1_out_dw")

    def gate_bwd(a, gt, d):
        _, vjp = jax.vjp(_fox_gate_fn, a, gt)
        da, dgt = vjp(d)
        return da, dgt, jnp.broadcast_to(jnp.sum(da * a, axis=1, keepdims=True), a.shape)

    d_att, d_gate1, delta = _tiles("fox_gate_bwd", gate_bwd, [att, gate1, d_out1], [],
                                   [(FOX_WIDTH, MXU_DT), (FOX_WIDTH, MXU_DT), (FOX_WIDTH, F32)], 0,
                                   tr=512, tc=FOX_HEAD_DIM)
    dq, dk, dv, dcq, dck = _fox_bwd(qkv, d_att, lse, delta, c0_rep, c_t)
    dc = dcq.reshape(length, FOX_HEADS, FOX_HEAD_DIM)[:, :, 0] + dck.reshape(FOX_HEADS, length).T
    df, dbf = _fgate_bwd(f_raw, bf_row, _pad_lanes(dc))
    g["l1_fox_b_f"] = dbf[0, :FOX_HEADS]
    dqkv = jnp.concatenate([dq, dk, dv], axis=1)
    dh1 = _mm(dqkv, w1_qkv, "nn", F32, "l1_in_dx_qkv")
    dh1 = _mm(d_gate1, w1_gate, "nn", F32, "l1_in_dx_gate", res=dh1)
    dh1 = _mm(df, w1_f, "nn", F32, "l1_in_dx_f", res=dh1)
    g["l1_w_in"] = jnp.concatenate([
        _mm(dqkv, h1, "tn", F32, "l1_in_dw_qkv"), _mm(d_gate1, h1, "tn", F32, "l1_in_dw_gate"),
        _mm(df, h1, "tn", F32, "l1_in_dw_f")[:FOX_HEADS]], axis=0)
    dx1, dx1b, g["l1_norm_w"] = _rmsnorm_bwd(x1, w["l1_norm_w"], dh1, dx2, "l1_norm_bwd")

    wout0 = w["l0_w_out"]
    d_s5 = _mm(dx1b, wout0[:S5_WIDTH], "nt", F32, "l0_out_dx_s5")
    d_ssd = _mm(dx1b, wout0[S5_WIDTH:], "nt", F32, "l0_out_dx_ssd")
    g["l0_w_out"] = _mm(mixed, dx1b, "tn", F32, "l0_out_dw")

    def post_bwd(a, b, c_, d, dch, nw):
        _, vjp = jax.vjp(_ssd_post_fn, a, b, c_, dch, nw)
        return vjp(d)

    dycore, dxs_post, dz, ddch, dnw = _tiles(
        "ssd_post_bwd", post_bwd, [ycore, act, p_z, d_ssd], [dchan, nw_row],
        [(SSD_WIDTH, F32), (SSD_WIDTH, F32), (SSD_WIDTH, MXU_DT)], 2, tr=256, tc=SSD_GW)
    g["l0_ssd_d"] = ddch[0].reshape(SSD_HEADS, SSD_HEAD_DIM).sum(axis=1)
    g["l0_ssd_norm_w"] = dnw[0]
    dxs, d_b, d_c, ddtg, dcumg, dcumtg, dclg = _ssd_core_bwd(dycore, dxs_post, act, dtg, cumg, cumtg, states)
    dact = jnp.concatenate([dxs, d_b, d_c], axis=1)
    dxbc, dconvw, dconvb = _conv_bwd(dact, cv, p_xbc, conv_w)
    g["l0_ssd_conv_w"] = dconvw[:SSD_CONV]
    g["l0_ssd_conv_b"] = dconvb[0]
    dcum = _from_groups(dcumg)
    dcum = dcum + dcumtg[:, :, :SSD_HPG].transpose(1, 3, 0, 2).reshape(length, SSD_HEADS)
    dcl = dclg[:, :, 0, :SSD_HPG].transpose(1, 0, 2).reshape(nc, SSD_HEADS)
    dcum = dcum.reshape(nc, SSD_CHUNK, SSD_HEADS).at[:, SSD_CHUNK - 1, :].add(dcl).reshape(length, SSD_HEADS)
    ddt_raw, dbias, dalog = _dt_bwd(p_dt, bias_row, alog_row, _pad_lanes(_from_groups(ddtg)), _pad_lanes(dcum))
    g["l0_ssd_dt_bias"] = dbias[0, :SSD_HEADS]
    g["l0_ssd_a_log"] = dalog[0, :SSD_HEADS]

    def s5_out_bwd(yb, zb, gb, d, bb):
        _, vjp = jax.vjp(_s5_out_fn, yb, zb, gb, bb)
        return vjp(d)

    dy_direct, dzg, dgate0, dbglu = _tiles(
        "s5_out_bwd", s5_out_bwd, [y5, zg, (p_s5, 1), d_s5], [b_glu],
        [(S5_WIDTH, F32), (S5_WIDTH, MXU_DT), (S5_WIDTH, MXU_DT)], 1, tr=256)
    g["l0_s5_b_glu"] = dbglu[0]
    g["l0_s5_w_glu"] = _mm(g_bf, dzg, "tn", F32, "s5_glu_dw")
    dg2 = _mm(dzg, w["l0_s5_w_glu"], "nt", F32, "s5_glu_dx")

    def gelu_bwd(yb, d, direct):
        _, vjp = jax.vjp(_gelu, yb)
        return (vjp(d)[0] + direct,)

    dy5 = _tiles("s5_gelu_bwd", gelu_bwd, [y5, dg2, dy_direct], [], [(S5_WIDTH, F32)], 0, tr=256)[0]
    du, dbbr3, dbbi3, dctr3, dcti3, dar, dai, dd5 = _s5_bwd(dy5, p_s5, s_re, s_im, bb_re, bb_im, ct_re, ct_im,
                                                           a_re3, a_im3, d3)
    from_bb = lambda a: _blockdiag_t(a, S5_GROUP, S5_STATE).transpose(1, 0, 2).reshape(S5_GROUP, S5_NS)
    from_ct = lambda a: _blockdiag_t(a, S5_STATE, S5_GROUP).transpose(0, 2, 1)
    g["l0_s5_c_re"], g["l0_s5_c_im"] = from_ct(dctr3), from_ct(dcti3)
    g["l0_s5_d"] = dd5[:, 0, :].reshape(S5_GROUPS, S5_GROUP)
    dlr, dli, dls, dbr, dbi = _s5_prep_bwd(*prep_in, dar[:, 0, :].reshape(1, S5_NS), dai[:, 0, :].reshape(1, S5_NS),
                                           from_bb(dbbr3), from_bb(dbbi3))
    g["l0_s5_lambda_re"] = dlr.reshape(S5_GROUPS, S5_STATE)
    g["l0_s5_lambda_im"] = dli.reshape(S5_GROUPS, S5_STATE)
    g["l0_s5_log_step"] = dls.reshape(S5_GROUPS, S5_STATE).sum(axis=1)
    from_rows = lambda a: a.reshape(S5_GROUP, S5_GROUPS, S5_STATE).transpose(1, 2, 0)
    g["l0_s5_b_re"], g["l0_s5_b_im"] = from_rows(dbr), from_rows(dbi)

    dus = jnp.concatenate([du, dgate0], axis=1)
    dh0 = _mm(dus, w0_s5, "nn", F32, "l0_in_dx_s5")
    dh0 = _mm(dz, w0_z, "nn", F32, "l0_in_dx_z", res=dh0)
    dh0 = _mm(dxbc, w0_xbc, "nn", F32, "l0_in_dx_xbc", res=dh0)
    dh0 = _mm(ddt_raw, w0_dt, "nn", F32, "l0_in_dx_dt", res=dh0)
    g["l0_w_in"] = jnp.concatenate([
        _mm(dus, h0, "tn", F32, "l0_in_dw_s5"), _mm(dz, h0, "tn", F32, "l0_in_dw_z"),
        _mm(dxbc, h0, "tn", F32, "l0_in_dw_xbc"),
        _mm(ddt_raw, h0, "tn", F32, "l0_in_dw_dt")[:SSD_HEADS]], axis=0)
    grad_x, _, g["l0_norm_w"] = _rmsnorm_bwd(x, w["l0_norm_w"], dh0, dx1, "l0_norm_bwd")
    return loss_part, grad_x, g


TRANSPOSED = ("l0_w_in", "l1_w_in")


def _gather_weight(name, shard):
    if name == "l0_ssd_conv_w":
        full = _gather(shard, "gather_" + name)
        return full.transpose(1, 0, 2).reshape(shard.shape[0], N_DEV * shard.shape[1])
    if name in TRANSPOSED:
        full = _gather(shard.T.astype(MXU_DT), "gather_" + name)
        return full.reshape(N_DEV * shard.shape[1], shard.shape[0])
    full = _gather(shard.astype(MXU_DT), "gather_" + name)
    return full.reshape(N_DEV * shard.shape[0], shard.shape[1])


def _reduce_grad(name, grad, shard_shape):
    rows, cols = shard_shape
    if name == "l0_ssd_conv_w":
        parts = grad.reshape(rows, N_DEV, cols).transpose(1, 0, 2)
    elif name in TRANSPOSED:
        parts = grad.reshape(N_DEV, cols, rows)
    else:
        parts = grad.reshape(N_DEV, rows, cols)
    recv = _pair_send(parts, "pair_" + name)
    sums = _pair_sum(parts, recv, F32 if name == "l0_ssd_conv_w" else MXU_DT, "pairsum_" + name)
    return _chip_exchange(sums, "scatter_" + name)


SMALL_ROWS_QUANTUM = 8 * LANES


def _step(args):
    x = args["x"][0]
    target = args["loss_target"][0]
    full = {n: args[n] for n in WEIGHTS if n not in SHARDED}
    for n in SHARDED:
        full[n] = _gather_weight(n, args[n])
    loss_part, grad_x, g = _local_step(x, target, full)

    out_g, out_d, out_m, out_v = {}, {}, {}, {}
    for n in SHARDED:
        parts = _reduce_grad(n, g[n], args[n].shape)
        view = (lambda a: a.T) if n in TRANSPOSED else (lambda a: a)
        outs = _adamw(view(args[n]), parts, view(args["m_" + n]), view(args["v_" + n]), "adamw_" + n)
        out_g[n], out_d[n], out_m[n], out_v[n] = [view(o) for o in outs]

    small = [n for n in WEIGHTS if n not in SHARDED]
    sizes = [int(math.prod(args[n].shape)) for n in small]
    total = sum(sizes) + 1
    padded = -(-total // SMALL_ROWS_QUANTUM) * SMALL_ROWS_QUANTUM

    def pack(pieces, extra):
        flat = jnp.concatenate([p.reshape(-1).astype(F32) for p in pieces] + [extra.reshape(1)])
        return jnp.pad(flat, (0, padded - total)).reshape(padded // LANES, LANES)

    zero = jnp.zeros((), F32)
    parts = _gather(pack([g[n] for n in small], loss_part), "gather_small_grads")
    sg, sd, sm, sv = _adamw(pack([args[n] for n in small], zero), parts,
                            pack([args["m_" + n] for n in small], zero),
                            pack([args["v_" + n] for n in small], zero), "adamw_small")
    off = 0
    for n, sz in zip(small, sizes):
        cut = lambda a: a.reshape(-1)[off:off + sz].reshape(args[n].shape)
        out_g[n], out_d[n], out_m[n], out_v[n] = cut(sg), cut(sd), cut(sm), cut(sv)
        off += sz
    loss = sg.reshape(-1)[total - 1]
    return (loss, grad_x[None], *[out_g[n] for n in WEIGHTS], *[out_d[n] for n in WEIGHTS],
            *[out_m[n] for n in WEIGHTS], *[out_v[n] for n in WEIGHTS])


def kernel(x, l0_norm_w, l0_w_in, l0_s5_lambda_re, l0_s5_lambda_im, l0_s5_log_step, l0_s5_b_re, l0_s5_b_im, l0_s5_c_re, l0_s5_c_im, l0_s5_d, l0_s5_w_glu, l0_s5_b_glu, l0_ssd_conv_w, l0_ssd_conv_b, l0_ssd_dt_bias, l0_ssd_a_log, l0_ssd_d, l0_ssd_norm_w, l0_w_out, l1_norm_w, l1_w_in, l1_fox_b_f, l1_w_out, final_norm_w, loss_target, m_l0_norm_w, m_l0_w_in, m_l0_s5_lambda_re, m_l0_s5_lambda_im, m_l0_s5_log_step, m_l0_s5_b_re, m_l0_s5_b_im, m_l0_s5_c_re, m_l0_s5_c_im, m_l0_s5_d, m_l0_s5_w_glu, m_l0_s5_b_glu, m_l0_ssd_conv_w, m_l0_ssd_conv_b, m_l0_ssd_dt_bias, m_l0_ssd_a_log, m_l0_ssd_d, m_l0_ssd_norm_w, m_l0_w_out, m_l1_norm_w, m_l1_w_in, m_l1_fox_b_f, m_l1_w_out, m_final_norm_w, v_l0_norm_w, v_l0_w_in, v_l0_s5_lambda_re, v_l0_s5_lambda_im, v_l0_s5_log_step, v_l0_s5_b_re, v_l0_s5_b_im, v_l0_s5_c_re, v_l0_s5_c_im, v_l0_s5_d, v_l0_s5_w_glu, v_l0_s5_b_glu, v_l0_ssd_conv_w, v_l0_ssd_conv_b, v_l0_ssd_dt_bias, v_l0_ssd_a_log, v_l0_ssd_d, v_l0_ssd_norm_w, v_l0_w_out, v_l1_norm_w, v_l1_w_in, v_l1_fox_b_f, v_l1_w_out, v_final_norm_w):
    return _step(dict(locals()))
```

```python
import functools
import math

import jax
import jax.numpy as jnp
from jax import lax
from jax.experimental import pallas as pl
from jax.experimental.pallas import tpu as pltpu
from jax.experimental.pallas import tpu_sc as plsc

F32 = jnp.float32
BF16 = jnp.bfloat16
MXU_DT = BF16

D_MODEL = 4096
S5_WIDTH = 2048
S5_GROUP = 16
S5_GROUPS = 128
S5_STATE = 64
S5_EIG_CLIP = -1e-4
S5_BLK = 16
SSD_WIDTH = 6144
SSD_HEAD_DIM = 64
SSD_HEADS = 96
SSD_GROUPS = 8
SSD_STATE = 128
SSD_CONV = 4
SSD_CHUNK = 128
SSD_XBC = 8192
SSD_HPG = SSD_HEADS // SSD_GROUPS
FOX_HEAD_DIM = 128
FOX_HEADS = 32
FOX_WIDTH = 4096
NORM_EPS = 1e-5
EVEN_IN = 18528
ODD_IN = 16416
EVEN_PAD = 18560
ODD_PAD = 16512
LANES = 128
N_DEV = 8

ADAM_LR = 0.001
ADAM_B1 = 0.9
ADAM_B2 = 0.999
ADAM_EPS = 1e-08
ADAM_WD = 0.01
ADAM_STEP = 10

VMEM_LIMIT_BYTES = 48 * 1024 * 1024


def _cparams(*sem):
    return pltpu.CompilerParams(dimension_semantics=sem, vmem_limit_bytes=VMEM_LIMIT_BYTES)


def _pick(n, target, quantum=LANES):
    if n <= target:
        return n
    t = (target // quantum) * quantum
    while t >= quantum:
        if n % t == 0:
            return t
        t -= quantum
    raise ValueError((n, target, quantum))


_MM_DIMS = {"nn": ((1,), (0,)), "nt": ((1,), (1,)), "tn": ((0,), (0,))}


MM_VMEM_BUDGET = 38 * 1024 * 1024


def _mm_tk(k, tm, tn, out_bytes, has_res, tk_t):
    fixed = 2 * tm * tn * out_bytes + (2 * tm * tn * 4 if has_res else 0)
    tk = min(k, tk_t)
    while True:
        if k % tk == 0 and (tk == k or tk % LANES == 0):
            need = fixed + 2 * (tm + tn) * tk * 2 + (tm * tn * 4 if tk < k else 0)
            if need <= MM_VMEM_BUDGET or tk <= LANES:
                return tk
        tk -= LANES if tk % LANES == 0 else tk % LANES


def _mm(a, b, mode, out_dtype, name, res=None, tm_t=1024, tn_t=512, tk_t=8192):
    if mode == "nn":
        (m, k), (k2, n) = a.shape, b.shape
    elif mode == "nt":
        (m, k), (n, k2) = a.shape, b.shape
    else:
        (k, m), (k2, n) = a.shape, b.shape
    assert k == k2, (a.shape, b.shape, mode)
    tm, tn = _pick(m, tm_t), _pick(n, tn_t)
    has_res = res is not None
    tk = _mm_tk(k, tm, tn, jnp.dtype(out_dtype).itemsize, has_res, tk_t)
    nk = k // tk
    dims = (_MM_DIMS[mode], ((), ()))

    def body(*refs):
        a_ref, b_ref = refs[:2]
        r_ref = refs[2] if has_res else None
        o_ref = refs[3] if has_res else refs[2]
        part = lax.dot_general(a_ref[...].astype(MXU_DT), b_ref[...].astype(MXU_DT), dims,
                               preferred_element_type=F32)

        def finish(r):
            if has_res:
                r = r + r_ref[...]
            o_ref[...] = r.astype(out_dtype)

        if nk == 1:
            finish(part)
            return
        acc = refs[-1]
        kk = pl.program_id(2)

        @pl.when(kk == 0)
        def _():
            acc[...] = part

        @pl.when(jnp.logical_and(kk > 0, kk < nk - 1))
        def _():
            acc[...] += part

        @pl.when(kk == nk - 1)
        def _():
            finish(acc[...] + part)

    a_spec = (pl.BlockSpec((tk, tm), lambda i, j, kk: (kk, i)) if mode == "tn"
              else pl.BlockSpec((tm, tk), lambda i, j, kk: (i, kk)))
    b_spec = (pl.BlockSpec((tn, tk), lambda i, j, kk: (j, kk)) if mode == "nt"
              else pl.BlockSpec((tk, tn), lambda i, j, kk: (kk, j)))
    o_spec = pl.BlockSpec((tm, tn), lambda i, j, kk: (i, j))
    in_specs = [a_spec, b_spec] + ([o_spec] if has_res else [])
    args = (a, b) + ((res,) if has_res else ())
    return pl.pallas_call(
        body, name=name, grid=(m // tm, n // tn, nk), in_specs=in_specs, out_specs=o_spec,
        out_shape=jax.ShapeDtypeStruct((m, n), out_dtype),
        scratch_shapes=[pltpu.VMEM((tm, tn), F32)] if nk > 1 else [],
        compiler_params=_cparams("parallel", "parallel", "arbitrary"),
    )(*args)


def _tiles(name, fn, tiled, rows, out_tiled, out_acc, tr, tc=None):
    tiled = [t if isinstance(t, tuple) else (t, 0) for t in tiled]
    length = tiled[0][0].shape[0]
    width = out_tiled[0][0] if out_tiled else rows[0].shape[1]
    tc = width if tc is None else tc
    tr = min(tr, length)
    n_in = len(tiled) + len(rows)
    n_ot = len(out_tiled)

    def body(*refs):
        outs = fn(*[r[...] for r in refs[:n_in]])
        outs_t, outs_a = outs[:n_ot], outs[n_ot:]
        for r, v in zip(refs[n_in:n_in + n_ot], outs_t):
            r[...] = v.astype(r.dtype)
        i = pl.program_id(1)
        for r, v in zip(refs[n_in + n_ot:], outs_a):
            @pl.when(i == 0)
            def _(r=r):
                r[...] = jnp.zeros_like(r)

            r[...] += jnp.broadcast_to(v, r.shape)

    def tspec(off):
        return pl.BlockSpec((tr, tc), lambda j, i: (i, j + off))

    in_specs = [tspec(off) for _, off in tiled] + [pl.BlockSpec((1, tc), lambda j, i: (0, j)) for _ in rows]
    out_specs = [tspec(0) for _ in out_tiled] + [pl.BlockSpec((8, tc), lambda j, i: (0, j)) for _ in range(out_acc)]
    out_shape = ([jax.ShapeDtypeStruct((length, w), dt) for w, dt in out_tiled]
                 + [jax.ShapeDtypeStruct((8, width), F32) for _ in range(out_acc)])
    return pl.pallas_call(
        body, name=name, grid=(width // tc, length // tr), in_specs=in_specs, out_specs=out_specs,
        out_shape=out_shape, compiler_params=_cparams("parallel", "arbitrary"),
    )(*[t for t, _ in tiled], *rows)


def _rms(x, w):
    return x * lax.rsqrt(jnp.mean(x * x, axis=-1, keepdims=True) + NORM_EPS) * w


def _colsum(v):
    return jnp.sum(v, axis=0, keepdims=True)


def _rmsnorm_fwd(x, w, name):
    def fn(xb, wb):
        return (_rms(xb, wb),)
    return _tiles(name, fn, [x], [w.reshape(1, -1)], [(x.shape[1], MXU_DT)], 0, tr=256)[0]


def _rmsnorm_bwd(x, w, dh, dres, name):
    def fn(xb, db, rb, wb):
        _, vjp = jax.vjp(_rms, xb, wb)
        dx, dw = vjp(db)
        return dx + rb, dx + rb, dw
    dx, dxb, dw = _tiles(name, fn, [x, dh, dres], [w.reshape(1, -1)],
                         [(x.shape[1], F32), (x.shape[1], MXU_DT)], 1, tr=256)
    return dx, dxb, dw[0]


def _final_loss(x, w, target, name):
    def fn(xb, tb, wb):
        def f(xv, wv):
            e = _rms(xv, wv) - tb
            return 0.5 * jnp.sum(jnp.mean(e * e, axis=-1, keepdims=True), axis=0, keepdims=True)
        lv, vjp = jax.vjp(f, xb, wb)
        dx, dw = vjp(jnp.ones_like(lv))
        return dx, dx, dw, jnp.broadcast_to(lv, (1, xb.shape[1]))
    dx, dxb, dw, lv = _tiles(name, fn, [x, target], [w.reshape(1, -1)],
                             [(x.shape[1], F32), (x.shape[1], MXU_DT)], 2, tr=256)
    return lv[0, 0], dx, dxb, dw[0]


def _dg(a, b, mode):
    return lax.dot_general(a.astype(MXU_DT), b.astype(MXU_DT), (_MM_DIMS[mode], ((), ())),
                           preferred_element_type=F32)


@jax.custom_vjp
def _dot_nn(a, b):
    return _dg(a, b, "nn")


@jax.custom_vjp
def _dot_nt(a, b):
    return _dg(a, b, "nt")


@jax.custom_vjp
def _dot_tn(a, b):
    return _dg(a, b, "tn")


_dot_nn.defvjp(lambda a, b: (_dg(a, b, "nn"), (a, b)),
               lambda r, g: (_dg(g, r[1], "nt"), _dg(r[0], g, "tn")))
_dot_nt.defvjp(lambda a, b: (_dg(a, b, "nt"), (a, b)),
               lambda r, g: (_dg(g, r[1], "nn"), _dg(g, r[0], "tn")))
_dot_tn.defvjp(lambda a, b: (_dg(a, b, "tn"), (a, b)),
               lambda r, g: (_dg(r[1], g, "nt"), _dg(r[0], g, "nn")))


def _dot_exact(a, b):
    return jnp.dot(a, b, precision=lax.Precision.HIGHEST, preferred_element_type=F32)


S5_NS = S5_GROUPS * S5_STATE
S5_BS = S5_NS // S5_BLK
S5_BC = S5_WIDTH // S5_BLK


def _s5_disc(lr_raw, li, ls, br, bi):
    lr = jnp.minimum(lr_raw, S5_EIG_CLIP)
    step = jnp.exp(ls)
    mag = jnp.exp(lr * step)
    ab_re = mag * jnp.cos(li * step)
    ab_im = mag * jnp.sin(li * step)
    denom = lr * lr + li * li
    nr = ab_re - 1.0
    ni = ab_im
    coef_re = (nr * lr + ni * li) / denom
    coef_im = (ni * lr - nr * li) / denom
    return ab_re, ab_im, coef_re * br - coef_im * bi, coef_re * bi + coef_im * br


def _s5_prep(lr_raw, li, ls, br, bi):
    shapes = [jax.ShapeDtypeStruct((1, S5_NS), F32)] * 2 + [jax.ShapeDtypeStruct((S5_GROUP, S5_NS), F32)] * 2

    def body(a, b, c, d, e, o1, o2, o3, o4):
        for r, v in zip((o1, o2, o3, o4), _s5_disc(a[...], b[...], c[...], d[...], e[...])):
            r[...] = v

    return pl.pallas_call(body, name="s5_prep", out_shape=shapes)(lr_raw, li, ls, br, bi)


def _s5_prep_bwd(lr_raw, li, ls, br, bi, d_are, d_aim, d_bbre, d_bbim):
    shapes = [jax.ShapeDtypeStruct((1, S5_NS), F32)] * 3 + [jax.ShapeDtypeStruct((S5_GROUP, S5_NS), F32)] * 2

    def body(a, b, c, d, e, g1, g2, g3, g4, o1, o2, o3, o4, o5):
        _, vjp = jax.vjp(_s5_disc, a[...], b[...], c[...], d[...], e[...])
        for r, v in zip((o1, o2, o3, o4, o5), vjp((g1[...], g2[...], g3[...], g4[...]))):
            r[...] = v

    return pl.pallas_call(body, name="s5_prep_bwd", out_shape=shapes)(
        lr_raw, li, ls, br, bi, d_are, d_aim, d_bbre, d_bbim)


def _cmul(ar, ai, br, bi):
    return ar * br - ai * bi, ar * bi + ai * br


def _s5_powers(ar, ai, n):
    pr, pi_ = [ar], [ai]
    for _ in range(n - 1):
        r, i = _cmul(pr[-1], pi_[-1], pr[-1], pi_[-1])
        pr.append(r)
        pi_.append(i)
    return pr, pi_


def _s5_scan_tile(xr, xi, pr, pi_, reverse):
    t = xr.shape[0]
    row = lax.broadcasted_iota(jnp.int32, xr.shape, 0)
    sr, si = xr, xi
    for k in range(len(pr)):
        d = 1 << k
        shift = (t - d) if reverse else d
        keep = (row < t - d) if reverse else (row >= d)
        qr = jnp.where(keep, pltpu.roll(sr, shift, 0), 0.0)
        qi = jnp.where(keep, pltpu.roll(si, shift, 0), 0.0)
        mr, mi = _cmul(pr[k], pi_[k], qr, qi)
        sr, si = sr + mr, si + mi
    return sr, si


def _s5_setup(ar, ai, t, reverse, apr, api, tabr, tabi):
    n = int(math.log2(t))
    pr, pi_ = _s5_powers(ar, ai, n)
    for k in range(n):
        apr[k:k + 1, :] = pr[k]
        api[k:k + 1, :] = pi_[k]
    row = lax.broadcasted_iota(jnp.int32, (t, ar.shape[1]), 0)
    first = (row == t - 1) if reverse else (row == 0)
    xr = jnp.where(first, ar, 0.0)
    xi = jnp.where(first, ai, 0.0)
    sr, si = _s5_scan_tile(xr, xi, pr, pi_, reverse)
    tabr[...] = sr
    tabi[...] = si


def _s5_fwd(p_s5, bb_re, bb_im, ct_re, ct_im, a_re, a_im, d_row, t_tile=256):
    length = p_s5.shape[0]
    t = min(t_tile, length)
    nt = length // t
    n = int(math.log2(t))

    def body(u_ref, bbr, bbi, ctr, cti, ar_ref, ai_ref, d_ref, y_ref, sr_ref, si_ref,
             apr, api, tabr, tabi, cr, ci):
        i = pl.program_id(1)

        @pl.when(i == 0)
        def _():
            _s5_setup(ar_ref[...], ai_ref[...], t, False, apr, api, tabr, tabi)
            cr[...] = jnp.zeros_like(cr)
            ci[...] = jnp.zeros_like(ci)

        u = u_ref[...]
        pr = [apr[k:k + 1, :] for k in range(n)]
        pi_ = [api[k:k + 1, :] for k in range(n)]
        sr, si = _s5_scan_tile(_dg(u, bbr[...], "nn"), _dg(u, bbi[...], "nn"), pr, pi_, False)
        mr, mi = _cmul(tabr[...], tabi[...], cr[0:1, :], ci[0:1, :])
        sr, si = sr + mr, si + mi
        cr[0:1, :] = sr[t - 1:t, :]
        ci[0:1, :] = si[t - 1:t, :]
        sr_ref[...] = sr
        si_ref[...] = si
        y_ref[...] = _dg(sr, ctr[...], "nn") - _dg(si, cti[...], "nn") + d_ref[...] * u

    blk3 = lambda a, b: pl.BlockSpec((None, a, b), lambda j, i: (j, 0, 0))
    return pl.pallas_call(
        body, name="s5_fwd", grid=(S5_BLK, nt),
        in_specs=[pl.BlockSpec((t, S5_BC), lambda j, i: (i, j)),
                  blk3(S5_BC, S5_BS), blk3(S5_BC, S5_BS), blk3(S5_BS, S5_BC), blk3(S5_BS, S5_BC),
                  blk3(1, S5_BS), blk3(1, S5_BS), blk3(1, S5_BC)],
        out_specs=[pl.BlockSpec((t, S5_BC), lambda j, i: (i, j)),
                   pl.BlockSpec((t, S5_BS), lambda j, i: (i, j)),
                   pl.BlockSpec((t, S5_BS), lambda j, i: (i, j))],
        out_shape=[jax.ShapeDtypeStruct((length, S5_WIDTH), F32),
                   jax.ShapeDtypeStruct((length, S5_NS), F32),
                   jax.ShapeDtypeStruct((length, S5_NS), F32)],
        scratch_shapes=[pltpu.VMEM((8, S5_BS), F32), pltpu.VMEM((8, S5_BS), F32),
                        pltpu.VMEM((t, S5_BS), F32), pltpu.VMEM((t, S5_BS), F32),
                        pltpu.VMEM((8, S5_BS), F32), pltpu.VMEM((8, S5_BS), F32)],
        compiler_params=_cparams("parallel", "arbitrary"),
    )(p_s5, bb_re, bb_im, ct_re, ct_im, a_re, a_im, d_row)


def _s5_bwd(dy, p_s5, s_re, s_im, bb_re, bb_im, ct_re, ct_im, a_re, a_im, d_row, t_tile=256):
    length = p_s5.shape[0]
    t = min(t_tile, length)
    nt = length // t
    n = int(math.log2(t))

    def body(dy_ref, u_ref, sr_ref, si_ref, pr_ref, pi_ref, bbr, bbi, ctr, cti, ar_ref, ai_ref, d_ref,
             du_ref, dbbr, dbbi, dctr, dcti, dar, dai, dd_ref, apr, api, tabr, tabi, cr, ci):
        i = pl.program_id(1)

        @pl.when(i == 0)
        def _():
            _s5_setup(ar_ref[...], -ai_ref[...], t, True, apr, api, tabr, tabi)
            for r in (cr, ci, dbbr, dbbi, dctr, dcti, dar, dai, dd_ref):
                r[...] = jnp.zeros_like(r)

        dyv = dy_ref[...]
        u = u_ref[...]
        pr = [apr[k:k + 1, :] for k in range(n)]
        pi_ = [api[k:k + 1, :] for k in range(n)]
        gr = _dg(dyv, ctr[...], "nt")
        gi = -_dg(dyv, cti[...], "nt")
        lr, li = _s5_scan_tile(gr, gi, pr, pi_, True)
        mr, mi = _cmul(tabr[...], tabi[...], cr[0:1, :], ci[0:1, :])
        lr, li = lr + mr, li + mi
        cr[0:1, :] = lr[0:1, :]
        ci[0:1, :] = li[0:1, :]
        du_ref[...] = (_dg(lr, bbr[...], "nt") + _dg(li, bbi[...], "nt") + d_ref[...] * dyv).astype(du_ref.dtype)
        dbbr[...] += _dg(u, lr, "tn")
        dbbi[...] += _dg(u, li, "tn")
        sr = sr_ref[...]
        si = si_ref[...]
        dctr[...] += _dg(sr, dyv, "tn")
        dcti[...] -= _dg(si, dyv, "tn")
        dd_ref[...] += jnp.broadcast_to(_colsum(dyv * u), dd_ref.shape)
        row = lax.broadcasted_iota(jnp.int32, sr.shape, 0)
        has_prev = (i < nt - 1).astype(F32)
        ssr = jnp.where(row == 0, pr_ref[7:8, :] * has_prev, pltpu.roll(sr, 1, 0))
        ssi = jnp.where(row == 0, pi_ref[7:8, :] * has_prev, pltpu.roll(si, 1, 0))
        dar[...] += jnp.broadcast_to(_colsum(lr * ssr + li * ssi), dar.shape)
        dai[...] += jnp.broadcast_to(_colsum(li * ssr - lr * ssi), dai.shape)

    rev = lambda j, i: (nt - 1 - i, j)
    prev = lambda j, i: (jnp.maximum((nt - 1 - i) * (t // 8) - 1, 0), j)
    blk3 = lambda a, b: pl.BlockSpec((None, a, b), lambda j, i: (j, 0, 0))
    return pl.pallas_call(
        body, name="s5_bwd", grid=(S5_BLK, nt),
        in_specs=[pl.BlockSpec((t, S5_BC), rev), pl.BlockSpec((t, S5_BC), rev),
                  pl.BlockSpec((t, S5_BS), rev), pl.BlockSpec((t, S5_BS), rev),
                  pl.BlockSpec((8, S5_BS), prev), pl.BlockSpec((8, S5_BS), prev),
                  blk3(S5_BC, S5_BS), blk3(S5_BC, S5_BS), blk3(S5_BS, S5_BC), blk3(S5_BS, S5_BC),
                  blk3(1, S5_BS), blk3(1, S5_BS), blk3(1, S5_BC)],
        out_specs=[pl.BlockSpec((t, S5_BC), rev),
                   blk3(S5_BC, S5_BS), blk3(S5_BC, S5_BS), blk3(S5_BS, S5_BC), blk3(S5_BS, S5_BC),
                   blk3(8, S5_BS), blk3(8, S5_BS), blk3(8, S5_BC)],
        out_shape=[jax.ShapeDtypeStruct((length, S5_WIDTH), MXU_DT),
                   jax.ShapeDtypeStruct((S5_BLK, S5_BC, S5_BS), F32), jax.ShapeDtypeStruct((S5_BLK, S5_BC, S5_BS), F32),
                   jax.ShapeDtypeStruct((S5_BLK, S5_BS, S5_BC), F32), jax.ShapeDtypeStruct((S5_BLK, S5_BS, S5_BC), F32),
                   jax.ShapeDtypeStruct((S5_BLK, 8, S5_BS), F32), jax.ShapeDtypeStruct((S5_BLK, 8, S5_BS), F32),
                   jax.ShapeDtypeStruct((S5_BLK, 8, S5_BC), F32)],
        scratch_shapes=[pltpu.VMEM((8, S5_BS), F32), pltpu.VMEM((8, S5_BS), F32),
                        pltpu.VMEM((t, S5_BS), F32), pltpu.VMEM((t, S5_BS), F32),
                        pltpu.VMEM((8, S5_BS), F32), pltpu.VMEM((8, S5_BS), F32)],
        compiler_params=_cparams("parallel", "arbitrary"),
    )(dy, p_s5, s_re, s_im, s_re, s_im, bb_re, bb_im, ct_re, ct_im, a_re, a_im, d_row)


def _blockdiag(m, rows, cols):
    m = m.reshape(S5_BLK, 8, rows, 1, cols)
    on_diag = jnp.eye(8, dtype=bool)[None, :, None, :, None]
    return jnp.where(on_diag, m, 0).reshape(S5_BLK, 8 * rows, 8 * cols)


def _blockdiag_t(m, rows, cols):
    m = m.reshape(S5_BLK, 8, rows, 8, cols)
    on_diag = jnp.eye(8, dtype=bool)[None, :, None, :, None]
    return jnp.sum(jnp.where(on_diag, m, 0), axis=3).reshape(S5_GROUPS, rows, cols)


def _gelu(y):
    return jax.nn.gelu(y)


def _s5_out_fn(y, zg, gate, b):
    return _gelu(y) * jax.nn.sigmoid(zg + b) * jax.nn.silu(gate)


HALO = 8


def _conv_fwd(xbc, w, b, tr=256, tc=1024):
    length, width = xbc.shape
    tr = min(tr, length)

    def body(x_ref, h_ref, w_ref, b_ref, cv_ref, act_ref):
        i = pl.program_id(1)
        x = x_ref[...]
        xx = jnp.concatenate([h_ref[...] * (i > 0).astype(F32), x], axis=0)
        acc = b_ref[...] + w_ref[3:4, :] * x
        for k in range(SSD_CONV - 1):
            acc = acc + w_ref[k:k + 1, :] * pltpu.roll(xx, SSD_CONV - 1 - k, 0)[HALO:, :]
        cv_ref[...] = acc
        act_ref[...] = jax.nn.silu(acc)

    main = pl.BlockSpec((tr, tc), lambda j, i: (i, j))
    before = pl.BlockSpec((HALO, tc), lambda j, i: (jnp.maximum(i * (tr // HALO) - 1, 0), j))
    return pl.pallas_call(
        body, name="ssd_conv_fwd", grid=(width // tc, length // tr),
        in_specs=[main, before, pl.BlockSpec((SSD_CONV, tc), lambda j, i: (0, j)),
                  pl.BlockSpec((1, tc), lambda j, i: (0, j))],
        out_specs=[main, main],
        out_shape=[jax.ShapeDtypeStruct((length, width), F32)] * 2,
        compiler_params=_cparams("parallel", "arbitrary"),
    )(xbc, xbc, w, b.reshape(1, -1))


def _conv_bwd(dact, cv, xbc, w, tr=256, tc=1024):
    length, width = xbc.shape
    tr = min(tr, length)
    nr = length // tr
    n = tr + HALO

    def dsilu(d, c):
        sg = jax.nn.sigmoid(c)
        return d * (sg * (1.0 + c * (1.0 - sg)))

    def body(da_ref, dan_ref, cv_ref, cvn_ref, x_ref, xp_ref, w_ref, dx_ref, dw_ref, db_ref):
        i = pl.program_id(1)

        @pl.when(i == 0)
        def _():
            dw_ref[...] = jnp.zeros_like(dw_ref)
            db_ref[...] = jnp.zeros_like(db_ref)

        dc = dsilu(da_ref[...], cv_ref[...])
        dcn = dsilu(dan_ref[...], cvn_ref[...]) * (i < nr - 1).astype(F32)
        dd = jnp.concatenate([dc, dcn], axis=0)
        x = x_ref[...]
        xx = jnp.concatenate([xp_ref[...] * (i > 0).astype(F32), x], axis=0)
        dx = w_ref[3:4, :] * dc
        dw_ref[3:4, :] += _colsum(dc * x)
        for k in range(SSD_CONV - 1):
            j = SSD_CONV - 1 - k
            dx = dx + w_ref[k:k + 1, :] * pltpu.roll(dd, n - j, 0)[:tr, :]
            dw_ref[k:k + 1, :] += _colsum(dc * pltpu.roll(xx, j, 0)[HALO:, :])
        dx_ref[...] = dx.astype(dx_ref.dtype)
        db_ref[...] += jnp.broadcast_to(_colsum(dc), db_ref.shape)

    main = pl.BlockSpec((tr, tc), lambda j, i: (i, j))
    before = pl.BlockSpec((HALO, tc), lambda j, i: (jnp.maximum(i * (tr // HALO) - 1, 0), j))
    after = pl.BlockSpec((HALO, tc), lambda j, i: (jnp.minimum((i + 1) * (tr // HALO), length // HALO - 1), j))
    acc = pl.BlockSpec((8, tc), lambda j, i: (0, j))
    return pl.pallas_call(
        body, name="ssd_conv_bwd", grid=(width // tc, nr),
        in_specs=[main, after, main, after, main, before, pl.BlockSpec((SSD_CONV, tc), lambda j, i: (0, j))],
        out_specs=[main, acc, acc],
        out_shape=[jax.ShapeDtypeStruct((length, width), MXU_DT),
                   jax.ShapeDtypeStruct((8, width), F32), jax.ShapeDtypeStruct((8, width), F32)],
        compiler_params=_cparams("parallel", "arbitrary"),
    )(dact, dact, cv, cv, xbc, xbc, w)


def _tri(lower):
    r = lax.broadcasted_iota(jnp.int32, (SSD_CHUNK, SSD_CHUNK), 0)
    c = lax.broadcasted_iota(jnp.int32, (SSD_CHUNK, SSD_CHUNK), 1)
    return ((r >= c) if lower else (r <= c)).astype(F32)


def _dt_fwd(raw, bias, a_log):
    length = raw.shape[0]
    nc = length // SSD_CHUNK

    def body(r_ref, b_ref, a_ref, dt_ref, cum_ref, cumt_ref):
        dt = jax.nn.softplus(r_ref[...] + b_ref[...])
        cum = _dot_exact(_tri(True), dt * (-jnp.exp(a_ref[...])))
        dt_ref[...] = dt
        cum_ref[...] = cum
        cumt_ref[...] = cum.T

    blk = pl.BlockSpec((SSD_CHUNK, LANES), lambda c: (c, 0))
    row = pl.BlockSpec((1, LANES), lambda c: (0, 0))
    return pl.pallas_call(
        body, name="ssd_dt_fwd", grid=(nc,), in_specs=[blk, row, row],
        out_specs=[blk, blk, pl.BlockSpec((None, LANES, SSD_CHUNK), lambda c: (c, 0, 0))],
        out_shape=[jax.ShapeDtypeStruct((length, LANES), F32)] * 2
        + [jax.ShapeDtypeStruct((nc, LANES, SSD_CHUNK), F32)],
        compiler_params=_cparams("parallel"),
    )(raw, bias, a_log)


def _dt_bwd(raw, bias, a_log, ddt, dcum):
    length = raw.shape[0]
    nc = length // SSD_CHUNK

    def body(r_ref, b_ref, a_ref, ddt_ref, dcum_ref, dr_ref, db_ref, da_ref):
        @pl.when(pl.program_id(0) == 0)
        def _():
            db_ref[...] = jnp.zeros_like(db_ref)
            da_ref[...] = jnp.zeros_like(da_ref)

        z = r_ref[...] + b_ref[...]
        a = -jnp.exp(a_ref[...])
        dla = _dot_exact(_tri(False), dcum_ref[...])
        draw = (ddt_ref[...] + dla * a) * jax.nn.sigmoid(z)
        dr_ref[...] = draw.astype(dr_ref.dtype)
        db_ref[...] += jnp.broadcast_to(_colsum(draw), db_ref.shape)
        da_ref[...] += jnp.broadcast_to(_colsum(dla * jax.nn.softplus(z)) * a, da_ref.shape)

    blk = pl.BlockSpec((SSD_CHUNK, LANES), lambda c: (c, 0))
    row = pl.BlockSpec((1, LANES), lambda c: (0, 0))
    acc = pl.BlockSpec((8, LANES), lambda c: (0, 0))
    return pl.pallas_call(
        body, name="ssd_dt_bwd", grid=(nc,), in_specs=[blk, row, row, blk, blk],
        out_specs=[blk, acc, acc],
        out_shape=[jax.ShapeDtypeStruct((length, LANES), MXU_DT),
                   jax.ShapeDtypeStruct((8, LANES), F32), jax.ShapeDtypeStruct((8, LANES), F32)],
        compiler_params=_cparams("arbitrary"),
    )(raw, bias, a_log, ddt, dcum)


def _ssd_head(x, dtc, cumc, cumr, cl, g, bm, cm, sp):
    xdt = x * dtc
    q = lax.broadcasted_iota(jnp.int32, g.shape, 0)
    k = lax.broadcasted_iota(jnp.int32, g.shape, 1)
    w = g * jnp.exp(jnp.where(q >= k, cumc - cumr, -1e30))
    y = _dot_nn(w, xdt) + _dot_nt(cm, sp) * jnp.exp(cumc)
    s_new = jnp.exp(cl) * sp + _dot_tn(xdt * jnp.exp(cl - cumc), bm)
    return y, s_new


SSD_GW = SSD_HPG * SSD_HEAD_DIM
SSD_B_OFF = SSD_WIDTH // SSD_STATE
SSD_C_OFF = SSD_B_OFF + SSD_GROUPS


def _ssd_core_specs(nc, rev):
    ch = (lambda c: nc - 1 - c) if rev else (lambda c: c)
    xs = pl.BlockSpec((SSD_CHUNK, SSD_GW), lambda g, c: (ch(c), g))
    bspec = pl.BlockSpec((SSD_CHUNK, SSD_STATE), lambda g, c: (ch(c), SSD_B_OFF + g))
    cspec = pl.BlockSpec((SSD_CHUNK, SSD_STATE), lambda g, c: (ch(c), SSD_C_OFF + g))
    lane = pl.BlockSpec((None, SSD_CHUNK, LANES), lambda g, c: (g, ch(c), 0))
    rows = pl.BlockSpec((None, None, 16, SSD_CHUNK), lambda g, c: (g, ch(c), 0, 0))
    st = pl.BlockSpec((None, None, SSD_HPG, SSD_HEAD_DIM, SSD_STATE), lambda g, c: (g, ch(c), 0, 0, 0))
    return xs, bspec, cspec, lane, rows, st


def _ssd_core_fwd(act, dtg, cumg, cumtg):
    length = act.shape[0]
    nc = length // SSD_CHUNK

    def body(x_ref, b_ref, c_ref, dt_ref, cum_ref, cumt_ref, y_ref, st_ref, s_scr):
        @pl.when(pl.program_id(1) == 0)
        def _():
            s_scr[...] = jnp.zeros_like(s_scr)

        bm = b_ref[...]
        cm = c_ref[...]
        g = _dot_nt(cm, bm)
        for r in range(SSD_HPG):
            cols = slice(r * SSD_HEAD_DIM, (r + 1) * SSD_HEAD_DIM)
            sp = s_scr[r]
            st_ref[r] = sp
            y, s_new = _ssd_head(x_ref[:, cols], dt_ref[:, r:r + 1], cum_ref[:, r:r + 1], cumt_ref[r:r + 1, :],
                                 cum_ref[SSD_CHUNK - 1:SSD_CHUNK, r:r + 1], g, bm, cm, sp)
            y_ref[:, cols] = y
            s_scr[r] = s_new

    xs, bspec, cspec, lane, rows, st = _ssd_core_specs(nc, False)
    return pl.pallas_call(
        body, name="ssd_core_fwd", grid=(SSD_GROUPS, nc),
        in_specs=[xs, bspec, cspec, lane, lane, rows], out_specs=[xs, st],
        out_shape=[jax.ShapeDtypeStruct((length, SSD_WIDTH), F32),
                   jax.ShapeDtypeStruct((SSD_GROUPS, nc, SSD_HPG, SSD_HEAD_DIM, SSD_STATE), F32)],
        scratch_shapes=[pltpu.VMEM((SSD_HPG, SSD_HEAD_DIM, SSD_STATE), F32)],
        compiler_params=_cparams("parallel", "arbitrary"),
    )(act, act, act, dtg, cumg, cumtg)


def _ssd_core_bwd(dy, dxs_add, act, dtg, cumg, cumtg, states):
    length = act.shape[0]
    nc = length // SSD_CHUNK

    def body(dy_ref, add_ref, x_ref, b_ref, c_ref, dt_ref, cum_ref, cumt_ref, st_ref,
             dx_ref, db_ref, dc_ref, ddt_ref, dcum_ref, dcumt_ref, dcl_ref, ds_scr):
        @pl.when(pl.program_id(1) == 0)
        def _():
            ds_scr[...] = jnp.zeros_like(ds_scr)

        for r_ in (ddt_ref, dcum_ref, dcumt_ref, dcl_ref):
            r_[...] = jnp.zeros_like(r_)
        bm = b_ref[...]
        cm = c_ref[...]
        g, g_vjp = jax.vjp(_dot_nt, cm, bm)
        dg = jnp.zeros_like(g)
        dbm = jnp.zeros_like(bm)
        dcm = jnp.zeros_like(cm)
        for r in range(SSD_HPG):
            cols = slice(r * SSD_HEAD_DIM, (r + 1) * SSD_HEAD_DIM)
            _, vjp = jax.vjp(_ssd_head, x_ref[:, cols], dt_ref[:, r:r + 1], cum_ref[:, r:r + 1],
                             cumt_ref[r:r + 1, :], cum_ref[SSD_CHUNK - 1:SSD_CHUNK, r:r + 1], g, bm, cm, st_ref[r])
            dx, ddtc, dcumc, dcumr, dcl, dg_r, dbm_r, dcm_r, dsp = vjp((dy_ref[:, cols], ds_scr[r]))
            dx_ref[:, cols] = dx + add_ref[:, cols]
            ddt_ref[:, r:r + 1] = ddtc
            dcum_ref[:, r:r + 1] = dcumc
            dcumt_ref[r:r + 1, :] = dcumr
            dcl_ref[0:1, r:r + 1] = dcl
            ds_scr[r] = dsp
            dg, dbm, dcm = dg + dg_r, dbm + dbm_r, dcm + dcm_r
        dcm_g, dbm_g = g_vjp(dg)
        db_ref[...] = dbm + dbm_g
        dc_ref[...] = dcm + dcm_g

    xs, bspec, cspec, lane, rows, st = _ssd_core_specs(nc, True)
    bc_out = pl.BlockSpec((SSD_CHUNK, SSD_STATE), lambda g, c: (nc - 1 - c, g))
    last = pl.BlockSpec((None, None, 8, LANES), lambda g, c: (g, nc - 1 - c, 0, 0))
    return pl.pallas_call(
        body, name="ssd_core_bwd", grid=(SSD_GROUPS, nc),
        in_specs=[xs, xs, xs, bspec, cspec, lane, lane, rows, st],
        out_specs=[xs, bc_out, bc_out, lane, lane, rows, last],
        out_shape=[jax.ShapeDtypeStruct((length, SSD_WIDTH), F32),
                   jax.ShapeDtypeStruct((length, SSD_GROUPS * SSD_STATE), F32),
                   jax.ShapeDtypeStruct((length, SSD_GROUPS * SSD_STATE), F32),
                   jax.ShapeDtypeStruct((SSD_GROUPS, length, LANES), F32),
                   jax.ShapeDtypeStruct((SSD_GROUPS, length, LANES), F32),
                   jax.ShapeDtypeStruct((SSD_GROUPS, nc, 16, SSD_CHUNK), F32),
                   jax.ShapeDtypeStruct((SSD_GROUPS, nc, 8, LANES), F32)],
        scratch_shapes=[pltpu.VMEM((SSD_HPG, SSD_HEAD_DIM, SSD_STATE), F32)],
        compiler_params=_cparams("parallel", "arbitrary"),
    )(dy, dxs_add, act, act, act, dtg, cumg, cumtg, states)


def _ssd_post_fn(yc, xs, z, dch, nw):
    y = (yc + dch * xs) * jax.nn.silu(z)
    return y * lax.rsqrt(jnp.mean(y * y, axis=-1, keepdims=True) + NORM_EPS) * nw


FOX_SCALE = 1.0 / math.sqrt(FOX_HEAD_DIM)
MASKED = -1e30


def _fgate_fwd(f_raw, b_f):
    length = f_raw.shape[0]
    nb = length // SSD_CHUNK

    def body(f_ref, b_ref, c_ref, carry):
        @pl.when(pl.program_id(0) == 0)
        def _():
            carry[...] = jnp.zeros_like(carry)

        c = _dot_exact(_tri(True), jax.nn.log_sigmoid(f_ref[...] + b_ref[...])) + carry[0:1, :]
        c_ref[...] = c
        carry[0:1, :] = c[SSD_CHUNK - 1:SSD_CHUNK, :]

    blk = pl.BlockSpec((SSD_CHUNK, LANES), lambda i: (i, 0))
    return pl.pallas_call(
        body, name="fox_fgate_fwd", grid=(nb,), in_specs=[blk, pl.BlockSpec((1, LANES), lambda i: (0, 0))],
        out_specs=blk, out_shape=jax.ShapeDtypeStruct((length, LANES), F32),
        scratch_shapes=[pltpu.VMEM((8, LANES), F32)], compiler_params=_cparams("arbitrary"),
    )(f_raw, b_f)


def _fgate_bwd(f_raw, b_f, dc):
    length = f_raw.shape[0]
    nb = length // SSD_CHUNK

    def body(f_ref, b_ref, dc_ref, df_ref, db_ref, carry):
        @pl.when(pl.program_id(0) == 0)
        def _():
            carry[...] = jnp.zeros_like(carry)
            db_ref[...] = jnp.zeros_like(db_ref)

        dcv = dc_ref[...]
        dlog = _dot_exact(_tri(False), dcv) + carry[0:1, :]
        carry[0:1, :] += _colsum(dcv)
        df = dlog * jax.nn.sigmoid(-(f_ref[...] + b_ref[...]))
        df_ref[...] = df.astype(df_ref.dtype)
        db_ref[...] += jnp.broadcast_to(_colsum(df), db_ref.shape)

    blk = pl.BlockSpec((SSD_CHUNK, LANES), lambda i: (nb - 1 - i, 0))
    return pl.pallas_call(
        body, name="fox_fgate_bwd", grid=(nb,),
        in_specs=[blk, pl.BlockSpec((1, LANES), lambda i: (0, 0)), blk],
        out_specs=[blk, pl.BlockSpec((8, LANES), lambda i: (0, 0))],
        out_shape=[jax.ShapeDtypeStruct((length, LANES), MXU_DT), jax.ShapeDtypeStruct((8, LANES), F32)],
        scratch_shapes=[pltpu.VMEM((8, LANES), F32)], compiler_params=_cparams("arbitrary"),
    )(f_raw, b_f, dc)


def _fox_scores(q, k, bias, diagonal):
    s = _dg(q, k, "nt") * FOX_SCALE + bias
    if diagonal:
        row = lax.broadcasted_iota(jnp.int32, s.shape, 0)
        col = lax.broadcasted_iota(jnp.int32, s.shape, 1)
        s = jnp.where(col <= row, s, MASKED)
    return s


FOX_TILE = 512


def _fox_c0(c, t):
    return jnp.repeat(jnp.repeat(c[::t, :FOX_HEADS], FOX_HEAD_DIM, axis=1), 8, axis=0)


def _fox_fwd(qkv, c0_rep, c_t, tile=FOX_TILE):
    length = qkv.shape[0]
    t = min(tile, length)
    nq = length // t

    def body(q_ref, k_ref, v_ref, c0_ref, ct_ref, o_ref, lse_ref):
        i = pl.program_id(1)
        q = q_ref[...]
        c0 = c0_ref[0:1, 0:1]

        def tile_step(k0, carry, diagonal):
            m, l, acc = carry
            s = _fox_scores(q, k_ref[pl.ds(k0, t), :], c0 - ct_ref[:, pl.ds(k0, t)], diagonal)
            m_new = jnp.maximum(m, jnp.max(s, axis=1, keepdims=True))
            p = jnp.exp(s - m_new)
            alpha = jnp.exp(m - m_new)
            return (m_new, alpha * l + jnp.sum(p, axis=1, keepdims=True),
                    alpha * acc + _dg(p, v_ref[pl.ds(k0, t), :], "nn"))

        init = (jnp.full((t, 1), MASKED, F32), jnp.zeros((t, 1), F32), jnp.zeros((t, FOX_HEAD_DIM), F32))
        carry = lax.fori_loop(0, i, lambda j, c: tile_step(pl.multiple_of(j * t, t), c, False), init)
        m, l, acc = tile_step(pl.multiple_of(i * t, t), carry, True)
        o_ref[...] = acc / l
        lse_ref[...] = jnp.broadcast_to(m + jnp.log(l), lse_ref.shape)

    qt = pl.BlockSpec((t, FOX_HEAD_DIM), lambda h, i: (i, h))
    return pl.pallas_call(
        body, name="fox_attn_fwd", grid=(FOX_HEADS, nq),
        in_specs=[qt,
                  pl.BlockSpec((length, FOX_HEAD_DIM), lambda h, i: (0, FOX_HEADS + h)),
                  pl.BlockSpec((length, FOX_HEAD_DIM), lambda h, i: (0, 2 * FOX_HEADS + h)),
                  pl.BlockSpec((8, FOX_HEAD_DIM), lambda h, i: (i, h)),
                  pl.BlockSpec((None, 1, length), lambda h, i: (h, 0, 0))],
        out_specs=[qt, qt],
        out_shape=[jax.ShapeDtypeStruct((length, FOX_WIDTH), F32)] * 2,
        compiler_params=_cparams("parallel", "arbitrary"),
    )(qkv, qkv, qkv, c0_rep, c_t)


def _fox_bwd(qkv, d_att, lse, delta, c0_rep, c_t, tile=FOX_TILE):
    length = qkv.shape[0]
    t = min(tile, length)
    nk = length // t

    def body(q_ref, k_ref, v_ref, do_ref, lse_ref, dl_ref, c0_ref, ct_ref,
             dq_ref, dk_ref, dv_ref, dcq_ref, dck_ref, dq_acc):
        j = pl.program_id(1)

        @pl.when(j == 0)
        def _():
            dq_acc[...] = jnp.zeros_like(dq_acc)
            dcq_ref[...] = jnp.zeros_like(dcq_ref)

        k = k_ref[...]
        v = v_ref[...]
        ck = ct_ref[...]

        def tile_step(i, carry, diagonal):
            dk, dv, dck = carry
            rows = pl.ds(pl.multiple_of(i * t, t), t)
            q = q_ref[rows, :]
            do = do_ref[rows, :]
            c0 = c0_ref[pl.ds(pl.multiple_of(i * 8, 8), 8), :][0:1, 0:1]
            s = _fox_scores(q, k, c0 - ck, diagonal)
            p = jnp.exp(s - lse_ref[rows, 0:1])
            dv = dv + _dg(p, do, "tn")
            ds = p * (_dg(do, v, "nt") - dl_ref[rows, 0:1])
            dk = dk + _dg(ds, q, "tn") * FOX_SCALE
            dq_acc[rows, :] += _dg(ds, k, "nn") * FOX_SCALE
            dcq_ref[rows, :] += jnp.broadcast_to(jnp.sum(ds, axis=1, keepdims=True), (t, FOX_HEAD_DIM))
            return dk, dv, dck + _colsum(ds)

        init = (jnp.zeros((t, FOX_HEAD_DIM), F32), jnp.zeros((t, FOX_HEAD_DIM), F32), jnp.zeros((1, t), F32))
        carry = tile_step(j, init, True)
        dk, dv, dck = lax.fori_loop(j + 1, nk, lambda i, c: tile_step(i, c, False), carry)
        dk_ref[...] = dk.astype(dk_ref.dtype)
        dv_ref[...] = dv.astype(dv_ref.dtype)
        dck_ref[...] = -dck

        @pl.when(j == nk - 1)
        def _():
            dq_ref[...] = dq_acc[...].astype(dq_ref.dtype)

    full = lambda off: pl.BlockSpec((length, FOX_HEAD_DIM), lambda h, j: (0, off + h))
    kt = lambda off: pl.BlockSpec((t, FOX_HEAD_DIM), lambda h, j: (j, off + h))
    ck_spec = pl.BlockSpec((None, 1, t), lambda h, j: (h, 0, j))
    return pl.pallas_call(
        body, name="fox_attn_bwd", grid=(FOX_HEADS, nk),
        in_specs=[full(0), kt(FOX_HEADS), kt(2 * FOX_HEADS), full(0), full(0), full(0),
                  pl.BlockSpec((8 * nk, FOX_HEAD_DIM), lambda h, j: (0, h)), ck_spec],
        out_specs=[full(0), kt(0), kt(0), full(0), ck_spec],
        out_shape=[jax.ShapeDtypeStruct((length, FOX_WIDTH), MXU_DT)] * 3
        + [jax.ShapeDtypeStruct((length, FOX_WIDTH), F32), jax.ShapeDtypeStruct((FOX_HEADS, 1, length), F32)],
        scratch_shapes=[pltpu.VMEM((length, FOX_HEAD_DIM), F32)],
        compiler_params=_cparams("parallel", "arbitrary"),
    )(qkv, qkv, qkv, d_att, lse, delta, c0_rep, c_t)


def _fox_gate_fn(att, gate):
    return att * jax.nn.silu(gate)


N_CHIP = 4
ANYSPACE = pl.BlockSpec(memory_space=pl.ANY)


def _other_chips(mx, my):
    return [(1 - mx, my), (mx, 1 - my), (1 - mx, 1 - my)]


def _handshake(peers):
    barrier = pltpu.get_barrier_semaphore()
    for peer in peers:
        pl.semaphore_signal(barrier, inc=1, device_id=peer, device_id_type=pl.DeviceIdType.MESH)
    pl.semaphore_wait(barrier, len(peers))


def _exchange_call(body, x, out_struct, name, n_sems, local_sem, collective_id):
    sems = [pltpu.SemaphoreType.DMA((n_sems,)), pltpu.SemaphoreType.DMA((n_sems,))]
    sems += [pltpu.SemaphoreType.DMA] if local_sem else []
    if collective_id is None:
        return pl.pallas_call(
            body, name=name, in_specs=[ANYSPACE], out_specs=ANYSPACE, out_shape=out_struct, scratch_shapes=sems,
            compiler_params=pltpu.CompilerParams(has_side_effects=True),
        )(x)
    x_ref = jax.new_ref(x, memory_space=pltpu.MemorySpace.HBM)
    o_ref = jax.empty_ref(out_struct, memory_space=pltpu.MemorySpace.HBM)

    @pl.kernel(mesh=plsc.ScalarSubcoreMesh(axis_name="sequencer", num_cores=1), name=name,
               scratch_types=tuple(sems), compiler_params=pltpu.CompilerParams(collective_id=collective_id))
    def launch(*sem_refs):
        body(x_ref, o_ref, *sem_refs)

    launch()
    return o_ref[...]


def _gather(x, name, collective_id=None):
    def body(x_ref, o_ref, send_sems, recv_sems, local_sem):
        mx, my, mc = lax.axis_index("x"), lax.axis_index("y"), lax.axis_index("c")
        me, sibling = (mx, my, mc), (mx, my, 1 - mc)
        chips = _other_chips(mx, my)
        if collective_id is not None:
            _handshake([sibling] + [(*chip, mc) for chip in chips])

        def slot(px, py, pc):
            return o_ref.at[4 * px + 2 * py + pc]

        def copy(k, block, to, src=None):
            return pltpu.make_async_remote_copy(
                src_ref=slot(*block) if src is None else src, dst_ref=slot(*block),
                send_sem=send_sems.at[k], recv_sem=recv_sems.at[k],
                device_id=to, device_id_type=pl.DeviceIdType.MESH)

        mine = pltpu.make_async_copy(x_ref, slot(*me), local_sem)
        mine.start()
        first = [copy(0, me, sibling, src=x_ref)]
        first += [copy(1 + j, me, (*chip, mc), src=x_ref) for j, chip in enumerate(chips)]
        for cp in first:
            cp.start()
        passed = [copy(4 + j, (*chip, mc), sibling) for j, chip in enumerate(chips)]
        for j, chip in enumerate(chips):
            copy(1 + j, (*chip, mc), me).wait_recv()
            passed[j].start()
        copy(0, sibling, me).wait_recv()
        for j, chip in enumerate(chips):
            copy(4 + j, (*chip, 1 - mc), me).wait_recv()
        for cp in first + passed:
            cp.wait_send()
        mine.wait()

    return _exchange_call(body, x, jax.ShapeDtypeStruct((N_DEV,) + x.shape, x.dtype), name, N_DEV - 1, True,
                          collective_id)


def _pair_send(parts, name, collective_id=None):
    def body(p_ref, o_ref, send_sems, recv_sems):
        mx, my, mc = lax.axis_index("x"), lax.axis_index("y"), lax.axis_index("c")
        if collective_id is not None:
            _handshake([(mx, my, 1 - mc)])
        copies = [pltpu.make_async_remote_copy(
            src_ref=p_ref.at[2 * chip + (1 - mc)], dst_ref=o_ref.at[chip],
            send_sem=send_sems.at[chip], recv_sem=recv_sems.at[chip],
            device_id=(mx, my, 1 - mc), device_id_type=pl.DeviceIdType.MESH) for chip in range(N_CHIP)]
        for cp in copies:
            cp.start()
        for cp in copies:
            cp.wait_recv()
        for cp in copies:
            cp.wait_send()

    return _exchange_call(body, parts, jax.ShapeDtypeStruct((N_CHIP,) + parts.shape[1:], parts.dtype), name,
                          N_CHIP, False, collective_id)


def _pair_sum(parts, recv, out_dtype, name):
    _, rows, cols = parts.shape
    tr, tc = _tile2d(rows, cols)

    def body(c_ref, p_ref, r_ref, o_ref):
        o_ref[...] = (p_ref[...] + r_ref[...]).astype(o_ref.dtype)

    return pl.pallas_call(
        body, name=name,
        grid_spec=pltpu.PrefetchScalarGridSpec(
            num_scalar_prefetch=1, grid=(N_CHIP, rows // tr, cols // tc),
            in_specs=[pl.BlockSpec((None, tr, tc), lambda k, i, j, c: (2 * k + c[0], i, j)),
                      pl.BlockSpec((None, tr, tc), lambda k, i, j, c: (k, i, j))],
            out_specs=pl.BlockSpec((None, tr, tc), lambda k, i, j, c: (k, i, j))),
        out_shape=jax.ShapeDtypeStruct((N_CHIP, rows, cols), out_dtype),
        compiler_params=_cparams("parallel", "parallel", "parallel"),
    )(lax.axis_index("c").astype(jnp.int32).reshape(1), parts, recv)


def _chip_exchange(sums, name, collective_id=None):
    def body(s_ref, o_ref, send_sems, recv_sems, local_sem):
        mx, my, mc = lax.axis_index("x"), lax.axis_index("y"), lax.axis_index("c")
        my_chip = 2 * mx + my
        if collective_id is not None:
            _handshake([(px, py, mc) for px, py in _other_chips(mx, my)])
        local = pltpu.make_async_copy(s_ref.at[my_chip], o_ref.at[my_chip], local_sem)
        local.start()
        sends, recvs = [], []
        for k, (px, py) in enumerate(_other_chips(mx, my)):
            peer = 2 * px + py

            def copy(src_slot, dst_slot, k=k, dev=(px, py, mc)):
                return pltpu.make_async_remote_copy(
                    src_ref=s_ref.at[src_slot], dst_ref=o_ref.at[dst_slot], send_sem=send_sems.at[k],
                    recv_sem=recv_sems.at[k], device_id=dev, device_id_type=pl.DeviceIdType.MESH)

            sends.append(copy(peer, my_chip))
            recvs.append(copy(peer, peer))
        for cp in sends:
            cp.start()
        for cp in recvs:
            cp.wait_recv()
        for cp in sends:
            cp.wait_send()
        local.wait()

    return _exchange_call(body, sums, jax.ShapeDtypeStruct(sums.shape, sums.dtype), name, N_CHIP - 1, True,
                          collective_id)


ADAM_TILE_ELEMS = 128 * 1024


def _tile2d(rows, cols):
    if rows * cols <= ADAM_TILE_ELEMS:
        return rows, cols
    if rows % 8 == 0:
        return _pick(rows, max(8, ADAM_TILE_ELEMS // cols), 8), cols
    return rows, _pick(cols, max(LANES, ADAM_TILE_ELEMS // rows))


def _adamw(w, parts, m, v, name):
    rows, cols = w.shape
    n_parts = parts.shape[0]
    tr, tc = _tile2d(rows, cols)

    def body(w_ref, p_ref, m_ref, v_ref, g_ref, d_ref, nm_ref, nv_ref):
        g = p_ref[0].astype(F32)
        for p in range(1, n_parts):
            g = g + p_ref[p].astype(F32)
        mm = ADAM_B1 * m_ref[...] + (1.0 - ADAM_B1) * g
        vv = ADAM_B2 * v_ref[...] + (1.0 - ADAM_B2) * jnp.square(g)
        m_hat = mm / (1.0 - ADAM_B1 ** ADAM_STEP)
        v_hat = vv / (1.0 - ADAM_B2 ** ADAM_STEP)
        g_ref[...] = g
        d_ref[...] = -ADAM_LR * (m_hat / (jnp.sqrt(v_hat) + ADAM_EPS) + ADAM_WD * w_ref[...])
        nm_ref[...] = mm
        nv_ref[...] = vv

    blk = pl.BlockSpec((tr, tc), lambda i, j: (i, j))
    return pl.pallas_call(
        body, name=name, grid=(rows // tr, cols // tc),
        in_specs=[blk, pl.BlockSpec((n_parts, tr, tc), lambda i, j: (0, i, j)), blk, blk],
        out_specs=[blk] * 4, out_shape=[jax.ShapeDtypeStruct((rows, cols), F32)] * 4,
        compiler_params=_cparams("parallel", "parallel"),
    )(w, parts, m, v)


WEIGHTS = ("l0_norm_w", "l0_w_in", "l0_s5_lambda_re", "l0_s5_lambda_im", "l0_s5_log_step", "l0_s5_b_re",
           "l0_s5_b_im", "l0_s5_c_re", "l0_s5_c_im", "l0_s5_d", "l0_s5_w_glu", "l0_s5_b_glu", "l0_ssd_conv_w",
           "l0_ssd_conv_b", "l0_ssd_dt_bias", "l0_ssd_a_log", "l0_ssd_d", "l0_ssd_norm_w", "l0_w_out",
           "l1_norm_w", "l1_w_in", "l1_fox_b_f", "l1_w_out", "final_norm_w")
SHARDED = ("l0_w_in", "l0_s5_w_glu", "l0_ssd_conv_w", "l0_w_out", "l1_w_in", "l1_w_out")


def _pad_lanes(a, width=LANES):
    return jnp.pad(a, [(0, 0)] * (a.ndim - 1) + [(0, width - a.shape[-1])])


def _pad_rows(a, height=LANES):
    return jnp.pad(a, [(0, height - a.shape[0])] + [(0, 0)] * (a.ndim - 1))


def _to_groups(a):
    length = a.shape[0]
    return _pad_lanes(a[:, :SSD_HEADS].reshape(length, SSD_GROUPS, SSD_HPG).transpose(1, 0, 2))


def _from_groups(a):
    length = a.shape[1]
    return a[:, :, :SSD_HPG].transpose(1, 0, 2).reshape(length, SSD_HEADS)


class _Grads(dict):
    def __init__(self, on_grad):
        super().__init__()
        self.on_grad = on_grad

    def __setitem__(self, name, value):
        if self.on_grad is not None and name in SHARDED:
            value = self.on_grad(name, value)
        super().__setitem__(name, value)


def _local_step(x, target, w, on_grad=None):
    length = x.shape[0]
    nc = length // SSD_CHUNK
    g = _Grads(on_grad)

    h0 = _rmsnorm_fwd(x, w["l0_norm_w"], "l0_norm")
    w0 = w["l0_w_in"]
    w0_s5, w0_z, w0_xbc = w0[:2 * S5_WIDTH], w0[2 * S5_WIDTH:2 * S5_WIDTH + SSD_WIDTH], w0[10240:18432]
    w0_dt = _pad_rows(w0[18432:])
    p_s5 = _mm(h0, w0_s5, "nt", F32, "l0_in_s5")
    p_z = _mm(h0, w0_z, "nt", F32, "l0_in_z")
    p_xbc = _mm(h0, w0_xbc, "nt", F32, "l0_in_xbc")
    p_dt = _mm(h0, w0_dt, "nt", F32, "l0_in_dt")

    row = lambda a: a.reshape(1, S5_NS)
    b_rows = lambda a: a.transpose(2, 0, 1).reshape(S5_GROUP, S5_NS)
    prep_in = (row(w["l0_s5_lambda_re"]), row(w["l0_s5_lambda_im"]),
               row(jnp.repeat(w["l0_s5_log_step"], S5_STATE)), b_rows(w["l0_s5_b_re"]), b_rows(w["l0_s5_b_im"]))
    ab_re, ab_im, bbr, bbi = _s5_prep(*prep_in)
    to_bb = lambda a: _blockdiag(a.reshape(S5_GROUP, S5_GROUPS, S5_STATE).transpose(1, 0, 2),
                                 S5_GROUP, S5_STATE).astype(MXU_DT)
    to_ct = lambda a: _blockdiag(a.transpose(0, 2, 1), S5_STATE, S5_GROUP).astype(MXU_DT)
    bb_re, bb_im = to_bb(bbr), to_bb(bbi)
    ct_re, ct_im = to_ct(w["l0_s5_c_re"]), to_ct(w["l0_s5_c_im"])
    a_re3, a_im3 = ab_re.reshape(S5_BLK, 1, S5_BS), ab_im.reshape(S5_BLK, 1, S5_BS)
    d3 = w["l0_s5_d"].reshape(S5_BLK, 1, S5_BC)
    y5, s_re, s_im = _s5_fwd(p_s5, bb_re, bb_im, ct_re, ct_im, a_re3, a_im3, d3)
    g_bf = _tiles("s5_gelu", lambda yb: (_gelu(yb),), [y5], [], [(S5_WIDTH, MXU_DT)], 0, tr=256)[0]
    zg = _mm(g_bf, w["l0_s5_w_glu"], "nn", F32, "s5_glu")
    b_glu = w["l0_s5_b_glu"].reshape(1, -1)
    s5_out = _tiles("s5_out", lambda yb, zb, gb, bb: (_s5_out_fn(yb, zb, gb, bb),),
                    [y5, zg, (p_s5, 1)], [b_glu], [(S5_WIDTH, MXU_DT)], 0, tr=256)[0]

    conv_w = w["l0_ssd_conv_w"]
    cv, act = _conv_fwd(p_xbc, conv_w, w["l0_ssd_conv_b"])
    bias_row = _pad_lanes(w["l0_ssd_dt_bias"].reshape(1, -1))
    alog_row = _pad_lanes(w["l0_ssd_a_log"].reshape(1, -1))
    dt, cum, cum_t = _dt_fwd(p_dt, bias_row, alog_row)
    dtg, cumg = _to_groups(dt), _to_groups(cum)
    cumtg = cum_t[:, :SSD_HEADS].reshape(nc, SSD_GROUPS, SSD_HPG, SSD_CHUNK).transpose(1, 0, 2, 3)
    cumtg = jnp.pad(cumtg, ((0, 0), (0, 0), (0, 16 - SSD_HPG), (0, 0)))
    ycore, states = _ssd_core_fwd(act, dtg, cumg, cumtg)
    dchan = jnp.repeat(w["l0_ssd_d"], SSD_HEAD_DIM).reshape(1, -1)
    nw_row = w["l0_ssd_norm_w"].reshape(1, -1)
    ssd_out = _tiles("ssd_post", lambda a, b, c, d, e: (_ssd_post_fn(a, b, c, d, e),),
                     [ycore, act, p_z], [dchan, nw_row], [(SSD_WIDTH, MXU_DT)], 0, tr=256, tc=SSD_GW)[0]
    mixed = jnp.concatenate([s5_out, ssd_out], axis=1)
    x1 = _mm(mixed, w["l0_w_out"], "nn", F32, "l0_out", res=x)

    h1 = _rmsnorm_fwd(x1, w["l1_norm_w"], "l1_norm")
    w1 = w["l1_w_in"]
    w1_qkv, w1_gate = w1[:3 * FOX_WIDTH], w1[3 * FOX_WIDTH:4 * FOX_WIDTH]
    w1_f = _pad_rows(w1[4 * FOX_WIDTH:])
    qkv = _mm(h1, w1_qkv, "nt", MXU_DT, "l1_in_qkv")
    gate1 = _mm(h1, w1_gate, "nt", F32, "l1_in_gate")
    f_raw = _mm(h1, w1_f, "nt", F32, "l1_in_f")
    bf_row = _pad_lanes(w["l1_fox_b_f"].reshape(1, -1))
    c = _fgate_fwd(f_raw, bf_row)
    c0_rep = _fox_c0(c, min(FOX_TILE, length))
    c_t = c[:, :FOX_HEADS].T.reshape(FOX_HEADS, 1, length)
    att, lse = _fox_fwd(qkv, c0_rep, c_t)
    out1 = _tiles("fox_gate", lambda a, b: (_fox_gate_fn(a, b),), [att, gate1], [],
                  [(FOX_WIDTH, MXU_DT)], 0, tr=256)[0]
    x2 = _mm(out1, w["l1_w_out"], "nn", F32, "l1_out", res=x1)

    loss_part, dx2, dx2b, g["final_norm_w"] = _final_loss(x2, w["final_norm_w"], target, "final_loss")

    d_out1 = _mm(dx2b, w["l1_w_out"], "nt", F32, "l1_out_dx")
    g["l1_w_out"] = _mm(out1, dx2b, "tn", F32, "l1_out_dw")

    def gate_bwd(a, gt, d):
        _, vjp = jax.vjp(_fox_gate_fn, a, gt)
        da, dgt = vjp(d)
        return da, dgt, jnp.broadcast_to(jnp.sum(da * a, axis=1, keepdims=True), a.shape)

    d_att, d_gate1, delta = _tiles("fox_gate_bwd", gate_bwd, [att, gate1, d_out1], [],
                                   [(FOX_WIDTH, MXU_DT), (FOX_WIDTH, MXU_DT), (FOX_WIDTH, F32)], 0,
                                   tr=512, tc=FOX_HEAD_DIM)
    dq, dk, dv, dcq, dck = _fox_bwd(qkv, d_att, lse, delta, c0_rep, c_t)
    dc = dcq.reshape(length, FOX_HEADS, FOX_HEAD_DIM)[:, :, 0] + dck.reshape(FOX_HEADS, length).T
    df, dbf = _fgate_bwd(f_raw, bf_row, _pad_lanes(dc))
    g["l1_fox_b_f"] = dbf[0, :FOX_HEADS]
    dqkv = jnp.concatenate([dq, dk, dv], axis=1)
    dh1 = _mm(dqkv, w1_qkv, "nn", F32, "l1_in_dx_qkv")
    dh1 = _mm(d_gate1, w1_gate, "nn", F32, "l1_in_dx_gate", res=dh1)
    dh1 = _mm(df, w1_f, "nn", F32, "l1_in_dx_f", res=dh1)
    g["l1_w_in"] = jnp.concatenate([
        _mm(dqkv, h1, "tn", F32, "l1_in_dw_qkv"), _mm(d_gate1, h1, "tn", F32, "l1_in_dw_gate"),
        _mm(df, h1, "tn", F32, "l1_in_dw_f")[:FOX_HEADS]], axis=0)
    dx1, dx1b, g["l1_norm_w"] = _rmsnorm_bwd(x1, w["l1_norm_w"], dh1, dx2, "l1_norm_bwd")

    wout0 = w["l0_w_out"]
    d_s5 = _mm(dx1b, wout0[:S5_WIDTH], "nt", F32, "l0_out_dx_s5")
    d_ssd = _mm(dx1b, wout0[S5_WIDTH:], "nt", F32, "l0_out_dx_ssd")
    g["l0_w_out"] = _mm(mixed, dx1b, "tn", F32, "l0_out_dw")

    def post_bwd(a, b, c_, d, dch, nw):
        _, vjp = jax.vjp(_ssd_post_fn, a, b, c_, dch, nw)
        return vjp(d)

    dycore, dxs_post, dz, ddch, dnw = _tiles(
        "ssd_post_bwd", post_bwd, [ycore, act, p_z, d_ssd], [dchan, nw_row],
        [(SSD_WIDTH, F32), (SSD_WIDTH, F32), (SSD_WIDTH, MXU_DT)], 2, tr=256, tc=SSD_GW)
    g["l0_ssd_d"] = ddch[0].reshape(SSD_HEADS, SSD_HEAD_DIM).sum(axis=1)
    g["l0_ssd_norm_w"] = dnw[0]
    dxs, d_b, d_c, ddtg, dcumg, dcumtg, dclg = _ssd_core_bwd(dycore, dxs_post, act, dtg, cumg, cumtg, states)
    dact = jnp.concatenate([dxs, d_b, d_c], axis=1)
    dxbc, dconvw, dconvb = _conv_bwd(dact, cv, p_xbc, conv_w)
    g["l0_ssd_conv_w"] = dconvw[:SSD_CONV]
    g["l0_ssd_conv_b"] = dconvb[0]
    dcum = _from_groups(dcumg)
    dcum = dcum + dcumtg[:, :, :SSD_HPG].transpose(1, 3, 0, 2).reshape(length, SSD_HEADS)
    dcl = dclg[:, :, 0, :SSD_HPG].transpose(1, 0, 2).reshape(nc, SSD_HEADS)
    dcum = dcum.reshape(nc, SSD_CHUNK, SSD_HEADS).at[:, SSD_CHUNK - 1, :].add(dcl).reshape(length, SSD_HEADS)
    ddt_raw, dbias, dalog = _dt_bwd(p_dt, bias_row, alog_row, _pad_lanes(_from_groups(ddtg)), _pad_lanes(dcum))
    g["l0_ssd_dt_bias"] = dbias[0, :SSD_HEADS]
    g["l0_ssd_a_log"] = dalog[0, :SSD_HEADS]

    def s5_out_bwd(yb, zb, gb, d, bb):
        _, vjp = jax.vjp(_s5_out_fn, yb, zb, gb, bb)
        return vjp(d)

    dy_direct, dzg, dgate0, dbglu = _tiles(
        "s5_out_bwd", s5_out_bwd, [y5, zg, (p_s5, 1), d_s5], [b_glu],
        [(S5_WIDTH, F32), (S5_WIDTH, MXU_DT), (S5_WIDTH, MXU_DT)], 1, tr=256)
    g["l0_s5_b_glu"] = dbglu[0]
    g["l0_s5_w_glu"] = _mm(g_bf, dzg, "tn", F32, "s5_glu_dw")
    dg2 = _mm(dzg, w["l0_s5_w_glu"], "nt", F32, "s5_glu_dx")

    def gelu_bwd(yb, d, direct):
        _, vjp = jax.vjp(_gelu, yb)
        return (vjp(d)[0] + direct,)

    dy5 = _tiles("s5_gelu_bwd", gelu_bwd, [y5, dg2, dy_direct], [], [(S5_WIDTH, F32)], 0, tr=256)[0]
    du, dbbr3, dbbi3, dctr3, dcti3, dar, dai, dd5 = _s5_bwd(dy5, p_s5, s_re, s_im, bb_re, bb_im, ct_re, ct_im,
                                                           a_re3, a_im3, d3)
    from_bb = lambda a: _blockdiag_t(a, S5_GROUP, S5_STATE).transpose(1, 0, 2).reshape(S5_GROUP, S5_NS)
    from_ct = lambda a: _blockdiag_t(a, S5_STATE, S5_GROUP).transpose(0, 2, 1)
    g["l0_s5_c_re"], g["l0_s5_c_im"] = from_ct(dctr3), from_ct(dcti3)
    g["l0_s5_d"] = dd5[:, 0, :].reshape(S5_GROUPS, S5_GROUP)
    dlr, dli, dls, dbr, dbi = _s5_prep_bwd(*prep_in, dar[:, 0, :].reshape(1, S5_NS), dai[:, 0, :].reshape(1, S5_NS),
                                           from_bb(dbbr3), from_bb(dbbi3))
    g["l0_s5_lambda_re"] = dlr.reshape(S5_GROUPS, S5_STATE)
    g["l0_s5_lambda_im"] = dli.reshape(S5_GROUPS, S5_STATE)
    g["l0_s5_log_step"] = dls.reshape(S5_GROUPS, S5_STATE).sum(axis=1)
    from_rows = lambda a: a.reshape(S5_GROUP, S5_GROUPS, S5_STATE).transpose(1, 2, 0)
    g["l0_s5_b_re"], g["l0_s5_b_im"] = from_rows(dbr), from_rows(dbi)

    dus = jnp.concatenate([du, dgate0], axis=1)
    g["l0_w_in"] = jnp.concatenate([
        _mm(dus, h0, "tn", F32, "l0_in_dw_s5"), _mm(dz, h0, "tn", F32, "l0_in_dw_z"),
        _mm(dxbc, h0, "tn", F32, "l0_in_dw_xbc"),
        _mm(ddt_raw, h0, "tn", F32, "l0_in_dw_dt")[:SSD_HEADS]], axis=0)
    dh0 = _mm(dus, w0_s5, "nn", F32, "l0_in_dx_s5")
    dh0 = _mm(dz, w0_z, "nn", F32, "l0_in_dx_z", res=dh0)
    dh0 = _mm(dxbc, w0_xbc, "nn", F32, "l0_in_dx_xbc", res=dh0)
    dh0 = _mm(ddt_raw, w0_dt, "nn", F32, "l0_in_dx_dt", res=dh0)
    grad_x, _, g["l0_norm_w"] = _rmsnorm_bwd(x, w["l0_norm_w"], dh0, dx1, "l0_norm_bwd")
    return loss_part, grad_x, g


TRANSPOSED = ("l0_w_in", "l1_w_in")


SEQUENCER_IDS = {"l0_w_out": (0, 1, 2), "l1_w_in": (3, 4, 5), "l1_w_out": (6, 7, 8), "l0_w_in": (None, 9, 10)}
NO_IDS = (None, None, None)


def _gather_weight(name, shard):
    cid = SEQUENCER_IDS.get(name, NO_IDS)[0]
    if name == "l0_ssd_conv_w":
        full = _gather(shard, "gather_" + name, cid)
        return full.transpose(1, 0, 2).reshape(shard.shape[0], N_DEV * shard.shape[1])
    if name in TRANSPOSED:
        full = _gather(shard.T.astype(MXU_DT), "gather_" + name, cid)
        return full.reshape(N_DEV * shard.shape[1], shard.shape[0])
    full = _gather(shard.astype(MXU_DT), "gather_" + name, cid)
    return full.reshape(N_DEV * shard.shape[0], shard.shape[1])


def _reduce_grad(name, grad, shard_shape):
    rows, cols = shard_shape
    if name == "l0_ssd_conv_w":
        parts = grad.reshape(rows, N_DEV, cols).transpose(1, 0, 2)
    elif name in TRANSPOSED:
        parts = grad.reshape(N_DEV, cols, rows)
    else:
        parts = grad.reshape(N_DEV, rows, cols)
    _, pair_id, chip_id = SEQUENCER_IDS.get(name, NO_IDS)
    recv = _pair_send(parts, "pair_" + name, pair_id)
    sums = _pair_sum(parts, recv, F32 if name == "l0_ssd_conv_w" else MXU_DT, "pairsum_" + name)
    return _chip_exchange(sums, "scatter_" + name, chip_id)


SMALL_ROWS_QUANTUM = 8 * LANES


def _step(args):
    x = args["x"][0]
    target = args["loss_target"][0]
    full = {n: args[n] for n in WEIGHTS if n not in SHARDED}
    for n in SHARDED:
        shard = args[n]
        if n != SHARDED[0]:
            shard, full[SHARDED[0]] = lax.optimization_barrier((shard, full[SHARDED[0]]))
        full[n] = _gather_weight(n, shard)
    out_g, out_d, out_m, out_v = {}, {}, {}, {}

    def reduce_and_update(n, grad):
        parts = _reduce_grad(n, grad, args[n].shape)
        view = (lambda a: a.T) if n in TRANSPOSED else (lambda a: a)
        outs = _adamw(view(args[n]), parts, view(args["m_" + n]), view(args["v_" + n]), "adamw_" + n)
        out_g[n], out_d[n], out_m[n], out_v[n] = [view(o) for o in outs]
        return out_g[n]

    loss_part, grad_x, g = _local_step(x, target, full, reduce_and_update)

    small = [n for n in WEIGHTS if n not in SHARDED]
    sizes = [int(math.prod(args[n].shape)) for n in small]
    total = sum(sizes) + 1
    padded = -(-total // SMALL_ROWS_QUANTUM) * SMALL_ROWS_QUANTUM

    def pack(pieces, extra):
        flat = jnp.concatenate([p.reshape(-1).astype(F32) for p in pieces] + [extra.reshape(1)])
        return jnp.pad(flat, (0, padded - total)).reshape(padded // LANES, LANES)

    zero = jnp.zeros((), F32)
    parts = _gather(pack([g[n] for n in small], loss_part), "gather_small_grads")
    sg, sd, sm, sv = _adamw(pack([args[n] for n in small], zero), parts,
                            pack([args["m_" + n] for n in small], zero),
                            pack([args["v_" + n] for n in small], zero), "adamw_small")
    off = 0
    for n, sz in zip(small, sizes):
        cut = lambda a: a.reshape(-1)[off:off + sz].reshape(args[n].shape)
        out_g[n], out_d[n], out_m[n], out_v[n] = cut(sg), cut(sd), cut(sm), cut(sv)
        off += sz
    loss = sg.reshape(-1)[total - 1]
    return (loss, grad_x[None], *[out_g[n] for n in WEIGHTS], *[out_d[n] for n in WEIGHTS],
            *[out_m[n] for n in WEIGHTS], *[out_v[n] for n in WEIGHTS])


def kernel(x, l0_norm_w, l0_w_in, l0_s5_lambda_re, l0_s5_lambda_im, l0_s5_log_step, l0_s5_b_re, l0_s5_b_im, l0_s5_c_re, l0_s5_c_im, l0_s5_d, l0_s5_w_glu, l0_s5_b_glu, l0_ssd_conv_w, l0_ssd_conv_b, l0_ssd_dt_bias, l0_ssd_a_log, l0_ssd_d, l0_ssd_norm_w, l0_w_out, l1_norm_w, l1_w_in, l1_fox_b_f, l1_w_out, final_norm_w, loss_target, m_l0_norm_w, m_l0_w_in, m_l0_s5_lambda_re, m_l0_s5_lambda_im, m_l0_s5_log_step, m_l0_s5_b_re, m_l0_s5_b_im, m_l0_s5_c_re, m_l0_s5_c_im, m_l0_s5_d, m_l0_s5_w_glu, m_l0_s5_b_glu, m_l0_ssd_conv_w, m_l0_ssd_conv_b, m_l0_ssd_dt_bias, m_l0_ssd_a_log, m_l0_ssd_d, m_l0_ssd_norm_w, m_l0_w_out, m_l1_norm_w, m_l1_w_in, m_l1_fox_b_f, m_l1_w_out, m_final_norm_w, v_l0_norm_w, v_l0_w_in, v_l0_s5_lambda_re, v_l0_s5_lambda_im, v_l0_s5_log_step, v_l0_s5_b_re, v_l0_s5_b_im, v_l0_s5_c_re, v_l0_s5_c_im, v_l0_s5_d, v_l0_s5_w_glu, v_l0_s5_b_glu, v_l0_ssd_conv_w, v_l0_ssd_conv_b, v_l0_ssd_dt_bias, v_l0_ssd_a_log, v_l0_ssd_d, v_l0_ssd_norm_w, v_l0_w_out, v_l1_norm_w, v_l1_w_in, v_l1_fox_b_f, v_l1_w_out, v_final_norm_w):
    return _step(dict(locals()))
```

```python
import functools
import math

import jax
import jax.numpy as jnp
from jax import lax
from jax.experimental import pallas as pl
from jax.experimental.pallas import tpu as pltpu
from jax.experimental.pallas import tpu_sc as plsc

F32 = jnp.float32
BF16 = jnp.bfloat16
MXU_DT = BF16

D_MODEL = 4096
S5_WIDTH = 2048
S5_GROUP = 16
S5_GROUPS = 128
S5_STATE = 64
S5_EIG_CLIP = -1e-4
S5_BLK = 16
SSD_WIDTH = 6144
SSD_HEAD_DIM = 64
SSD_HEADS = 96
SSD_GROUPS = 8
SSD_STATE = 128
SSD_CONV = 4
SSD_CHUNK = 128
SSD_XBC = 8192
SSD_HPG = SSD_HEADS // SSD_GROUPS
FOX_HEAD_DIM = 128
FOX_HEADS = 32
FOX_WIDTH = 4096
NORM_EPS = 1e-5
EVEN_IN = 18528
ODD_IN = 16416
EVEN_PAD = 18560
ODD_PAD = 16512
LANES = 128
N_DEV = 8

ADAM_LR = 0.001
ADAM_B1 = 0.9
ADAM_B2 = 0.999
ADAM_EPS = 1e-08
ADAM_WD = 0.01
ADAM_STEP = 10

VMEM_LIMIT_BYTES = 48 * 1024 * 1024


ANYSPACE = pl.BlockSpec(memory_space=pl.ANY)


def _cparams(*sem):
    return pltpu.CompilerParams(dimension_semantics=sem, vmem_limit_bytes=VMEM_LIMIT_BYTES)


def _pick(n, target, quantum=LANES):
    if n <= target:
        return n
    t = (target // quantum) * quantum
    while t >= quantum:
        if n % t == 0:
            return t
        t -= quantum
    raise ValueError((n, target, quantum))


_MM_DIMS = {"nn": ((1,), (0,)), "nt": ((1,), (1,)), "tn": ((0,), (0,))}


MM_VMEM_BUDGET = 38 * 1024 * 1024


def _mm_tk(k, tm, tn, out_bytes, has_res, tk_t, start=0):
    fixed = 2 * tm * tn * out_bytes + (2 * tm * tn * 4 if has_res else 0)
    tk = min(k, tk_t)
    while True:
        if k % tk == 0 and start % tk == 0 and (tk == k or tk % LANES == 0):
            need = fixed + 2 * (tm + tn) * tk * 2 + (tm * tn * 4 if tk < k else 0)
            if need <= MM_VMEM_BUDGET or tk <= LANES:
                return tk
        tk -= LANES if tk % LANES == 0 else tk % LANES


def _mm(a, b, mode, out_dtype, name, res=None, b_rows=None, into=None, tm_t=1024, tn_t=512, tk_t=8192):
    b_start, b_size = b_rows if b_rows is not None else (0, b.shape[0])
    if mode == "nn":
        (m, k), (k2, n) = a.shape, (b_size, b.shape[1])
    elif mode == "nt":
        (m, k), (n, k2) = a.shape, (b_size, b.shape[1])
    else:
        (k, m), (k2, n) = a.shape, (b_size, b.shape[1])
    assert k == k2, (a.shape, b.shape, mode)
    tm, tn = _pick(m, tm_t), _pick(n, tn_t)
    has_res = res is not None
    has_into = into is not None
    tk = _mm_tk(k, tm, tn, jnp.dtype(out_dtype).itemsize, has_res, tk_t, b_start if mode == "nn" else 0)
    nk = k // tk
    dims = (_MM_DIMS[mode], ((), ()))
    o_row = 0
    if has_into:
        assert into[1] % tm == 0 and into[0].shape[1] == n and into[0].dtype == out_dtype, (into[1], tm)
        o_row = into[1] // tm
    if mode == "nt":
        assert b_start % tn == 0, (b_start, tn)
    b_blk = b_start // (tn if mode == "nt" else tk)

    def body(*refs):
        a_ref, b_ref = refs[:2]
        r_ref = refs[2] if has_res else None
        o_ref = refs[2 + has_res + has_into]
        part = lax.dot_general(a_ref[...].astype(MXU_DT), b_ref[...].astype(MXU_DT), dims,
                               preferred_element_type=F32)

        def finish(r):
            if has_res:
                r = r + r_ref[...]
            o_ref[...] = r.astype(out_dtype)

        if nk == 1:
            finish(part)
            return
        acc = refs[-1]
        kk = pl.program_id(2)

        @pl.when(kk == 0)
        def _():
            acc[...] = part

        @pl.when(jnp.logical_and(kk > 0, kk < nk - 1))
        def _():
            acc[...] += part

        @pl.when(kk == nk - 1)
        def _():
            finish(acc[...] + part)

    a_spec = (pl.BlockSpec((tk, tm), lambda i, j, kk: (kk, i)) if mode == "tn"
              else pl.BlockSpec((tm, tk), lambda i, j, kk: (i, kk)))
    b_spec = (pl.BlockSpec((tn, tk), lambda i, j, kk: (j + b_blk, kk)) if mode == "nt"
              else pl.BlockSpec((tk, tn), lambda i, j, kk: (kk + b_blk, j)))
    r_spec = pl.BlockSpec((tm, tn), lambda i, j, kk: (i, j))
    o_spec = pl.BlockSpec((tm, tn), lambda i, j, kk: (i + o_row, j))
    in_specs = [a_spec, b_spec] + ([r_spec] if has_res else []) + ([ANYSPACE] if has_into else [])
    args = (a, b) + ((res,) if has_res else ()) + ((into[0],) if has_into else ())
    return pl.pallas_call(
        body, name=name, grid=(m // tm, n // tn, nk), in_specs=in_specs, out_specs=o_spec,
        out_shape=jax.ShapeDtypeStruct(into[0].shape if has_into else (m, n), out_dtype),
        scratch_shapes=[pltpu.VMEM((tm, tn), F32)] if nk > 1 else [],
        input_output_aliases={len(args) - 1: 0} if has_into else {},
        compiler_params=_cparams("parallel", "parallel", "arbitrary"),
    )(*args)


def _tiles(name, fn, tiled, rows, out_tiled, out_acc, tr, tc=None):
    tiled = [t if isinstance(t, tuple) else (t, 0) for t in tiled]
    length = tiled[0][0].shape[0]
    width = out_tiled[0][0] if out_tiled else rows[0].shape[1]
    tc = width if tc is None else tc
    tr = min(tr, length)
    n_in = len(tiled) + len(rows)
    n_ot = len(out_tiled)

    def body(*refs):
        outs = fn(*[r[...] for r in refs[:n_in]])
        outs_t, outs_a = outs[:n_ot], outs[n_ot:]
        for r, v in zip(refs[n_in:n_in + n_ot], outs_t):
            r[...] = v.astype(r.dtype)
        i = pl.program_id(1)
        for r, v in zip(refs[n_in + n_ot:], outs_a):
            @pl.when(i == 0)
            def _(r=r):
                r[...] = jnp.zeros_like(r)

            r[...] += jnp.broadcast_to(v, r.shape)

    def tspec(off):
        return pl.BlockSpec((tr, tc), lambda j, i: (i, j + off))

    in_specs = [tspec(off) for _, off in tiled] + [pl.BlockSpec((1, tc), lambda j, i: (0, j)) for _ in rows]
    out_specs = [tspec(0) for _ in out_tiled] + [pl.BlockSpec((8, tc), lambda j, i: (0, j)) for _ in range(out_acc)]
    out_shape = ([jax.ShapeDtypeStruct((length, w), dt) for w, dt in out_tiled]
                 + [jax.ShapeDtypeStruct((8, width), F32) for _ in range(out_acc)])
    return pl.pallas_call(
        body, name=name, grid=(width // tc, length // tr), in_specs=in_specs, out_specs=out_specs,
        out_shape=out_shape, compiler_params=_cparams("parallel", "arbitrary"),
    )(*[t for t, _ in tiled], *rows)


def _rms(x, w):
    return x * lax.rsqrt(jnp.mean(x * x, axis=-1, keepdims=True) + NORM_EPS) * w


def _colsum(v):
    return jnp.sum(v, axis=0, keepdims=True)


def _rmsnorm_fwd(x, w, name):
    def fn(xb, wb):
        return (_rms(xb, wb),)
    return _tiles(name, fn, [x], [w.reshape(1, -1)], [(x.shape[1], MXU_DT)], 0, tr=256)[0]


def _rmsnorm_bwd(x, w, dh, dres, name):
    def fn(xb, db, rb, wb):
        _, vjp = jax.vjp(_rms, xb, wb)
        dx, dw = vjp(db)
        return dx + rb, dx + rb, dw
    dx, dxb, dw = _tiles(name, fn, [x, dh, dres], [w.reshape(1, -1)],
                         [(x.shape[1], F32), (x.shape[1], MXU_DT)], 1, tr=256)
    return dx, dxb, dw[0]


def _final_loss(x, w, target, name):
    def fn(xb, tb, wb):
        def f(xv, wv):
            e = _rms(xv, wv) - tb
            return 0.5 * jnp.sum(jnp.mean(e * e, axis=-1, keepdims=True), axis=0, keepdims=True)
        lv, vjp = jax.vjp(f, xb, wb)
        dx, dw = vjp(jnp.ones_like(lv))
        return dx, dx, dw, jnp.broadcast_to(lv, (1, xb.shape[1]))
    dx, dxb, dw, lv = _tiles(name, fn, [x, target], [w.reshape(1, -1)],
                             [(x.shape[1], F32), (x.shape[1], MXU_DT)], 2, tr=256)
    return lv[0, 0], dx, dxb, dw[0]


def _dg(a, b, mode):
    return lax.dot_general(a.astype(MXU_DT), b.astype(MXU_DT), (_MM_DIMS[mode], ((), ())),
                           preferred_element_type=F32)


@jax.custom_vjp
def _dot_nn(a, b):
    return _dg(a, b, "nn")


@jax.custom_vjp
def _dot_nt(a, b):
    return _dg(a, b, "nt")


@jax.custom_vjp
def _dot_tn(a, b):
    return _dg(a, b, "tn")


_dot_nn.defvjp(lambda a, b: (_dg(a, b, "nn"), (a, b)),
               lambda r, g: (_dg(g, r[1], "nt"), _dg(r[0], g, "tn")))
_dot_nt.defvjp(lambda a, b: (_dg(a, b, "nt"), (a, b)),
               lambda r, g: (_dg(g, r[1], "nn"), _dg(g, r[0], "tn")))
_dot_tn.defvjp(lambda a, b: (_dg(a, b, "tn"), (a, b)),
               lambda r, g: (_dg(r[1], g, "nt"), _dg(r[0], g, "nn")))


def _dot_exact(a, b):
    return jnp.dot(a, b, precision=lax.Precision.HIGHEST, preferred_element_type=F32)


S5_NS = S5_GROUPS * S5_STATE
S5_BS = S5_NS // S5_BLK
S5_BC = S5_WIDTH // S5_BLK


def _s5_disc(lr_raw, li, ls, br, bi):
    lr = jnp.minimum(lr_raw, S5_EIG_CLIP)
    step = jnp.exp(ls)
    mag = jnp.exp(lr * step)
    ab_re = mag * jnp.cos(li * step)
    ab_im = mag * jnp.sin(li * step)
    denom = lr * lr + li * li
    nr = ab_re - 1.0
    ni = ab_im
    coef_re = (nr * lr + ni * li) / denom
    coef_im = (ni * lr - nr * li) / denom
    return ab_re, ab_im, coef_re * br - coef_im * bi, coef_re * bi + coef_im * br


def _s5_prep(lr_raw, li, ls, br, bi):
    shapes = [jax.ShapeDtypeStruct((1, S5_NS), F32)] * 2 + [jax.ShapeDtypeStruct((S5_GROUP, S5_NS), F32)] * 2

    def body(a, b, c, d, e, o1, o2, o3, o4):
        for r, v in zip((o1, o2, o3, o4), _s5_disc(a[...], b[...], c[...], d[...], e[...])):
            r[...] = v

    return pl.pallas_call(body, name="s5_prep", out_shape=shapes)(lr_raw, li, ls, br, bi)


def _s5_prep_bwd(lr_raw, li, ls, br, bi, d_are, d_aim, d_bbre, d_bbim):
    shapes = [jax.ShapeDtypeStruct((1, S5_NS), F32)] * 3 + [jax.ShapeDtypeStruct((S5_GROUP, S5_NS), F32)] * 2

    def body(a, b, c, d, e, g1, g2, g3, g4, o1, o2, o3, o4, o5):
        _, vjp = jax.vjp(_s5_disc, a[...], b[...], c[...], d[...], e[...])
        for r, v in zip((o1, o2, o3, o4, o5), vjp((g1[...], g2[...], g3[...], g4[...]))):
            r[...] = v

    return pl.pallas_call(body, name="s5_prep_bwd", out_shape=shapes)(
        lr_raw, li, ls, br, bi, d_are, d_aim, d_bbre, d_bbim)


def _cmul(ar, ai, br, bi):
    return ar * br - ai * bi, ar * bi + ai * br


def _s5_powers(ar, ai, n):
    pr, pi_ = [ar], [ai]
    for _ in range(n - 1):
        r, i = _cmul(pr[-1], pi_[-1], pr[-1], pi_[-1])
        pr.append(r)
        pi_.append(i)
    return pr, pi_


def _s5_scan_tile(xr, xi, pr, pi_, reverse):
    t = xr.shape[0]
    row = lax.broadcasted_iota(jnp.int32, xr.shape, 0)
    sr, si = xr, xi
    for k in range(len(pr)):
        d = 1 << k
        shift = (t - d) if reverse else d
        keep = (row < t - d) if reverse else (row >= d)
        qr = jnp.where(keep, pltpu.roll(sr, shift, 0), 0.0)
        qi = jnp.where(keep, pltpu.roll(si, shift, 0), 0.0)
        mr, mi = _cmul(pr[k], pi_[k], qr, qi)
        sr, si = sr + mr, si + mi
    return sr, si


def _s5_setup(ar, ai, t, reverse, apr, api, tabr, tabi):
    n = int(math.log2(t))
    pr, pi_ = _s5_powers(ar, ai, n)
    for k in range(n):
        apr[k:k + 1, :] = pr[k]
        api[k:k + 1, :] = pi_[k]
    row = lax.broadcasted_iota(jnp.int32, (t, ar.shape[1]), 0)
    first = (row == t - 1) if reverse else (row == 0)
    xr = jnp.where(first, ar, 0.0)
    xi = jnp.where(first, ai, 0.0)
    sr, si = _s5_scan_tile(xr, xi, pr, pi_, reverse)
    tabr[...] = sr
    tabi[...] = si


def _s5_fwd(p_s5, bb_re, bb_im, ct_re, ct_im, a_re, a_im, d_row, t_tile=256):
    length = p_s5.shape[0]
    t = min(t_tile, length)
    nt = length // t
    n = int(math.log2(t))

    def body(u_ref, bbr, bbi, ctr, cti, ar_ref, ai_ref, d_ref, y_ref, sr_ref, si_ref,
             apr, api, tabr, tabi, cr, ci):
        i = pl.program_id(1)

        @pl.when(i == 0)
        def _():
            _s5_setup(ar_ref[...], ai_ref[...], t, False, apr, api, tabr, tabi)
            cr[...] = jnp.zeros_like(cr)
            ci[...] = jnp.zeros_like(ci)

        u = u_ref[...]
        pr = [apr[k:k + 1, :] for k in range(n)]
        pi_ = [api[k:k + 1, :] for k in range(n)]
        sr, si = _s5_scan_tile(_dg(u, bbr[...], "nn"), _dg(u, bbi[...], "nn"), pr, pi_, False)
        mr, mi = _cmul(tabr[...], tabi[...], cr[0:1, :], ci[0:1, :])
        sr, si = sr + mr, si + mi
        cr[0:1, :] = sr[t - 1:t, :]
        ci[0:1, :] = si[t - 1:t, :]
        sr_ref[...] = sr
        si_ref[...] = si
        y_ref[...] = _dg(sr, ctr[...], "nn") - _dg(si, cti[...], "nn") + d_ref[...] * u

    blk3 = lambda a, b: pl.BlockSpec((None, a, b), lambda j, i: (j, 0, 0))
    return pl.pallas_call(
        body, name="s5_fwd", grid=(S5_BLK, nt),
        in_specs=[pl.BlockSpec((t, S5_BC), lambda j, i: (i, j)),
                  blk3(S5_BC, S5_BS), blk3(S5_BC, S5_BS), blk3(S5_BS, S5_BC), blk3(S5_BS, S5_BC),
                  blk3(1, S5_BS), blk3(1, S5_BS), blk3(1, S5_BC)],
        out_specs=[pl.BlockSpec((t, S5_BC), lambda j, i: (i, j)),
                   pl.BlockSpec((t, S5_BS), lambda j, i: (i, j)),
                   pl.BlockSpec((t, S5_BS), lambda j, i: (i, j))],
        out_shape=[jax.ShapeDtypeStruct((length, S5_WIDTH), F32),
                   jax.ShapeDtypeStruct((length, S5_NS), F32),
                   jax.ShapeDtypeStruct((length, S5_NS), F32)],
        scratch_shapes=[pltpu.VMEM((8, S5_BS), F32), pltpu.VMEM((8, S5_BS), F32),
                        pltpu.VMEM((t, S5_BS), F32), pltpu.VMEM((t, S5_BS), F32),
                        pltpu.VMEM((8, S5_BS), F32), pltpu.VMEM((8, S5_BS), F32)],
        compiler_params=_cparams("parallel", "arbitrary"),
    )(p_s5, bb_re, bb_im, ct_re, ct_im, a_re, a_im, d_row)


def _s5_bwd(dy, p_s5, s_re, s_im, bb_re, bb_im, ct_re, ct_im, a_re, a_im, d_row, t_tile=256):
    length = p_s5.shape[0]
    t = min(t_tile, length)
    nt = length // t
    n = int(math.log2(t))

    def body(dy_ref, u_ref, sr_ref, si_ref, pr_ref, pi_ref, bbr, bbi, ctr, cti, ar_ref, ai_ref, d_ref,
             du_ref, dbbr, dbbi, dctr, dcti, dar, dai, dd_ref, apr, api, tabr, tabi, cr, ci):
        i = pl.program_id(1)

        @pl.when(i == 0)
        def _():
            _s5_setup(ar_ref[...], -ai_ref[...], t, True, apr, api, tabr, tabi)
            for r in (cr, ci, dbbr, dbbi, dctr, dcti, dar, dai, dd_ref):
                r[...] = jnp.zeros_like(r)

        dyv = dy_ref[...]
        u = u_ref[...]
        pr = [apr[k:k + 1, :] for k in range(n)]
        pi_ = [api[k:k + 1, :] for k in range(n)]
        gr = _dg(dyv, ctr[...], "nt")
        gi = -_dg(dyv, cti[...], "nt")
        lr, li = _s5_scan_tile(gr, gi, pr, pi_, True)
        mr, mi = _cmul(tabr[...], tabi[...], cr[0:1, :], ci[0:1, :])
        lr, li = lr + mr, li + mi
        cr[0:1, :] = lr[0:1, :]
        ci[0:1, :] = li[0:1, :]
        du_ref[...] = (_dg(lr, bbr[...], "nt") + _dg(li, bbi[...], "nt") + d_ref[...] * dyv).astype(du_ref.dtype)
        dbbr[...] += _dg(u, lr, "tn")
        dbbi[...] += _dg(u, li, "tn")
        sr = sr_ref[...]
        si = si_ref[...]
        dctr[...] += _dg(sr, dyv, "tn")
        dcti[...] -= _dg(si, dyv, "tn")
        dd_ref[...] += jnp.broadcast_to(_colsum(dyv * u), dd_ref.shape)
        row = lax.broadcasted_iota(jnp.int32, sr.shape, 0)
        has_prev = (i < nt - 1).astype(F32)
        ssr = jnp.where(row == 0, pr_ref[7:8, :] * has_prev, pltpu.roll(sr, 1, 0))
        ssi = jnp.where(row == 0, pi_ref[7:8, :] * has_prev, pltpu.roll(si, 1, 0))
        dar[...] += jnp.broadcast_to(_colsum(lr * ssr + li * ssi), dar.shape)
        dai[...] += jnp.broadcast_to(_colsum(li * ssr - lr * ssi), dai.shape)

    rev = lambda j, i: (nt - 1 - i, j)
    prev = lambda j, i: (jnp.maximum((nt - 1 - i) * (t // 8) - 1, 0), j)
    blk3 = lambda a, b: pl.BlockSpec((None, a, b), lambda j, i: (j, 0, 0))
    return pl.pallas_call(
        body, name="s5_bwd", grid=(S5_BLK, nt),
        in_specs=[pl.BlockSpec((t, S5_BC), rev), pl.BlockSpec((t, S5_BC), rev),
                  pl.BlockSpec((t, S5_BS), rev), pl.BlockSpec((t, S5_BS), rev),
                  pl.BlockSpec((8, S5_BS), prev), pl.BlockSpec((8, S5_BS), prev),
                  blk3(S5_BC, S5_BS), blk3(S5_BC, S5_BS), blk3(S5_BS, S5_BC), blk3(S5_BS, S5_BC),
                  blk3(1, S5_BS), blk3(1, S5_BS), blk3(1, S5_BC)],
        out_specs=[pl.BlockSpec((t, S5_BC), rev),
                   blk3(S5_BC, S5_BS), blk3(S5_BC, S5_BS), blk3(S5_BS, S5_BC), blk3(S5_BS, S5_BC),
                   blk3(8, S5_BS), blk3(8, S5_BS), blk3(8, S5_BC)],
        out_shape=[jax.ShapeDtypeStruct((length, S5_WIDTH), MXU_DT),
                   jax.ShapeDtypeStruct((S5_BLK, S5_BC, S5_BS), F32), jax.ShapeDtypeStruct((S5_BLK, S5_BC, S5_BS), F32),
                   jax.ShapeDtypeStruct((S5_BLK, S5_BS, S5_BC), F32), jax.ShapeDtypeStruct((S5_BLK, S5_BS, S5_BC), F32),
                   jax.ShapeDtypeStruct((S5_BLK, 8, S5_BS), F32), jax.ShapeDtypeStruct((S5_BLK, 8, S5_BS), F32),
                   jax.ShapeDtypeStruct((S5_BLK, 8, S5_BC), F32)],
        scratch_shapes=[pltpu.VMEM((8, S5_BS), F32), pltpu.VMEM((8, S5_BS), F32),
                        pltpu.VMEM((t, S5_BS), F32), pltpu.VMEM((t, S5_BS), F32),
                        pltpu.VMEM((8, S5_BS), F32), pltpu.VMEM((8, S5_BS), F32)],
        compiler_params=_cparams("parallel", "arbitrary"),
    )(dy, p_s5, s_re, s_im, s_re, s_im, bb_re, bb_im, ct_re, ct_im, a_re, a_im, d_row)


def _blockdiag(m, rows, cols):
    m = m.reshape(S5_BLK, 8, rows, 1, cols)
    on_diag = jnp.eye(8, dtype=bool)[None, :, None, :, None]
    return jnp.where(on_diag, m, 0).reshape(S5_BLK, 8 * rows, 8 * cols)


def _blockdiag_t(m, rows, cols):
    m = m.reshape(S5_BLK, 8, rows, 8, cols)
    on_diag = jnp.eye(8, dtype=bool)[None, :, None, :, None]
    return jnp.sum(jnp.where(on_diag, m, 0), axis=3).reshape(S5_GROUPS, rows, cols)


def _gelu(y):
    return jax.nn.gelu(y)


def _s5_out_fn(y, zg, gate, b):
    return _gelu(y) * jax.nn.sigmoid(zg + b) * jax.nn.silu(gate)


HALO = 8


def _conv_fwd(xbc, w, b, tr=256, tc=1024):
    length, width = xbc.shape
    tr = min(tr, length)

    def body(x_ref, h_ref, w_ref, b_ref, cv_ref, act_ref):
        i = pl.program_id(1)
        x = x_ref[...]
        xx = jnp.concatenate([h_ref[...] * (i > 0).astype(F32), x], axis=0)
        acc = b_ref[...] + w_ref[3:4, :] * x
        for k in range(SSD_CONV - 1):
            acc = acc + w_ref[k:k + 1, :] * pltpu.roll(xx, SSD_CONV - 1 - k, 0)[HALO:, :]
        cv_ref[...] = acc
        act_ref[...] = jax.nn.silu(acc)

    main = pl.BlockSpec((tr, tc), lambda j, i: (i, j))
    before = pl.BlockSpec((HALO, tc), lambda j, i: (jnp.maximum(i * (tr // HALO) - 1, 0), j))
    return pl.pallas_call(
        body, name="ssd_conv_fwd", grid=(width // tc, length // tr),
        in_specs=[main, before, pl.BlockSpec((SSD_CONV, tc), lambda j, i: (0, j)),
                  pl.BlockSpec((1, tc), lambda j, i: (0, j))],
        out_specs=[main, main],
        out_shape=[jax.ShapeDtypeStruct((length, width), F32)] * 2,
        compiler_params=_cparams("parallel", "arbitrary"),
    )(xbc, xbc, w, b.reshape(1, -1))


def _conv_bwd(dact, cv, xbc, w, tr=256, tc=1024):
    length, width = xbc.shape
    tr = min(tr, length)
    nr = length // tr
    n = tr + HALO

    def dsilu(d, c):
        sg = jax.nn.sigmoid(c)
        return d * (sg * (1.0 + c * (1.0 - sg)))

    def body(da_ref, dan_ref, cv_ref, cvn_ref, x_ref, xp_ref, w_ref, dx_ref, dw_ref, db_ref):
        i = pl.program_id(1)

        @pl.when(i == 0)
        def _():
            dw_ref[...] = jnp.zeros_like(dw_ref)
            db_ref[...] = jnp.zeros_like(db_ref)

        dc = dsilu(da_ref[...], cv_ref[...])
        dcn = dsilu(dan_ref[...], cvn_ref[...]) * (i < nr - 1).astype(F32)
        dd = jnp.concatenate([dc, dcn], axis=0)
        x = x_ref[...]
        xx = jnp.concatenate([xp_ref[...] * (i > 0).astype(F32), x], axis=0)
        dx = w_ref[3:4, :] * dc
        dw_ref[3:4, :] += _colsum(dc * x)
        for k in range(SSD_CONV - 1):
            j = SSD_CONV - 1 - k
            dx = dx + w_ref[k:k + 1, :] * pltpu.roll(dd, n - j, 0)[:tr, :]
            dw_ref[k:k + 1, :] += _colsum(dc * pltpu.roll(xx, j, 0)[HALO:, :])
        dx_ref[...] = dx.astype(dx_ref.dtype)
        db_ref[...] += jnp.broadcast_to(_colsum(dc), db_ref.shape)

    main = pl.BlockSpec((tr, tc), lambda j, i: (i, j))
    before = pl.BlockSpec((HALO, tc), lambda j, i: (jnp.maximum(i * (tr // HALO) - 1, 0), j))
    after = pl.BlockSpec((HALO, tc), lambda j, i: (jnp.minimum((i + 1) * (tr // HALO), length // HALO - 1), j))
    acc = pl.BlockSpec((8, tc), lambda j, i: (0, j))
    return pl.pallas_call(
        body, name="ssd_conv_bwd", grid=(width // tc, nr),
        in_specs=[main, after, main, after, main, before, pl.BlockSpec((SSD_CONV, tc), lambda j, i: (0, j))],
        out_specs=[main, acc, acc],
        out_shape=[jax.ShapeDtypeStruct((length, width), MXU_DT),
                   jax.ShapeDtypeStruct((8, width), F32), jax.ShapeDtypeStruct((8, width), F32)],
        compiler_params=_cparams("parallel", "arbitrary"),
    )(dact, dact, cv, cv, xbc, xbc, w)


def _tri(lower):
    r = lax.broadcasted_iota(jnp.int32, (SSD_CHUNK, SSD_CHUNK), 0)
    c = lax.broadcasted_iota(jnp.int32, (SSD_CHUNK, SSD_CHUNK), 1)
    return ((r >= c) if lower else (r <= c)).astype(F32)


def _dt_fwd(raw, bias, a_log):
    length = raw.shape[0]
    nc = length // SSD_CHUNK

    def body(r_ref, b_ref, a_ref, dt_ref, cum_ref, cumt_ref):
        dt = jax.nn.softplus(r_ref[...] + b_ref[...])
        cum = _dot_exact(_tri(True), dt * (-jnp.exp(a_ref[...])))
        dt_ref[...] = dt
        cum_ref[...] = cum
        cumt_ref[...] = cum.T

    blk = pl.BlockSpec((SSD_CHUNK, LANES), lambda c: (c, 0))
    row = pl.BlockSpec((1, LANES), lambda c: (0, 0))
    return pl.pallas_call(
        body, name="ssd_dt_fwd", grid=(nc,), in_specs=[blk, row, row],
        out_specs=[blk, blk, pl.BlockSpec((None, LANES, SSD_CHUNK), lambda c: (c, 0, 0))],
        out_shape=[jax.ShapeDtypeStruct((length, LANES), F32)] * 2
        + [jax.ShapeDtypeStruct((nc, LANES, SSD_CHUNK), F32)],
        compiler_params=_cparams("parallel"),
    )(raw, bias, a_log)


def _dt_bwd(raw, bias, a_log, ddt, dcum):
    length = raw.shape[0]
    nc = length // SSD_CHUNK

    def body(r_ref, b_ref, a_ref, ddt_ref, dcum_ref, dr_ref, db_ref, da_ref):
        @pl.when(pl.program_id(0) == 0)
        def _():
            db_ref[...] = jnp.zeros_like(db_ref)
            da_ref[...] = jnp.zeros_like(da_ref)

        z = r_ref[...] + b_ref[...]
        a = -jnp.exp(a_ref[...])
        dla = _dot_exact(_tri(False), dcum_ref[...])
        draw = (ddt_ref[...] + dla * a) * jax.nn.sigmoid(z)
        dr_ref[...] = draw.astype(dr_ref.dtype)
        db_ref[...] += jnp.broadcast_to(_colsum(draw), db_ref.shape)
        da_ref[...] += jnp.broadcast_to(_colsum(dla * jax.nn.softplus(z)) * a, da_ref.shape)

    blk = pl.BlockSpec((SSD_CHUNK, LANES), lambda c: (c, 0))
    row = pl.BlockSpec((1, LANES), lambda c: (0, 0))
    acc = pl.BlockSpec((8, LANES), lambda c: (0, 0))
    return pl.pallas_call(
        body, name="ssd_dt_bwd", grid=(nc,), in_specs=[blk, row, row, blk, blk],
        out_specs=[blk, acc, acc],
        out_shape=[jax.ShapeDtypeStruct((length, LANES), MXU_DT),
                   jax.ShapeDtypeStruct((8, LANES), F32), jax.ShapeDtypeStruct((8, LANES), F32)],
        compiler_params=_cparams("arbitrary"),
    )(raw, bias, a_log, ddt, dcum)


SSD_GW = SSD_HPG * SSD_HEAD_DIM
SSD_B_OFF = SSD_WIDTH // SSD_STATE
SSD_C_OFF = SSD_B_OFF + SSD_GROUPS


def _ssd_expand():
    r = lax.broadcasted_iota(jnp.int32, (LANES, SSD_GW), 0)
    c = lax.broadcasted_iota(jnp.int32, (LANES, SSD_GW), 1)
    return (c // SSD_HEAD_DIM == r).astype(F32)


def _ssd_to_channels(v16, e):
    return _dot_exact(v16, e)


def _ssd_to_heads(v, e):
    return lax.dot_general(v, e, (((1,), (1,)), ((), ())), precision=lax.Precision.HIGHEST,
                           preferred_element_type=F32)


def _ssd_decay(cum_ref, cumt_ref, r):
    q = lax.broadcasted_iota(jnp.int32, (SSD_CHUNK, SSD_CHUNK), 0)
    k = lax.broadcasted_iota(jnp.int32, (SSD_CHUNK, SSD_CHUNK), 1)
    return jnp.exp(jnp.where(q >= k, cum_ref[:, r:r + 1] - cumt_ref[r:r + 1, :], -1e30))


def _ssd_core_specs(nc, rev):
    ch = (lambda c: nc - 1 - c) if rev else (lambda c: c)
    xs = pl.BlockSpec((SSD_CHUNK, SSD_GW), lambda g, c: (ch(c), g))
    bspec = pl.BlockSpec((SSD_CHUNK, SSD_STATE), lambda g, c: (ch(c), SSD_B_OFF + g))
    cspec = pl.BlockSpec((SSD_CHUNK, SSD_STATE), lambda g, c: (ch(c), SSD_C_OFF + g))
    lane = pl.BlockSpec((None, SSD_CHUNK, LANES), lambda g, c: (g, ch(c), 0))
    rows = pl.BlockSpec((None, None, 16, SSD_CHUNK), lambda g, c: (g, ch(c), 0, 0))
    st = pl.BlockSpec((None, None, SSD_STATE, SSD_GW), lambda g, c: (g, ch(c), 0, 0))
    return xs, bspec, cspec, lane, rows, st


def _ssd_core_fwd(act, dtg, cumg, cumtg):
    length = act.shape[0]
    nc = length // SSD_CHUNK

    def body(x_ref, b_ref, c_ref, dt_ref, cum_ref, cumt_ref, y_ref, st_ref, s_scr):
        @pl.when(pl.program_id(1) == 0)
        def _():
            s_scr[...] = jnp.zeros_like(s_scr)

        e = _ssd_expand()
        bm = b_ref[...]
        cm = c_ref[...]
        cum_e = _ssd_to_channels(cum_ref[...], e)
        cl_e = cum_e[SSD_CHUNK - 1:SSD_CHUNK, :]
        xdt = x_ref[...] * _ssd_to_channels(dt_ref[...], e)
        st = s_scr[...]
        st_ref[...] = st
        g = _dg(cm, bm, "nt")
        y_off = _dg(cm, st, "nn") * jnp.exp(cum_e)
        for r in range(SSD_HPG):
            cols = slice(r * SSD_HEAD_DIM, (r + 1) * SSD_HEAD_DIM)
            y_ref[:, cols] = _dg(g * _ssd_decay(cum_ref, cumt_ref, r), xdt[:, cols], "nn") + y_off[:, cols]
        s_scr[...] = jnp.exp(cl_e) * st + _dg(bm, xdt * jnp.exp(cl_e - cum_e), "tn")

    xs, bspec, cspec, lane, rows, st = _ssd_core_specs(nc, False)
    return pl.pallas_call(
        body, name="ssd_core_fwd", grid=(SSD_GROUPS, nc),
        in_specs=[xs, bspec, cspec, lane, lane, rows], out_specs=[xs, st],
        out_shape=[jax.ShapeDtypeStruct((length, SSD_WIDTH), F32),
                   jax.ShapeDtypeStruct((SSD_GROUPS, nc, SSD_STATE, SSD_GW), F32)],
        scratch_shapes=[pltpu.VMEM((SSD_STATE, SSD_GW), F32)],
        compiler_params=_cparams("parallel", "arbitrary"),
    )(act, act, act, dtg, cumg, cumtg)


def _ssd_core_bwd(dy, dxs_add, act, dtg, cumg, cumtg, states):
    length = act.shape[0]
    nc = length // SSD_CHUNK

    def body(dy_ref, add_ref, x_ref, b_ref, c_ref, dt_ref, cum_ref, cumt_ref, st_ref,
             dx_ref, db_ref, dc_ref, ddt_ref, dcum_ref, dcumt_ref, dcl_ref, ds_scr, dxdt_scr):
        @pl.when(pl.program_id(1) == 0)
        def _():
            ds_scr[...] = jnp.zeros_like(ds_scr)

        e = _ssd_expand()
        bm = b_ref[...]
        cm = c_ref[...]
        x = x_ref[...]
        dyv = dy_ref[...]
        st = st_ref[...]
        dst_new = ds_scr[...]
        dt_e = _ssd_to_channels(dt_ref[...], e)
        cum_e = _ssd_to_channels(cum_ref[...], e)
        cl_e = cum_e[SSD_CHUNK - 1:SSD_CHUNK, :]
        xdt = x * dt_e
        exp_cum = jnp.exp(cum_e)
        exp_cl = jnp.exp(cl_e)
        z = xdt * jnp.exp(cl_e - cum_e)
        d_cs = dyv * exp_cum
        dcum_e = d_cs * _dg(cm, st, "nn")
        dcm = _dg(d_cs, st, "nt")
        ds_scr[...] = _dg(cm, d_cs, "tn") + exp_cl * dst_new
        dcl_e = _colsum(dst_new * st) * exp_cl
        dbm = _dg(z, dst_new, "nt")
        dz = _dg(bm, dst_new, "nn")
        de = dz * z
        dcl_e = dcl_e + _colsum(de)
        dcum_e = dcum_e - de
        dxdt_scr[...] = dz * jnp.exp(cl_e - cum_e)
        g = _dg(cm, bm, "nt")
        dg = jnp.zeros_like(g)
        dcum_ref[...] = _ssd_to_heads(dcum_e, e)
        dcumt_ref[...] = jnp.zeros_like(dcumt_ref)
        for r in range(SSD_HPG):
            cols = slice(r * SSD_HEAD_DIM, (r + 1) * SSD_HEAD_DIM)
            decay = _ssd_decay(cum_ref, cumt_ref, r)
            w = g * decay
            dy_r = dyv[:, cols]
            dw = _dg(dy_r, xdt[:, cols], "nt")
            dxdt_scr[:, cols] += _dg(w, dy_r, "tn")
            dg = dg + dw * decay
            dseg = dw * w
            dcum_ref[:, r:r + 1] += jnp.sum(dseg, axis=1, keepdims=True)
            dcumt_ref[r:r + 1, :] = -_colsum(dseg)
        dc_ref[...] = dcm + _dg(dg, bm, "nn")
        db_ref[...] = dbm + _dg(dg, cm, "tn")
        dxdt = dxdt_scr[...]
        dx_ref[...] = dxdt * dt_e + add_ref[...]
        ddt_ref[...] = _ssd_to_heads(dxdt * x, e)
        dcl_ref[...] = _ssd_to_heads(jnp.broadcast_to(dcl_e, (8, SSD_GW)), e)

    xs, bspec, cspec, lane, rows, st = _ssd_core_specs(nc, True)
    bc_out = pl.BlockSpec((SSD_CHUNK, SSD_STATE), lambda g, c: (nc - 1 - c, g))
    last = pl.BlockSpec((None, None, 8, LANES), lambda g, c: (g, nc - 1 - c, 0, 0))
    return pl.pallas_call(
        body, name="ssd_core_bwd", grid=(SSD_GROUPS, nc),
        in_specs=[xs, xs, xs, bspec, cspec, lane, lane, rows, st],
        out_specs=[xs, bc_out, bc_out, lane, lane, rows, last],
        out_shape=[jax.ShapeDtypeStruct((length, SSD_WIDTH), F32),
                   jax.ShapeDtypeStruct((length, SSD_GROUPS * SSD_STATE), F32),
                   jax.ShapeDtypeStruct((length, SSD_GROUPS * SSD_STATE), F32),
                   jax.ShapeDtypeStruct((SSD_GROUPS, length, LANES), F32),
                   jax.ShapeDtypeStruct((SSD_GROUPS, length, LANES), F32),
                   jax.ShapeDtypeStruct((SSD_GROUPS, nc, 16, SSD_CHUNK), F32),
                   jax.ShapeDtypeStruct((SSD_GROUPS, nc, 8, LANES), F32)],
        scratch_shapes=[pltpu.VMEM((SSD_STATE, SSD_GW), F32), pltpu.VMEM((SSD_CHUNK, SSD_GW), F32)],
        compiler_params=_cparams("parallel", "arbitrary"),
    )(dy, dxs_add, act, act, act, dtg, cumg, cumtg, states)


def _ssd_post_fn(yc, xs, z, dch, nw):
    y = (yc + dch * xs) * jax.nn.silu(z)
    return y * lax.rsqrt(jnp.mean(y * y, axis=-1, keepdims=True) + NORM_EPS) * nw


FOX_SCALE = 1.0 / math.sqrt(FOX_HEAD_DIM)
MASKED = -1e30


def _fgate_fwd(f_raw, b_f):
    length = f_raw.shape[0]
    nb = length // SSD_CHUNK

    def body(f_ref, b_ref, c_ref, carry):
        @pl.when(pl.program_id(0) == 0)
        def _():
            carry[...] = jnp.zeros_like(carry)

        c = _dot_exact(_tri(True), jax.nn.log_sigmoid(f_ref[...] + b_ref[...])) + carry[0:1, :]
        c_ref[...] = c
        carry[0:1, :] = c[SSD_CHUNK - 1:SSD_CHUNK, :]

    blk = pl.BlockSpec((SSD_CHUNK, LANES), lambda i: (i, 0))
    return pl.pallas_call(
        body, name="fox_fgate_fwd", grid=(nb,), in_specs=[blk, pl.BlockSpec((1, LANES), lambda i: (0, 0))],
        out_specs=blk, out_shape=jax.ShapeDtypeStruct((length, LANES), F32),
        scratch_shapes=[pltpu.VMEM((8, LANES), F32)], compiler_params=_cparams("arbitrary"),
    )(f_raw, b_f)


def _fgate_bwd(f_raw, b_f, dc):
    length = f_raw.shape[0]
    nb = length // SSD_CHUNK

    def body(f_ref, b_ref, dc_ref, df_ref, db_ref, carry):
        @pl.when(pl.program_id(0) == 0)
        def _():
            carry[...] = jnp.zeros_like(carry)
            db_ref[...] = jnp.zeros_like(db_ref)

        dcv = dc_ref[...]
        dlog = _dot_exact(_tri(False), dcv) + carry[0:1, :]
        carry[0:1, :] += _colsum(dcv)
        df = dlog * jax.nn.sigmoid(-(f_ref[...] + b_ref[...]))
        df_ref[...] = df.astype(df_ref.dtype)
        db_ref[...] += jnp.broadcast_to(_colsum(df), db_ref.shape)

    blk = pl.BlockSpec((SSD_CHUNK, LANES), lambda i: (nb - 1 - i, 0))
    return pl.pallas_call(
        body, name="fox_fgate_bwd", grid=(nb,),
        in_specs=[blk, pl.BlockSpec((1, LANES), lambda i: (0, 0)), blk],
        out_specs=[blk, pl.BlockSpec((8, LANES), lambda i: (0, 0))],
        out_shape=[jax.ShapeDtypeStruct((length, LANES), MXU_DT), jax.ShapeDtypeStruct((8, LANES), F32)],
        scratch_shapes=[pltpu.VMEM((8, LANES), F32)], compiler_params=_cparams("arbitrary"),
    )(f_raw, b_f, dc)


def _fox_scores(q, k, bias, diagonal):
    s = _dg(q, k, "nt") * FOX_SCALE + bias
    if diagonal:
        row = lax.broadcasted_iota(jnp.int32, s.shape, 0)
        col = lax.broadcasted_iota(jnp.int32, s.shape, 1)
        s = jnp.where(col <= row, s, MASKED)
    return s


FOX_TILE = 512


def _fox_c0(c, t):
    return jnp.repeat(jnp.repeat(c[::t, :FOX_HEADS], FOX_HEAD_DIM, axis=1), 8, axis=0)


def _fox_fwd(qkv, c0_rep, c_t, tile=FOX_TILE):
    length = qkv.shape[0]
    t = min(tile, length)
    nq = length // t

    def body(q_ref, k_ref, v_ref, c0_ref, ct_ref, o_ref, lse_ref):
        i = pl.program_id(1)
        q = q_ref[...]
        c0 = c0_ref[0:1, 0:1]

        def tile_step(k0, carry, diagonal):
            m, l, acc = carry
            s = _fox_scores(q, k_ref[pl.ds(k0, t), :], c0 - ct_ref[:, pl.ds(k0, t)], diagonal)
            m_new = jnp.maximum(m, jnp.max(s, axis=1, keepdims=True))
            p = jnp.exp(s - m_new)
            alpha = jnp.exp(m - m_new)
            return (m_new, alpha * l + jnp.sum(p, axis=1, keepdims=True),
                    alpha * acc + _dg(p, v_ref[pl.ds(k0, t), :], "nn"))

        init = (jnp.full((t, 1), MASKED, F32), jnp.zeros((t, 1), F32), jnp.zeros((t, FOX_HEAD_DIM), F32))
        carry = lax.fori_loop(0, i, lambda j, c: tile_step(pl.multiple_of(j * t, t), c, False), init)
        m, l, acc = tile_step(pl.multiple_of(i * t, t), carry, True)
        o_ref[...] = acc / l
        lse_ref[...] = jnp.broadcast_to(m + jnp.log(l), lse_ref.shape)

    qt = pl.BlockSpec((t, FOX_HEAD_DIM), lambda h, i: (i, h))
    return pl.pallas_call(
        body, name="fox_attn_fwd", grid=(FOX_HEADS, nq),
        in_specs=[qt,
                  pl.BlockSpec((length, FOX_HEAD_DIM), lambda h, i: (0, FOX_HEADS + h)),
                  pl.BlockSpec((length, FOX_HEAD_DIM), lambda h, i: (0, 2 * FOX_HEADS + h)),
                  pl.BlockSpec((8, FOX_HEAD_DIM), lambda h, i: (i, h)),
                  pl.BlockSpec((None, 1, length), lambda h, i: (h, 0, 0))],
        out_specs=[qt, qt],
        out_shape=[jax.ShapeDtypeStruct((length, FOX_WIDTH), F32)] * 2,
        compiler_params=_cparams("parallel", "arbitrary"),
    )(qkv, qkv, qkv, c0_rep, c_t)


def _fox_bwd(qkv, d_att, lse, delta, c0_rep, c_t, tile=FOX_TILE):
    length = qkv.shape[0]
    t = min(tile, length)
    nk = length // t

    def body(q_ref, k_ref, v_ref, do_ref, lse_ref, dl_ref, c0_ref, ct_ref,
             dq_ref, dk_ref, dv_ref, dcq_ref, dck_ref, dq_acc):
        j = pl.program_id(1)

        @pl.when(j == 0)
        def _():
            dq_acc[...] = jnp.zeros_like(dq_acc)
            dcq_ref[...] = jnp.zeros_like(dcq_ref)

        k = k_ref[...]
        v = v_ref[...]
        ck = ct_ref[...]

        def tile_step(i, carry, diagonal):
            dk, dv, dck = carry
            rows = pl.ds(pl.multiple_of(i * t, t), t)
            q = q_ref[rows, :]
            do = do_ref[rows, :]
            c0 = c0_ref[pl.ds(pl.multiple_of(i * 8, 8), 8), :][0:1, 0:1]
            s = _fox_scores(q, k, c0 - ck, diagonal)
            p = jnp.exp(s - lse_ref[rows, 0:1])
            dv = dv + _dg(p, do, "tn")
            ds = p * (_dg(do, v, "nt") - dl_ref[rows, 0:1])
            dk = dk + _dg(ds, q, "tn") * FOX_SCALE
            dq_acc[rows, :] += _dg(ds, k, "nn") * FOX_SCALE
            dcq_ref[rows, :] += jnp.broadcast_to(jnp.sum(ds, axis=1, keepdims=True), (t, FOX_HEAD_DIM))
            return dk, dv, dck + _colsum(ds)

        init = (jnp.zeros((t, FOX_HEAD_DIM), F32), jnp.zeros((t, FOX_HEAD_DIM), F32), jnp.zeros((1, t), F32))
        carry = tile_step(j, init, True)
        dk, dv, dck = lax.fori_loop(j + 1, nk, lambda i, c: tile_step(i, c, False), carry)
        dk_ref[...] = dk.astype(dk_ref.dtype)
        dv_ref[...] = dv.astype(dv_ref.dtype)
        dck_ref[...] = -dck

        @pl.when(j == nk - 1)
        def _():
            dq_ref[...] = dq_acc[...].astype(dq_ref.dtype)

    full = lambda off: pl.BlockSpec((length, FOX_HEAD_DIM), lambda h, j: (0, off + h))
    kt = lambda off: pl.BlockSpec((t, FOX_HEAD_DIM), lambda h, j: (j, off + h))
    ck_spec = pl.BlockSpec((None, 1, t), lambda h, j: (h, 0, j))
    return pl.pallas_call(
        body, name="fox_attn_bwd", grid=(FOX_HEADS, nk),
        in_specs=[full(0), kt(FOX_HEADS), kt(2 * FOX_HEADS), full(0), full(0), full(0),
                  pl.BlockSpec((8 * nk, FOX_HEAD_DIM), lambda h, j: (0, h)), ck_spec],
        out_specs=[full(0), kt(0), kt(0), full(0), ck_spec],
        out_shape=[jax.ShapeDtypeStruct((length, FOX_WIDTH), MXU_DT)] * 3
        + [jax.ShapeDtypeStruct((length, FOX_WIDTH), F32), jax.ShapeDtypeStruct((FOX_HEADS, 1, length), F32)],
        scratch_shapes=[pltpu.VMEM((length, FOX_HEAD_DIM), F32)],
        compiler_params=_cparams("parallel", "arbitrary"),
    )(qkv, qkv, qkv, d_att, lse, delta, c0_rep, c_t)


def _fox_gate_fn(att, gate):
    return att * jax.nn.silu(gate)


N_CHIP = 4


def _other_chips(mx, my):
    return [(1 - mx, my), (mx, 1 - my), (1 - mx, 1 - my)]


def _handshake(peers):
    barrier = pltpu.get_barrier_semaphore()
    for peer in peers:
        pl.semaphore_signal(barrier, inc=1, device_id=peer, device_id_type=pl.DeviceIdType.MESH)
    pl.semaphore_wait(barrier, len(peers))


def _exchange_call(body, x, out_struct, name, n_sems, local_sem, collective_id):
    sems = [pltpu.SemaphoreType.DMA((n_sems,)), pltpu.SemaphoreType.DMA((n_sems,))]
    sems += [pltpu.SemaphoreType.DMA] if local_sem else []
    if collective_id is None:
        return pl.pallas_call(
            body, name=name, in_specs=[ANYSPACE], out_specs=ANYSPACE, out_shape=out_struct, scratch_shapes=sems,
            compiler_params=pltpu.CompilerParams(has_side_effects=True),
        )(x)
    x_ref = jax.new_ref(x, memory_space=pltpu.MemorySpace.HBM)
    o_ref = jax.empty_ref(out_struct, memory_space=pltpu.MemorySpace.HBM)

    @pl.kernel(mesh=plsc.ScalarSubcoreMesh(axis_name="sequencer", num_cores=1), name=name,
               scratch_types=tuple(sems), compiler_params=pltpu.CompilerParams(collective_id=collective_id))
    def launch(*sem_refs):
        body(x_ref, o_ref, *sem_refs)

    launch()
    return o_ref[...]


def _gather(x, name, collective_id=None):
    def body(x_ref, o_ref, send_sems, recv_sems, local_sem):
        mx, my, mc = lax.axis_index("x"), lax.axis_index("y"), lax.axis_index("c")
        me, sibling = (mx, my, mc), (mx, my, 1 - mc)
        chips = _other_chips(mx, my)
        if collective_id is not None:
            _handshake([sibling] + [(*chip, mc) for chip in chips])

        def slot(px, py, pc):
            return o_ref.at[4 * px + 2 * py + pc]

        def copy(k, block, to, src=None):
            return pltpu.make_async_remote_copy(
                src_ref=slot(*block) if src is None else src, dst_ref=slot(*block),
                send_sem=send_sems.at[k], recv_sem=recv_sems.at[k],
                device_id=to, device_id_type=pl.DeviceIdType.MESH)

        mine = pltpu.make_async_copy(x_ref, slot(*me), local_sem)
        mine.start()
        first = [copy(0, me, sibling, src=x_ref)]
        first += [copy(1 + j, me, (*chip, mc), src=x_ref) for j, chip in enumerate(chips)]
        for cp in first:
            cp.start()
        passed = [copy(4 + j, (*chip, mc), sibling) for j, chip in enumerate(chips)]
        for j, chip in enumerate(chips):
            copy(1 + j, (*chip, mc), me).wait_recv()
            passed[j].start()
        copy(0, sibling, me).wait_recv()
        for j, chip in enumerate(chips):
            copy(4 + j, (*chip, 1 - mc), me).wait_recv()
        for cp in first + passed:
            cp.wait_send()
        mine.wait()

    return _exchange_call(body, x, jax.ShapeDtypeStruct((N_DEV,) + x.shape, x.dtype), name, N_DEV - 1, True,
                          collective_id)


def _pair_send(parts, name, collective_id=None):
    def body(p_ref, o_ref, send_sems, recv_sems):
        mx, my, mc = lax.axis_index("x"), lax.axis_index("y"), lax.axis_index("c")
        if collective_id is not None:
            _handshake([(mx, my, 1 - mc)])
        copies = [pltpu.make_async_remote_copy(
            src_ref=p_ref.at[2 * chip + (1 - mc)], dst_ref=o_ref.at[chip],
            send_sem=send_sems.at[chip], recv_sem=recv_sems.at[chip],
            device_id=(mx, my, 1 - mc), device_id_type=pl.DeviceIdType.MESH) for chip in range(N_CHIP)]
        for cp in copies:
            cp.start()
        for cp in copies:
            cp.wait_recv()
        for cp in copies:
            cp.wait_send()

    return _exchange_call(body, parts, jax.ShapeDtypeStruct((N_CHIP,) + parts.shape[1:], parts.dtype), name,
                          N_CHIP, False, collective_id)


def _pair_sum(parts, recv, out_dtype, name):
    _, rows, cols = parts.shape
    tr, tc = _tile2d(rows, cols)

    def body(c_ref, p_ref, r_ref, o_ref):
        o_ref[...] = (p_ref[...] + r_ref[...]).astype(o_ref.dtype)

    return pl.pallas_call(
        body, name=name,
        grid_spec=pltpu.PrefetchScalarGridSpec(
            num_scalar_prefetch=1, grid=(N_CHIP, rows // tr, cols // tc),
            in_specs=[pl.BlockSpec((None, tr, tc), lambda k, i, j, c: (2 * k + c[0], i, j)),
                      pl.BlockSpec((None, tr, tc), lambda k, i, j, c: (k, i, j))],
            out_specs=pl.BlockSpec((None, tr, tc), lambda k, i, j, c: (k, i, j))),
        out_shape=jax.ShapeDtypeStruct((N_CHIP, rows, cols), out_dtype),
        compiler_params=_cparams("parallel", "parallel", "parallel"),
    )(lax.axis_index("c").astype(jnp.int32).reshape(1), parts, recv)


def _chip_exchange(sums, name, collective_id=None):
    def body(s_ref, o_ref, send_sems, recv_sems, local_sem):
        mx, my, mc = lax.axis_index("x"), lax.axis_index("y"), lax.axis_index("c")
        my_chip = 2 * mx + my
        if collective_id is not None:
            _handshake([(px, py, mc) for px, py in _other_chips(mx, my)])
        local = pltpu.make_async_copy(s_ref.at[my_chip], o_ref.at[my_chip], local_sem)
        local.start()
        sends, recvs = [], []
        for k, (px, py) in enumerate(_other_chips(mx, my)):
            peer = 2 * px + py

            def copy(src_slot, dst_slot, k=k, dev=(px, py, mc)):
                return pltpu.make_async_remote_copy(
                    src_ref=s_ref.at[src_slot], dst_ref=o_ref.at[dst_slot], send_sem=send_sems.at[k],
                    recv_sem=recv_sems.at[k], device_id=dev, device_id_type=pl.DeviceIdType.MESH)

            sends.append(copy(peer, my_chip))
            recvs.append(copy(peer, peer))
        for cp in sends:
            cp.start()
        for cp in recvs:
            cp.wait_recv()
        for cp in sends:
            cp.wait_send()
        local.wait()

    return _exchange_call(body, sums, jax.ShapeDtypeStruct(sums.shape, sums.dtype), name, N_CHIP - 1, True,
                          collective_id)


ADAM_TILE_ELEMS = 128 * 1024


def _tile2d(rows, cols):
    if rows * cols <= ADAM_TILE_ELEMS:
        return rows, cols
    if rows % 8 == 0:
        return _pick(rows, max(8, ADAM_TILE_ELEMS // cols), 8), cols
    return rows, _pick(cols, max(LANES, ADAM_TILE_ELEMS // rows))


def _adamw(w, parts, m, v, name):
    rows, cols = w.shape
    n_parts = parts.shape[0]
    tr, tc = _tile2d(rows, cols)

    def body(w_ref, p_ref, m_ref, v_ref, g_ref, d_ref, nm_ref, nv_ref):
        g = p_ref[0].astype(F32)
        for p in range(1, n_parts):
            g = g + p_ref[p].astype(F32)
        mm = ADAM_B1 * m_ref[...] + (1.0 - ADAM_B1) * g
        vv = ADAM_B2 * v_ref[...] + (1.0 - ADAM_B2) * jnp.square(g)
        m_hat = mm / (1.0 - ADAM_B1 ** ADAM_STEP)
        v_hat = vv / (1.0 - ADAM_B2 ** ADAM_STEP)
        g_ref[...] = g
        d_ref[...] = -ADAM_LR * (m_hat / (jnp.sqrt(v_hat) + ADAM_EPS) + ADAM_WD * w_ref[...])
        nm_ref[...] = mm
        nv_ref[...] = vv

    blk = pl.BlockSpec((tr, tc), lambda i, j: (i, j))
    return pl.pallas_call(
        body, name=name, grid=(rows // tr, cols // tc),
        in_specs=[blk, pl.BlockSpec((n_parts, tr, tc), lambda i, j: (0, i, j)), blk, blk],
        out_specs=[blk] * 4, out_shape=[jax.ShapeDtypeStruct((rows, cols), F32)] * 4,
        compiler_params=_cparams("parallel", "parallel"),
    )(w, parts, m, v)


WEIGHTS = ("l0_norm_w", "l0_w_in", "l0_s5_lambda_re", "l0_s5_lambda_im", "l0_s5_log_step", "l0_s5_b_re",
           "l0_s5_b_im", "l0_s5_c_re", "l0_s5_c_im", "l0_s5_d", "l0_s5_w_glu", "l0_s5_b_glu", "l0_ssd_conv_w",
           "l0_ssd_conv_b", "l0_ssd_dt_bias", "l0_ssd_a_log", "l0_ssd_d", "l0_ssd_norm_w", "l0_w_out",
           "l1_norm_w", "l1_w_in", "l1_fox_b_f", "l1_w_out", "final_norm_w")
SHARDED = ("l0_w_in", "l0_s5_w_glu", "l0_ssd_conv_w", "l0_w_out", "l1_w_in", "l1_w_out")


def _pad_lanes(a, width=LANES):
    return jnp.pad(a, [(0, 0)] * (a.ndim - 1) + [(0, width - a.shape[-1])])


def _pad_rows(a, height=LANES):
    return jnp.pad(a, [(0, height - a.shape[0])] + [(0, 0)] * (a.ndim - 1))


def _to_groups(a):
    length = a.shape[0]
    return _pad_lanes(a[:, :SSD_HEADS].reshape(length, SSD_GROUPS, SSD_HPG).transpose(1, 0, 2))


def _from_groups(a):
    length = a.shape[1]
    return a[:, :, :SSD_HPG].transpose(1, 0, 2).reshape(length, SSD_HEADS)


class _Grads(dict):
    def __init__(self, on_grad):
        super().__init__()
        self.on_grad = on_grad

    def __setitem__(self, name, value):
        if self.on_grad is not None and name in SHARDED:
            value = self.on_grad(name, value)
        super().__setitem__(name, value)


def _local_step(x, target, w, on_grad=None):
    length = x.shape[0]
    nc = length // SSD_CHUNK
    g = _Grads(on_grad)

    h0 = _rmsnorm_fwd(x, w["l0_norm_w"], "l0_norm")
    w0 = w["l0_w_in"]
    r0_s5, r0_z, r0_xbc = (0, 2 * S5_WIDTH), (2 * S5_WIDTH, SSD_WIDTH), (2 * S5_WIDTH + SSD_WIDTH, SSD_XBC)
    r0_dt = 2 * S5_WIDTH + SSD_WIDTH + SSD_XBC
    w0_dt = _pad_rows(w0[r0_dt:])
    p_s5 = _mm(h0, w0, "nt", F32, "l0_in_s5", b_rows=r0_s5)
    p_z = _mm(h0, w0, "nt", F32, "l0_in_z", b_rows=r0_z)
    p_xbc = _mm(h0, w0, "nt", F32, "l0_in_xbc", b_rows=r0_xbc)
    p_dt = _mm(h0, w0_dt, "nt", F32, "l0_in_dt")

    row = lambda a: a.reshape(1, S5_NS)
    b_rows = lambda a: a.transpose(2, 0, 1).reshape(S5_GROUP, S5_NS)
    prep_in = (row(w["l0_s5_lambda_re"]), row(w["l0_s5_lambda_im"]),
               row(jnp.repeat(w["l0_s5_log_step"], S5_STATE)), b_rows(w["l0_s5_b_re"]), b_rows(w["l0_s5_b_im"]))
    ab_re, ab_im, bbr, bbi = _s5_prep(*prep_in)
    to_bb = lambda a: _blockdiag(a.reshape(S5_GROUP, S5_GROUPS, S5_STATE).transpose(1, 0, 2),
                                 S5_GROUP, S5_STATE).astype(MXU_DT)
    to_ct = lambda a: _blockdiag(a.transpose(0, 2, 1), S5_STATE, S5_GROUP).astype(MXU_DT)
    bb_re, bb_im = to_bb(bbr), to_bb(bbi)
    ct_re, ct_im = to_ct(w["l0_s5_c_re"]), to_ct(w["l0_s5_c_im"])
    a_re3, a_im3 = ab_re.reshape(S5_BLK, 1, S5_BS), ab_im.reshape(S5_BLK, 1, S5_BS)
    d3 = w["l0_s5_d"].reshape(S5_BLK, 1, S5_BC)
    y5, s_re, s_im = _s5_fwd(p_s5, bb_re, bb_im, ct_re, ct_im, a_re3, a_im3, d3)
    g_bf = _tiles("s5_gelu", lambda yb: (_gelu(yb),), [y5], [], [(S5_WIDTH, MXU_DT)], 0, tr=256)[0]
    zg = _mm(g_bf, w["l0_s5_w_glu"], "nn", F32, "s5_glu")
    b_glu = w["l0_s5_b_glu"].reshape(1, -1)
    s5_out = _tiles("s5_out", lambda yb, zb, gb, bb: (_s5_out_fn(yb, zb, gb, bb),),
                    [y5, zg, (p_s5, 1)], [b_glu], [(S5_WIDTH, MXU_DT)], 0, tr=256)[0]

    conv_w = w["l0_ssd_conv_w"]
    cv, act = _conv_fwd(p_xbc, conv_w, w["l0_ssd_conv_b"])
    bias_row = _pad_lanes(w["l0_ssd_dt_bias"].reshape(1, -1))
    alog_row = _pad_lanes(w["l0_ssd_a_log"].reshape(1, -1))
    dt, cum, cum_t = _dt_fwd(p_dt, bias_row, alog_row)
    dtg, cumg = _to_groups(dt), _to_groups(cum)
    cumtg = cum_t[:, :SSD_HEADS].reshape(nc, SSD_GROUPS, SSD_HPG, SSD_CHUNK).transpose(1, 0, 2, 3)
    cumtg = jnp.pad(cumtg, ((0, 0), (0, 0), (0, 16 - SSD_HPG), (0, 0)))
    ycore, states = _ssd_core_fwd(act, dtg, cumg, cumtg)
    dchan = jnp.repeat(w["l0_ssd_d"], SSD_HEAD_DIM).reshape(1, -1)
    nw_row = w["l0_ssd_norm_w"].reshape(1, -1)
    ssd_out = _tiles("ssd_post", lambda a, b, c, d, e: (_ssd_post_fn(a, b, c, d, e),),
                     [ycore, act, p_z], [dchan, nw_row], [(SSD_WIDTH, MXU_DT)], 0, tr=256, tc=SSD_GW)[0]
    mixed = jnp.concatenate([s5_out, ssd_out], axis=1)
    x1 = _mm(mixed, w["l0_w_out"], "nn", F32, "l0_out", res=x)

    h1 = _rmsnorm_fwd(x1, w["l1_norm_w"], "l1_norm")
    w1 = w["l1_w_in"]
    r1_qkv, r1_gate = (0, 3 * FOX_WIDTH), (3 * FOX_WIDTH, FOX_WIDTH)
    w1_f = _pad_rows(w1[4 * FOX_WIDTH:])
    qkv = _mm(h1, w1, "nt", MXU_DT, "l1_in_qkv", b_rows=r1_qkv)
    gate1 = _mm(h1, w1, "nt", F32, "l1_in_gate", b_rows=r1_gate)
    f_raw = _mm(h1, w1_f, "nt", F32, "l1_in_f")
    bf_row = _pad_lanes(w["l1_fox_b_f"].reshape(1, -1))
    c = _fgate_fwd(f_raw, bf_row)
    c0_rep = _fox_c0(c, min(FOX_TILE, length))
    c_t = c[:, :FOX_HEADS].T.reshape(FOX_HEADS, 1, length)
    att, lse = _fox_fwd(qkv, c0_rep, c_t)
    out1 = _tiles("fox_gate", lambda a, b: (_fox_gate_fn(a, b),), [att, gate1], [],
                  [(FOX_WIDTH, MXU_DT)], 0, tr=256)[0]
    x2 = _mm(out1, w["l1_w_out"], "nn", F32, "l1_out", res=x1)

    loss_part, dx2, dx2b, g["final_norm_w"] = _final_loss(x2, w["final_norm_w"], target, "final_loss")

    d_out1 = _mm(dx2b, w["l1_w_out"], "nt", F32, "l1_out_dx")
    g["l1_w_out"] = _mm(out1, dx2b, "tn", F32, "l1_out_dw")

    def gate_bwd(a, gt, d):
        _, vjp = jax.vjp(_fox_gate_fn, a, gt)
        da, dgt = vjp(d)
        return da, dgt, jnp.broadcast_to(jnp.sum(da * a, axis=1, keepdims=True), a.shape)

    d_att, d_gate1, delta = _tiles("fox_gate_bwd", gate_bwd, [att, gate1, d_out1], [],
                                   [(FOX_WIDTH, MXU_DT), (FOX_WIDTH, MXU_DT), (FOX_WIDTH, F32)], 0,
                                   tr=512, tc=FOX_HEAD_DIM)
    dq, dk, dv, dcq, dck = _fox_bwd(qkv, d_att, lse, delta, c0_rep, c_t)
    dc = dcq.reshape(length, FOX_HEADS, FOX_HEAD_DIM)[:, :, 0] + dck.reshape(FOX_HEADS, length).T
    df, dbf = _fgate_bwd(f_raw, bf_row, _pad_lanes(dc))
    g["l1_fox_b_f"] = dbf[0, :FOX_HEADS]
    dqkv = jnp.concatenate([dq, dk, dv], axis=1)
    dh1 = _mm(dqkv, w1, "nn", F32, "l1_in_dx_qkv", b_rows=r1_qkv)
    dh1 = _mm(d_gate1, w1, "nn", F32, "l1_in_dx_gate", res=dh1, b_rows=r1_gate)
    dh1 = _mm(df, w1_f, "nn", F32, "l1_in_dx_f", res=dh1)
    dw1 = _mm(dqkv, h1, "tn", F32, "l1_in_dw_qkv", into=(lax.empty((ODD_IN, D_MODEL), F32), r1_qkv[0]))
    dw1 = _mm(d_gate1, h1, "tn", F32, "l1_in_dw_gate", into=(dw1, r1_gate[0]))
    g["l1_w_in"] = dw1.at[4 * FOX_WIDTH:].set(_mm(df, h1, "tn", F32, "l1_in_dw_f")[:FOX_HEADS])
    dx1, dx1b, g["l1_norm_w"] = _rmsnorm_bwd(x1, w["l1_norm_w"], dh1, dx2, "l1_norm_bwd")

    wout0 = w["l0_w_out"]
    d_s5 = _mm(dx1b, wout0, "nt", F32, "l0_out_dx_s5", b_rows=(0, S5_WIDTH))
    d_ssd = _mm(dx1b, wout0, "nt", F32, "l0_out_dx_ssd", b_rows=(S5_WIDTH, SSD_WIDTH))
    g["l0_w_out"] = _mm(mixed, dx1b, "tn", F32, "l0_out_dw")

    def post_bwd(a, b, c_, d, dch, nw):
        _, vjp = jax.vjp(_ssd_post_fn, a, b, c_, dch, nw)
        return vjp(d)

    dycore, dxs_post, dz, ddch, dnw = _tiles(
        "ssd_post_bwd", post_bwd, [ycore, act, p_z, d_ssd], [dchan, nw_row],
        [(SSD_WIDTH, F32), (SSD_WIDTH, F32), (SSD_WIDTH, MXU_DT)], 2, tr=256, tc=SSD_GW)
    g["l0_ssd_d"] = ddch[0].reshape(SSD_HEADS, SSD_HEAD_DIM).sum(axis=1)
    g["l0_ssd_norm_w"] = dnw[0]
    dxs, d_b, d_c, ddtg, dcumg, dcumtg, dclg = _ssd_core_bwd(dycore, dxs_post, act, dtg, cumg, cumtg, states)
    dact = jnp.concatenate([dxs, d_b, d_c], axis=1)
    dxbc, dconvw, dconvb = _conv_bwd(dact, cv, p_xbc, conv_w)
    g["l0_ssd_conv_w"] = dconvw[:SSD_CONV]
    g["l0_ssd_conv_b"] = dconvb[0]
    dcum = _from_groups(dcumg)
    dcum = dcum + dcumtg[:, :, :SSD_HPG].transpose(1, 3, 0, 2).reshape(length, SSD_HEADS)
    dcl = dclg[:, :, 0, :SSD_HPG].transpose(1, 0, 2).reshape(nc, SSD_HEADS)
    dcum = dcum.reshape(nc, SSD_CHUNK, SSD_HEADS).at[:, SSD_CHUNK - 1, :].add(dcl).reshape(length, SSD_HEADS)
    ddt_raw, dbias, dalog = _dt_bwd(p_dt, bias_row, alog_row, _pad_lanes(_from_groups(ddtg)), _pad_lanes(dcum))
    g["l0_ssd_dt_bias"] = dbias[0, :SSD_HEADS]
    g["l0_ssd_a_log"] = dalog[0, :SSD_HEADS]

    def s5_out_bwd(yb, zb, gb, d, bb):
        _, vjp = jax.vjp(_s5_out_fn, yb, zb, gb, bb)
        return vjp(d)

    dy_direct, dzg, dgate0, dbglu = _tiles(
        "s5_out_bwd", s5_out_bwd, [y5, zg, (p_s5, 1), d_s5], [b_glu],
        [(S5_WIDTH, F32), (S5_WIDTH, MXU_DT), (S5_WIDTH, MXU_DT)], 1, tr=256)
    g["l0_s5_b_glu"] = dbglu[0]
    g["l0_s5_w_glu"] = _mm(g_bf, dzg, "tn", F32, "s5_glu_dw")
    dg2 = _mm(dzg, w["l0_s5_w_glu"], "nt", F32, "s5_glu_dx")

    def gelu_bwd(yb, d, direct):
        _, vjp = jax.vjp(_gelu, yb)
        return (vjp(d)[0] + direct,)

    dy5 = _tiles("s5_gelu_bwd", gelu_bwd, [y5, dg2, dy_direct], [], [(S5_WIDTH, F32)], 0, tr=256)[0]
    du, dbbr3, dbbi3, dctr3, dcti3, dar, dai, dd5 = _s5_bwd(dy5, p_s5, s_re, s_im, bb_re, bb_im, ct_re, ct_im,
                                                           a_re3, a_im3, d3)
    from_bb = lambda a: _blockdiag_t(a, S5_GROUP, S5_STATE).transpose(1, 0, 2).reshape(S5_GROUP, S5_NS)
    from_ct = lambda a: _blockdiag_t(a, S5_STATE, S5_GROUP).transpose(0, 2, 1)
    g["l0_s5_c_re"], g["l0_s5_c_im"] = from_ct(dctr3), from_ct(dcti3)
    g["l0_s5_d"] = dd5[:, 0, :].reshape(S5_GROUPS, S5_GROUP)
    dlr, dli, dls, dbr, dbi = _s5_prep_bwd(*prep_in, dar[:, 0, :].reshape(1, S5_NS), dai[:, 0, :].reshape(1, S5_NS),
                                           from_bb(dbbr3), from_bb(dbbi3))
    g["l0_s5_lambda_re"] = dlr.reshape(S5_GROUPS, S5_STATE)
    g["l0_s5_lambda_im"] = dli.reshape(S5_GROUPS, S5_STATE)
    g["l0_s5_log_step"] = dls.reshape(S5_GROUPS, S5_STATE).sum(axis=1)
    from_rows = lambda a: a.reshape(S5_GROUP, S5_GROUPS, S5_STATE).transpose(1, 2, 0)
    g["l0_s5_b_re"], g["l0_s5_b_im"] = from_rows(dbr), from_rows(dbi)

    dus = jnp.concatenate([du, dgate0], axis=1)
    dw0 = _mm(dus, h0, "tn", F32, "l0_in_dw_s5", into=(lax.empty((EVEN_IN, D_MODEL), F32), r0_s5[0]))
    dw0 = _mm(dz, h0, "tn", F32, "l0_in_dw_z", into=(dw0, r0_z[0]))
    dw0 = _mm(dxbc, h0, "tn", F32, "l0_in_dw_xbc", into=(dw0, r0_xbc[0]))
    g["l0_w_in"] = dw0.at[r0_dt:].set(_mm(ddt_raw, h0, "tn", F32, "l0_in_dw_dt")[:SSD_HEADS])
    dh0 = _mm(dus, w0, "nn", F32, "l0_in_dx_s5", b_rows=r0_s5)
    dh0 = _mm(dz, w0, "nn", F32, "l0_in_dx_z", res=dh0, b_rows=r0_z)
    dh0 = _mm(dxbc, w0, "nn", F32, "l0_in_dx_xbc", res=dh0, b_rows=r0_xbc)
    dh0 = _mm(ddt_raw, w0_dt, "nn", F32, "l0_in_dx_dt", res=dh0)
    grad_x, _, g["l0_norm_w"] = _rmsnorm_bwd(x, w["l0_norm_w"], dh0, dx1, "l0_norm_bwd")
    return loss_part, grad_x, g


TRANSPOSED = ("l0_w_in", "l1_w_in")


SEQUENCER_IDS = {"l0_w_out": (0, 1, 2), "l1_w_in": (3, 4, 5), "l1_w_out": (6, 7, 8), "l0_w_in": (None, 9, 10),
                 "l0_s5_w_glu": (11, 12, 13), "l0_ssd_conv_w": (14, 15, 16)}
NO_IDS = (None, None, None)


def _gather_weight(name, shard):
    cid = SEQUENCER_IDS.get(name, NO_IDS)[0]
    if name == "l0_ssd_conv_w":
        full = _gather(shard, "gather_" + name, cid)
        return full.transpose(1, 0, 2).reshape(shard.shape[0], N_DEV * shard.shape[1])
    if name in TRANSPOSED:
        full = _gather(shard.T.astype(MXU_DT), "gather_" + name, cid)
        return full.reshape(N_DEV * shard.shape[1], shard.shape[0])
    full = _gather(shard.astype(MXU_DT), "gather_" + name, cid)
    return full.reshape(N_DEV * shard.shape[0], shard.shape[1])


def _reduce_grad(name, grad, shard_shape):
    rows, cols = shard_shape
    if name == "l0_ssd_conv_w":
        parts = grad.reshape(rows, N_DEV, cols).transpose(1, 0, 2)
    elif name in TRANSPOSED:
        parts = grad.reshape(N_DEV, cols, rows)
    else:
        parts = grad.reshape(N_DEV, rows, cols)
    _, pair_id, chip_id = SEQUENCER_IDS.get(name, NO_IDS)
    recv = _pair_send(parts, "pair_" + name, pair_id)
    sums = _pair_sum(parts, recv, F32 if name == "l0_ssd_conv_w" else MXU_DT, "pairsum_" + name)
    return _chip_exchange(sums, "scatter_" + name, chip_id)


SMALL_ROWS_QUANTUM = 8 * LANES


def _step(args):
    x = args["x"][0]
    target = args["loss_target"][0]
    full = {n: args[n] for n in WEIGHTS if n not in SHARDED}
    for n in SHARDED:
        shard = args[n]
        if n != SHARDED[0]:
            shard, full[SHARDED[0]] = lax.optimization_barrier((shard, full[SHARDED[0]]))
        full[n] = _gather_weight(n, shard)
    out_g, out_d, out_m, out_v = {}, {}, {}, {}

    def reduce_and_update(n, grad):
        parts = _reduce_grad(n, grad, args[n].shape)
        view = (lambda a: a.T) if n in TRANSPOSED else (lambda a: a)
        outs = _adamw(view(args[n]), parts, view(args["m_" + n]), view(args["v_" + n]), "adamw_" + n)
        out_g[n], out_d[n], out_m[n], out_v[n] = [view(o) for o in outs]
        return out_g[n]

    loss_part, grad_x, g = _local_step(x, target, full, reduce_and_update)

    small = [n for n in WEIGHTS if n not in SHARDED]
    sizes = [int(math.prod(args[n].shape)) for n in small]
    total = sum(sizes) + 1
    padded = -(-total // SMALL_ROWS_QUANTUM) * SMALL_ROWS_QUANTUM

    def pack(pieces, extra):
        flat = jnp.concatenate([p.reshape(-1).astype(F32) for p in pieces] + [extra.reshape(1)])
        return jnp.pad(flat, (0, padded - total)).reshape(padded // LANES, LANES)

    zero = jnp.zeros((), F32)
    parts = _gather(pack([g[n] for n in small], loss_part), "gather_small_grads")
    sg, sd, sm, sv = _adamw(pack([args[n] for n in small], zero), parts,
                            pack([args["m_" + n] for n in small], zero),
                            pack([args["v_" + n] for n in small], zero), "adamw_small")
    off = 0
    for n, sz in zip(small, sizes):
        cut = lambda a: a.reshape(-1)[off:off + sz].reshape(args[n].shape)
        out_g[n], out_d[n], out_m[n], out_v[n] = cut(sg), cut(sd), cut(sm), cut(sv)
        off += sz
    loss = sg.reshape(-1)[total - 1]
    return (loss, grad_x[None], *[out_g[n] for n in WEIGHTS], *[out_d[n] for n in WEIGHTS],
            *[out_m[n] for n in WEIGHTS], *[out_v[n] for n in WEIGHTS])


def kernel(x, l0_norm_w, l0_w_in, l0_s5_lambda_re, l0_s5_lambda_im, l0_s5_log_step, l0_s5_b_re, l0_s5_b_im, l0_s5_c_re, l0_s5_c_im, l0_s5_d, l0_s5_w_glu, l0_s5_b_glu, l0_ssd_conv_w, l0_ssd_conv_b, l0_ssd_dt_bias, l0_ssd_a_log, l0_ssd_d, l0_ssd_norm_w, l0_w_out, l1_norm_w, l1_w_in, l1_fox_b_f, l1_w_out, final_norm_w, loss_target, m_l0_norm_w, m_l0_w_in, m_l0_s5_lambda_re, m_l0_s5_lambda_im, m_l0_s5_log_step, m_l0_s5_b_re, m_l0_s5_b_im, m_l0_s5_c_re, m_l0_s5_c_im, m_l0_s5_d, m_l0_s5_w_glu, m_l0_s5_b_glu, m_l0_ssd_conv_w, m_l0_ssd_conv_b, m_l0_ssd_dt_bias, m_l0_ssd_a_log, m_l0_ssd_d, m_l0_ssd_norm_w, m_l0_w_out, m_l1_norm_w, m_l1_w_in, m_l1_fox_b_f, m_l1_w_out, m_final_norm_w, v_l0_norm_w, v_l0_w_in, v_l0_s5_lambda_re, v_l0_s5_lambda_im, v_l0_s5_log_step, v_l0_s5_b_re, v_l0_s5_b_im, v_l0_s5_c_re, v_l0_s5_c_im, v_l0_s5_d, v_l0_s5_w_glu, v_l0_s5_b_glu, v_l0_ssd_conv_w, v_l0_ssd_conv_b, v_l0_ssd_dt_bias, v_l0_ssd_a_log, v_l0_ssd_d, v_l0_ssd_norm_w, v_l0_w_out, v_l1_norm_w, v_l1_w_in, v_l1_fox_b_f, v_l1_w_out, v_final_norm_w):
    return _step(dict(locals()))
```

```python
import functools
import math

import jax
import jax.numpy as jnp
from jax import lax
from jax.experimental import pallas as pl
from jax.experimental.pallas import tpu as pltpu
from jax.experimental.pallas import tpu_sc as plsc

F32 = jnp.float32
BF16 = jnp.bfloat16
MXU_DT = BF16

D_MODEL = 4096
S5_WIDTH = 2048
S5_GROUP = 16
S5_GROUPS = 128
S5_STATE = 64
S5_EIG_CLIP = -1e-4
S5_BLK = 16
SSD_WIDTH = 6144
SSD_HEAD_DIM = 64
SSD_HEADS = 96
SSD_GROUPS = 8
SSD_STATE = 128
SSD_CONV = 4
SSD_CHUNK = 128
SSD_XBC = 8192
SSD_HPG = SSD_HEADS // SSD_GROUPS
FOX_HEAD_DIM = 128
FOX_HEADS = 32
FOX_WIDTH = 4096
NORM_EPS = 1e-5
EVEN_IN = 18528
ODD_IN = 16416
EVEN_PAD = 18560
ODD_PAD = 16512
LANES = 128
N_DEV = 8

ADAM_LR = 0.001
ADAM_B1 = 0.9
ADAM_B2 = 0.999
ADAM_EPS = 1e-08
ADAM_WD = 0.01
ADAM_STEP = 10

VMEM_LIMIT_BYTES = 48 * 1024 * 1024


ANYSPACE = pl.BlockSpec(memory_space=pl.ANY)


def _cparams(*sem):
    return pltpu.CompilerParams(dimension_semantics=sem, vmem_limit_bytes=VMEM_LIMIT_BYTES)


def _pick(n, target, quantum=LANES):
    if n <= target:
        return n
    t = (target // quantum) * quantum
    while t >= quantum:
        if n % t == 0:
            return t
        t -= quantum
    raise ValueError((n, target, quantum))


_MM_DIMS = {"nn": ((1,), (0,)), "nt": ((1,), (1,)), "tn": ((0,), (0,))}


MM_VMEM_BUDGET = 38 * 1024 * 1024


def _mm_tk(k, tm, tn, out_bytes, has_res, tk_t, start=0):
    fixed = 2 * tm * tn * out_bytes + (2 * tm * tn * 4 if has_res else 0)
    tk = min(k, tk_t)
    while True:
        if k % tk == 0 and start % tk == 0 and (tk == k or tk % LANES == 0):
            need = fixed + 2 * (tm + tn) * tk * 2 + (tm * tn * 4 if tk < k else 0)
            if need <= MM_VMEM_BUDGET or tk <= LANES:
                return tk
        tk -= LANES if tk % LANES == 0 else tk % LANES


def _mm(a, b, mode, out_dtype, name, res=None, b_rows=None, into=None, tm_t=1024, tn_t=512, tk_t=8192):
    b_start, b_size = b_rows if b_rows is not None else (0, b.shape[0])
    if mode == "nn":
        (m, k), (k2, n) = a.shape, (b_size, b.shape[1])
    elif mode == "nt":
        (m, k), (n, k2) = a.shape, (b_size, b.shape[1])
    else:
        (k, m), (k2, n) = a.shape, (b_size, b.shape[1])
    assert k == k2, (a.shape, b.shape, mode)
    tm, tn = _pick(m, tm_t), _pick(n, tn_t)
    has_res = res is not None
    has_into = into is not None
    tk = _mm_tk(k, tm, tn, jnp.dtype(out_dtype).itemsize, has_res, tk_t, b_start if mode == "nn" else 0)
    nk = k // tk
    dims = (_MM_DIMS[mode], ((), ()))
    o_row = 0
    if has_into:
        assert into[1] % tm == 0 and into[0].shape[1] == n and into[0].dtype == out_dtype, (into[1], tm)
        o_row = into[1] // tm
    if mode == "nt":
        assert b_start % tn == 0, (b_start, tn)
    b_blk = b_start // (tn if mode == "nt" else tk)

    def body(*refs):
        a_ref, b_ref = refs[:2]
        r_ref = refs[2] if has_res else None
        o_ref = refs[2 + has_res + has_into]
        part = lax.dot_general(a_ref[...].astype(MXU_DT), b_ref[...].astype(MXU_DT), dims,
                               preferred_element_type=F32)

        def finish(r):
            if has_res:
                r = r + r_ref[...]
            o_ref[...] = r.astype(out_dtype)

        if nk == 1:
            finish(part)
            return
        acc = refs[-1]
        kk = pl.program_id(2)

        @pl.when(kk == 0)
        def _():
            acc[...] = part

        @pl.when(jnp.logical_and(kk > 0, kk < nk - 1))
        def _():
            acc[...] += part

        @pl.when(kk == nk - 1)
        def _():
            finish(acc[...] + part)

    a_spec = (pl.BlockSpec((tk, tm), lambda i, j, kk: (kk, i)) if mode == "tn"
              else pl.BlockSpec((tm, tk), lambda i, j, kk: (i, kk)))
    b_spec = (pl.BlockSpec((tn, tk), lambda i, j, kk: (j + b_blk, kk)) if mode == "nt"
              else pl.BlockSpec((tk, tn), lambda i, j, kk: (kk + b_blk, j)))
    r_spec = pl.BlockSpec((tm, tn), lambda i, j, kk: (i, j))
    o_spec = pl.BlockSpec((tm, tn), lambda i, j, kk: (i + o_row, j))
    in_specs = [a_spec, b_spec] + ([r_spec] if has_res else []) + ([ANYSPACE] if has_into else [])
    args = (a, b) + ((res,) if has_res else ()) + ((into[0],) if has_into else ())
    return pl.pallas_call(
        body, name=name, grid=(m // tm, n // tn, nk), in_specs=in_specs, out_specs=o_spec,
        out_shape=jax.ShapeDtypeStruct(into[0].shape if has_into else (m, n), out_dtype),
        scratch_shapes=[pltpu.VMEM((tm, tn), F32)] if nk > 1 else [],
        input_output_aliases={len(args) - 1: 0} if has_into else {},
        compiler_params=_cparams("parallel", "parallel", "arbitrary"),
    )(*args)


def _tiles(name, fn, tiled, rows, out_tiled, out_acc, tr, tc=None):
    tiled = [t if isinstance(t, tuple) else (t, 0) for t in tiled]
    length = tiled[0][0].shape[0]
    width = out_tiled[0][0] if out_tiled else rows[0].shape[1]
    tc = width if tc is None else tc
    tr = min(tr, length)
    n_in = len(tiled) + len(rows)
    n_ot = len(out_tiled)

    def body(*refs):
        outs = fn(*[r[...] for r in refs[:n_in]])
        outs_t, outs_a = outs[:n_ot], outs[n_ot:]
        for r, v in zip(refs[n_in:n_in + n_ot], outs_t):
            r[...] = v.astype(r.dtype)
        i = pl.program_id(1)
        for r, v in zip(refs[n_in + n_ot:], outs_a):
            @pl.when(i == 0)
            def _(r=r):
                r[...] = jnp.zeros_like(r)

            r[...] += jnp.broadcast_to(v, r.shape)

    def tspec(off):
        return pl.BlockSpec((tr, tc), lambda j, i: (i, j + off))

    in_specs = [tspec(off) for _, off in tiled] + [pl.BlockSpec((1, tc), lambda j, i: (0, j)) for _ in rows]
    out_specs = [tspec(0) for _ in out_tiled] + [pl.BlockSpec((8, tc), lambda j, i: (0, j)) for _ in range(out_acc)]
    out_shape = ([jax.ShapeDtypeStruct((length, w), dt) for w, dt in out_tiled]
                 + [jax.ShapeDtypeStruct((8, width), F32) for _ in range(out_acc)])
    return pl.pallas_call(
        body, name=name, grid=(width // tc, length // tr), in_specs=in_specs, out_specs=out_specs,
        out_shape=out_shape, compiler_params=_cparams("parallel", "arbitrary"),
    )(*[t for t, _ in tiled], *rows)


def _rms(x, w):
    return x * lax.rsqrt(jnp.mean(x * x, axis=-1, keepdims=True) + NORM_EPS) * w


def _colsum(v):
    return jnp.sum(v, axis=0, keepdims=True)


def _rmsnorm_fwd(x, w, name):
    def fn(xb, wb):
        return (_rms(xb, wb),)
    return _tiles(name, fn, [x], [w.reshape(1, -1)], [(x.shape[1], MXU_DT)], 0, tr=256)[0]


def _rmsnorm_bwd(x, w, dh, dres, name):
    def fn(xb, db, rb, wb):
        _, vjp = jax.vjp(_rms, xb, wb)
        dx, dw = vjp(db)
        return dx + rb, dx + rb, dw
    dx, dxb, dw = _tiles(name, fn, [x, dh, dres], [w.reshape(1, -1)],
                         [(x.shape[1], F32), (x.shape[1], MXU_DT)], 1, tr=256)
    return dx, dxb, dw[0]


def _final_loss(x, w, target, name):
    def fn(xb, tb, wb):
        def f(xv, wv):
            e = _rms(xv, wv) - tb
            return 0.5 * jnp.sum(jnp.mean(e * e, axis=-1, keepdims=True), axis=0, keepdims=True)
        lv, vjp = jax.vjp(f, xb, wb)
        dx, dw = vjp(jnp.ones_like(lv))
        return dx, dx, dw, jnp.broadcast_to(lv, (1, xb.shape[1]))
    dx, dxb, dw, lv = _tiles(name, fn, [x, target], [w.reshape(1, -1)],
                             [(x.shape[1], F32), (x.shape[1], MXU_DT)], 2, tr=256)
    return lv[0, 0], dx, dxb, dw[0]


def _dg(a, b, mode):
    return lax.dot_general(a.astype(MXU_DT), b.astype(MXU_DT), (_MM_DIMS[mode], ((), ())),
                           preferred_element_type=F32)


@jax.custom_vjp
def _dot_nn(a, b):
    return _dg(a, b, "nn")


@jax.custom_vjp
def _dot_nt(a, b):
    return _dg(a, b, "nt")


@jax.custom_vjp
def _dot_tn(a, b):
    return _dg(a, b, "tn")


_dot_nn.defvjp(lambda a, b: (_dg(a, b, "nn"), (a, b)),
               lambda r, g: (_dg(g, r[1], "nt"), _dg(r[0], g, "tn")))
_dot_nt.defvjp(lambda a, b: (_dg(a, b, "nt"), (a, b)),
               lambda r, g: (_dg(g, r[1], "nn"), _dg(g, r[0], "tn")))
_dot_tn.defvjp(lambda a, b: (_dg(a, b, "tn"), (a, b)),
               lambda r, g: (_dg(r[1], g, "nt"), _dg(r[0], g, "nn")))


def _dot_exact(a, b):
    return jnp.dot(a, b, precision=lax.Precision.HIGHEST, preferred_element_type=F32)


S5_NS = S5_GROUPS * S5_STATE
S5_BS = S5_NS // S5_BLK
S5_BC = S5_WIDTH // S5_BLK


def _s5_disc(lr_raw, li, ls, br, bi):
    lr = jnp.minimum(lr_raw, S5_EIG_CLIP)
    step = jnp.exp(ls)
    mag = jnp.exp(lr * step)
    ab_re = mag * jnp.cos(li * step)
    ab_im = mag * jnp.sin(li * step)
    denom = lr * lr + li * li
    nr = ab_re - 1.0
    ni = ab_im
    coef_re = (nr * lr + ni * li) / denom
    coef_im = (ni * lr - nr * li) / denom
    return ab_re, ab_im, coef_re * br - coef_im * bi, coef_re * bi + coef_im * br


def _s5_prep(lr_raw, li, ls, br, bi):
    shapes = [jax.ShapeDtypeStruct((1, S5_NS), F32)] * 2 + [jax.ShapeDtypeStruct((S5_GROUP, S5_NS), F32)] * 2

    def body(a, b, c, d, e, o1, o2, o3, o4):
        for r, v in zip((o1, o2, o3, o4), _s5_disc(a[...], b[...], c[...], d[...], e[...])):
            r[...] = v

    return pl.pallas_call(body, name="s5_prep", out_shape=shapes)(lr_raw, li, ls, br, bi)


def _s5_prep_bwd(lr_raw, li, ls, br, bi, d_are, d_aim, d_bbre, d_bbim):
    shapes = [jax.ShapeDtypeStruct((1, S5_NS), F32)] * 3 + [jax.ShapeDtypeStruct((S5_GROUP, S5_NS), F32)] * 2

    def body(a, b, c, d, e, g1, g2, g3, g4, o1, o2, o3, o4, o5):
        _, vjp = jax.vjp(_s5_disc, a[...], b[...], c[...], d[...], e[...])
        for r, v in zip((o1, o2, o3, o4, o5), vjp((g1[...], g2[...], g3[...], g4[...]))):
            r[...] = v

    return pl.pallas_call(body, name="s5_prep_bwd", out_shape=shapes)(
        lr_raw, li, ls, br, bi, d_are, d_aim, d_bbre, d_bbim)


def _cmul(ar, ai, br, bi):
    return ar * br - ai * bi, ar * bi + ai * br


def _s5_powers(ar, ai, n):
    pr, pi_ = [ar], [ai]
    for _ in range(n - 1):
        r, i = _cmul(pr[-1], pi_[-1], pr[-1], pi_[-1])
        pr.append(r)
        pi_.append(i)
    return pr, pi_


S5_SUB = 8
S5_NPOW = 3


def _s5_in_groups(xr, xi, pr, pi_, reverse):
    t = xr.shape[0]
    sub = lax.broadcasted_iota(jnp.int32, xr.shape, 0) & (S5_SUB - 1)
    sr, si = xr, xi
    for k in range(S5_NPOW):
        d = 1 << k
        shift = (t - d) if reverse else d
        keep = (sub < S5_SUB - d) if reverse else (sub >= d)
        qr = jnp.where(keep, pltpu.roll(sr, shift, 0), 0.0)
        qi = jnp.where(keep, pltpu.roll(si, shift, 0), 0.0)
        mr, mi = _cmul(pr[k], pi_[k], qr, qi)
        sr, si = sr + mr, si + mi
    return sr, si


def _s5_scan_tile(xr, xi, apr, api, tabr, tabi, cr, ci, out_r, out_i, reverse):
    t = xr.shape[0]
    pr = [apr[k:k + 1, :] for k in range(S5_NPOW)]
    pi_ = [api[k:k + 1, :] for k in range(S5_NPOW)]
    sr, si = _s5_in_groups(xr, xi, pr, pi_, reverse)
    tr, ti = tabr[...], tabi[...]
    car_r, car_i = cr[0:1, :], ci[0:1, :]
    groups = range(t // S5_SUB)
    for g in (reversed(groups) if reverse else groups):
        rows = slice(g * S5_SUB, (g + 1) * S5_SUB)
        mr, mi = _cmul(tr, ti, car_r, car_i)
        br, bi = sr[rows] + mr, si[rows] + mi
        out_r[rows, :] = br
        out_i[rows, :] = bi
        edge = slice(0, 1) if reverse else slice(S5_SUB - 1, S5_SUB)
        car_r, car_i = br[edge], bi[edge]
    cr[0:1, :] = car_r
    ci[0:1, :] = car_i


def _s5_setup(ar, ai, reverse, apr, api, tabr, tabi):
    pr, pi_ = _s5_powers(ar, ai, S5_NPOW)
    for k in range(S5_NPOW):
        apr[k:k + 1, :] = pr[k]
        api[k:k + 1, :] = pi_[k]
    row = lax.broadcasted_iota(jnp.int32, (S5_SUB, ar.shape[1]), 0)
    first = (row == S5_SUB - 1) if reverse else (row == 0)
    sr, si = _s5_in_groups(jnp.where(first, ar, 0.0), jnp.where(first, ai, 0.0), pr, pi_, reverse)
    tabr[...] = sr
    tabi[...] = si


def _s5_fwd(p_s5, bb_re, bb_im, ct_re, ct_im, a_re, a_im, d_row, t_tile=256):
    length = p_s5.shape[0]
    t = min(t_tile, length)
    nt = length // t

    def body(u_ref, bbr, bbi, ctr, cti, ar_ref, ai_ref, d_ref, y_ref, sr_ref, si_ref,
             apr, api, tabr, tabi, cr, ci):
        i = pl.program_id(1)

        @pl.when(i == 0)
        def _():
            _s5_setup(ar_ref[...], ai_ref[...], False, apr, api, tabr, tabi)
            cr[...] = jnp.zeros_like(cr)
            ci[...] = jnp.zeros_like(ci)

        u = u_ref[...]
        _s5_scan_tile(_dg(u, bbr[...], "nn"), _dg(u, bbi[...], "nn"), apr, api, tabr, tabi, cr, ci,
                      sr_ref, si_ref, False)
        y_ref[...] = _dg(sr_ref[...], ctr[...], "nn") - _dg(si_ref[...], cti[...], "nn") + d_ref[...] * u

    blk3 = lambda a, b: pl.BlockSpec((None, a, b), lambda j, i: (j, 0, 0))
    return pl.pallas_call(
        body, name="s5_fwd", grid=(S5_BLK, nt),
        in_specs=[pl.BlockSpec((t, S5_BC), lambda j, i: (i, j)),
                  blk3(S5_BC, S5_BS), blk3(S5_BC, S5_BS), blk3(S5_BS, S5_BC), blk3(S5_BS, S5_BC),
                  blk3(1, S5_BS), blk3(1, S5_BS), blk3(1, S5_BC)],
        out_specs=[pl.BlockSpec((t, S5_BC), lambda j, i: (i, j)),
                   pl.BlockSpec((t, S5_BS), lambda j, i: (i, j)),
                   pl.BlockSpec((t, S5_BS), lambda j, i: (i, j))],
        out_shape=[jax.ShapeDtypeStruct((length, S5_WIDTH), F32),
                   jax.ShapeDtypeStruct((length, S5_NS), F32),
                   jax.ShapeDtypeStruct((length, S5_NS), F32)],
        scratch_shapes=[pltpu.VMEM((S5_SUB, S5_BS), F32)] * 6,
        compiler_params=_cparams("parallel", "arbitrary"),
    )(p_s5, bb_re, bb_im, ct_re, ct_im, a_re, a_im, d_row)


def _s5_bwd(dy, p_s5, s_re, s_im, bb_re, bb_im, ct_re, ct_im, a_re, a_im, d_row, t_tile=256):
    length = p_s5.shape[0]
    t = min(t_tile, length)
    nt = length // t

    def body(dy_ref, u_ref, sr_ref, si_ref, pr_ref, pi_ref, bbr, bbi, ctr, cti, ar_ref, ai_ref, d_ref,
             du_ref, dbbr, dbbi, dctr, dcti, dar, dai, dd_ref, apr, api, tabr, tabi, cr, ci, lam_r, lam_i):
        i = pl.program_id(1)

        @pl.when(i == 0)
        def _():
            _s5_setup(ar_ref[...], -ai_ref[...], True, apr, api, tabr, tabi)
            for r in (cr, ci, dbbr, dbbi, dctr, dcti, dar, dai, dd_ref):
                r[...] = jnp.zeros_like(r)

        dyv = dy_ref[...]
        u = u_ref[...]
        _s5_scan_tile(_dg(dyv, ctr[...], "nt"), -_dg(dyv, cti[...], "nt"), apr, api, tabr, tabi, cr, ci,
                      lam_r, lam_i, True)
        lr, li = lam_r[...], lam_i[...]
        du_ref[...] = (_dg(lr, bbr[...], "nt") + _dg(li, bbi[...], "nt") + d_ref[...] * dyv).astype(du_ref.dtype)
        dbbr[...] += _dg(u, lr, "tn")
        dbbi[...] += _dg(u, li, "tn")
        sr = sr_ref[...]
        si = si_ref[...]
        dctr[...] += _dg(sr, dyv, "tn")
        dcti[...] -= _dg(si, dyv, "tn")
        dd_ref[...] += jnp.broadcast_to(_colsum(dyv * u), dd_ref.shape)
        row = lax.broadcasted_iota(jnp.int32, sr.shape, 0)
        has_prev = (i < nt - 1).astype(F32)
        ssr = jnp.where(row == 0, pr_ref[7:8, :] * has_prev, pltpu.roll(sr, 1, 0))
        ssi = jnp.where(row == 0, pi_ref[7:8, :] * has_prev, pltpu.roll(si, 1, 0))
        dar[...] += jnp.broadcast_to(_colsum(lr * ssr + li * ssi), dar.shape)
        dai[...] += jnp.broadcast_to(_colsum(li * ssr - lr * ssi), dai.shape)

    rev = lambda j, i: (nt - 1 - i, j)
    prev = lambda j, i: (jnp.maximum((nt - 1 - i) * (t // 8) - 1, 0), j)
    blk3 = lambda a, b: pl.BlockSpec((None, a, b), lambda j, i: (j, 0, 0))
    return pl.pallas_call(
        body, name="s5_bwd", grid=(S5_BLK, nt),
        in_specs=[pl.BlockSpec((t, S5_BC), rev), pl.BlockSpec((t, S5_BC), rev),
                  pl.BlockSpec((t, S5_BS), rev), pl.BlockSpec((t, S5_BS), rev),
                  pl.BlockSpec((8, S5_BS), prev), pl.BlockSpec((8, S5_BS), prev),
                  blk3(S5_BC, S5_BS), blk3(S5_BC, S5_BS), blk3(S5_BS, S5_BC), blk3(S5_BS, S5_BC),
                  blk3(1, S5_BS), blk3(1, S5_BS), blk3(1, S5_BC)],
        out_specs=[pl.BlockSpec((t, S5_BC), rev),
                   blk3(S5_BC, S5_BS), blk3(S5_BC, S5_BS), blk3(S5_BS, S5_BC), blk3(S5_BS, S5_BC),
                   blk3(8, S5_BS), blk3(8, S5_BS), blk3(8, S5_BC)],
        out_shape=[jax.ShapeDtypeStruct((length, S5_WIDTH), MXU_DT),
                   jax.ShapeDtypeStruct((S5_BLK, S5_BC, S5_BS), F32), jax.ShapeDtypeStruct((S5_BLK, S5_BC, S5_BS), F32),
                   jax.ShapeDtypeStruct((S5_BLK, S5_BS, S5_BC), F32), jax.ShapeDtypeStruct((S5_BLK, S5_BS, S5_BC), F32),
                   jax.ShapeDtypeStruct((S5_BLK, 8, S5_BS), F32), jax.ShapeDtypeStruct((S5_BLK, 8, S5_BS), F32),
                   jax.ShapeDtypeStruct((S5_BLK, 8, S5_BC), F32)],
        scratch_shapes=[pltpu.VMEM((S5_SUB, S5_BS), F32)] * 6 + [pltpu.VMEM((t, S5_BS), F32)] * 2,
        compiler_params=_cparams("parallel", "arbitrary"),
    )(dy, p_s5, s_re, s_im, s_re, s_im, bb_re, bb_im, ct_re, ct_im, a_re, a_im, d_row)


def _blockdiag(m, rows, cols):
    m = m.reshape(S5_BLK, 8, rows, 1, cols)
    on_diag = jnp.eye(8, dtype=bool)[None, :, None, :, None]
    return jnp.where(on_diag, m, 0).reshape(S5_BLK, 8 * rows, 8 * cols)


def _blockdiag_t(m, rows, cols):
    m = m.reshape(S5_BLK, 8, rows, 8, cols)
    on_diag = jnp.eye(8, dtype=bool)[None, :, None, :, None]
    return jnp.sum(jnp.where(on_diag, m, 0), axis=3).reshape(S5_GROUPS, rows, cols)


def _gelu(y):
    return jax.nn.gelu(y)


def _s5_out_fn(y, zg, gate, b):
    return _gelu(y) * jax.nn.sigmoid(zg + b) * jax.nn.silu(gate)


HALO = 8


def _conv_fwd(xbc, w, b, tr=256, tc=1024):
    length, width = xbc.shape
    tr = min(tr, length)

    def body(x_ref, h_ref, w_ref, b_ref, cv_ref, act_ref):
        i = pl.program_id(1)
        x = x_ref[...]
        xx = jnp.concatenate([h_ref[...] * (i > 0).astype(F32), x], axis=0)
        acc = b_ref[...] + w_ref[3:4, :] * x
        for k in range(SSD_CONV - 1):
            acc = acc + w_ref[k:k + 1, :] * pltpu.roll(xx, SSD_CONV - 1 - k, 0)[HALO:, :]
        cv_ref[...] = acc
        act_ref[...] = jax.nn.silu(acc)

    main = pl.BlockSpec((tr, tc), lambda j, i: (i, j))
    before = pl.BlockSpec((HALO, tc), lambda j, i: (jnp.maximum(i * (tr // HALO) - 1, 0), j))
    return pl.pallas_call(
        body, name="ssd_conv_fwd", grid=(width // tc, length // tr),
        in_specs=[main, before, pl.BlockSpec((SSD_CONV, tc), lambda j, i: (0, j)),
                  pl.BlockSpec((1, tc), lambda j, i: (0, j))],
        out_specs=[main, main],
        out_shape=[jax.ShapeDtypeStruct((length, width), F32)] * 2,
        compiler_params=_cparams("parallel", "arbitrary"),
    )(xbc, xbc, w, b.reshape(1, -1))


def _conv_bwd(dact, cv, xbc, w, tr=256, tc=1024):
    length, width = xbc.shape
    tr = min(tr, length)
    nr = length // tr
    n = tr + HALO

    def dsilu(d, c):
        sg = jax.nn.sigmoid(c)
        return d * (sg * (1.0 + c * (1.0 - sg)))

    def body(da_ref, dan_ref, cv_ref, cvn_ref, x_ref, xp_ref, w_ref, dx_ref, dw_ref, db_ref):
        i = pl.program_id(1)

        @pl.when(i == 0)
        def _():
            dw_ref[...] = jnp.zeros_like(dw_ref)
            db_ref[...] = jnp.zeros_like(db_ref)

        dc = dsilu(da_ref[...], cv_ref[...])
        dcn = dsilu(dan_ref[...], cvn_ref[...]) * (i < nr - 1).astype(F32)
        dd = jnp.concatenate([dc, dcn], axis=0)
        x = x_ref[...]
        xx = jnp.concatenate([xp_ref[...] * (i > 0).astype(F32), x], axis=0)
        dx = w_ref[3:4, :] * dc
        dw_ref[3:4, :] += _colsum(dc * x)
        for k in range(SSD_CONV - 1):
            j = SSD_CONV - 1 - k
            dx = dx + w_ref[k:k + 1, :] * pltpu.roll(dd, n - j, 0)[:tr, :]
            dw_ref[k:k + 1, :] += _colsum(dc * pltpu.roll(xx, j, 0)[HALO:, :])
        dx_ref[...] = dx.astype(dx_ref.dtype)
        db_ref[...] += jnp.broadcast_to(_colsum(dc), db_ref.shape)

    main = pl.BlockSpec((tr, tc), lambda j, i: (i, j))
    before = pl.BlockSpec((HALO, tc), lambda j, i: (jnp.maximum(i * (tr // HALO) - 1, 0), j))
    after = pl.BlockSpec((HALO, tc), lambda j, i: (jnp.minimum((i + 1) * (tr // HALO), length // HALO - 1), j))
    acc = pl.BlockSpec((8, tc), lambda j, i: (0, j))
    return pl.pallas_call(
        body, name="ssd_conv_bwd", grid=(width // tc, nr),
        in_specs=[main, after, main, after, main, before, pl.BlockSpec((SSD_CONV, tc), lambda j, i: (0, j))],
        out_specs=[main, acc, acc],
        out_shape=[jax.ShapeDtypeStruct((length, width), MXU_DT),
                   jax.ShapeDtypeStruct((8, width), F32), jax.ShapeDtypeStruct((8, width), F32)],
        compiler_params=_cparams("parallel", "arbitrary"),
    )(dact, dact, cv, cv, xbc, xbc, w)


def _tri(lower):
    r = lax.broadcasted_iota(jnp.int32, (SSD_CHUNK, SSD_CHUNK), 0)
    c = lax.broadcasted_iota(jnp.int32, (SSD_CHUNK, SSD_CHUNK), 1)
    return ((r >= c) if lower else (r <= c)).astype(F32)


def _dt_fwd(raw, bias, a_log):
    length = raw.shape[0]
    nc = length // SSD_CHUNK

    def body(r_ref, b_ref, a_ref, dt_ref, cum_ref, cumt_ref):
        dt = jax.nn.softplus(r_ref[...] + b_ref[...])
        cum = _dot_exact(_tri(True), dt * (-jnp.exp(a_ref[...])))
        dt_ref[...] = dt
        cum_ref[...] = cum
        cumt_ref[...] = cum.T

    blk = pl.BlockSpec((SSD_CHUNK, LANES), lambda c: (c, 0))
    row = pl.BlockSpec((1, LANES), lambda c: (0, 0))
    return pl.pallas_call(
        body, name="ssd_dt_fwd", grid=(nc,), in_specs=[blk, row, row],
        out_specs=[blk, blk, pl.BlockSpec((None, LANES, SSD_CHUNK), lambda c: (c, 0, 0))],
        out_shape=[jax.ShapeDtypeStruct((length, LANES), F32)] * 2
        + [jax.ShapeDtypeStruct((nc, LANES, SSD_CHUNK), F32)],
        compiler_params=_cparams("parallel"),
    )(raw, bias, a_log)


def _dt_bwd(raw, bias, a_log, ddt, dcum):
    length = raw.shape[0]
    nc = length // SSD_CHUNK

    def body(r_ref, b_ref, a_ref, ddt_ref, dcum_ref, dr_ref, db_ref, da_ref):
        @pl.when(pl.program_id(0) == 0)
        def _():
            db_ref[...] = jnp.zeros_like(db_ref)
            da_ref[...] = jnp.zeros_like(da_ref)

        z = r_ref[...] + b_ref[...]
        a = -jnp.exp(a_ref[...])
        dla = _dot_exact(_tri(False), dcum_ref[...])
        draw = (ddt_ref[...] + dla * a) * jax.nn.sigmoid(z)
        dr_ref[...] = draw.astype(dr_ref.dtype)
        db_ref[...] += jnp.broadcast_to(_colsum(draw), db_ref.shape)
        da_ref[...] += jnp.broadcast_to(_colsum(dla * jax.nn.softplus(z)) * a, da_ref.shape)

    blk = pl.BlockSpec((SSD_CHUNK, LANES), lambda c: (c, 0))
    row = pl.BlockSpec((1, LANES), lambda c: (0, 0))
    acc = pl.BlockSpec((8, LANES), lambda c: (0, 0))
    return pl.pallas_call(
        body, name="ssd_dt_bwd", grid=(nc,), in_specs=[blk, row, row, blk, blk],
        out_specs=[blk, acc, acc],
        out_shape=[jax.ShapeDtypeStruct((length, LANES), MXU_DT),
                   jax.ShapeDtypeStruct((8, LANES), F32), jax.ShapeDtypeStruct((8, LANES), F32)],
        compiler_params=_cparams("arbitrary"),
    )(raw, bias, a_log, ddt, dcum)


SSD_GW = SSD_HPG * SSD_HEAD_DIM
SSD_B_OFF = SSD_WIDTH // SSD_STATE
SSD_C_OFF = SSD_B_OFF + SSD_GROUPS


def _ssd_expand():
    r = lax.broadcasted_iota(jnp.int32, (LANES, SSD_GW), 0)
    c = lax.broadcasted_iota(jnp.int32, (LANES, SSD_GW), 1)
    return (c // SSD_HEAD_DIM == r).astype(F32)


def _ssd_to_channels(v16, e):
    return _dot_exact(v16, e)


def _ssd_to_heads(v, e):
    return lax.dot_general(v, e, (((1,), (1,)), ((), ())), precision=lax.Precision.HIGHEST,
                           preferred_element_type=F32)


def _ssd_decay(cum_ref, cumt_ref, r):
    q = lax.broadcasted_iota(jnp.int32, (SSD_CHUNK, SSD_CHUNK), 0)
    k = lax.broadcasted_iota(jnp.int32, (SSD_CHUNK, SSD_CHUNK), 1)
    return jnp.exp(jnp.where(q >= k, cum_ref[:, r:r + 1] - cumt_ref[r:r + 1, :], -1e30))


def _ssd_core_specs(nc, rev):
    ch = (lambda c: nc - 1 - c) if rev else (lambda c: c)
    xs = pl.BlockSpec((SSD_CHUNK, SSD_GW), lambda g, c: (ch(c), g))
    bspec = pl.BlockSpec((SSD_CHUNK, SSD_STATE), lambda g, c: (ch(c), SSD_B_OFF + g))
    cspec = pl.BlockSpec((SSD_CHUNK, SSD_STATE), lambda g, c: (ch(c), SSD_C_OFF + g))
    lane = pl.BlockSpec((None, SSD_CHUNK, LANES), lambda g, c: (g, ch(c), 0))
    rows = pl.BlockSpec((None, None, 16, SSD_CHUNK), lambda g, c: (g, ch(c), 0, 0))
    st = pl.BlockSpec((None, None, SSD_STATE, SSD_GW), lambda g, c: (g, ch(c), 0, 0))
    return xs, bspec, cspec, lane, rows, st


def _ssd_core_fwd(act, dtg, cumg, cumtg):
    length = act.shape[0]
    nc = length // SSD_CHUNK

    def body(x_ref, b_ref, c_ref, dt_ref, cum_ref, cumt_ref, y_ref, st_ref, s_scr):
        @pl.when(pl.program_id(1) == 0)
        def _():
            s_scr[...] = jnp.zeros_like(s_scr)

        e = _ssd_expand()
        bm = b_ref[...]
        cm = c_ref[...]
        cum_e = _ssd_to_channels(cum_ref[...], e)
        cl_e = cum_e[SSD_CHUNK - 1:SSD_CHUNK, :]
        xdt = x_ref[...] * _ssd_to_channels(dt_ref[...], e)
        st = s_scr[...]
        st_ref[...] = st
        g = _dg(cm, bm, "nt")
        y_off = _dg(cm, st, "nn") * jnp.exp(cum_e)
        for r in range(SSD_HPG):
            cols = slice(r * SSD_HEAD_DIM, (r + 1) * SSD_HEAD_DIM)
            y_ref[:, cols] = _dg(g * _ssd_decay(cum_ref, cumt_ref, r), xdt[:, cols], "nn") + y_off[:, cols]
        s_scr[...] = jnp.exp(cl_e) * st + _dg(bm, xdt * jnp.exp(cl_e - cum_e), "tn")

    xs, bspec, cspec, lane, rows, st = _ssd_core_specs(nc, False)
    return pl.pallas_call(
        body, name="ssd_core_fwd", grid=(SSD_GROUPS, nc),
        in_specs=[xs, bspec, cspec, lane, lane, rows], out_specs=[xs, st],
        out_shape=[jax.ShapeDtypeStruct((length, SSD_WIDTH), F32),
                   jax.ShapeDtypeStruct((SSD_GROUPS, nc, SSD_STATE, SSD_GW), F32)],
        scratch_shapes=[pltpu.VMEM((SSD_STATE, SSD_GW), F32)],
        compiler_params=_cparams("parallel", "arbitrary"),
    )(act, act, act, dtg, cumg, cumtg)


def _ssd_core_bwd(dy, dxs_add, act, dtg, cumg, cumtg, states):
    length = act.shape[0]
    nc = length // SSD_CHUNK

    def body(dy_ref, add_ref, x_ref, b_ref, c_ref, dt_ref, cum_ref, cumt_ref, st_ref,
             dx_ref, db_ref, dc_ref, ddt_ref, dcum_ref, dcumt_ref, dcl_ref, ds_scr, dxdt_scr):
        @pl.when(pl.program_id(1) == 0)
        def _():
            ds_scr[...] = jnp.zeros_like(ds_scr)

        e = _ssd_expand()
        bm = b_ref[...]
        cm = c_ref[...]
        x = x_ref[...]
        dyv = dy_ref[...]
        st = st_ref[...]
        dst_new = ds_scr[...]
        dt_e = _ssd_to_channels(dt_ref[...], e)
        cum_e = _ssd_to_channels(cum_ref[...], e)
        cl_e = cum_e[SSD_CHUNK - 1:SSD_CHUNK, :]
        xdt = x * dt_e
        exp_cum = jnp.exp(cum_e)
        exp_cl = jnp.exp(cl_e)
        z = xdt * jnp.exp(cl_e - cum_e)
        d_cs = dyv * exp_cum
        dcum_e = d_cs * _dg(cm, st, "nn")
        dcm = _dg(d_cs, st, "nt")
        ds_scr[...] = _dg(cm, d_cs, "tn") + exp_cl * dst_new
        dcl_e = _colsum(dst_new * st) * exp_cl
        dbm = _dg(z, dst_new, "nt")
        dz = _dg(bm, dst_new, "nn")
        de = dz * z
        dcl_e = dcl_e + _colsum(de)
        dcum_e = dcum_e - de
        dxdt_scr[...] = dz * jnp.exp(cl_e - cum_e)
        g = _dg(cm, bm, "nt")
        dg = jnp.zeros_like(g)
        dcum_ref[...] = _ssd_to_heads(dcum_e, e)
        dcumt_ref[...] = jnp.zeros_like(dcumt_ref)
        for r in range(SSD_HPG):
            cols = slice(r * SSD_HEAD_DIM, (r + 1) * SSD_HEAD_DIM)
            decay = _ssd_decay(cum_ref, cumt_ref, r)
            w = g * decay
            dy_r = dyv[:, cols]
            dw = _dg(dy_r, xdt[:, cols], "nt")
            dxdt_scr[:, cols] += _dg(w, dy_r, "tn")
            dg = dg + dw * decay
            dseg = dw * w
            dcum_ref[:, r:r + 1] += jnp.sum(dseg, axis=1, keepdims=True)
            dcumt_ref[r:r + 1, :] = -_colsum(dseg)
        dc_ref[...] = dcm + _dg(dg, bm, "nn")
        db_ref[...] = dbm + _dg(dg, cm, "tn")
        dxdt = dxdt_scr[...]
        dx_ref[...] = dxdt * dt_e + add_ref[...]
        ddt_ref[...] = _ssd_to_heads(dxdt * x, e)
        dcl_ref[...] = _ssd_to_heads(jnp.broadcast_to(dcl_e, (8, SSD_GW)), e)

    xs, bspec, cspec, lane, rows, st = _ssd_core_specs(nc, True)
    bc_out = pl.BlockSpec((SSD_CHUNK, SSD_STATE), lambda g, c: (nc - 1 - c, g))
    last = pl.BlockSpec((None, None, 8, LANES), lambda g, c: (g, nc - 1 - c, 0, 0))
    return pl.pallas_call(
        body, name="ssd_core_bwd", grid=(SSD_GROUPS, nc),
        in_specs=[xs, xs, xs, bspec, cspec, lane, lane, rows, st],
        out_specs=[xs, bc_out, bc_out, lane, lane, rows, last],
        out_shape=[jax.ShapeDtypeStruct((length, SSD_WIDTH), F32),
                   jax.ShapeDtypeStruct((length, SSD_GROUPS * SSD_STATE), F32),
                   jax.ShapeDtypeStruct((length, SSD_GROUPS * SSD_STATE), F32),
                   jax.ShapeDtypeStruct((SSD_GROUPS, length, LANES), F32),
                   jax.ShapeDtypeStruct((SSD_GROUPS, length, LANES), F32),
                   jax.ShapeDtypeStruct((SSD_GROUPS, nc, 16, SSD_CHUNK), F32),
                   jax.ShapeDtypeStruct((SSD_GROUPS, nc, 8, LANES), F32)],
        scratch_shapes=[pltpu.VMEM((SSD_STATE, SSD_GW), F32), pltpu.VMEM((SSD_CHUNK, SSD_GW), F32)],
        compiler_params=_cparams("parallel", "arbitrary"),
    )(dy, dxs_add, act, act, act, dtg, cumg, cumtg, states)


def _ssd_post_fn(yc, xs, z, dch, nw):
    y = (yc + dch * xs) * jax.nn.silu(z)
    return y * lax.rsqrt(jnp.mean(y * y, axis=-1, keepdims=True) + NORM_EPS) * nw


FOX_SCALE = 1.0 / math.sqrt(FOX_HEAD_DIM)
MASKED = -1e30


def _fgate_fwd(f_raw, b_f):
    length = f_raw.shape[0]
    nb = length // SSD_CHUNK

    def body(f_ref, b_ref, c_ref, carry):
        @pl.when(pl.program_id(0) == 0)
        def _():
            carry[...] = jnp.zeros_like(carry)

        c = _dot_exact(_tri(True), jax.nn.log_sigmoid(f_ref[...] + b_ref[...])) + carry[0:1, :]
        c_ref[...] = c
        carry[0:1, :] = c[SSD_CHUNK - 1:SSD_CHUNK, :]

    blk = pl.BlockSpec((SSD_CHUNK, LANES), lambda i: (i, 0))
    return pl.pallas_call(
        body, name="fox_fgate_fwd", grid=(nb,), in_specs=[blk, pl.BlockSpec((1, LANES), lambda i: (0, 0))],
        out_specs=blk, out_shape=jax.ShapeDtypeStruct((length, LANES), F32),
        scratch_shapes=[pltpu.VMEM((8, LANES), F32)], compiler_params=_cparams("arbitrary"),
    )(f_raw, b_f)


def _fgate_bwd(f_raw, b_f, dc):
    length = f_raw.shape[0]
    nb = length // SSD_CHUNK

    def body(f_ref, b_ref, dc_ref, df_ref, db_ref, carry):
        @pl.when(pl.program_id(0) == 0)
        def _():
            carry[...] = jnp.zeros_like(carry)
            db_ref[...] = jnp.zeros_like(db_ref)

        dcv = dc_ref[...]
        dlog = _dot_exact(_tri(False), dcv) + carry[0:1, :]
        carry[0:1, :] += _colsum(dcv)
        df = dlog * jax.nn.sigmoid(-(f_ref[...] + b_ref[...]))
        df_ref[...] = df.astype(df_ref.dtype)
        db_ref[...] += jnp.broadcast_to(_colsum(df), db_ref.shape)

    blk = pl.BlockSpec((SSD_CHUNK, LANES), lambda i: (nb - 1 - i, 0))
    return pl.pallas_call(
        body, name="fox_fgate_bwd", grid=(nb,),
        in_specs=[blk, pl.BlockSpec((1, LANES), lambda i: (0, 0)), blk],
        out_specs=[blk, pl.BlockSpec((8, LANES), lambda i: (0, 0))],
        out_shape=[jax.ShapeDtypeStruct((length, LANES), MXU_DT), jax.ShapeDtypeStruct((8, LANES), F32)],
        scratch_shapes=[pltpu.VMEM((8, LANES), F32)], compiler_params=_cparams("arbitrary"),
    )(f_raw, b_f, dc)


def _fox_scores(q, k, bias, diagonal):
    s = _dg(q, k, "nt") * FOX_SCALE + bias
    if diagonal:
        row = lax.broadcasted_iota(jnp.int32, s.shape, 0)
        col = lax.broadcasted_iota(jnp.int32, s.shape, 1)
        s = jnp.where(col <= row, s, MASKED)
    return s


FOX_TILE = 512


def _fox_c0(c, t):
    return jnp.repeat(jnp.repeat(c[::t, :FOX_HEADS], FOX_HEAD_DIM, axis=1), 8, axis=0)


def _fox_fwd(qkv, c0_rep, c_t, tile=FOX_TILE):
    length = qkv.shape[0]
    t = min(tile, length)
    nq = length // t

    def body(q_ref, k_ref, v_ref, c0_ref, ct_ref, o_ref, lse_ref):
        i = pl.program_id(1)
        q = q_ref[...]
        c0 = c0_ref[0:1, 0:1]

        def tile_step(k0, carry, diagonal):
            m, l, acc = carry
            s = _fox_scores(q, k_ref[pl.ds(k0, t), :], c0 - ct_ref[:, pl.ds(k0, t)], diagonal)
            m_new = jnp.maximum(m, jnp.max(s, axis=1, keepdims=True))
            p = jnp.exp(s - m_new)
            alpha = jnp.exp(m - m_new)
            return (m_new, alpha * l + jnp.sum(p, axis=1, keepdims=True),
                    alpha * acc + _dg(p, v_ref[pl.ds(k0, t), :], "nn"))

        init = (jnp.full((t, 1), MASKED, F32), jnp.zeros((t, 1), F32), jnp.zeros((t, FOX_HEAD_DIM), F32))
        carry = lax.fori_loop(0, i, lambda j, c: tile_step(pl.multiple_of(j * t, t), c, False), init)
        m, l, acc = tile_step(pl.multiple_of(i * t, t), carry, True)
        o_ref[...] = acc / l
        lse_ref[...] = jnp.broadcast_to(m + jnp.log(l), lse_ref.shape)

    qt = pl.BlockSpec((t, FOX_HEAD_DIM), lambda h, i: (i, h))
    return pl.pallas_call(
        body, name="fox_attn_fwd", grid=(FOX_HEADS, nq),
        in_specs=[qt,
                  pl.BlockSpec((length, FOX_HEAD_DIM), lambda h, i: (0, FOX_HEADS + h)),
                  pl.BlockSpec((length, FOX_HEAD_DIM), lambda h, i: (0, 2 * FOX_HEADS + h)),
                  pl.BlockSpec((8, FOX_HEAD_DIM), lambda h, i: (i, h)),
                  pl.BlockSpec((None, 1, length), lambda h, i: (h, 0, 0))],
        out_specs=[qt, qt],
        out_shape=[jax.ShapeDtypeStruct((length, FOX_WIDTH), F32)] * 2,
        compiler_params=_cparams("parallel", "arbitrary"),
    )(qkv, qkv, qkv, c0_rep, c_t)


def _fox_bwd(qkv, d_att, lse, delta, c0_rep, c_t, tile=FOX_TILE):
    length = qkv.shape[0]
    t = min(tile, length)
    nk = length // t

    def body(q_ref, k_ref, v_ref, do_ref, lse_ref, dl_ref, c0_ref, ct_ref,
             dq_ref, dk_ref, dv_ref, dcq_ref, dck_ref, dq_acc):
        j = pl.program_id(1)

        @pl.when(j == 0)
        def _():
            dq_acc[...] = jnp.zeros_like(dq_acc)
            dcq_ref[...] = jnp.zeros_like(dcq_ref)

        k = k_ref[...]
        v = v_ref[...]
        ck = ct_ref[...]

        def tile_step(i, carry, diagonal):
            dk, dv, dck = carry
            rows = pl.ds(pl.multiple_of(i * t, t), t)
            q = q_ref[rows, :]
            do = do_ref[rows, :]
            c0 = c0_ref[pl.ds(pl.multiple_of(i * 8, 8), 8), :][0:1, 0:1]
            s = _fox_scores(q, k, c0 - ck, diagonal)
            p = jnp.exp(s - lse_ref[rows, 0:1])
            dv = dv + _dg(p, do, "tn")
            ds = p * (_dg(do, v, "nt") - dl_ref[rows, 0:1])
            dk = dk + _dg(ds, q, "tn") * FOX_SCALE
            dq_acc[rows, :] += _dg(ds, k, "nn") * FOX_SCALE
            dcq_ref[rows, :] += jnp.broadcast_to(jnp.sum(ds, axis=1, keepdims=True), (t, FOX_HEAD_DIM))
            return dk, dv, dck + _colsum(ds)

        init = (jnp.zeros((t, FOX_HEAD_DIM), F32), jnp.zeros((t, FOX_HEAD_DIM), F32), jnp.zeros((1, t), F32))
        carry = tile_step(j, init, True)
        dk, dv, dck = lax.fori_loop(j + 1, nk, lambda i, c: tile_step(i, c, False), carry)
        dk_ref[...] = dk.astype(dk_ref.dtype)
        dv_ref[...] = dv.astype(dv_ref.dtype)
        dck_ref[...] = -dck

        @pl.when(j == nk - 1)
        def _():
            dq_ref[...] = dq_acc[...].astype(dq_ref.dtype)

    full = lambda off: pl.BlockSpec((length, FOX_HEAD_DIM), lambda h, j: (0, off + h))
    kt = lambda off: pl.BlockSpec((t, FOX_HEAD_DIM), lambda h, j: (j, off + h))
    ck_spec = pl.BlockSpec((None, 1, t), lambda h, j: (h, 0, j))
    return pl.pallas_call(
        body, name="fox_attn_bwd", grid=(FOX_HEADS, nk),
        in_specs=[full(0), kt(FOX_HEADS), kt(2 * FOX_HEADS), full(0), full(0), full(0),
                  pl.BlockSpec((8 * nk, FOX_HEAD_DIM), lambda h, j: (0, h)), ck_spec],
        out_specs=[full(0), kt(0), kt(0), full(0), ck_spec],
        out_shape=[jax.ShapeDtypeStruct((length, FOX_WIDTH), MXU_DT)] * 3
        + [jax.ShapeDtypeStruct((length, FOX_WIDTH), F32), jax.ShapeDtypeStruct((FOX_HEADS, 1, length), F32)],
        scratch_shapes=[pltpu.VMEM((length, FOX_HEAD_DIM), F32)],
        compiler_params=_cparams("parallel", "arbitrary"),
    )(qkv, qkv, qkv, d_att, lse, delta, c0_rep, c_t)


def _fox_gate_fn(att, gate):
    return att * jax.nn.silu(gate)


N_CHIP = 4


def _other_chips(mx, my):
    return [(1 - mx, my), (mx, 1 - my), (1 - mx, 1 - my)]


def _handshake(peers):
    barrier = pltpu.get_barrier_semaphore()
    for peer in peers:
        pl.semaphore_signal(barrier, inc=1, device_id=peer, device_id_type=pl.DeviceIdType.MESH)
    pl.semaphore_wait(barrier, len(peers))


def _exchange_call(body, x, out_struct, name, n_sems, local_sem, collective_id):
    sems = [pltpu.SemaphoreType.DMA((n_sems,)), pltpu.SemaphoreType.DMA((n_sems,))]
    sems += [pltpu.SemaphoreType.DMA] if local_sem else []
    if collective_id is None:
        return pl.pallas_call(
            body, name=name, in_specs=[ANYSPACE], out_specs=ANYSPACE, out_shape=out_struct, scratch_shapes=sems,
            compiler_params=pltpu.CompilerParams(has_side_effects=True),
        )(x)
    x_ref = jax.new_ref(x, memory_space=pltpu.MemorySpace.HBM)
    o_ref = jax.empty_ref(out_struct, memory_space=pltpu.MemorySpace.HBM)

    @pl.kernel(mesh=plsc.ScalarSubcoreMesh(axis_name="sequencer", num_cores=1), name=name,
               scratch_types=tuple(sems), compiler_params=pltpu.CompilerParams(collective_id=collective_id))
    def launch(*sem_refs):
        body(x_ref, o_ref, *sem_refs)

    launch()
    return o_ref[...]


def _gather(x, name, collective_id=None):
    def body(x_ref, o_ref, send_sems, recv_sems, local_sem):
        mx, my, mc = lax.axis_index("x"), lax.axis_index("y"), lax.axis_index("c")
        me, sibling = (mx, my, mc), (mx, my, 1 - mc)
        chips = _other_chips(mx, my)
        if collective_id is not None:
            _handshake([sibling] + [(*chip, mc) for chip in chips])

        def slot(px, py, pc):
            return o_ref.at[4 * px + 2 * py + pc]

        def copy(k, block, to, src=None):
            return pltpu.make_async_remote_copy(
                src_ref=slot(*block) if src is None else src, dst_ref=slot(*block),
                send_sem=send_sems.at[k], recv_sem=recv_sems.at[k],
                device_id=to, device_id_type=pl.DeviceIdType.MESH)

        mine = pltpu.make_async_copy(x_ref, slot(*me), local_sem)
        mine.start()
        first = [copy(0, me, sibling, src=x_ref)]
        first += [copy(1 + j, me, (*chip, mc), src=x_ref) for j, chip in enumerate(chips)]
        for cp in first:
            cp.start()
        passed = [copy(4 + j, (*chip, mc), sibling) for j, chip in enumerate(chips)]
        for j, chip in enumerate(chips):
            copy(1 + j, (*chip, mc), me).wait_recv()
            passed[j].start()
        copy(0, sibling, me).wait_recv()
        for j, chip in enumerate(chips):
            copy(4 + j, (*chip, 1 - mc), me).wait_recv()
        for cp in first + passed:
            cp.wait_send()
        mine.wait()

    return _exchange_call(body, x, jax.ShapeDtypeStruct((N_DEV,) + x.shape, x.dtype), name, N_DEV - 1, True,
                          collective_id)


def _pair_send(parts, name, collective_id=None):
    def body(p_ref, o_ref, send_sems, recv_sems):
        mx, my, mc = lax.axis_index("x"), lax.axis_index("y"), lax.axis_index("c")
        if collective_id is not None:
            _handshake([(mx, my, 1 - mc)])
        copies = [pltpu.make_async_remote_copy(
            src_ref=p_ref.at[2 * chip + (1 - mc)], dst_ref=o_ref.at[chip],
            send_sem=send_sems.at[chip], recv_sem=recv_sems.at[chip],
            device_id=(mx, my, 1 - mc), device_id_type=pl.DeviceIdType.MESH) for chip in range(N_CHIP)]
        for cp in copies:
            cp.start()
        for cp in copies:
            cp.wait_recv()
        for cp in copies:
            cp.wait_send()

    return _exchange_call(body, parts, jax.ShapeDtypeStruct((N_CHIP,) + parts.shape[1:], parts.dtype), name,
                          N_CHIP, False, collective_id)


def _pair_sum(parts, recv, out_dtype, name):
    _, rows, cols = parts.shape
    tr, tc = _tile2d(rows, cols)

    def body(c_ref, p_ref, r_ref, o_ref):
        o_ref[...] = (p_ref[...] + r_ref[...]).astype(o_ref.dtype)

    return pl.pallas_call(
        body, name=name,
        grid_spec=pltpu.PrefetchScalarGridSpec(
            num_scalar_prefetch=1, grid=(N_CHIP, rows // tr, cols // tc),
            in_specs=[pl.BlockSpec((None, tr, tc), lambda k, i, j, c: (2 * k + c[0], i, j)),
                      pl.BlockSpec((None, tr, tc), lambda k, i, j, c: (k, i, j))],
            out_specs=pl.BlockSpec((None, tr, tc), lambda k, i, j, c: (k, i, j))),
        out_shape=jax.ShapeDtypeStruct((N_CHIP, rows, cols), out_dtype),
        compiler_params=_cparams("parallel", "parallel", "parallel"),
    )(lax.axis_index("c").astype(jnp.int32).reshape(1), parts, recv)


def _chip_exchange(sums, name, collective_id=None):
    def body(s_ref, o_ref, send_sems, recv_sems, local_sem):
        mx, my, mc = lax.axis_index("x"), lax.axis_index("y"), lax.axis_index("c")
        my_chip = 2 * mx + my
        if collective_id is not None:
            _handshake([(px, py, mc) for px, py in _other_chips(mx, my)])
        local = pltpu.make_async_copy(s_ref.at[my_chip], o_ref.at[my_chip], local_sem)
        local.start()
        sends, recvs = [], []
        for k, (px, py) in enumerate(_other_chips(mx, my)):
            peer = 2 * px + py

            def copy(src_slot, dst_slot, k=k, dev=(px, py, mc)):
                return pltpu.make_async_remote_copy(
                    src_ref=s_ref.at[src_slot], dst_ref=o_ref.at[dst_slot], send_sem=send_sems.at[k],
                    recv_sem=recv_sems.at[k], device_id=dev, device_id_type=pl.DeviceIdType.MESH)

            sends.append(copy(peer, my_chip))
            recvs.append(copy(peer, peer))
        for cp in sends:
            cp.start()
        for cp in recvs:
            cp.wait_recv()
        for cp in sends:
            cp.wait_send()
        local.wait()

    return _exchange_call(body, sums, jax.ShapeDtypeStruct(sums.shape, sums.dtype), name, N_CHIP - 1, True,
                          collective_id)


ADAM_TILE_ELEMS = 128 * 1024


def _tile2d(rows, cols):
    if rows * cols <= ADAM_TILE_ELEMS:
        return rows, cols
    if rows % 8 == 0:
        return _pick(rows, max(8, ADAM_TILE_ELEMS // cols), 8), cols
    return rows, _pick(cols, max(LANES, ADAM_TILE_ELEMS // rows))


def _adamw(w, parts, m, v, name):
    rows, cols = w.shape
    n_parts = parts.shape[0]
    tr, tc = _tile2d(rows, cols)

    def body(w_ref, p_ref, m_ref, v_ref, g_ref, d_ref, nm_ref, nv_ref):
        g = p_ref[0].astype(F32)
        for p in range(1, n_parts):
            g = g + p_ref[p].astype(F32)
        mm = ADAM_B1 * m_ref[...] + (1.0 - ADAM_B1) * g
        vv = ADAM_B2 * v_ref[...] + (1.0 - ADAM_B2) * jnp.square(g)
        m_hat = mm / (1.0 - ADAM_B1 ** ADAM_STEP)
        v_hat = vv / (1.0 - ADAM_B2 ** ADAM_STEP)
        g_ref[...] = g
        d_ref[...] = -ADAM_LR * (m_hat / (jnp.sqrt(v_hat) + ADAM_EPS) + ADAM_WD * w_ref[...])
        nm_ref[...] = mm
        nv_ref[...] = vv

    blk = pl.BlockSpec((tr, tc), lambda i, j: (i, j))
    return pl.pallas_call(
        body, name=name, grid=(rows // tr, cols // tc),
        in_specs=[blk, pl.BlockSpec((n_parts, tr, tc), lambda i, j: (0, i, j)), blk, blk],
        out_specs=[blk] * 4, out_shape=[jax.ShapeDtypeStruct((rows, cols), F32)] * 4,
        compiler_params=_cparams("parallel", "parallel"),
    )(w, parts, m, v)


WEIGHTS = ("l0_norm_w", "l0_w_in", "l0_s5_lambda_re", "l0_s5_lambda_im", "l0_s5_log_step", "l0_s5_b_re",
           "l0_s5_b_im", "l0_s5_c_re", "l0_s5_c_im", "l0_s5_d", "l0_s5_w_glu", "l0_s5_b_glu", "l0_ssd_conv_w",
           "l0_ssd_conv_b", "l0_ssd_dt_bias", "l0_ssd_a_log", "l0_ssd_d", "l0_ssd_norm_w", "l0_w_out",
           "l1_norm_w", "l1_w_in", "l1_fox_b_f", "l1_w_out", "final_norm_w")
SHARDED = ("l0_w_in", "l0_s5_w_glu", "l0_ssd_conv_w", "l0_w_out", "l1_w_in", "l1_w_out")


def _pad_lanes(a, width=LANES):
    return jnp.pad(a, [(0, 0)] * (a.ndim - 1) + [(0, width - a.shape[-1])])


def _pad_rows(a, height=LANES):
    return jnp.pad(a, [(0, height - a.shape[0])] + [(0, 0)] * (a.ndim - 1))


def _to_groups(a):
    length = a.shape[0]
    return _pad_lanes(a[:, :SSD_HEADS].reshape(length, SSD_GROUPS, SSD_HPG).transpose(1, 0, 2))


def _from_groups(a):
    length = a.shape[1]
    return a[:, :, :SSD_HPG].transpose(1, 0, 2).reshape(length, SSD_HEADS)


class _Grads(dict):
    def __init__(self, on_grad):
        super().__init__()
        self.on_grad = on_grad

    def __setitem__(self, name, value):
        if self.on_grad is not None and name in SHARDED:
            value = self.on_grad(name, value)
        super().__setitem__(name, value)


def _local_step(x, target, w, on_grad=None):
    length = x.shape[0]
    nc = length // SSD_CHUNK
    g = _Grads(on_grad)

    h0 = _rmsnorm_fwd(x, w["l0_norm_w"], "l0_norm")
    w0 = w["l0_w_in"]
    r0_s5, r0_z, r0_xbc = (0, 2 * S5_WIDTH), (2 * S5_WIDTH, SSD_WIDTH), (2 * S5_WIDTH + SSD_WIDTH, SSD_XBC)
    r0_dt = 2 * S5_WIDTH + SSD_WIDTH + SSD_XBC
    w0_dt = _pad_rows(w0[r0_dt:])
    p_s5 = _mm(h0, w0, "nt", F32, "l0_in_s5", b_rows=r0_s5)
    p_z = _mm(h0, w0, "nt", F32, "l0_in_z", b_rows=r0_z)
    p_xbc = _mm(h0, w0, "nt", F32, "l0_in_xbc", b_rows=r0_xbc)
    p_dt = _mm(h0, w0_dt, "nt", F32, "l0_in_dt")

    row = lambda a: a.reshape(1, S5_NS)
    b_rows = lambda a: a.transpose(2, 0, 1).reshape(S5_GROUP, S5_NS)
    prep_in = (row(w["l0_s5_lambda_re"]), row(w["l0_s5_lambda_im"]),
               row(jnp.repeat(w["l0_s5_log_step"], S5_STATE)), b_rows(w["l0_s5_b_re"]), b_rows(w["l0_s5_b_im"]))
    ab_re, ab_im, bbr, bbi = _s5_prep(*prep_in)
    to_bb = lambda a: _blockdiag(a.reshape(S5_GROUP, S5_GROUPS, S5_STATE).transpose(1, 0, 2),
                                 S5_GROUP, S5_STATE).astype(MXU_DT)
    to_ct = lambda a: _blockdiag(a.transpose(0, 2, 1), S5_STATE, S5_GROUP).astype(MXU_DT)
    bb_re, bb_im = to_bb(bbr), to_bb(bbi)
    ct_re, ct_im = to_ct(w["l0_s5_c_re"]), to_ct(w["l0_s5_c_im"])
    a_re3, a_im3 = ab_re.reshape(S5_BLK, 1, S5_BS), ab_im.reshape(S5_BLK, 1, S5_BS)
    d3 = w["l0_s5_d"].reshape(S5_BLK, 1, S5_BC)
    y5, s_re, s_im = _s5_fwd(p_s5, bb_re, bb_im, ct_re, ct_im, a_re3, a_im3, d3)
    g_bf = _tiles("s5_gelu", lambda yb: (_gelu(yb),), [y5], [], [(S5_WIDTH, MXU_DT)], 0, tr=256)[0]
    zg = _mm(g_bf, w["l0_s5_w_glu"], "nn", F32, "s5_glu")
    b_glu = w["l0_s5_b_glu"].reshape(1, -1)
    s5_out = _tiles("s5_out", lambda yb, zb, gb, bb: (_s5_out_fn(yb, zb, gb, bb),),
                    [y5, zg, (p_s5, 1)], [b_glu], [(S5_WIDTH, MXU_DT)], 0, tr=256)[0]

    conv_w = w["l0_ssd_conv_w"]
    cv, act = _conv_fwd(p_xbc, conv_w, w["l0_ssd_conv_b"])
    bias_row = _pad_lanes(w["l0_ssd_dt_bias"].reshape(1, -1))
    alog_row = _pad_lanes(w["l0_ssd_a_log"].reshape(1, -1))
    dt, cum, cum_t = _dt_fwd(p_dt, bias_row, alog_row)
    dtg, cumg = _to_groups(dt), _to_groups(cum)
    cumtg = cum_t[:, :SSD_HEADS].reshape(nc, SSD_GROUPS, SSD_HPG, SSD_CHUNK).transpose(1, 0, 2, 3)
    cumtg = jnp.pad(cumtg, ((0, 0), (0, 0), (0, 16 - SSD_HPG), (0, 0)))
    ycore, states = _ssd_core_fwd(act, dtg, cumg, cumtg)
    dchan = jnp.repeat(w["l0_ssd_d"], SSD_HEAD_DIM).reshape(1, -1)
    nw_row = w["l0_ssd_norm_w"].reshape(1, -1)
    ssd_out = _tiles("ssd_post", lambda a, b, c, d, e: (_ssd_post_fn(a, b, c, d, e),),
                     [ycore, act, p_z], [dchan, nw_row], [(SSD_WIDTH, MXU_DT)], 0, tr=256, tc=SSD_GW)[0]
    mixed = jnp.concatenate([s5_out, ssd_out], axis=1)
    x1 = _mm(mixed, w["l0_w_out"], "nn", F32, "l0_out", res=x)

    h1 = _rmsnorm_fwd(x1, w["l1_norm_w"], "l1_norm")
    w1 = w["l1_w_in"]
    r1_qkv, r1_gate = (0, 3 * FOX_WIDTH), (3 * FOX_WIDTH, FOX_WIDTH)
    w1_f = _pad_rows(w1[4 * FOX_WIDTH:])
    qkv = _mm(h1, w1, "nt", MXU_DT, "l1_in_qkv", b_rows=r1_qkv)
    gate1 = _mm(h1, w1, "nt", F32, "l1_in_gate", b_rows=r1_gate)
    f_raw = _mm(h1, w1_f, "nt", F32, "l1_in_f")
    bf_row = _pad_lanes(w["l1_fox_b_f"].reshape(1, -1))
    c = _fgate_fwd(f_raw, bf_row)
    c0_rep = _fox_c0(c, min(FOX_TILE, length))
    c_t = c[:, :FOX_HEADS].T.reshape(FOX_HEADS, 1, length)
    att, lse = _fox_fwd(qkv, c0_rep, c_t)
    out1 = _tiles("fox_gate", lambda a, b: (_fox_gate_fn(a, b),), [att, gate1], [],
                  [(FOX_WIDTH, MXU_DT)], 0, tr=256)[0]
    x2 = _mm(out1, w["l1_w_out"], "nn", F32, "l1_out", res=x1)

    loss_part, dx2, dx2b, g["final_norm_w"] = _final_loss(x2, w["final_norm_w"], target, "final_loss")

    d_out1 = _mm(dx2b, w["l1_w_out"], "nt", F32, "l1_out_dx")
    g["l1_w_out"] = _mm(out1, dx2b, "tn", F32, "l1_out_dw")

    def gate_bwd(a, gt, d):
        _, vjp = jax.vjp(_fox_gate_fn, a, gt)
        da, dgt = vjp(d)
        return da, dgt, jnp.broadcast_to(jnp.sum(da * a, axis=1, keepdims=True), a.shape)

    d_att, d_gate1, delta = _tiles("fox_gate_bwd", gate_bwd, [att, gate1, d_out1], [],
                                   [(FOX_WIDTH, MXU_DT), (FOX_WIDTH, MXU_DT), (FOX_WIDTH, F32)], 0,
                                   tr=512, tc=FOX_HEAD_DIM)
    dq, dk, dv, dcq, dck = _fox_bwd(qkv, d_att, lse, delta, c0_rep, c_t)
    dc = dcq.reshape(length, FOX_HEADS, FOX_HEAD_DIM)[:, :, 0] + dck.reshape(FOX_HEADS, length).T
    df, dbf = _fgate_bwd(f_raw, bf_row, _pad_lanes(dc))
    g["l1_fox_b_f"] = dbf[0, :FOX_HEADS]
    dqkv = jnp.concatenate([dq, dk, dv], axis=1)
    dh1 = _mm(dqkv, w1, "nn", F32, "l1_in_dx_qkv", b_rows=r1_qkv)
    dh1 = _mm(d_gate1, w1, "nn", F32, "l1_in_dx_gate", res=dh1, b_rows=r1_gate)
    dh1 = _mm(df, w1_f, "nn", F32, "l1_in_dx_f", res=dh1)
    dw1 = _mm(dqkv, h1, "tn", F32, "l1_in_dw_qkv", into=(lax.empty((ODD_IN, D_MODEL), F32), r1_qkv[0]))
    dw1 = _mm(d_gate1, h1, "tn", F32, "l1_in_dw_gate", into=(dw1, r1_gate[0]))
    g["l1_w_in"] = dw1.at[4 * FOX_WIDTH:].set(_mm(df, h1, "tn", F32, "l1_in_dw_f")[:FOX_HEADS])
    dx1, dx1b, g["l1_norm_w"] = _rmsnorm_bwd(x1, w["l1_norm_w"], dh1, dx2, "l1_norm_bwd")

    wout0 = w["l0_w_out"]
    d_s5 = _mm(dx1b, wout0, "nt", F32, "l0_out_dx_s5", b_rows=(0, S5_WIDTH))
    d_ssd = _mm(dx1b, wout0, "nt", F32, "l0_out_dx_ssd", b_rows=(S5_WIDTH, SSD_WIDTH))
    g["l0_w_out"] = _mm(mixed, dx1b, "tn", F32, "l0_out_dw")

    def post_bwd(a, b, c_, d, dch, nw):
        _, vjp = jax.vjp(_ssd_post_fn, a, b, c_, dch, nw)
        return vjp(d)

    dycore, dxs_post, dz, ddch, dnw = _tiles(
        "ssd_post_bwd", post_bwd, [ycore, act, p_z, d_ssd], [dchan, nw_row],
        [(SSD_WIDTH, F32), (SSD_WIDTH, F32), (SSD_WIDTH, MXU_DT)], 2, tr=256, tc=SSD_GW)
    g["l0_ssd_d"] = ddch[0].reshape(SSD_HEADS, SSD_HEAD_DIM).sum(axis=1)
    g["l0_ssd_norm_w"] = dnw[0]
    dxs, d_b, d_c, ddtg, dcumg, dcumtg, dclg = _ssd_core_bwd(dycore, dxs_post, act, dtg, cumg, cumtg, states)
    dact = jnp.concatenate([dxs, d_b, d_c], axis=1)
    dxbc, dconvw, dconvb = _conv_bwd(dact, cv, p_xbc, conv_w)
    g["l0_ssd_conv_w"] = dconvw[:SSD_CONV]
    g["l0_ssd_conv_b"] = dconvb[0]
    dcum = _from_groups(dcumg)
    dcum = dcum + dcumtg[:, :, :SSD_HPG].transpose(1, 3, 0, 2).reshape(length, SSD_HEADS)
    dcl = dclg[:, :, 0, :SSD_HPG].transpose(1, 0, 2).reshape(nc, SSD_HEADS)
    dcum = dcum.reshape(nc, SSD_CHUNK, SSD_HEADS).at[:, SSD_CHUNK - 1, :].add(dcl).reshape(length, SSD_HEADS)
    ddt_raw, dbias, dalog = _dt_bwd(p_dt, bias_row, alog_row, _pad_lanes(_from_groups(ddtg)), _pad_lanes(dcum))
    g["l0_ssd_dt_bias"] = dbias[0, :SSD_HEADS]
    g["l0_ssd_a_log"] = dalog[0, :SSD_HEADS]

    def s5_out_bwd(yb, zb, gb, d, bb):
        _, vjp = jax.vjp(_s5_out_fn, yb, zb, gb, bb)
        return vjp(d)

    dy_direct, dzg, dgate0, dbglu = _tiles(
        "s5_out_bwd", s5_out_bwd, [y5, zg, (p_s5, 1), d_s5], [b_glu],
        [(S5_WIDTH, F32), (S5_WIDTH, MXU_DT), (S5_WIDTH, MXU_DT)], 1, tr=256)
    g["l0_s5_b_glu"] = dbglu[0]
    g["l0_s5_w_glu"] = _mm(g_bf, dzg, "tn", F32, "s5_glu_dw")
    dg2 = _mm(dzg, w["l0_s5_w_glu"], "nt", F32, "s5_glu_dx")

    def gelu_bwd(yb, d, direct):
        _, vjp = jax.vjp(_gelu, yb)
        return (vjp(d)[0] + direct,)

    dy5 = _tiles("s5_gelu_bwd", gelu_bwd, [y5, dg2, dy_direct], [], [(S5_WIDTH, F32)], 0, tr=256)[0]
    du, dbbr3, dbbi3, dctr3, dcti3, dar, dai, dd5 = _s5_bwd(dy5, p_s5, s_re, s_im, bb_re, bb_im, ct_re, ct_im,
                                                           a_re3, a_im3, d3)
    from_bb = lambda a: _blockdiag_t(a, S5_GROUP, S5_STATE).transpose(1, 0, 2).reshape(S5_GROUP, S5_NS)
    from_ct = lambda a: _blockdiag_t(a, S5_STATE, S5_GROUP).transpose(0, 2, 1)
    g["l0_s5_c_re"], g["l0_s5_c_im"] = from_ct(dctr3), from_ct(dcti3)
    g["l0_s5_d"] = dd5[:, 0, :].reshape(S5_GROUPS, S5_GROUP)
    dlr, dli, dls, dbr, dbi = _s5_prep_bwd(*prep_in, dar[:, 0, :].reshape(1, S5_NS), dai[:, 0, :].reshape(1, S5_NS),
                                           from_bb(dbbr3), from_bb(dbbi3))
    g["l0_s5_lambda_re"] = dlr.reshape(S5_GROUPS, S5_STATE)
    g["l0_s5_lambda_im"] = dli.reshape(S5_GROUPS, S5_STATE)
    g["l0_s5_log_step"] = dls.reshape(S5_GROUPS, S5_STATE).sum(axis=1)
    from_rows = lambda a: a.reshape(S5_GROUP, S5_GROUPS, S5_STATE).transpose(1, 2, 0)
    g["l0_s5_b_re"], g["l0_s5_b_im"] = from_rows(dbr), from_rows(dbi)

    dus = jnp.concatenate([du, dgate0], axis=1)
    dw0 = _mm(dus, h0, "tn", F32, "l0_in_dw_s5", into=(lax.empty((EVEN_IN, D_MODEL), F32), r0_s5[0]))
    dw0 = _mm(dz, h0, "tn", F32, "l0_in_dw_z", into=(dw0, r0_z[0]))
    dw0 = _mm(dxbc, h0, "tn", F32, "l0_in_dw_xbc", into=(dw0, r0_xbc[0]))
    g["l0_w_in"] = dw0.at[r0_dt:].set(_mm(ddt_raw, h0, "tn", F32, "l0_in_dw_dt")[:SSD_HEADS])
    dh0 = _mm(dus, w0, "nn", F32, "l0_in_dx_s5", b_rows=r0_s5)
    dh0 = _mm(dz, w0, "nn", F32, "l0_in_dx_z", res=dh0, b_rows=r0_z)
    dh0 = _mm(dxbc, w0, "nn", F32, "l0_in_dx_xbc", res=dh0, b_rows=r0_xbc)
    dh0 = _mm(ddt_raw, w0_dt, "nn", F32, "l0_in_dx_dt", res=dh0)
    grad_x, _, g["l0_norm_w"] = _rmsnorm_bwd(x, w["l0_norm_w"], dh0, dx1, "l0_norm_bwd")
    return loss_part, grad_x, g


TRANSPOSED = ("l0_w_in", "l1_w_in")


SEQUENCER_IDS = {"l0_w_out": (0, 1, 2), "l1_w_in": (3, 4, 5), "l1_w_out": (6, 7, 8), "l0_w_in": (None, 9, 10),
                 "l0_s5_w_glu": (11, 12, 13), "l0_ssd_conv_w": (14, 15, 16)}
NO_IDS = (None, None, None)


def _gather_weight(name, shard):
    cid = SEQUENCER_IDS.get(name, NO_IDS)[0]
    if name == "l0_ssd_conv_w":
        full = _gather(shard, "gather_" + name, cid)
        return full.transpose(1, 0, 2).reshape(shard.shape[0], N_DEV * shard.shape[1])
    if name in TRANSPOSED:
        full = _gather(shard.T.astype(MXU_DT), "gather_" + name, cid)
        return full.reshape(N_DEV * shard.shape[1], shard.shape[0])
    full = _gather(shard.astype(MXU_DT), "gather_" + name, cid)
    return full.reshape(N_DEV * shard.shape[0], shard.shape[1])


def _reduce_grad(name, grad, shard_shape):
    rows, cols = shard_shape
    if name == "l0_ssd_conv_w":
        parts = grad.reshape(rows, N_DEV, cols).transpose(1, 0, 2)
    elif name in TRANSPOSED:
        parts = grad.reshape(N_DEV, cols, rows)
    else:
        parts = grad.reshape(N_DEV, rows, cols)
    _, pair_id, chip_id = SEQUENCER_IDS.get(name, NO_IDS)
    recv = _pair_send(parts, "pair_" + name, pair_id)
    sums = _pair_sum(parts, recv, F32 if name == "l0_ssd_conv_w" else MXU_DT, "pairsum_" + name)
    return _chip_exchange(sums, "scatter_" + name, chip_id)


SMALL_ROWS_QUANTUM = 8 * LANES


def _step(args):
    x = args["x"][0]
    target = args["loss_target"][0]
    full = {n: args[n] for n in WEIGHTS if n not in SHARDED}
    for n in SHARDED:
        shard = args[n]
        if n != SHARDED[0]:
            shard, full[SHARDED[0]] = lax.optimization_barrier((shard, full[SHARDED[0]]))
        full[n] = _gather_weight(n, shard)
    out_g, out_d, out_m, out_v = {}, {}, {}, {}

    loss_part, grad_x, g = _local_step(x, target, full, lambda n, grad: _reduce_grad(n, grad, args[n].shape))
    for n in reversed(SHARDED):
        view = (lambda a: a.T) if n in TRANSPOSED else (lambda a: a)
        outs = _adamw(view(args[n]), g[n], view(args["m_" + n]), view(args["v_" + n]), "adamw_" + n)
        out_g[n], out_d[n], out_m[n], out_v[n] = [view(o) for o in outs]

    small = [n for n in WEIGHTS if n not in SHARDED]
    sizes = [int(math.prod(args[n].shape)) for n in small]
    total = sum(sizes) + 1
    padded = -(-total // SMALL_ROWS_QUANTUM) * SMALL_ROWS_QUANTUM

    def pack(pieces, extra):
        flat = jnp.concatenate([p.reshape(-1).astype(F32) for p in pieces] + [extra.reshape(1)])
        return jnp.pad(flat, (0, padded - total)).reshape(padded // LANES, LANES)

    zero = jnp.zeros((), F32)
    parts = _gather(pack([g[n] for n in small], loss_part), "gather_small_grads")
    sg, sd, sm, sv = _adamw(pack([args[n] for n in small], zero), parts,
                            pack([args["m_" + n] for n in small], zero),
                            pack([args["v_" + n] for n in small], zero), "adamw_small")
    off = 0
    for n, sz in zip(small, sizes):
        cut = lambda a: a.reshape(-1)[off:off + sz].reshape(args[n].shape)
        out_g[n], out_d[n], out_m[n], out_v[n] = cut(sg), cut(sd), cut(sm), cut(sv)
        off += sz
    loss = sg.reshape(-1)[total - 1]
    return (loss, grad_x[None], *[out_g[n] for n in WEIGHTS], *[out_d[n] for n in WEIGHTS],
            *[out_m[n] for n in WEIGHTS], *[out_v[n] for n in WEIGHTS])


def kernel(x, l0_norm_w, l0_w_in, l0_s5_lambda_re, l0_s5_lambda_im, l0_s5_log_step, l0_s5_b_re, l0_s5_b_im, l0_s5_c_re, l0_s5_c_im, l0_s5_d, l0_s5_w_glu, l0_s5_b_glu, l0_ssd_conv_w, l0_ssd_conv_b, l0_ssd_dt_bias, l0_ssd_a_log, l0_ssd_d, l0_ssd_norm_w, l0_w_out, l1_norm_w, l1_w_in, l1_fox_b_f, l1_w_out, final_norm_w, loss_target, m_l0_norm_w, m_l0_w_in, m_l0_s5_lambda_re, m_l0_s5_lambda_im, m_l0_s5_log_step, m_l0_s5_b_re, m_l0_s5_b_im, m_l0_s5_c_re, m_l0_s5_c_im, m_l0_s5_d, m_l0_s5_w_glu, m_l0_s5_b_glu, m_l0_ssd_conv_w, m_l0_ssd_conv_b, m_l0_ssd_dt_bias, m_l0_ssd_a_log, m_l0_ssd_d, m_l0_ssd_norm_w, m_l0_w_out, m_l1_norm_w, m_l1_w_in, m_l1_fox_b_f, m_l1_w_out, m_final_norm_w, v_l0_norm_w, v_l0_w_in, v_l0_s5_lambda_re, v_l0_s5_lambda_im, v_l0_s5_log_step, v_l0_s5_b_re, v_l0_s5_b_im, v_l0_s5_c_re, v_l0_s5_c_im, v_l0_s5_d, v_l0_s5_w_glu, v_l0_s5_b_glu, v_l0_ssd_conv_w, v_l0_ssd_conv_b, v_l0_ssd_dt_bias, v_l0_ssd_a_log, v_l0_ssd_d, v_l0_ssd_norm_w, v_l0_w_out, v_l1_norm_w, v_l1_w_in, v_l1_fox_b_f, v_l1_w_out, v_final_norm_w):
    return _step(dict(locals()))
```

```python
import functools
import math

import jax
import jax.numpy as jnp
from jax import lax
from jax.experimental import pallas as pl
from jax.experimental.pallas import tpu as pltpu
from jax.experimental.pallas import tpu_sc as plsc

F32 = jnp.float32
BF16 = jnp.bfloat16
MXU_DT = BF16

D_MODEL = 4096
S5_WIDTH = 2048
S5_GROUP = 16
S5_GROUPS = 128
S5_STATE = 64
S5_EIG_CLIP = -1e-4
S5_BLK = 16
SSD_WIDTH = 6144
SSD_HEAD_DIM = 64
SSD_HEADS = 96
SSD_GROUPS = 8
SSD_STATE = 128
SSD_CONV = 4
SSD_CHUNK = 128
SSD_XBC = 8192
SSD_HPG = SSD_HEADS // SSD_GROUPS
FOX_HEAD_DIM = 128
FOX_HEADS = 32
FOX_WIDTH = 4096
NORM_EPS = 1e-5
EVEN_IN = 18528
ODD_IN = 16416
EVEN_PAD = 18560
ODD_PAD = 16512
LANES = 128
N_DEV = 8

ADAM_LR = 0.001
ADAM_B1 = 0.9
ADAM_B2 = 0.999
ADAM_EPS = 1e-08
ADAM_WD = 0.01
ADAM_STEP = 10

VMEM_LIMIT_BYTES = 48 * 1024 * 1024


ANYSPACE = pl.BlockSpec(memory_space=pl.ANY)


def _cparams(*sem):
    return pltpu.CompilerParams(dimension_semantics=sem, vmem_limit_bytes=VMEM_LIMIT_BYTES)


def _pick(n, target, quantum=LANES):
    if n <= target:
        return n
    t = (target // quantum) * quantum
    while t >= quantum:
        if n % t == 0:
            return t
        t -= quantum
    raise ValueError((n, target, quantum))


_MM_DIMS = {"nn": ((1,), (0,)), "nt": ((1,), (1,)), "tn": ((0,), (0,))}


MM_VMEM_BUDGET = 38 * 1024 * 1024


def _mm_tk(k, tm, tn, out_bytes, has_res, tk_t, start=0):
    fixed = 2 * tm * tn * out_bytes + (2 * tm * tn * 4 if has_res else 0)
    tk = min(k, tk_t)
    while True:
        if k % tk == 0 and start % tk == 0 and (tk == k or tk % LANES == 0):
            need = fixed + 2 * (tm + tn) * tk * 2 + (tm * tn * 4 if tk < k else 0)
            if need <= MM_VMEM_BUDGET or tk <= LANES:
                return tk
        tk -= LANES if tk % LANES == 0 else tk % LANES


def _mm(a, b, mode, out_dtype, name, res=None, b_rows=None, into=None, tm_t=1024, tn_t=512, tk_t=8192):
    b_start, b_size = b_rows if b_rows is not None else (0, b.shape[0])
    if mode == "nn":
        (m, k), (k2, n) = a.shape, (b_size, b.shape[1])
    elif mode == "nt":
        (m, k), (n, k2) = a.shape, (b_size, b.shape[1])
    else:
        (k, m), (k2, n) = a.shape, (b_size, b.shape[1])
    assert k == k2, (a.shape, b.shape, mode)
    tm, tn = _pick(m, tm_t), _pick(n, tn_t)
    has_res = res is not None
    has_into = into is not None
    tk = _mm_tk(k, tm, tn, jnp.dtype(out_dtype).itemsize, has_res, tk_t, b_start if mode == "nn" else 0)
    nk = k // tk
    dims = (_MM_DIMS[mode], ((), ()))
    o_row = 0
    if has_into:
        assert into[1] % tm == 0 and into[0].shape[1] == n and into[0].dtype == out_dtype, (into[1], tm)
        o_row = into[1] // tm
    if mode == "nt":
        assert b_start % tn == 0, (b_start, tn)
    b_blk = b_start // (tn if mode == "nt" else tk)

    def body(*refs):
        a_ref, b_ref = refs[:2]
        r_ref = refs[2] if has_res else None
        o_ref = refs[2 + has_res + has_into]
        part = lax.dot_general(a_ref[...].astype(MXU_DT), b_ref[...].astype(MXU_DT), dims,
                               preferred_element_type=F32)

        def finish(r):
            if has_res:
                r = r + r_ref[...]
            o_ref[...] = r.astype(out_dtype)

        if nk == 1:
            finish(part)
            return
        acc = refs[-1]
        kk = pl.program_id(2)

        @pl.when(kk == 0)
        def _():
            acc[...] = part

        @pl.when(jnp.logical_and(kk > 0, kk < nk - 1))
        def _():
            acc[...] += part

        @pl.when(kk == nk - 1)
        def _():
            finish(acc[...] + part)

    a_spec = (pl.BlockSpec((tk, tm), lambda i, j, kk: (kk, i)) if mode == "tn"
              else pl.BlockSpec((tm, tk), lambda i, j, kk: (i, kk)))
    b_spec = (pl.BlockSpec((tn, tk), lambda i, j, kk: (j + b_blk, kk)) if mode == "nt"
              else pl.BlockSpec((tk, tn), lambda i, j, kk: (kk + b_blk, j)))
    r_spec = pl.BlockSpec((tm, tn), lambda i, j, kk: (i, j))
    o_spec = pl.BlockSpec((tm, tn), lambda i, j, kk: (i + o_row, j))
    in_specs = [a_spec, b_spec] + ([r_spec] if has_res else []) + ([ANYSPACE] if has_into else [])
    args = (a, b) + ((res,) if has_res else ()) + ((into[0],) if has_into else ())
    return pl.pallas_call(
        body, name=name, grid=(m // tm, n // tn, nk), in_specs=in_specs, out_specs=o_spec,
        out_shape=jax.ShapeDtypeStruct(into[0].shape if has_into else (m, n), out_dtype),
        scratch_shapes=[pltpu.VMEM((tm, tn), F32)] if nk > 1 else [],
        input_output_aliases={len(args) - 1: 0} if has_into else {},
        compiler_params=_cparams("parallel", "parallel", "arbitrary"),
    )(*args)


def _tiles(name, fn, tiled, rows, out_tiled, out_acc, tr, tc=None):
    tiled = [t if isinstance(t, tuple) else (t, 0) for t in tiled]
    length = tiled[0][0].shape[0]
    width = out_tiled[0][0] if out_tiled else rows[0].shape[1]
    tc = width if tc is None else tc
    tr = min(tr, length)
    n_in = len(tiled) + len(rows)
    n_ot = len(out_tiled)

    def body(*refs):
        outs = fn(*[r[...] for r in refs[:n_in]])
        outs_t, outs_a = outs[:n_ot], outs[n_ot:]
        for r, v in zip(refs[n_in:n_in + n_ot], outs_t):
            r[...] = v.astype(r.dtype)
        i = pl.program_id(1)
        for r, v in zip(refs[n_in + n_ot:], outs_a):
            @pl.when(i == 0)
            def _(r=r):
                r[...] = jnp.zeros_like(r)

            r[...] += jnp.broadcast_to(v, r.shape)

    def tspec(off):
        return pl.BlockSpec((tr, tc), lambda j, i: (i, j + off))

    in_specs = [tspec(off) for _, off in tiled] + [pl.BlockSpec((1, tc), lambda j, i: (0, j)) for _ in rows]
    out_specs = [tspec(0) for _ in out_tiled] + [pl.BlockSpec((8, tc), lambda j, i: (0, j)) for _ in range(out_acc)]
    out_shape = ([jax.ShapeDtypeStruct((length, w), dt) for w, dt in out_tiled]
                 + [jax.ShapeDtypeStruct((8, width), F32) for _ in range(out_acc)])
    return pl.pallas_call(
        body, name=name, grid=(width // tc, length // tr), in_specs=in_specs, out_specs=out_specs,
        out_shape=out_shape, compiler_params=_cparams("parallel", "arbitrary"),
    )(*[t for t, _ in tiled], *rows)


def _rms(x, w):
    return x * lax.rsqrt(jnp.mean(x * x, axis=-1, keepdims=True) + NORM_EPS) * w


def _colsum(v):
    return jnp.sum(v, axis=0, keepdims=True)


def _rmsnorm_fwd(x, w, name):
    def fn(xb, wb):
        return (_rms(xb, wb),)
    return _tiles(name, fn, [x], [w.reshape(1, -1)], [(x.shape[1], MXU_DT)], 0, tr=256)[0]


def _rmsnorm_bwd(x, w, dh, dres, name):
    def fn(xb, db, rb, wb):
        _, vjp = jax.vjp(_rms, xb, wb)
        dx, dw = vjp(db)
        return dx + rb, dx + rb, dw
    dx, dxb, dw = _tiles(name, fn, [x, dh, dres], [w.reshape(1, -1)],
                         [(x.shape[1], F32), (x.shape[1], MXU_DT)], 1, tr=256)
    return dx, dxb, dw[0]


def _final_loss(x, w, target, name):
    def fn(xb, tb, wb):
        def f(xv, wv):
            e = _rms(xv, wv) - tb
            return 0.5 * jnp.sum(jnp.mean(e * e, axis=-1, keepdims=True), axis=0, keepdims=True)
        lv, vjp = jax.vjp(f, xb, wb)
        dx, dw = vjp(jnp.ones_like(lv))
        return dx, dx, dw, jnp.broadcast_to(lv, (1, xb.shape[1]))
    dx, dxb, dw, lv = _tiles(name, fn, [x, target], [w.reshape(1, -1)],
                             [(x.shape[1], F32), (x.shape[1], MXU_DT)], 2, tr=256)
    return lv[0, 0], dx, dxb, dw[0]


def _dg(a, b, mode):
    return lax.dot_general(a.astype(MXU_DT), b.astype(MXU_DT), (_MM_DIMS[mode], ((), ())),
                           preferred_element_type=F32)


@jax.custom_vjp
def _dot_nn(a, b):
    return _dg(a, b, "nn")


@jax.custom_vjp
def _dot_nt(a, b):
    return _dg(a, b, "nt")


@jax.custom_vjp
def _dot_tn(a, b):
    return _dg(a, b, "tn")


_dot_nn.defvjp(lambda a, b: (_dg(a, b, "nn"), (a, b)),
               lambda r, g: (_dg(g, r[1], "nt"), _dg(r[0], g, "tn")))
_dot_nt.defvjp(lambda a, b: (_dg(a, b, "nt"), (a, b)),
               lambda r, g: (_dg(g, r[1], "nn"), _dg(g, r[0], "tn")))
_dot_tn.defvjp(lambda a, b: (_dg(a, b, "tn"), (a, b)),
               lambda r, g: (_dg(r[1], g, "nt"), _dg(r[0], g, "nn")))


def _dot_exact(a, b):
    return jnp.dot(a, b, precision=lax.Precision.HIGHEST, preferred_element_type=F32)


S5_NS = S5_GROUPS * S5_STATE
S5_BS = S5_NS // S5_BLK
S5_BC = S5_WIDTH // S5_BLK


def _s5_disc(lr_raw, li, ls, br, bi):
    lr = jnp.minimum(lr_raw, S5_EIG_CLIP)
    step = jnp.exp(ls)
    mag = jnp.exp(lr * step)
    ab_re = mag * jnp.cos(li * step)
    ab_im = mag * jnp.sin(li * step)
    denom = lr * lr + li * li
    nr = ab_re - 1.0
    ni = ab_im
    coef_re = (nr * lr + ni * li) / denom
    coef_im = (ni * lr - nr * li) / denom
    return ab_re, ab_im, coef_re * br - coef_im * bi, coef_re * bi + coef_im * br


def _s5_prep(lr_raw, li, ls, br, bi):
    shapes = [jax.ShapeDtypeStruct((1, S5_NS), F32)] * 2 + [jax.ShapeDtypeStruct((S5_GROUP, S5_NS), F32)] * 2

    def body(a, b, c, d, e, o1, o2, o3, o4):
        for r, v in zip((o1, o2, o3, o4), _s5_disc(a[...], b[...], c[...], d[...], e[...])):
            r[...] = v

    return pl.pallas_call(body, name="s5_prep", out_shape=shapes)(lr_raw, li, ls, br, bi)


def _s5_prep_bwd(lr_raw, li, ls, br, bi, d_are, d_aim, d_bbre, d_bbim):
    shapes = [jax.ShapeDtypeStruct((1, S5_NS), F32)] * 3 + [jax.ShapeDtypeStruct((S5_GROUP, S5_NS), F32)] * 2

    def body(a, b, c, d, e, g1, g2, g3, g4, o1, o2, o3, o4, o5):
        _, vjp = jax.vjp(_s5_disc, a[...], b[...], c[...], d[...], e[...])
        for r, v in zip((o1, o2, o3, o4, o5), vjp((g1[...], g2[...], g3[...], g4[...]))):
            r[...] = v

    return pl.pallas_call(body, name="s5_prep_bwd", out_shape=shapes)(
        lr_raw, li, ls, br, bi, d_are, d_aim, d_bbre, d_bbim)


def _cmul(ar, ai, br, bi):
    return ar * br - ai * bi, ar * bi + ai * br


def _s5_powers(ar, ai, n):
    pr, pi_ = [ar], [ai]
    for _ in range(n - 1):
        r, i = _cmul(pr[-1], pi_[-1], pr[-1], pi_[-1])
        pr.append(r)
        pi_.append(i)
    return pr, pi_


S5_SUB = 8
S5_NPOW = 3


def _s5_in_groups(xr, xi, pr, pi_, reverse):
    t = xr.shape[0]
    sub = lax.broadcasted_iota(jnp.int32, xr.shape, 0) & (S5_SUB - 1)
    sr, si = xr, xi
    for k in range(S5_NPOW):
        d = 1 << k
        shift = (t - d) if reverse else d
        keep = (sub < S5_SUB - d) if reverse else (sub >= d)
        qr = jnp.where(keep, pltpu.roll(sr, shift, 0), 0.0)
        qi = jnp.where(keep, pltpu.roll(si, shift, 0), 0.0)
        mr, mi = _cmul(pr[k], pi_[k], qr, qi)
        sr, si = sr + mr, si + mi
    return sr, si


def _s5_scan_tile(xr, xi, apr, api, tabr, tabi, cr, ci, out_r, out_i, reverse):
    t = xr.shape[0]
    pr = [apr[k:k + 1, :] for k in range(S5_NPOW)]
    pi_ = [api[k:k + 1, :] for k in range(S5_NPOW)]
    sr, si = _s5_in_groups(xr, xi, pr, pi_, reverse)
    tr, ti = tabr[...], tabi[...]
    car_r, car_i = cr[0:1, :], ci[0:1, :]
    groups = range(t // S5_SUB)
    for g in (reversed(groups) if reverse else groups):
        rows = slice(g * S5_SUB, (g + 1) * S5_SUB)
        mr, mi = _cmul(tr, ti, car_r, car_i)
        br, bi = sr[rows] + mr, si[rows] + mi
        out_r[rows, :] = br
        out_i[rows, :] = bi
        edge = slice(0, 1) if reverse else slice(S5_SUB - 1, S5_SUB)
        car_r, car_i = br[edge], bi[edge]
    cr[0:1, :] = car_r
    ci[0:1, :] = car_i


def _s5_setup(ar, ai, reverse, apr, api, tabr, tabi):
    pr, pi_ = _s5_powers(ar, ai, S5_NPOW)
    for k in range(S5_NPOW):
        apr[k:k + 1, :] = pr[k]
        api[k:k + 1, :] = pi_[k]
    row = lax.broadcasted_iota(jnp.int32, (S5_SUB, ar.shape[1]), 0)
    first = (row == S5_SUB - 1) if reverse else (row == 0)
    sr, si = _s5_in_groups(jnp.where(first, ar, 0.0), jnp.where(first, ai, 0.0), pr, pi_, reverse)
    tabr[...] = sr
    tabi[...] = si


def _s5_fwd(p_s5, bb_re, bb_im, ct_re, ct_im, a_re, a_im, d_row, t_tile=256):
    length = p_s5.shape[0]
    t = min(t_tile, length)
    nt = length // t

    def body(u_ref, bbr, bbi, ctr, cti, ar_ref, ai_ref, d_ref, y_ref, sr_ref, si_ref,
             apr, api, tabr, tabi, cr, ci):
        i = pl.program_id(1)

        @pl.when(i == 0)
        def _():
            _s5_setup(ar_ref[...], ai_ref[...], False, apr, api, tabr, tabi)
            cr[...] = jnp.zeros_like(cr)
            ci[...] = jnp.zeros_like(ci)

        u = u_ref[...]
        _s5_scan_tile(_dg(u, bbr[...], "nn"), _dg(u, bbi[...], "nn"), apr, api, tabr, tabi, cr, ci,
                      sr_ref, si_ref, False)
        y_ref[...] = _dg(sr_ref[...], ctr[...], "nn") - _dg(si_ref[...], cti[...], "nn") + d_ref[...] * u

    blk3 = lambda a, b: pl.BlockSpec((None, a, b), lambda j, i: (j, 0, 0))
    return pl.pallas_call(
        body, name="s5_fwd", grid=(S5_BLK, nt),
        in_specs=[pl.BlockSpec((t, S5_BC), lambda j, i: (i, j)),
                  blk3(S5_BC, S5_BS), blk3(S5_BC, S5_BS), blk3(S5_BS, S5_BC), blk3(S5_BS, S5_BC),
                  blk3(1, S5_BS), blk3(1, S5_BS), blk3(1, S5_BC)],
        out_specs=[pl.BlockSpec((t, S5_BC), lambda j, i: (i, j)),
                   pl.BlockSpec((t, S5_BS), lambda j, i: (i, j)),
                   pl.BlockSpec((t, S5_BS), lambda j, i: (i, j))],
        out_shape=[jax.ShapeDtypeStruct((length, S5_WIDTH), F32),
                   jax.ShapeDtypeStruct((length, S5_NS), F32),
                   jax.ShapeDtypeStruct((length, S5_NS), F32)],
        scratch_shapes=[pltpu.VMEM((S5_SUB, S5_BS), F32)] * 6,
        compiler_params=_cparams("parallel", "arbitrary"),
    )(p_s5, bb_re, bb_im, ct_re, ct_im, a_re, a_im, d_row)


def _s5_bwd(dy, p_s5, s_re, s_im, bb_re, bb_im, ct_re, ct_im, a_re, a_im, d_row, t_tile=256):
    length = p_s5.shape[0]
    t = min(t_tile, length)
    nt = length // t

    def body(dy_ref, u_ref, sr_ref, si_ref, pr_ref, pi_ref, bbr, bbi, ctr, cti, ar_ref, ai_ref, d_ref,
             du_ref, dbbr, dbbi, dctr, dcti, dar, dai, dd_ref, apr, api, tabr, tabi, cr, ci, lam_r, lam_i):
        i = pl.program_id(1)

        @pl.when(i == 0)
        def _():
            _s5_setup(ar_ref[...], -ai_ref[...], True, apr, api, tabr, tabi)
            for r in (cr, ci, dbbr, dbbi, dctr, dcti, dar, dai, dd_ref):
                r[...] = jnp.zeros_like(r)

        dyv = dy_ref[...]
        u = u_ref[...]
        _s5_scan_tile(_dg(dyv, ctr[...], "nt"), -_dg(dyv, cti[...], "nt"), apr, api, tabr, tabi, cr, ci,
                      lam_r, lam_i, True)
        lr, li = lam_r[...], lam_i[...]
        du_ref[...] = (_dg(lr, bbr[...], "nt") + _dg(li, bbi[...], "nt") + d_ref[...] * dyv).astype(du_ref.dtype)
        dbbr[...] += _dg(u, lr, "tn")
        dbbi[...] += _dg(u, li, "tn")
        sr = sr_ref[...]
        si = si_ref[...]
        dctr[...] += _dg(sr, dyv, "tn")
        dcti[...] -= _dg(si, dyv, "tn")
        dd_ref[...] += jnp.broadcast_to(_colsum(dyv * u), dd_ref.shape)
        row = lax.broadcasted_iota(jnp.int32, sr.shape, 0)
        has_prev = (i < nt - 1).astype(F32)
        ssr = jnp.where(row == 0, pr_ref[7:8, :] * has_prev, pltpu.roll(sr, 1, 0))
        ssi = jnp.where(row == 0, pi_ref[7:8, :] * has_prev, pltpu.roll(si, 1, 0))
        dar[...] += jnp.broadcast_to(_colsum(lr * ssr + li * ssi), dar.shape)
        dai[...] += jnp.broadcast_to(_colsum(li * ssr - lr * ssi), dai.shape)

    rev = lambda j, i: (nt - 1 - i, j)
    prev = lambda j, i: (jnp.maximum((nt - 1 - i) * (t // 8) - 1, 0), j)
    blk3 = lambda a, b: pl.BlockSpec((None, a, b), lambda j, i: (j, 0, 0))
    return pl.pallas_call(
        body, name="s5_bwd", grid=(S5_BLK, nt),
        in_specs=[pl.BlockSpec((t, S5_BC), rev), pl.BlockSpec((t, S5_BC), rev),
                  pl.BlockSpec((t, S5_BS), rev), pl.BlockSpec((t, S5_BS), rev),
                  pl.BlockSpec((8, S5_BS), prev), pl.BlockSpec((8, S5_BS), prev),
                  blk3(S5_BC, S5_BS), blk3(S5_BC, S5_BS), blk3(S5_BS, S5_BC), blk3(S5_BS, S5_BC),
                  blk3(1, S5_BS), blk3(1, S5_BS), blk3(1, S5_BC)],
        out_specs=[pl.BlockSpec((t, S5_BC), rev),
                   blk3(S5_BC, S5_BS), blk3(S5_BC, S5_BS), blk3(S5_BS, S5_BC), blk3(S5_BS, S5_BC),
                   blk3(8, S5_BS), blk3(8, S5_BS), blk3(8, S5_BC)],
        out_shape=[jax.ShapeDtypeStruct((length, S5_WIDTH), MXU_DT),
                   jax.ShapeDtypeStruct((S5_BLK, S5_BC, S5_BS), F32), jax.ShapeDtypeStruct((S5_BLK, S5_BC, S5_BS), F32),
                   jax.ShapeDtypeStruct((S5_BLK, S5_BS, S5_BC), F32), jax.ShapeDtypeStruct((S5_BLK, S5_BS, S5_BC), F32),
                   jax.ShapeDtypeStruct((S5_BLK, 8, S5_BS), F32), jax.ShapeDtypeStruct((S5_BLK, 8, S5_BS), F32),
                   jax.ShapeDtypeStruct((S5_BLK, 8, S5_BC), F32)],
        scratch_shapes=[pltpu.VMEM((S5_SUB, S5_BS), F32)] * 6 + [pltpu.VMEM((t, S5_BS), F32)] * 2,
        compiler_params=_cparams("parallel", "arbitrary"),
    )(dy, p_s5, s_re, s_im, s_re, s_im, bb_re, bb_im, ct_re, ct_im, a_re, a_im, d_row)


def _blockdiag(m, rows, cols):
    m = m.reshape(S5_BLK, 8, rows, 1, cols)
    on_diag = jnp.eye(8, dtype=bool)[None, :, None, :, None]
    return jnp.where(on_diag, m, 0).reshape(S5_BLK, 8 * rows, 8 * cols)


def _blockdiag_t(m, rows, cols):
    m = m.reshape(S5_BLK, 8, rows, 8, cols)
    on_diag = jnp.eye(8, dtype=bool)[None, :, None, :, None]
    return jnp.sum(jnp.where(on_diag, m, 0), axis=3).reshape(S5_GROUPS, rows, cols)


def _gelu(y):
    return jax.nn.gelu(y)


def _s5_out_fn(y, zg, gate, b):
    return _gelu(y) * jax.nn.sigmoid(zg + b) * jax.nn.silu(gate)


HALO = 8


def _conv_fwd(xbc, w, b, tr=256, tc=1024):
    length, width = xbc.shape
    tr = min(tr, length)

    def body(x_ref, h_ref, w_ref, b_ref, cv_ref, act_ref):
        i = pl.program_id(1)
        x = x_ref[...]
        xx = jnp.concatenate([h_ref[...] * (i > 0).astype(F32), x], axis=0)
        acc = b_ref[...] + w_ref[3:4, :] * x
        for k in range(SSD_CONV - 1):
            acc = acc + w_ref[k:k + 1, :] * pltpu.roll(xx, SSD_CONV - 1 - k, 0)[HALO:, :]
        cv_ref[...] = acc
        act_ref[...] = jax.nn.silu(acc)

    main = pl.BlockSpec((tr, tc), lambda j, i: (i, j))
    before = pl.BlockSpec((HALO, tc), lambda j, i: (jnp.maximum(i * (tr // HALO) - 1, 0), j))
    return pl.pallas_call(
        body, name="ssd_conv_fwd", grid=(width // tc, length // tr),
        in_specs=[main, before, pl.BlockSpec((SSD_CONV, tc), lambda j, i: (0, j)),
                  pl.BlockSpec((1, tc), lambda j, i: (0, j))],
        out_specs=[main, main],
        out_shape=[jax.ShapeDtypeStruct((length, width), F32)] * 2,
        compiler_params=_cparams("parallel", "arbitrary"),
    )(xbc, xbc, w, b.reshape(1, -1))


def _conv_bwd(dact, cv, xbc, w, tr=256, tc=1024):
    length, width = xbc.shape
    tr = min(tr, length)
    nr = length // tr
    n = tr + HALO

    def dsilu(d, c):
        sg = jax.nn.sigmoid(c)
        return d * (sg * (1.0 + c * (1.0 - sg)))

    def body(da_ref, dan_ref, cv_ref, cvn_ref, x_ref, xp_ref, w_ref, dx_ref, dw_ref, db_ref):
        i = pl.program_id(1)

        @pl.when(i == 0)
        def _():
            dw_ref[...] = jnp.zeros_like(dw_ref)
            db_ref[...] = jnp.zeros_like(db_ref)

        dc = dsilu(da_ref[...], cv_ref[...])
        dcn = dsilu(dan_ref[...], cvn_ref[...]) * (i < nr - 1).astype(F32)
        dd = jnp.concatenate([dc, dcn], axis=0)
        x = x_ref[...]
        xx = jnp.concatenate([xp_ref[...] * (i > 0).astype(F32), x], axis=0)
        dx = w_ref[3:4, :] * dc
        dw_ref[3:4, :] += _colsum(dc * x)
        for k in range(SSD_CONV - 1):
            j = SSD_CONV - 1 - k
            dx = dx + w_ref[k:k + 1, :] * pltpu.roll(dd, n - j, 0)[:tr, :]
            dw_ref[k:k + 1, :] += _colsum(dc * pltpu.roll(xx, j, 0)[HALO:, :])
        dx_ref[...] = dx.astype(dx_ref.dtype)
        db_ref[...] += jnp.broadcast_to(_colsum(dc), db_ref.shape)

    main = pl.BlockSpec((tr, tc), lambda j, i: (i, j))
    before = pl.BlockSpec((HALO, tc), lambda j, i: (jnp.maximum(i * (tr // HALO) - 1, 0), j))
    after = pl.BlockSpec((HALO, tc), lambda j, i: (jnp.minimum((i + 1) * (tr // HALO), length // HALO - 1), j))
    acc = pl.BlockSpec((8, tc), lambda j, i: (0, j))
    return pl.pallas_call(
        body, name="ssd_conv_bwd", grid=(width // tc, nr),
        in_specs=[main, after, main, after, main, before, pl.BlockSpec((SSD_CONV, tc), lambda j, i: (0, j))],
        out_specs=[main, acc, acc],
        out_shape=[jax.ShapeDtypeStruct((length, width), MXU_DT),
                   jax.ShapeDtypeStruct((8, width), F32), jax.ShapeDtypeStruct((8, width), F32)],
        compiler_params=_cparams("parallel", "arbitrary"),
    )(dact, dact, cv, cv, xbc, xbc, w)


def _tri(lower):
    r = lax.broadcasted_iota(jnp.int32, (SSD_CHUNK, SSD_CHUNK), 0)
    c = lax.broadcasted_iota(jnp.int32, (SSD_CHUNK, SSD_CHUNK), 1)
    return ((r >= c) if lower else (r <= c)).astype(F32)


def _dt_fwd(raw, bias, a_log):
    length = raw.shape[0]
    nc = length // SSD_CHUNK

    def body(r_ref, b_ref, a_ref, dt_ref, cum_ref, cumt_ref):
        dt = jax.nn.softplus(r_ref[...] + b_ref[...])
        cum = _dot_exact(_tri(True), dt * (-jnp.exp(a_ref[...])))
        dt_ref[...] = dt
        cum_ref[...] = cum
        cumt_ref[...] = cum.T

    blk = pl.BlockSpec((SSD_CHUNK, LANES), lambda c: (c, 0))
    row = pl.BlockSpec((1, LANES), lambda c: (0, 0))
    return pl.pallas_call(
        body, name="ssd_dt_fwd", grid=(nc,), in_specs=[blk, row, row],
        out_specs=[blk, blk, pl.BlockSpec((None, LANES, SSD_CHUNK), lambda c: (c, 0, 0))],
        out_shape=[jax.ShapeDtypeStruct((length, LANES), F32)] * 2
        + [jax.ShapeDtypeStruct((nc, LANES, SSD_CHUNK), F32)],
        compiler_params=_cparams("parallel"),
    )(raw, bias, a_log)


def _dt_bwd(raw, bias, a_log, ddt, dcum):
    length = raw.shape[0]
    nc = length // SSD_CHUNK

    def body(r_ref, b_ref, a_ref, ddt_ref, dcum_ref, dr_ref, db_ref, da_ref):
        @pl.when(pl.program_id(0) == 0)
        def _():
            db_ref[...] = jnp.zeros_like(db_ref)
            da_ref[...] = jnp.zeros_like(da_ref)

        z = r_ref[...] + b_ref[...]
        a = -jnp.exp(a_ref[...])
        dla = _dot_exact(_tri(False), dcum_ref[...])
        draw = (ddt_ref[...] + dla * a) * jax.nn.sigmoid(z)
        dr_ref[...] = draw.astype(dr_ref.dtype)
        db_ref[...] += jnp.broadcast_to(_colsum(draw), db_ref.shape)
        da_ref[...] += jnp.broadcast_to(_colsum(dla * jax.nn.softplus(z)) * a, da_ref.shape)

    blk = pl.BlockSpec((SSD_CHUNK, LANES), lambda c: (c, 0))
    row = pl.BlockSpec((1, LANES), lambda c: (0, 0))
    acc = pl.BlockSpec((8, LANES), lambda c: (0, 0))
    return pl.pallas_call(
        body, name="ssd_dt_bwd", grid=(nc,), in_specs=[blk, row, row, blk, blk],
        out_specs=[blk, acc, acc],
        out_shape=[jax.ShapeDtypeStruct((length, LANES), MXU_DT),
                   jax.ShapeDtypeStruct((8, LANES), F32), jax.ShapeDtypeStruct((8, LANES), F32)],
        compiler_params=_cparams("arbitrary"),
    )(raw, bias, a_log, ddt, dcum)


SSD_GW = SSD_HPG * SSD_HEAD_DIM
SSD_B_OFF = SSD_WIDTH // SSD_STATE
SSD_C_OFF = SSD_B_OFF + SSD_GROUPS


def _ssd_expand():
    r = lax.broadcasted_iota(jnp.int32, (LANES, SSD_GW), 0)
    c = lax.broadcasted_iota(jnp.int32, (LANES, SSD_GW), 1)
    return (c // SSD_HEAD_DIM == r).astype(F32)


def _ssd_to_channels(v16, e):
    return _dot_exact(v16, e)


def _ssd_to_heads(v, e):
    return lax.dot_general(v, e, (((1,), (1,)), ((), ())), precision=lax.Precision.HIGHEST,
                           preferred_element_type=F32)


def _ssd_decay(cum_ref, cumt_ref, r):
    q = lax.broadcasted_iota(jnp.int32, (SSD_CHUNK, SSD_CHUNK), 0)
    k = lax.broadcasted_iota(jnp.int32, (SSD_CHUNK, SSD_CHUNK), 1)
    return jnp.exp(jnp.where(q >= k, cum_ref[:, r:r + 1] - cumt_ref[r:r + 1, :], -1e30))


def _ssd_core_specs(nc, rev):
    ch = (lambda c: nc - 1 - c) if rev else (lambda c: c)
    xs = pl.BlockSpec((SSD_CHUNK, SSD_GW), lambda g, c: (ch(c), g))
    bspec = pl.BlockSpec((SSD_CHUNK, SSD_STATE), lambda g, c: (ch(c), SSD_B_OFF + g))
    cspec = pl.BlockSpec((SSD_CHUNK, SSD_STATE), lambda g, c: (ch(c), SSD_C_OFF + g))
    lane = pl.BlockSpec((None, SSD_CHUNK, LANES), lambda g, c: (g, ch(c), 0))
    rows = pl.BlockSpec((None, None, 16, SSD_CHUNK), lambda g, c: (g, ch(c), 0, 0))
    st = pl.BlockSpec((None, None, SSD_STATE, SSD_GW), lambda g, c: (g, ch(c), 0, 0))
    return xs, bspec, cspec, lane, rows, st


def _ssd_core_fwd(act, dtg, cumg, cumtg):
    length = act.shape[0]
    nc = length // SSD_CHUNK

    def body(x_ref, b_ref, c_ref, dt_ref, cum_ref, cumt_ref, y_ref, st_ref, s_scr):
        @pl.when(pl.program_id(1) == 0)
        def _():
            s_scr[...] = jnp.zeros_like(s_scr)

        e = _ssd_expand()
        bm = b_ref[...]
        cm = c_ref[...]
        cum_e = _ssd_to_channels(cum_ref[...], e)
        cl_e = cum_e[SSD_CHUNK - 1:SSD_CHUNK, :]
        xdt = x_ref[...] * _ssd_to_channels(dt_ref[...], e)
        st = s_scr[...]
        st_ref[...] = st
        g = _dg(cm, bm, "nt")
        y_off = _dg(cm, st, "nn") * jnp.exp(cum_e)
        for r in range(SSD_HPG):
            cols = slice(r * SSD_HEAD_DIM, (r + 1) * SSD_HEAD_DIM)
            y_ref[:, cols] = _dg(g * _ssd_decay(cum_ref, cumt_ref, r), xdt[:, cols], "nn") + y_off[:, cols]
        s_scr[...] = jnp.exp(cl_e) * st + _dg(bm, xdt * jnp.exp(cl_e - cum_e), "tn")

    xs, bspec, cspec, lane, rows, st = _ssd_core_specs(nc, False)
    return pl.pallas_call(
        body, name="ssd_core_fwd", grid=(SSD_GROUPS, nc),
        in_specs=[xs, bspec, cspec, lane, lane, rows], out_specs=[xs, st],
        out_shape=[jax.ShapeDtypeStruct((length, SSD_WIDTH), F32),
                   jax.ShapeDtypeStruct((SSD_GROUPS, nc, SSD_STATE, SSD_GW), F32)],
        scratch_shapes=[pltpu.VMEM((SSD_STATE, SSD_GW), F32)],
        compiler_params=_cparams("parallel", "arbitrary"),
    )(act, act, act, dtg, cumg, cumtg)


def _ssd_core_bwd(dy, dxs_add, act, dtg, cumg, cumtg, states):
    length = act.shape[0]
    nc = length // SSD_CHUNK

    def body(dy_ref, add_ref, x_ref, b_ref, c_ref, dt_ref, cum_ref, cumt_ref, st_ref,
             dx_ref, db_ref, dc_ref, ddt_ref, dcum_ref, dcumt_ref, dcl_ref, ds_scr, dxdt_scr):
        @pl.when(pl.program_id(1) == 0)
        def _():
            ds_scr[...] = jnp.zeros_like(ds_scr)

        e = _ssd_expand()
        bm = b_ref[...]
        cm = c_ref[...]
        x = x_ref[...]
        dyv = dy_ref[...]
        st = st_ref[...]
        dst_new = ds_scr[...]
        dt_e = _ssd_to_channels(dt_ref[...], e)
        cum_e = _ssd_to_channels(cum_ref[...], e)
        cl_e = cum_e[SSD_CHUNK - 1:SSD_CHUNK, :]
        xdt = x * dt_e
        exp_cum = jnp.exp(cum_e)
        exp_cl = jnp.exp(cl_e)
        z = xdt * jnp.exp(cl_e - cum_e)
        d_cs = dyv * exp_cum
        dcum_e = d_cs * _dg(cm, st, "nn")
        dcm = _dg(d_cs, st, "nt")
        ds_scr[...] = _dg(cm, d_cs, "tn") + exp_cl * dst_new
        dcl_e = _colsum(dst_new * st) * exp_cl
        dbm = _dg(z, dst_new, "nt")
        dz = _dg(bm, dst_new, "nn")
        de = dz * z
        dcl_e = dcl_e + _colsum(de)
        dcum_e = dcum_e - de
        dxdt_scr[...] = dz * jnp.exp(cl_e - cum_e)
        g = _dg(cm, bm, "nt")
        dg = jnp.zeros_like(g)
        dcum_ref[...] = _ssd_to_heads(dcum_e, e)
        dcumt_ref[...] = jnp.zeros_like(dcumt_ref)
        for r in range(SSD_HPG):
            cols = slice(r * SSD_HEAD_DIM, (r + 1) * SSD_HEAD_DIM)
            decay = _ssd_decay(cum_ref, cumt_ref, r)
            w = g * decay
            dy_r = dyv[:, cols]
            dw = _dg(dy_r, xdt[:, cols], "nt")
            dxdt_scr[:, cols] += _dg(w, dy_r, "tn")
            dg = dg + dw * decay
            dseg = dw * w
            dcum_ref[:, r:r + 1] += jnp.sum(dseg, axis=1, keepdims=True)
            dcumt_ref[r:r + 1, :] = -_colsum(dseg)
        dc_ref[...] = dcm + _dg(dg, bm, "nn")
        db_ref[...] = dbm + _dg(dg, cm, "tn")
        dxdt = dxdt_scr[...]
        dx_ref[...] = dxdt * dt_e + add_ref[...]
        ddt_ref[...] = _ssd_to_heads(dxdt * x, e)
        dcl_ref[...] = _ssd_to_heads(jnp.broadcast_to(dcl_e, (8, SSD_GW)), e)

    xs, bspec, cspec, lane, rows, st = _ssd_core_specs(nc, True)
    bc_out = pl.BlockSpec((SSD_CHUNK, SSD_STATE), lambda g, c: (nc - 1 - c, g))
    last = pl.BlockSpec((None, None, 8, LANES), lambda g, c: (g, nc - 1 - c, 0, 0))
    return pl.pallas_call(
        body, name="ssd_core_bwd", grid=(SSD_GROUPS, nc),
        in_specs=[xs, xs, xs, bspec, cspec, lane, lane, rows, st],
        out_specs=[xs, bc_out, bc_out, lane, lane, rows, last],
        out_shape=[jax.ShapeDtypeStruct((length, SSD_WIDTH), F32),
                   jax.ShapeDtypeStruct((length, SSD_GROUPS * SSD_STATE), F32),
                   jax.ShapeDtypeStruct((length, SSD_GROUPS * SSD_STATE), F32),
                   jax.ShapeDtypeStruct((SSD_GROUPS, length, LANES), F32),
                   jax.ShapeDtypeStruct((SSD_GROUPS, length, LANES), F32),
                   jax.ShapeDtypeStruct((SSD_GROUPS, nc, 16, SSD_CHUNK), F32),
                   jax.ShapeDtypeStruct((SSD_GROUPS, nc, 8, LANES), F32)],
        scratch_shapes=[pltpu.VMEM((SSD_STATE, SSD_GW), F32), pltpu.VMEM((SSD_CHUNK, SSD_GW), F32)],
        compiler_params=_cparams("parallel", "arbitrary"),
    )(dy, dxs_add, act, act, act, dtg, cumg, cumtg, states)


def _ssd_post_fn(yc, xs, z, dch, nw):
    y = (yc + dch * xs) * jax.nn.silu(z)
    return y * lax.rsqrt(jnp.mean(y * y, axis=-1, keepdims=True) + NORM_EPS) * nw


FOX_SCALE = 1.0 / math.sqrt(FOX_HEAD_DIM)
MASKED = -1e30


def _fgate_fwd(f_raw, b_f):
    length = f_raw.shape[0]
    nb = length // SSD_CHUNK

    def body(f_ref, b_ref, c_ref, carry):
        @pl.when(pl.program_id(0) == 0)
        def _():
            carry[...] = jnp.zeros_like(carry)

        c = _dot_exact(_tri(True), jax.nn.log_sigmoid(f_ref[...] + b_ref[...])) + carry[0:1, :]
        c_ref[...] = c
        carry[0:1, :] = c[SSD_CHUNK - 1:SSD_CHUNK, :]

    blk = pl.BlockSpec((SSD_CHUNK, LANES), lambda i: (i, 0))
    return pl.pallas_call(
        body, name="fox_fgate_fwd", grid=(nb,), in_specs=[blk, pl.BlockSpec((1, LANES), lambda i: (0, 0))],
        out_specs=blk, out_shape=jax.ShapeDtypeStruct((length, LANES), F32),
        scratch_shapes=[pltpu.VMEM((8, LANES), F32)], compiler_params=_cparams("arbitrary"),
    )(f_raw, b_f)


def _fgate_bwd(f_raw, b_f, dc):
    length = f_raw.shape[0]
    nb = length // SSD_CHUNK

    def body(f_ref, b_ref, dc_ref, df_ref, db_ref, carry):
        @pl.when(pl.program_id(0) == 0)
        def _():
            carry[...] = jnp.zeros_like(carry)
            db_ref[...] = jnp.zeros_like(db_ref)

        dcv = dc_ref[...]
        dlog = _dot_exact(_tri(False), dcv) + carry[0:1, :]
        carry[0:1, :] += _colsum(dcv)
        df = dlog * jax.nn.sigmoid(-(f_ref[...] + b_ref[...]))
        df_ref[...] = df.astype(df_ref.dtype)
        db_ref[...] += jnp.broadcast_to(_colsum(df), db_ref.shape)

    blk = pl.BlockSpec((SSD_CHUNK, LANES), lambda i: (nb - 1 - i, 0))
    return pl.pallas_call(
        body, name="fox_fgate_bwd", grid=(nb,),
        in_specs=[blk, pl.BlockSpec((1, LANES), lambda i: (0, 0)), blk],
        out_specs=[blk, pl.BlockSpec((8, LANES), lambda i: (0, 0))],
        out_shape=[jax.ShapeDtypeStruct((length, LANES), MXU_DT), jax.ShapeDtypeStruct((8, LANES), F32)],
        scratch_shapes=[pltpu.VMEM((8, LANES), F32)], compiler_params=_cparams("arbitrary"),
    )(f_raw, b_f, dc)


def _fox_scores(q, k, bias, diagonal):
    s = _dg(q, k, "nt") * FOX_SCALE + bias
    if diagonal:
        row = lax.broadcasted_iota(jnp.int32, s.shape, 0)
        col = lax.broadcasted_iota(jnp.int32, s.shape, 1)
        s = jnp.where(col <= row, s, MASKED)
    return s


FOX_TILE = 512


def _fox_c0(c, t):
    return jnp.repeat(jnp.repeat(c[::t, :FOX_HEADS], FOX_HEAD_DIM, axis=1), 8, axis=0)


def _fox_fwd(qkv, c0_rep, c_t, tile=FOX_TILE):
    length = qkv.shape[0]
    t = min(tile, length)
    nq = length // t

    def body(q_ref, k_ref, v_ref, c0_ref, ct_ref, o_ref, lse_ref):
        i = pl.program_id(1)
        q = q_ref[...]
        c0 = c0_ref[0:1, 0:1]

        def tile_step(k0, carry, diagonal):
            m, l, acc = carry
            s = _fox_scores(q, k_ref[pl.ds(k0, t), :], c0 - ct_ref[:, pl.ds(k0, t)], diagonal)
            m_new = jnp.maximum(m, jnp.max(s, axis=1, keepdims=True))
            p = jnp.exp(s - m_new)
            alpha = jnp.exp(m - m_new)
            return (m_new, alpha * l + jnp.sum(p, axis=1, keepdims=True),
                    alpha * acc + _dg(p, v_ref[pl.ds(k0, t), :], "nn"))

        init = (jnp.full((t, 1), MASKED, F32), jnp.zeros((t, 1), F32), jnp.zeros((t, FOX_HEAD_DIM), F32))
        carry = lax.fori_loop(0, i, lambda j, c: tile_step(pl.multiple_of(j * t, t), c, False), init)
        m, l, acc = tile_step(pl.multiple_of(i * t, t), carry, True)
        o_ref[...] = acc / l
        lse_ref[...] = jnp.broadcast_to(m + jnp.log(l), lse_ref.shape)

    qt = pl.BlockSpec((t, FOX_HEAD_DIM), lambda h, i: (i, h))
    return pl.pallas_call(
        body, name="fox_attn_fwd", grid=(FOX_HEADS, nq),
        in_specs=[qt,
                  pl.BlockSpec((length, FOX_HEAD_DIM), lambda h, i: (0, FOX_HEADS + h)),
                  pl.BlockSpec((length, FOX_HEAD_DIM), lambda h, i: (0, 2 * FOX_HEADS + h)),
                  pl.BlockSpec((8, FOX_HEAD_DIM), lambda h, i: (i, h)),
                  pl.BlockSpec((None, 1, length), lambda h, i: (h, 0, 0))],
        out_specs=[qt, qt],
        out_shape=[jax.ShapeDtypeStruct((length, FOX_WIDTH), F32)] * 2,
        compiler_params=_cparams("parallel", "arbitrary"),
    )(qkv, qkv, qkv, c0_rep, c_t)


def _fox_bwd(qkv, d_att, lse, delta, c0_rep, c_t, tile=FOX_TILE):
    length = qkv.shape[0]
    t = min(tile, length)
    nk = length // t

    def body(q_ref, k_ref, v_ref, do_ref, lse_ref, dl_ref, c0_ref, ct_ref,
             dq_ref, dk_ref, dv_ref, dcq_ref, dck_ref, dq_acc):
        j = pl.program_id(1)

        @pl.when(j == 0)
        def _():
            dq_acc[...] = jnp.zeros_like(dq_acc)
            dcq_ref[...] = jnp.zeros_like(dcq_ref)

        k = k_ref[...]
        v = v_ref[...]
        ck = ct_ref[...]

        def tile_step(i, carry, diagonal):
            dk, dv, dck = carry
            rows = pl.ds(pl.multiple_of(i * t, t), t)
            q = q_ref[rows, :]
            do = do_ref[rows, :]
            c0 = c0_ref[pl.ds(pl.multiple_of(i * 8, 8), 8), :][0:1, 0:1]
            s = _fox_scores(q, k, c0 - ck, diagonal)
            p = jnp.exp(s - lse_ref[rows, 0:1])
            dv = dv + _dg(p, do, "tn")
            ds = p * (_dg(do, v, "nt") - dl_ref[rows, 0:1])
            dk = dk + _dg(ds, q, "tn") * FOX_SCALE
            dq_acc[rows, :] += _dg(ds, k, "nn") * FOX_SCALE
            dcq_ref[rows, :] += jnp.broadcast_to(jnp.sum(ds, axis=1, keepdims=True), (t, FOX_HEAD_DIM))
            return dk, dv, dck + _colsum(ds)

        init = (jnp.zeros((t, FOX_HEAD_DIM), F32), jnp.zeros((t, FOX_HEAD_DIM), F32), jnp.zeros((1, t), F32))
        carry = tile_step(j, init, True)
        dk, dv, dck = lax.fori_loop(j + 1, nk, lambda i, c: tile_step(i, c, False), carry)
        dk_ref[...] = dk.astype(dk_ref.dtype)
        dv_ref[...] = dv.astype(dv_ref.dtype)
        dck_ref[...] = -dck

        @pl.when(j == nk - 1)
        def _():
            dq_ref[...] = dq_acc[...].astype(dq_ref.dtype)

    full = lambda off: pl.BlockSpec((length, FOX_HEAD_DIM), lambda h, j: (0, off + h))
    kt = lambda off: pl.BlockSpec((t, FOX_HEAD_DIM), lambda h, j: (j, off + h))
    ck_spec = pl.BlockSpec((None, 1, t), lambda h, j: (h, 0, j))
    return pl.pallas_call(
        body, name="fox_attn_bwd", grid=(FOX_HEADS, nk),
        in_specs=[full(0), kt(FOX_HEADS), kt(2 * FOX_HEADS), full(0), full(0), full(0),
                  pl.BlockSpec((8 * nk, FOX_HEAD_DIM), lambda h, j: (0, h)), ck_spec],
        out_specs=[full(0), kt(0), kt(0), full(0), ck_spec],
        out_shape=[jax.ShapeDtypeStruct((length, FOX_WIDTH), MXU_DT)] * 3
        + [jax.ShapeDtypeStruct((length, FOX_WIDTH), F32), jax.ShapeDtypeStruct((FOX_HEADS, 1, length), F32)],
        scratch_shapes=[pltpu.VMEM((length, FOX_HEAD_DIM), F32)],
        compiler_params=_cparams("parallel", "arbitrary"),
    )(qkv, qkv, qkv, d_att, lse, delta, c0_rep, c_t)


def _fox_gate_fn(att, gate):
    return att * jax.nn.silu(gate)


N_CHIP = 4


def _other_chips(mx, my):
    return [(1 - mx, my), (mx, 1 - my), (1 - mx, 1 - my)]


def _handshake(peers):
    barrier = pltpu.get_barrier_semaphore()
    for peer in peers:
        pl.semaphore_signal(barrier, inc=1, device_id=peer, device_id_type=pl.DeviceIdType.MESH)
    pl.semaphore_wait(barrier, len(peers))


def _exchange_call(body, x, out_struct, name, n_sems, local_sem, collective_id):
    sems = [pltpu.SemaphoreType.DMA((n_sems,)), pltpu.SemaphoreType.DMA((n_sems,))]
    sems += [pltpu.SemaphoreType.DMA] if local_sem else []
    if collective_id is None:
        return pl.pallas_call(
            body, name=name, in_specs=[ANYSPACE], out_specs=ANYSPACE, out_shape=out_struct, scratch_shapes=sems,
            compiler_params=pltpu.CompilerParams(has_side_effects=True),
        )(x)
    x_ref = jax.new_ref(x, memory_space=pltpu.MemorySpace.HBM)
    o_ref = jax.empty_ref(out_struct, memory_space=pltpu.MemorySpace.HBM)

    @pl.kernel(mesh=plsc.ScalarSubcoreMesh(axis_name="sequencer", num_cores=1), name=name,
               scratch_types=tuple(sems), compiler_params=pltpu.CompilerParams(collective_id=collective_id))
    def launch(*sem_refs):
        body(x_ref, o_ref, *sem_refs)

    launch()
    return o_ref[...]


def _gather(x, name, collective_id=None):
    def body(x_ref, o_ref, send_sems, recv_sems, local_sem):
        mx, my, mc = lax.axis_index("x"), lax.axis_index("y"), lax.axis_index("c")
        me, sibling = (mx, my, mc), (mx, my, 1 - mc)
        chips = _other_chips(mx, my)
        if collective_id is not None:
            _handshake([sibling] + [(*chip, mc) for chip in chips])

        def slot(px, py, pc):
            return o_ref.at[4 * px + 2 * py + pc]

        def copy(k, block, to, src=None):
            return pltpu.make_async_remote_copy(
                src_ref=slot(*block) if src is None else src, dst_ref=slot(*block),
                send_sem=send_sems.at[k], recv_sem=recv_sems.at[k],
                device_id=to, device_id_type=pl.DeviceIdType.MESH)

        mine = pltpu.make_async_copy(x_ref, slot(*me), local_sem)
        mine.start()
        first = [copy(0, me, sibling, src=x_ref)]
        first += [copy(1 + j, me, (*chip, mc), src=x_ref) for j, chip in enumerate(chips)]
        for cp in first:
            cp.start()
        passed = [copy(4 + j, (*chip, mc), sibling) for j, chip in enumerate(chips)]
        for j, chip in enumerate(chips):
            copy(1 + j, (*chip, mc), me).wait_recv()
            passed[j].start()
        copy(0, sibling, me).wait_recv()
        for j, chip in enumerate(chips):
            copy(4 + j, (*chip, 1 - mc), me).wait_recv()
        for cp in first + passed:
            cp.wait_send()
        mine.wait()

    return _exchange_call(body, x, jax.ShapeDtypeStruct((N_DEV,) + x.shape, x.dtype), name, N_DEV - 1, True,
                          collective_id)


def _pair_send(parts, name, collective_id=None):
    def body(p_ref, o_ref, send_sems, recv_sems):
        mx, my, mc = lax.axis_index("x"), lax.axis_index("y"), lax.axis_index("c")
        if collective_id is not None:
            _handshake([(mx, my, 1 - mc)])
        copies = [pltpu.make_async_remote_copy(
            src_ref=p_ref.at[2 * chip + (1 - mc)], dst_ref=o_ref.at[chip],
            send_sem=send_sems.at[chip], recv_sem=recv_sems.at[chip],
            device_id=(mx, my, 1 - mc), device_id_type=pl.DeviceIdType.MESH) for chip in range(N_CHIP)]
        for cp in copies:
            cp.start()
        for cp in copies:
            cp.wait_recv()
        for cp in copies:
            cp.wait_send()

    return _exchange_call(body, parts, jax.ShapeDtypeStruct((N_CHIP,) + parts.shape[1:], parts.dtype), name,
                          N_CHIP, False, collective_id)


def _pair_sum(parts, recv, out_dtype, name):
    _, rows, cols = parts.shape
    tr, tc = _tile2d(rows, cols)

    def body(c_ref, p_ref, r_ref, o_ref):
        o_ref[...] = (p_ref[...] + r_ref[...]).astype(o_ref.dtype)

    return pl.pallas_call(
        body, name=name,
        grid_spec=pltpu.PrefetchScalarGridSpec(
            num_scalar_prefetch=1, grid=(N_CHIP, rows // tr, cols // tc),
            in_specs=[pl.BlockSpec((None, tr, tc), lambda k, i, j, c: (2 * k + c[0], i, j)),
                      pl.BlockSpec((None, tr, tc), lambda k, i, j, c: (k, i, j))],
            out_specs=pl.BlockSpec((None, tr, tc), lambda k, i, j, c: (k, i, j))),
        out_shape=jax.ShapeDtypeStruct((N_CHIP, rows, cols), out_dtype),
        compiler_params=_cparams("parallel", "parallel", "parallel"),
    )(lax.axis_index("c").astype(jnp.int32).reshape(1), parts, recv)


def _chip_exchange(sums, name, collective_id=None):
    def body(s_ref, o_ref, send_sems, recv_sems, local_sem):
        mx, my, mc = lax.axis_index("x"), lax.axis_index("y"), lax.axis_index("c")
        my_chip = 2 * mx + my
        if collective_id is not None:
            _handshake([(px, py, mc) for px, py in _other_chips(mx, my)])
        local = pltpu.make_async_copy(s_ref.at[my_chip], o_ref.at[my_chip], local_sem)
        local.start()
        sends, recvs = [], []
        for k, (px, py) in enumerate(_other_chips(mx, my)):
            peer = 2 * px + py

            def copy(src_slot, dst_slot, k=k, dev=(px, py, mc)):
                return pltpu.make_async_remote_copy(
                    src_ref=s_ref.at[src_slot], dst_ref=o_ref.at[dst_slot], send_sem=send_sems.at[k],
                    recv_sem=recv_sems.at[k], device_id=dev, device_id_type=pl.DeviceIdType.MESH)

            sends.append(copy(peer, my_chip))
            recvs.append(copy(peer, peer))
        for cp in sends:
            cp.start()
        for cp in recvs:
            cp.wait_recv()
        for cp in sends:
            cp.wait_send()
        local.wait()

    return _exchange_call(body, sums, jax.ShapeDtypeStruct(sums.shape, sums.dtype), name, N_CHIP - 1, True,
                          collective_id)


ADAM_TILE_ELEMS = 128 * 1024


def _tile2d(rows, cols):
    if rows * cols <= ADAM_TILE_ELEMS:
        return rows, cols
    if rows % 8 == 0:
        return _pick(rows, max(8, ADAM_TILE_ELEMS // cols), 8), cols
    return rows, _pick(cols, max(LANES, ADAM_TILE_ELEMS // rows))


def _adamw(w, parts, m, v, name):
    rows, cols = w.shape
    n_parts = parts.shape[0]
    tr, tc = _tile2d(rows, cols)

    def body(w_ref, p_ref, m_ref, v_ref, g_ref, d_ref, nm_ref, nv_ref):
        g = p_ref[0].astype(F32)
        for p in range(1, n_parts):
            g = g + p_ref[p].astype(F32)
        mm = ADAM_B1 * m_ref[...] + (1.0 - ADAM_B1) * g
        vv = ADAM_B2 * v_ref[...] + (1.0 - ADAM_B2) * jnp.square(g)
        m_hat = mm / (1.0 - ADAM_B1 ** ADAM_STEP)
        v_hat = vv / (1.0 - ADAM_B2 ** ADAM_STEP)
        g_ref[...] = g
        d_ref[...] = -ADAM_LR * (m_hat / (jnp.sqrt(v_hat) + ADAM_EPS) + ADAM_WD * w_ref[...])
        nm_ref[...] = mm
        nv_ref[...] = vv

    blk = pl.BlockSpec((tr, tc), lambda i, j: (i, j))
    return pl.pallas_call(
        body, name=name, grid=(rows // tr, cols // tc),
        in_specs=[blk, pl.BlockSpec((n_parts, tr, tc), lambda i, j: (0, i, j)), blk, blk],
        out_specs=[blk] * 4, out_shape=[jax.ShapeDtypeStruct((rows, cols), F32)] * 4,
        compiler_params=_cparams("parallel", "parallel"),
    )(w, parts, m, v)


WEIGHTS = ("l0_norm_w", "l0_w_in", "l0_s5_lambda_re", "l0_s5_lambda_im", "l0_s5_log_step", "l0_s5_b_re",
           "l0_s5_b_im", "l0_s5_c_re", "l0_s5_c_im", "l0_s5_d", "l0_s5_w_glu", "l0_s5_b_glu", "l0_ssd_conv_w",
           "l0_ssd_conv_b", "l0_ssd_dt_bias", "l0_ssd_a_log", "l0_ssd_d", "l0_ssd_norm_w", "l0_w_out",
           "l1_norm_w", "l1_w_in", "l1_fox_b_f", "l1_w_out", "final_norm_w")
SHARDED = ("l0_w_in", "l0_s5_w_glu", "l0_ssd_conv_w", "l0_w_out", "l1_w_in", "l1_w_out")


def _pad_lanes(a, width=LANES):
    return jnp.pad(a, [(0, 0)] * (a.ndim - 1) + [(0, width - a.shape[-1])])


def _pad_rows(a, height=LANES):
    return jnp.pad(a, [(0, height - a.shape[0])] + [(0, 0)] * (a.ndim - 1))


def _to_groups(a):
    length = a.shape[0]
    return _pad_lanes(a[:, :SSD_HEADS].reshape(length, SSD_GROUPS, SSD_HPG).transpose(1, 0, 2))


def _from_groups(a):
    length = a.shape[1]
    return a[:, :, :SSD_HPG].transpose(1, 0, 2).reshape(length, SSD_HEADS)


class _Grads(dict):
    def __init__(self, on_grad):
        super().__init__()
        self.on_grad = on_grad
        self.raw = {}
        self.sums = {}

    def __setitem__(self, name, value):
        self.store(name, value)

    def store(self, name, value, between=None):
        if self.on_grad is not None and name in SHARDED:
            self.raw[name] = value
            self.sums[name], value = self.on_grad(name, value, between)
        elif between is not None:
            between()
        super().__setitem__(name, value)

    def before(self, x, stage, name):
        if self.on_grad is None:
            return x
        made = self if stage == "done" else getattr(self, stage)
        return lax.optimization_barrier((x, made[name]))[0]


def _local_step(x, target, w, on_grad=None):
    length = x.shape[0]
    nc = length // SSD_CHUNK
    g = _Grads(on_grad)

    h0 = _rmsnorm_fwd(x, w["l0_norm_w"], "l0_norm")
    w0 = w["l0_w_in"]
    r0_s5, r0_z, r0_xbc = (0, 2 * S5_WIDTH), (2 * S5_WIDTH, SSD_WIDTH), (2 * S5_WIDTH + SSD_WIDTH, SSD_XBC)
    r0_dt = 2 * S5_WIDTH + SSD_WIDTH + SSD_XBC
    w0_dt = _pad_rows(w0[r0_dt:])
    p_s5 = _mm(h0, w0, "nt", F32, "l0_in_s5", b_rows=r0_s5)
    p_z = _mm(h0, w0, "nt", F32, "l0_in_z", b_rows=r0_z)
    p_xbc = _mm(h0, w0, "nt", F32, "l0_in_xbc", b_rows=r0_xbc)
    p_dt = _mm(h0, w0_dt, "nt", F32, "l0_in_dt")

    row = lambda a: a.reshape(1, S5_NS)
    b_rows = lambda a: a.transpose(2, 0, 1).reshape(S5_GROUP, S5_NS)
    prep_in = (row(w["l0_s5_lambda_re"]), row(w["l0_s5_lambda_im"]),
               row(jnp.repeat(w["l0_s5_log_step"], S5_STATE)), b_rows(w["l0_s5_b_re"]), b_rows(w["l0_s5_b_im"]))
    ab_re, ab_im, bbr, bbi = _s5_prep(*prep_in)
    to_bb = lambda a: _blockdiag(a.reshape(S5_GROUP, S5_GROUPS, S5_STATE).transpose(1, 0, 2),
                                 S5_GROUP, S5_STATE).astype(MXU_DT)
    to_ct = lambda a: _blockdiag(a.transpose(0, 2, 1), S5_STATE, S5_GROUP).astype(MXU_DT)
    bb_re, bb_im = to_bb(bbr), to_bb(bbi)
    ct_re, ct_im = to_ct(w["l0_s5_c_re"]), to_ct(w["l0_s5_c_im"])
    a_re3, a_im3 = ab_re.reshape(S5_BLK, 1, S5_BS), ab_im.reshape(S5_BLK, 1, S5_BS)
    d3 = w["l0_s5_d"].reshape(S5_BLK, 1, S5_BC)
    y5, s_re, s_im = _s5_fwd(p_s5, bb_re, bb_im, ct_re, ct_im, a_re3, a_im3, d3)
    g_bf = _tiles("s5_gelu", lambda yb: (_gelu(yb),), [y5], [], [(S5_WIDTH, MXU_DT)], 0, tr=256)[0]
    zg = _mm(g_bf, w["l0_s5_w_glu"], "nn", F32, "s5_glu")
    b_glu = w["l0_s5_b_glu"].reshape(1, -1)
    s5_out = _tiles("s5_out", lambda yb, zb, gb, bb: (_s5_out_fn(yb, zb, gb, bb),),
                    [y5, zg, (p_s5, 1)], [b_glu], [(S5_WIDTH, MXU_DT)], 0, tr=256)[0]

    conv_w = w["l0_ssd_conv_w"]
    cv, act = _conv_fwd(p_xbc, conv_w, w["l0_ssd_conv_b"])
    bias_row = _pad_lanes(w["l0_ssd_dt_bias"].reshape(1, -1))
    alog_row = _pad_lanes(w["l0_ssd_a_log"].reshape(1, -1))
    dt, cum, cum_t = _dt_fwd(p_dt, bias_row, alog_row)
    dtg, cumg = _to_groups(dt), _to_groups(cum)
    cumtg = cum_t[:, :SSD_HEADS].reshape(nc, SSD_GROUPS, SSD_HPG, SSD_CHUNK).transpose(1, 0, 2, 3)
    cumtg = jnp.pad(cumtg, ((0, 0), (0, 0), (0, 16 - SSD_HPG), (0, 0)))
    ycore, states = _ssd_core_fwd(act, dtg, cumg, cumtg)
    dchan = jnp.repeat(w["l0_ssd_d"], SSD_HEAD_DIM).reshape(1, -1)
    nw_row = w["l0_ssd_norm_w"].reshape(1, -1)
    ssd_out = _tiles("ssd_post", lambda a, b, c, d, e: (_ssd_post_fn(a, b, c, d, e),),
                     [ycore, act, p_z], [dchan, nw_row], [(SSD_WIDTH, MXU_DT)], 0, tr=256, tc=SSD_GW)[0]
    mixed = jnp.concatenate([s5_out, ssd_out], axis=1)
    x1 = _mm(mixed, w["l0_w_out"], "nn", F32, "l0_out", res=x)

    h1 = _rmsnorm_fwd(x1, w["l1_norm_w"], "l1_norm")
    w1 = w["l1_w_in"]
    r1_qkv, r1_gate = (0, 3 * FOX_WIDTH), (3 * FOX_WIDTH, FOX_WIDTH)
    w1_f = _pad_rows(w1[4 * FOX_WIDTH:])
    qkv = _mm(h1, w1, "nt", MXU_DT, "l1_in_qkv", b_rows=r1_qkv)
    gate1 = _mm(h1, w1, "nt", F32, "l1_in_gate", b_rows=r1_gate)
    f_raw = _mm(h1, w1_f, "nt", F32, "l1_in_f")
    bf_row = _pad_lanes(w["l1_fox_b_f"].reshape(1, -1))
    c = _fgate_fwd(f_raw, bf_row)
    c0_rep = _fox_c0(c, min(FOX_TILE, length))
    c_t = c[:, :FOX_HEADS].T.reshape(FOX_HEADS, 1, length)
    att, lse = _fox_fwd(qkv, c0_rep, c_t)
    out1 = _tiles("fox_gate", lambda a, b: (_fox_gate_fn(a, b),), [att, gate1], [],
                  [(FOX_WIDTH, MXU_DT)], 0, tr=256)[0]
    x2 = _mm(out1, w["l1_w_out"], "nn", F32, "l1_out", res=x1)

    loss_part, dx2, dx2b, g["final_norm_w"] = _final_loss(x2, w["final_norm_w"], target, "final_loss")

    d_out1 = _mm(dx2b, w["l1_w_out"], "nt", F32, "l1_out_dx")
    g["l1_w_out"] = _mm(out1, dx2b, "tn", F32, "l1_out_dw")

    def gate_bwd(a, gt, d):
        _, vjp = jax.vjp(_fox_gate_fn, a, gt)
        da, dgt = vjp(d)
        return da, dgt, jnp.broadcast_to(jnp.sum(da * a, axis=1, keepdims=True), a.shape)

    d_att, d_gate1, delta = _tiles("fox_gate_bwd", gate_bwd, [att, gate1, d_out1], [],
                                   [(FOX_WIDTH, MXU_DT), (FOX_WIDTH, MXU_DT), (FOX_WIDTH, F32)], 0,
                                   tr=512, tc=FOX_HEAD_DIM)
    dq, dk, dv, dcq, dck = _fox_bwd(qkv, g.before(d_att, "raw", "l1_w_out"), lse, delta, c0_rep, c_t)
    dc = dcq.reshape(length, FOX_HEADS, FOX_HEAD_DIM)[:, :, 0] + dck.reshape(FOX_HEADS, length).T
    df, dbf = _fgate_bwd(f_raw, bf_row, _pad_lanes(dc))
    g["l1_fox_b_f"] = dbf[0, :FOX_HEADS]
    dqkv = jnp.concatenate([dq, dk, dv], axis=1)
    dh1 = _mm(dqkv, w1, "nn", F32, "l1_in_dx_qkv", b_rows=r1_qkv)
    dh1 = _mm(d_gate1, w1, "nn", F32, "l1_in_dx_gate", res=dh1, b_rows=r1_gate)
    dh1 = _mm(df, w1_f, "nn", F32, "l1_in_dx_f", res=dh1)
    dw1 = _mm(dqkv, h1, "tn", F32, "l1_in_dw_qkv", into=(lax.empty((ODD_IN, D_MODEL), F32), r1_qkv[0]))
    dw1 = _mm(d_gate1, h1, "tn", F32, "l1_in_dw_gate", into=(dw1, r1_gate[0]))
    g["l1_w_in"] = dw1.at[4 * FOX_WIDTH:].set(_mm(df, h1, "tn", F32, "l1_in_dw_f")[:FOX_HEADS])
    dh1 = g.before(g.before(dh1, "raw", "l1_w_in"), "done", "l1_w_out")
    dx1, dx1b, g["l1_norm_w"] = _rmsnorm_bwd(x1, w["l1_norm_w"], dh1, dx2, "l1_norm_bwd")

    wout0 = w["l0_w_out"]
    d_s5 = _mm(dx1b, wout0, "nt", F32, "l0_out_dx_s5", b_rows=(0, S5_WIDTH))
    d_ssd = _mm(dx1b, wout0, "nt", F32, "l0_out_dx_ssd", b_rows=(S5_WIDTH, SSD_WIDTH))
    g["l0_w_out"] = _mm(mixed, dx1b, "tn", F32, "l0_out_dw")
    d_ssd = g.before(d_ssd, "sums", "l1_w_in")

    def post_bwd(a, b, c_, d, dch, nw):
        _, vjp = jax.vjp(_ssd_post_fn, a, b, c_, dch, nw)
        return vjp(d)

    dycore, dxs_post, dz, ddch, dnw = _tiles(
        "ssd_post_bwd", post_bwd, [ycore, act, p_z, d_ssd], [dchan, nw_row],
        [(SSD_WIDTH, F32), (SSD_WIDTH, F32), (SSD_WIDTH, MXU_DT)], 2, tr=256, tc=SSD_GW)
    g["l0_ssd_d"] = ddch[0].reshape(SSD_HEADS, SSD_HEAD_DIM).sum(axis=1)
    g["l0_ssd_norm_w"] = dnw[0]
    dycore = g.before(dycore, "raw", "l0_w_out")
    dxs, d_b, d_c, ddtg, dcumg, dcumtg, dclg = _ssd_core_bwd(dycore, dxs_post, act, dtg, cumg, cumtg, states)
    dact = g.before(jnp.concatenate([dxs, d_b, d_c], axis=1), "sums", "l0_w_out")
    dxbc, dconvw, dconvb = _conv_bwd(dact, cv, p_xbc, conv_w)
    g["l0_ssd_conv_w"] = dconvw[:SSD_CONV]
    g["l0_ssd_conv_b"] = dconvb[0]
    dcum = _from_groups(dcumg)
    dcum = dcum + dcumtg[:, :, :SSD_HPG].transpose(1, 3, 0, 2).reshape(length, SSD_HEADS)
    dcl = dclg[:, :, 0, :SSD_HPG].transpose(1, 0, 2).reshape(nc, SSD_HEADS)
    dcum = dcum.reshape(nc, SSD_CHUNK, SSD_HEADS).at[:, SSD_CHUNK - 1, :].add(dcl).reshape(length, SSD_HEADS)
    ddt_raw, dbias, dalog = _dt_bwd(p_dt, bias_row, alog_row, _pad_lanes(_from_groups(ddtg)), _pad_lanes(dcum))
    g["l0_ssd_dt_bias"] = dbias[0, :SSD_HEADS]
    g["l0_ssd_a_log"] = dalog[0, :SSD_HEADS]

    def s5_out_bwd(yb, zb, gb, d, bb):
        _, vjp = jax.vjp(_s5_out_fn, yb, zb, gb, bb)
        return vjp(d)

    dy_direct, dzg, dgate0, dbglu = _tiles(
        "s5_out_bwd", s5_out_bwd, [y5, zg, (p_s5, 1), d_s5], [b_glu],
        [(S5_WIDTH, F32), (S5_WIDTH, MXU_DT), (S5_WIDTH, MXU_DT)], 1, tr=256)
    g["l0_s5_b_glu"] = dbglu[0]
    g["l0_s5_w_glu"] = _mm(g_bf, dzg, "tn", F32, "s5_glu_dw")
    dg2 = _mm(dzg, w["l0_s5_w_glu"], "nt", F32, "s5_glu_dx")

    def gelu_bwd(yb, d, direct):
        _, vjp = jax.vjp(_gelu, yb)
        return (vjp(d)[0] + direct,)

    dy5 = _tiles("s5_gelu_bwd", gelu_bwd, [y5, dg2, dy_direct], [], [(S5_WIDTH, F32)], 0, tr=256)[0]
    dy5 = g.before(dy5, "done", "l1_w_in")
    du, dbbr3, dbbi3, dctr3, dcti3, dar, dai, dd5 = _s5_bwd(dy5, p_s5, s_re, s_im, bb_re, bb_im, ct_re, ct_im,
                                                           a_re3, a_im3, d3)
    from_bb = lambda a: _blockdiag_t(a, S5_GROUP, S5_STATE).transpose(1, 0, 2).reshape(S5_GROUP, S5_NS)
    from_ct = lambda a: _blockdiag_t(a, S5_STATE, S5_GROUP).transpose(0, 2, 1)
    g["l0_s5_c_re"], g["l0_s5_c_im"] = from_ct(dctr3), from_ct(dcti3)
    g["l0_s5_d"] = dd5[:, 0, :].reshape(S5_GROUPS, S5_GROUP)
    dlr, dli, dls, dbr, dbi = _s5_prep_bwd(*prep_in, dar[:, 0, :].reshape(1, S5_NS), dai[:, 0, :].reshape(1, S5_NS),
                                           from_bb(dbbr3), from_bb(dbbi3))
    g["l0_s5_lambda_re"] = dlr.reshape(S5_GROUPS, S5_STATE)
    g["l0_s5_lambda_im"] = dli.reshape(S5_GROUPS, S5_STATE)
    g["l0_s5_log_step"] = dls.reshape(S5_GROUPS, S5_STATE).sum(axis=1)
    from_rows = lambda a: a.reshape(S5_GROUP, S5_GROUPS, S5_STATE).transpose(1, 2, 0)
    g["l0_s5_b_re"], g["l0_s5_b_im"] = from_rows(dbr), from_rows(dbi)

    dus = g.before(jnp.concatenate([du, dgate0], axis=1), "done", "l0_w_out")
    dw0 = _mm(dus, h0, "tn", F32, "l0_in_dw_s5", into=(lax.empty((EVEN_IN, D_MODEL), F32), r0_s5[0]))
    dw0 = _mm(dz, h0, "tn", F32, "l0_in_dw_z", into=(dw0, r0_z[0]))
    dw0 = _mm(dxbc, h0, "tn", F32, "l0_in_dw_xbc", into=(dw0, r0_xbc[0]))
    dw0 = dw0.at[r0_dt:].set(_mm(ddt_raw, h0, "tn", F32, "l0_in_dw_dt")[:SSD_HEADS])
    first = {}

    def first_half():
        dus_late = lax.optimization_barrier((dus, dw0))[0] if on_grad is not None else dus
        dh = _mm(dus_late, w0, "nn", F32, "l0_in_dx_s5", b_rows=r0_s5)
        first["dh0"] = _mm(dz, w0, "nn", F32, "l0_in_dx_z", res=dh, b_rows=r0_z)
        return first["dh0"]

    g.store("l0_w_in", dw0, between=first_half)
    dh0 = _mm(dxbc, w0, "nn", F32, "l0_in_dx_xbc", res=g.before(first["dh0"], "sums", "l0_w_in"), b_rows=r0_xbc)
    dh0 = _mm(ddt_raw, w0_dt, "nn", F32, "l0_in_dx_dt", res=dh0)
    grad_x, _, g["l0_norm_w"] = _rmsnorm_bwd(x, w["l0_norm_w"], dh0, dx1, "l0_norm_bwd")
    return loss_part, grad_x, g


TRANSPOSED = ("l0_w_in", "l1_w_in")


SEQUENCER_IDS = {"l0_w_out": (0, 1, 2), "l1_w_in": (3, 4, 5), "l1_w_out": (6, 7, 8), "l0_w_in": (None, 9, 10),
                 "l0_s5_w_glu": (11, 12, 13), "l0_ssd_conv_w": (14, 15, 16)}
NO_IDS = (None, None, None)


def _gather_weight(name, shard):
    cid = SEQUENCER_IDS.get(name, NO_IDS)[0]
    if name == "l0_ssd_conv_w":
        full = _gather(shard, "gather_" + name, cid)
        return full.transpose(1, 0, 2).reshape(shard.shape[0], N_DEV * shard.shape[1])
    if name in TRANSPOSED:
        full = _gather(shard.T.astype(MXU_DT), "gather_" + name, cid)
        return full.reshape(N_DEV * shard.shape[1], shard.shape[0])
    full = _gather(shard.astype(MXU_DT), "gather_" + name, cid)
    return full.reshape(N_DEV * shard.shape[0], shard.shape[1])


def _reduce_grad(name, grad, shard_shape, between=None):
    rows, cols = shard_shape
    if name == "l0_ssd_conv_w":
        parts = grad.reshape(rows, N_DEV, cols).transpose(1, 0, 2)
    elif name in TRANSPOSED:
        parts = grad.reshape(N_DEV, cols, rows)
    else:
        parts = grad.reshape(N_DEV, rows, cols)
    _, pair_id, chip_id = SEQUENCER_IDS.get(name, NO_IDS)
    recv = _pair_send(parts, "pair_" + name, pair_id)
    if between is not None:
        parts = lax.optimization_barrier((parts, between()))[0]
    sums = _pair_sum(parts, recv, F32 if name == "l0_ssd_conv_w" else MXU_DT, "pairsum_" + name)
    return sums, _chip_exchange(sums, "scatter_" + name, chip_id)


SMALL_ROWS_QUANTUM = 8 * LANES


def _step(args):
    x = args["x"][0]
    target = args["loss_target"][0]
    full = {n: args[n] for n in WEIGHTS if n not in SHARDED}
    for n in SHARDED:
        shard = args[n]
        if n != SHARDED[0]:
            shard, full[SHARDED[0]] = lax.optimization_barrier((shard, full[SHARDED[0]]))
        full[n] = _gather_weight(n, shard)
    out_g, out_d, out_m, out_v = {}, {}, {}, {}

    loss_part, grad_x, g = _local_step(
        x, target, full, lambda n, grad, between: _reduce_grad(n, grad, args[n].shape, between))
    for n in reversed(SHARDED):
        view = (lambda a: a.T) if n in TRANSPOSED else (lambda a: a)
        outs = _adamw(view(args[n]), g[n], view(args["m_" + n]), view(args["v_" + n]), "adamw_" + n)
        out_g[n], out_d[n], out_m[n], out_v[n] = [view(o) for o in outs]

    small = [n for n in WEIGHTS if n not in SHARDED]
    sizes = [int(math.prod(args[n].shape)) for n in small]
    total = sum(sizes) + 1
    padded = -(-total // SMALL_ROWS_QUANTUM) * SMALL_ROWS_QUANTUM

    def pack(pieces, extra):
        flat = jnp.concatenate([p.reshape(-1).astype(F32) for p in pieces] + [extra.reshape(1)])
        return jnp.pad(flat, (0, padded - total)).reshape(padded // LANES, LANES)

    zero = jnp.zeros((), F32)
    parts = _gather(pack([g[n] for n in small], loss_part), "gather_small_grads")
    sg, sd, sm, sv = _adamw(pack([args[n] for n in small], zero), parts,
                            pack([args["m_" + n] for n in small], zero),
                            pack([args["v_" + n] for n in small], zero), "adamw_small")
    off = 0
    for n, sz in zip(small, sizes):
        cut = lambda a: a.reshape(-1)[off:off + sz].reshape(args[n].shape)
        out_g[n], out_d[n], out_m[n], out_v[n] = cut(sg), cut(sd), cut(sm), cut(sv)
        off += sz
    loss = sg.reshape(-1)[total - 1]
    return (loss, grad_x[None], *[out_g[n] for n in WEIGHTS], *[out_d[n] for n in WEIGHTS],
            *[out_m[n] for n in WEIGHTS], *[out_v[n] for n in WEIGHTS])


def kernel(x, l0_norm_w, l0_w_in, l0_s5_lambda_re, l0_s5_lambda_im, l0_s5_log_step, l0_s5_b_re, l0_s5_b_im, l0_s5_c_re, l0_s5_c_im, l0_s5_d, l0_s5_w_glu, l0_s5_b_glu, l0_ssd_conv_w, l0_ssd_conv_b, l0_ssd_dt_bias, l0_ssd_a_log, l0_ssd_d, l0_ssd_norm_w, l0_w_out, l1_norm_w, l1_w_in, l1_fox_b_f, l1_w_out, final_norm_w, loss_target, m_l0_norm_w, m_l0_w_in, m_l0_s5_lambda_re, m_l0_s5_lambda_im, m_l0_s5_log_step, m_l0_s5_b_re, m_l0_s5_b_im, m_l0_s5_c_re, m_l0_s5_c_im, m_l0_s5_d, m_l0_s5_w_glu, m_l0_s5_b_glu, m_l0_ssd_conv_w, m_l0_ssd_conv_b, m_l0_ssd_dt_bias, m_l0_ssd_a_log, m_l0_ssd_d, m_l0_ssd_norm_w, m_l0_w_out, m_l1_norm_w, m_l1_w_in, m_l1_fox_b_f, m_l1_w_out, m_final_norm_w, v_l0_norm_w, v_l0_w_in, v_l0_s5_lambda_re, v_l0_s5_lambda_im, v_l0_s5_log_step, v_l0_s5_b_re, v_l0_s5_b_im, v_l0_s5_c_re, v_l0_s5_c_im, v_l0_s5_d, v_l0_s5_w_glu, v_l0_s5_b_glu, v_l0_ssd_conv_w, v_l0_ssd_conv_b, v_l0_ssd_dt_bias, v_l0_ssd_a_log, v_l0_ssd_d, v_l0_ssd_norm_w, v_l0_w_out, v_l1_norm_w, v_l1_w_in, v_l1_fox_b_f, v_l1_w_out, v_final_norm_w):
    return _step(dict(locals()))
```

```python
import functools
import math

import jax
import jax.numpy as jnp
from jax import lax
from jax.experimental import pallas as pl
from jax.experimental.pallas import tpu as pltpu
from jax.experimental.pallas import tpu_sc as plsc

F32 = jnp.float32
BF16 = jnp.bfloat16
MXU_DT = BF16

D_MODEL = 4096
S5_WIDTH = 2048
S5_GROUP = 16
S5_GROUPS = 128
S5_STATE = 64
S5_EIG_CLIP = -1e-4
S5_BLK = 16
SSD_WIDTH = 6144
SSD_HEAD_DIM = 64
SSD_HEADS = 96
SSD_GROUPS = 8
SSD_STATE = 128
SSD_CONV = 4
SSD_CHUNK = 128
SSD_XBC = 8192
SSD_HPG = SSD_HEADS // SSD_GROUPS
FOX_HEAD_DIM = 128
FOX_HEADS = 32
FOX_WIDTH = 4096
NORM_EPS = 1e-5
EVEN_IN = 18528
ODD_IN = 16416
EVEN_PAD = 18560
ODD_PAD = 16512
LANES = 128
N_DEV = 8

ADAM_LR = 0.001
ADAM_B1 = 0.9
ADAM_B2 = 0.999
ADAM_EPS = 1e-08
ADAM_WD = 0.01
ADAM_STEP = 10

VMEM_LIMIT_BYTES = 48 * 1024 * 1024


ANYSPACE = pl.BlockSpec(memory_space=pl.ANY)


def _cparams(*sem):
    return pltpu.CompilerParams(dimension_semantics=sem, vmem_limit_bytes=VMEM_LIMIT_BYTES)


def _pick(n, target, quantum=LANES):
    if n <= target:
        return n
    t = (target // quantum) * quantum
    while t >= quantum:
        if n % t == 0:
            return t
        t -= quantum
    raise ValueError((n, target, quantum))


_MM_DIMS = {"nn": ((1,), (0,)), "nt": ((1,), (1,)), "tn": ((0,), (0,))}


MM_VMEM_BUDGET = 38 * 1024 * 1024


def _mm_tk(k, tm, tn, out_bytes, has_res, tk_t, start=0):
    fixed = 2 * tm * tn * out_bytes + (2 * tm * tn * 4 if has_res else 0)
    tk = min(k, tk_t)
    while True:
        if k % tk == 0 and start % tk == 0 and (tk == k or tk % LANES == 0):
            need = fixed + 2 * (tm + tn) * tk * 2 + (tm * tn * 4 if tk < k else 0)
            if need <= MM_VMEM_BUDGET or tk <= LANES:
                return tk
        tk -= LANES if tk % LANES == 0 else tk % LANES


def _mm(a, b, mode, out_dtype, name, res=None, b_rows=None, into=None, tm_t=1024, tn_t=512, tk_t=8192):
    b_start, b_size = b_rows if b_rows is not None else (0, b.shape[0])
    if mode == "nn":
        (m, k), (k2, n) = a.shape, (b_size, b.shape[1])
    elif mode == "nt":
        (m, k), (n, k2) = a.shape, (b_size, b.shape[1])
    else:
        (k, m), (k2, n) = a.shape, (b_size, b.shape[1])
    assert k == k2, (a.shape, b.shape, mode)
    tm, tn = _pick(m, tm_t), _pick(n, tn_t)
    has_res = res is not None
    has_into = into is not None
    tk = _mm_tk(k, tm, tn, jnp.dtype(out_dtype).itemsize, has_res, tk_t, b_start if mode == "nn" else 0)
    nk = k // tk
    dims = (_MM_DIMS[mode], ((), ()))
    o_row = 0
    if has_into:
        assert into[1] % tm == 0 and into[0].shape[1] == n and into[0].dtype == out_dtype, (into[1], tm)
        o_row = into[1] // tm
    if mode == "nt":
        assert b_start % tn == 0, (b_start, tn)
    b_blk = b_start // (tn if mode == "nt" else tk)

    def body(*refs):
        a_ref, b_ref = refs[:2]
        r_ref = refs[2] if has_res else None
        o_ref = refs[2 + has_res + has_into]
        part = lax.dot_general(a_ref[...].astype(MXU_DT), b_ref[...].astype(MXU_DT), dims,
                               preferred_element_type=F32)

        def finish(r):
            if has_res:
                r = r + r_ref[...]
            o_ref[...] = r.astype(out_dtype)

        if nk == 1:
            finish(part)
            return
        acc = refs[-1]
        kk = pl.program_id(2)

        @pl.when(kk == 0)
        def _():
            acc[...] = part

        @pl.when(jnp.logical_and(kk > 0, kk < nk - 1))
        def _():
            acc[...] += part

        @pl.when(kk == nk - 1)
        def _():
            finish(acc[...] + part)

    a_spec = (pl.BlockSpec((tk, tm), lambda i, j, kk: (kk, i)) if mode == "tn"
              else pl.BlockSpec((tm, tk), lambda i, j, kk: (i, kk)))
    b_spec = (pl.BlockSpec((tn, tk), lambda i, j, kk: (j + b_blk, kk)) if mode == "nt"
              else pl.BlockSpec((tk, tn), lambda i, j, kk: (kk + b_blk, j)))
    r_spec = pl.BlockSpec((tm, tn), lambda i, j, kk: (i, j))
    o_spec = pl.BlockSpec((tm, tn), lambda i, j, kk: (i + o_row, j))
    in_specs = [a_spec, b_spec] + ([r_spec] if has_res else []) + ([ANYSPACE] if has_into else [])
    args = (a, b) + ((res,) if has_res else ()) + ((into[0],) if has_into else ())
    return pl.pallas_call(
        body, name=name, grid=(m // tm, n // tn, nk), in_specs=in_specs, out_specs=o_spec,
        out_shape=jax.ShapeDtypeStruct(into[0].shape if has_into else (m, n), out_dtype),
        scratch_shapes=[pltpu.VMEM((tm, tn), F32)] if nk > 1 else [],
        input_output_aliases={len(args) - 1: 0} if has_into else {},
        compiler_params=_cparams("parallel", "parallel", "arbitrary"),
    )(*args)


def _tiles(name, fn, tiled, rows, out_tiled, out_acc, tr, tc=None):
    tiled = [t if isinstance(t, tuple) else (t, 0) for t in tiled]
    length = tiled[0][0].shape[0]
    width = out_tiled[0][0] if out_tiled else rows[0].shape[1]
    tc = width if tc is None else tc
    tr = min(tr, length)
    n_in = len(tiled) + len(rows)
    n_ot = len(out_tiled)

    def body(*refs):
        outs = fn(*[r[...] for r in refs[:n_in]])
        outs_t, outs_a = outs[:n_ot], outs[n_ot:]
        for r, v in zip(refs[n_in:n_in + n_ot], outs_t):
            r[...] = v.astype(r.dtype)
        i = pl.program_id(1)
        for r, v in zip(refs[n_in + n_ot:], outs_a):
            @pl.when(i == 0)
            def _(r=r):
                r[...] = jnp.zeros_like(r)

            r[...] += jnp.broadcast_to(v, r.shape)

    def tspec(off):
        return pl.BlockSpec((tr, tc), lambda j, i: (i, j + off))

    in_specs = [tspec(off) for _, off in tiled] + [pl.BlockSpec((1, tc), lambda j, i: (0, j)) for _ in rows]
    out_specs = [tspec(0) for _ in out_tiled] + [pl.BlockSpec((8, tc), lambda j, i: (0, j)) for _ in range(out_acc)]
    out_shape = ([jax.ShapeDtypeStruct((length, w), dt) for w, dt in out_tiled]
                 + [jax.ShapeDtypeStruct((8, width), F32) for _ in range(out_acc)])
    return pl.pallas_call(
        body, name=name, grid=(width // tc, length // tr), in_specs=in_specs, out_specs=out_specs,
        out_shape=out_shape, compiler_params=_cparams("parallel", "arbitrary"),
    )(*[t for t, _ in tiled], *rows)


def _rms(x, w):
    return x * lax.rsqrt(jnp.mean(x * x, axis=-1, keepdims=True) + NORM_EPS) * w


def _colsum(v):
    return jnp.sum(v, axis=0, keepdims=True)


def _rmsnorm_fwd(x, w, name):
    def fn(xb, wb):
        return (_rms(xb, wb),)
    return _tiles(name, fn, [x], [w.reshape(1, -1)], [(x.shape[1], MXU_DT)], 0, tr=256)[0]


def _rmsnorm_bwd(x, w, dh, dres, name):
    def fn(xb, db, rb, wb):
        _, vjp = jax.vjp(_rms, xb, wb)
        dx, dw = vjp(db)
        return dx + rb, dx + rb, dw
    dx, dxb, dw = _tiles(name, fn, [x, dh, dres], [w.reshape(1, -1)],
                         [(x.shape[1], F32), (x.shape[1], MXU_DT)], 1, tr=256)
    return dx, dxb, dw[0]


def _final_loss(x, w, target, name):
    def fn(xb, tb, wb):
        def f(xv, wv):
            e = _rms(xv, wv) - tb
            return 0.5 * jnp.sum(jnp.mean(e * e, axis=-1, keepdims=True), axis=0, keepdims=True)
        lv, vjp = jax.vjp(f, xb, wb)
        dx, dw = vjp(jnp.ones_like(lv))
        return dx, dx, dw, jnp.broadcast_to(lv, (1, xb.shape[1]))
    dx, dxb, dw, lv = _tiles(name, fn, [x, target], [w.reshape(1, -1)],
                             [(x.shape[1], F32), (x.shape[1], MXU_DT)], 2, tr=256)
    return lv[0, 0], dx, dxb, dw[0]


def _dg(a, b, mode):
    return lax.dot_general(a.astype(MXU_DT), b.astype(MXU_DT), (_MM_DIMS[mode], ((), ())),
                           preferred_element_type=F32)


@jax.custom_vjp
def _dot_nn(a, b):
    return _dg(a, b, "nn")


@jax.custom_vjp
def _dot_nt(a, b):
    return _dg(a, b, "nt")


@jax.custom_vjp
def _dot_tn(a, b):
    return _dg(a, b, "tn")


_dot_nn.defvjp(lambda a, b: (_dg(a, b, "nn"), (a, b)),
               lambda r, g: (_dg(g, r[1], "nt"), _dg(r[0], g, "tn")))
_dot_nt.defvjp(lambda a, b: (_dg(a, b, "nt"), (a, b)),
               lambda r, g: (_dg(g, r[1], "nn"), _dg(g, r[0], "tn")))
_dot_tn.defvjp(lambda a, b: (_dg(a, b, "tn"), (a, b)),
               lambda r, g: (_dg(r[1], g, "nt"), _dg(r[0], g, "nn")))


def _dot_exact(a, b):
    return jnp.dot(a, b, precision=lax.Precision.HIGHEST, preferred_element_type=F32)


S5_NS = S5_GROUPS * S5_STATE
S5_BS = S5_NS // S5_BLK
S5_BC = S5_WIDTH // S5_BLK


def _s5_disc(lr_raw, li, ls, br, bi):
    lr = jnp.minimum(lr_raw, S5_EIG_CLIP)
    step = jnp.exp(ls)
    mag = jnp.exp(lr * step)
    ab_re = mag * jnp.cos(li * step)
    ab_im = mag * jnp.sin(li * step)
    denom = lr * lr + li * li
    nr = ab_re - 1.0
    ni = ab_im
    coef_re = (nr * lr + ni * li) / denom
    coef_im = (ni * lr - nr * li) / denom
    return ab_re, ab_im, coef_re * br - coef_im * bi, coef_re * bi + coef_im * br


def _s5_prep(lr_raw, li, ls, br, bi):
    shapes = [jax.ShapeDtypeStruct((1, S5_NS), F32)] * 2 + [jax.ShapeDtypeStruct((S5_GROUP, S5_NS), F32)] * 2

    def body(a, b, c, d, e, o1, o2, o3, o4):
        for r, v in zip((o1, o2, o3, o4), _s5_disc(a[...], b[...], c[...], d[...], e[...])):
            r[...] = v

    return pl.pallas_call(body, name="s5_prep", out_shape=shapes)(lr_raw, li, ls, br, bi)


def _s5_prep_bwd(lr_raw, li, ls, br, bi, d_are, d_aim, d_bbre, d_bbim):
    shapes = [jax.ShapeDtypeStruct((1, S5_NS), F32)] * 3 + [jax.ShapeDtypeStruct((S5_GROUP, S5_NS), F32)] * 2

    def body(a, b, c, d, e, g1, g2, g3, g4, o1, o2, o3, o4, o5):
        _, vjp = jax.vjp(_s5_disc, a[...], b[...], c[...], d[...], e[...])
        for r, v in zip((o1, o2, o3, o4, o5), vjp((g1[...], g2[...], g3[...], g4[...]))):
            r[...] = v

    return pl.pallas_call(body, name="s5_prep_bwd", out_shape=shapes)(
        lr_raw, li, ls, br, bi, d_are, d_aim, d_bbre, d_bbim)


def _cmul(ar, ai, br, bi):
    return ar * br - ai * bi, ar * bi + ai * br


def _s5_powers(ar, ai, n):
    pr, pi_ = [ar], [ai]
    for _ in range(n - 1):
        r, i = _cmul(pr[-1], pi_[-1], pr[-1], pi_[-1])
        pr.append(r)
        pi_.append(i)
    return pr, pi_


S5_SUB = 8
S5_NPOW = 3


def _s5_in_groups(xr, xi, pr, pi_, reverse):
    t = xr.shape[0]
    sub = lax.broadcasted_iota(jnp.int32, xr.shape, 0) & (S5_SUB - 1)
    sr, si = xr, xi
    for k in range(S5_NPOW):
        d = 1 << k
        shift = (t - d) if reverse else d
        keep = (sub < S5_SUB - d) if reverse else (sub >= d)
        qr = jnp.where(keep, pltpu.roll(sr, shift, 0), 0.0)
        qi = jnp.where(keep, pltpu.roll(si, shift, 0), 0.0)
        mr, mi = _cmul(pr[k], pi_[k], qr, qi)
        sr, si = sr + mr, si + mi
    return sr, si


def _s5_scan_tile(xr, xi, apr, api, tabr, tabi, cr, ci, out_r, out_i, reverse):
    t = xr.shape[0]
    pr = [apr[k:k + 1, :] for k in range(S5_NPOW)]
    pi_ = [api[k:k + 1, :] for k in range(S5_NPOW)]
    sr, si = _s5_in_groups(xr, xi, pr, pi_, reverse)
    tr, ti = tabr[...], tabi[...]
    car_r, car_i = cr[0:1, :], ci[0:1, :]
    groups = range(t // S5_SUB)
    for g in (reversed(groups) if reverse else groups):
        rows = slice(g * S5_SUB, (g + 1) * S5_SUB)
        mr, mi = _cmul(tr, ti, car_r, car_i)
        br, bi = sr[rows] + mr, si[rows] + mi
        out_r[rows, :] = br
        out_i[rows, :] = bi
        edge = slice(0, 1) if reverse else slice(S5_SUB - 1, S5_SUB)
        car_r, car_i = br[edge], bi[edge]
    cr[0:1, :] = car_r
    ci[0:1, :] = car_i


def _s5_setup(ar, ai, reverse, apr, api, tabr, tabi):
    pr, pi_ = _s5_powers(ar, ai, S5_NPOW)
    for k in range(S5_NPOW):
        apr[k:k + 1, :] = pr[k]
        api[k:k + 1, :] = pi_[k]
    row = lax.broadcasted_iota(jnp.int32, (S5_SUB, ar.shape[1]), 0)
    first = (row == S5_SUB - 1) if reverse else (row == 0)
    sr, si = _s5_in_groups(jnp.where(first, ar, 0.0), jnp.where(first, ai, 0.0), pr, pi_, reverse)
    tabr[...] = sr
    tabi[...] = si


def _s5_fwd(p_s5, bb_re, bb_im, ct_re, ct_im, a_re, a_im, d_row, t_tile=256):
    length = p_s5.shape[0]
    t = min(t_tile, length)
    nt = length // t

    def body(u_ref, bbr, bbi, ctr, cti, ar_ref, ai_ref, d_ref, y_ref, sr_ref, si_ref,
             apr, api, tabr, tabi, cr, ci):
        i = pl.program_id(1)

        @pl.when(i == 0)
        def _():
            _s5_setup(ar_ref[...], ai_ref[...], False, apr, api, tabr, tabi)
            cr[...] = jnp.zeros_like(cr)
            ci[...] = jnp.zeros_like(ci)

        u = u_ref[...]
        _s5_scan_tile(_dg(u, bbr[...], "nn"), _dg(u, bbi[...], "nn"), apr, api, tabr, tabi, cr, ci,
                      sr_ref, si_ref, False)
        y_ref[...] = _dg(sr_ref[...], ctr[...], "nn") - _dg(si_ref[...], cti[...], "nn") + d_ref[...] * u

    blk3 = lambda a, b: pl.BlockSpec((None, a, b), lambda j, i: (j, 0, 0))
    return pl.pallas_call(
        body, name="s5_fwd", grid=(S5_BLK, nt),
        in_specs=[pl.BlockSpec((t, S5_BC), lambda j, i: (i, j)),
                  blk3(S5_BC, S5_BS), blk3(S5_BC, S5_BS), blk3(S5_BS, S5_BC), blk3(S5_BS, S5_BC),
                  blk3(1, S5_BS), blk3(1, S5_BS), blk3(1, S5_BC)],
        out_specs=[pl.BlockSpec((t, S5_BC), lambda j, i: (i, j)),
                   pl.BlockSpec((t, S5_BS), lambda j, i: (i, j)),
                   pl.BlockSpec((t, S5_BS), lambda j, i: (i, j))],
        out_shape=[jax.ShapeDtypeStruct((length, S5_WIDTH), F32),
                   jax.ShapeDtypeStruct((length, S5_NS), F32),
                   jax.ShapeDtypeStruct((length, S5_NS), F32)],
        scratch_shapes=[pltpu.VMEM((S5_SUB, S5_BS), F32)] * 6,
        compiler_params=_cparams("parallel", "arbitrary"),
    )(p_s5, bb_re, bb_im, ct_re, ct_im, a_re, a_im, d_row)


def _s5_bwd(dy, p_s5, s_re, s_im, bb_re, bb_im, ct_re, ct_im, a_re, a_im, d_row, t_tile=256):
    length = p_s5.shape[0]
    t = min(t_tile, length)
    nt = length // t

    def body(dy_ref, u_ref, sr_ref, si_ref, pr_ref, pi_ref, bbr, bbi, ctr, cti, ar_ref, ai_ref, d_ref,
             du_ref, dbbr, dbbi, dctr, dcti, dar, dai, dd_ref, apr, api, tabr, tabi, cr, ci, lam_r, lam_i):
        i = pl.program_id(1)

        @pl.when(i == 0)
        def _():
            _s5_setup(ar_ref[...], -ai_ref[...], True, apr, api, tabr, tabi)
            for r in (cr, ci, dbbr, dbbi, dctr, dcti, dar, dai, dd_ref):
                r[...] = jnp.zeros_like(r)

        dyv = dy_ref[...]
        u = u_ref[...]
        _s5_scan_tile(_dg(dyv, ctr[...], "nt"), -_dg(dyv, cti[...], "nt"), apr, api, tabr, tabi, cr, ci,
                      lam_r, lam_i, True)
        lr, li = lam_r[...], lam_i[...]
        du_ref[...] = (_dg(lr, bbr[...], "nt") + _dg(li, bbi[...], "nt") + d_ref[...] * dyv).astype(du_ref.dtype)
        dbbr[...] += _dg(u, lr, "tn")
        dbbi[...] += _dg(u, li, "tn")
        sr = sr_ref[...]
        si = si_ref[...]
        dctr[...] += _dg(sr, dyv, "tn")
        dcti[...] -= _dg(si, dyv, "tn")
        dd_ref[...] += jnp.broadcast_to(_colsum(dyv * u), dd_ref.shape)
        row = lax.broadcasted_iota(jnp.int32, sr.shape, 0)
        has_prev = (i < nt - 1).astype(F32)
        ssr = jnp.where(row == 0, pr_ref[7:8, :] * has_prev, pltpu.roll(sr, 1, 0))
        ssi = jnp.where(row == 0, pi_ref[7:8, :] * has_prev, pltpu.roll(si, 1, 0))
        dar[...] += jnp.broadcast_to(_colsum(lr * ssr + li * ssi), dar.shape)
        dai[...] += jnp.broadcast_to(_colsum(li * ssr - lr * ssi), dai.shape)

    rev = lambda j, i: (nt - 1 - i, j)
    prev = lambda j, i: (jnp.maximum((nt - 1 - i) * (t // 8) - 1, 0), j)
    blk3 = lambda a, b: pl.BlockSpec((None, a, b), lambda j, i: (j, 0, 0))
    return pl.pallas_call(
        body, name="s5_bwd", grid=(S5_BLK, nt),
        in_specs=[pl.BlockSpec((t, S5_BC), rev), pl.BlockSpec((t, S5_BC), rev),
                  pl.BlockSpec((t, S5_BS), rev), pl.BlockSpec((t, S5_BS), rev),
                  pl.BlockSpec((8, S5_BS), prev), pl.BlockSpec((8, S5_BS), prev),
                  blk3(S5_BC, S5_BS), blk3(S5_BC, S5_BS), blk3(S5_BS, S5_BC), blk3(S5_BS, S5_BC),
                  blk3(1, S5_BS), blk3(1, S5_BS), blk3(1, S5_BC)],
        out_specs=[pl.BlockSpec((t, S5_BC), rev),
                   blk3(S5_BC, S5_BS), blk3(S5_BC, S5_BS), blk3(S5_BS, S5_BC), blk3(S5_BS, S5_BC),
                   blk3(8, S5_BS), blk3(8, S5_BS), blk3(8, S5_BC)],
        out_shape=[jax.ShapeDtypeStruct((length, S5_WIDTH), MXU_DT),
                   jax.ShapeDtypeStruct((S5_BLK, S5_BC, S5_BS), F32), jax.ShapeDtypeStruct((S5_BLK, S5_BC, S5_BS), F32),
                   jax.ShapeDtypeStruct((S5_BLK, S5_BS, S5_BC), F32), jax.ShapeDtypeStruct((S5_BLK, S5_BS, S5_BC), F32),
                   jax.ShapeDtypeStruct((S5_BLK, 8, S5_BS), F32), jax.ShapeDtypeStruct((S5_BLK, 8, S5_BS), F32),
                   jax.ShapeDtypeStruct((S5_BLK, 8, S5_BC), F32)],
        scratch_shapes=[pltpu.VMEM((S5_SUB, S5_BS), F32)] * 6 + [pltpu.VMEM((t, S5_BS), F32)] * 2,
        compiler_params=_cparams("parallel", "arbitrary"),
    )(dy, p_s5, s_re, s_im, s_re, s_im, bb_re, bb_im, ct_re, ct_im, a_re, a_im, d_row)


def _blockdiag(m, rows, cols):
    m = m.reshape(S5_BLK, 8, rows, 1, cols)
    on_diag = jnp.eye(8, dtype=bool)[None, :, None, :, None]
    return jnp.where(on_diag, m, 0).reshape(S5_BLK, 8 * rows, 8 * cols)


def _blockdiag_t(m, rows, cols):
    m = m.reshape(S5_BLK, 8, rows, 8, cols)
    on_diag = jnp.eye(8, dtype=bool)[None, :, None, :, None]
    return jnp.sum(jnp.where(on_diag, m, 0), axis=3).reshape(S5_GROUPS, rows, cols)


def _gelu(y):
    return jax.nn.gelu(y)


def _s5_out_fn(y, zg, gate, b):
    return _gelu(y) * jax.nn.sigmoid(zg + b) * jax.nn.silu(gate)


HALO = 8


def _conv_fwd(xbc, w, b, tr=256, tc=1024):
    length, width = xbc.shape
    tr = min(tr, length)

    def body(x_ref, h_ref, w_ref, b_ref, cv_ref, act_ref):
        i = pl.program_id(1)
        x = x_ref[...]
        xx = jnp.concatenate([h_ref[...] * (i > 0).astype(F32), x], axis=0)
        acc = b_ref[...] + w_ref[3:4, :] * x
        for k in range(SSD_CONV - 1):
            acc = acc + w_ref[k:k + 1, :] * pltpu.roll(xx, SSD_CONV - 1 - k, 0)[HALO:, :]
        cv_ref[...] = acc
        act_ref[...] = jax.nn.silu(acc)

    main = pl.BlockSpec((tr, tc), lambda j, i: (i, j))
    before = pl.BlockSpec((HALO, tc), lambda j, i: (jnp.maximum(i * (tr // HALO) - 1, 0), j))
    return pl.pallas_call(
        body, name="ssd_conv_fwd", grid=(width // tc, length // tr),
        in_specs=[main, before, pl.BlockSpec((SSD_CONV, tc), lambda j, i: (0, j)),
                  pl.BlockSpec((1, tc), lambda j, i: (0, j))],
        out_specs=[main, main],
        out_shape=[jax.ShapeDtypeStruct((length, width), F32)] * 2,
        compiler_params=_cparams("parallel", "arbitrary"),
    )(xbc, xbc, w, b.reshape(1, -1))


def _conv_bwd(dact, cv, xbc, w, tr=256, tc=1024):
    length, width = xbc.shape
    tr = min(tr, length)
    nr = length // tr
    n = tr + HALO

    def dsilu(d, c):
        sg = jax.nn.sigmoid(c)
        return d * (sg * (1.0 + c * (1.0 - sg)))

    def body(da_ref, dan_ref, cv_ref, cvn_ref, x_ref, xp_ref, w_ref, dx_ref, dw_ref, db_ref):
        i = pl.program_id(1)

        @pl.when(i == 0)
        def _():
            dw_ref[...] = jnp.zeros_like(dw_ref)
            db_ref[...] = jnp.zeros_like(db_ref)

        dc = dsilu(da_ref[...], cv_ref[...])
        dcn = dsilu(dan_ref[...], cvn_ref[...]) * (i < nr - 1).astype(F32)
        dd = jnp.concatenate([dc, dcn], axis=0)
        x = x_ref[...]
        xx = jnp.concatenate([xp_ref[...] * (i > 0).astype(F32), x], axis=0)
        dx = w_ref[3:4, :] * dc
        dw_ref[3:4, :] += _colsum(dc * x)
        for k in range(SSD_CONV - 1):
            j = SSD_CONV - 1 - k
            dx = dx + w_ref[k:k + 1, :] * pltpu.roll(dd, n - j, 0)[:tr, :]
            dw_ref[k:k + 1, :] += _colsum(dc * pltpu.roll(xx, j, 0)[HALO:, :])
        dx_ref[...] = dx.astype(dx_ref.dtype)
        db_ref[...] += jnp.broadcast_to(_colsum(dc), db_ref.shape)

    main = pl.BlockSpec((tr, tc), lambda j, i: (i, j))
    before = pl.BlockSpec((HALO, tc), lambda j, i: (jnp.maximum(i * (tr // HALO) - 1, 0), j))
    after = pl.BlockSpec((HALO, tc), lambda j, i: (jnp.minimum((i + 1) * (tr // HALO), length // HALO - 1), j))
    acc = pl.BlockSpec((8, tc), lambda j, i: (0, j))
    return pl.pallas_call(
        body, name="ssd_conv_bwd", grid=(width // tc, nr),
        in_specs=[main, after, main, after, main, before, pl.BlockSpec((SSD_CONV, tc), lambda j, i: (0, j))],
        out_specs=[main, acc, acc],
        out_shape=[jax.ShapeDtypeStruct((length, width), MXU_DT),
                   jax.ShapeDtypeStruct((8, width), F32), jax.ShapeDtypeStruct((8, width), F32)],
        compiler_params=_cparams("parallel", "arbitrary"),
    )(dact, dact, cv, cv, xbc, xbc, w)


def _tri(lower):
    r = lax.broadcasted_iota(jnp.int32, (SSD_CHUNK, SSD_CHUNK), 0)
    c = lax.broadcasted_iota(jnp.int32, (SSD_CHUNK, SSD_CHUNK), 1)
    return ((r >= c) if lower else (r <= c)).astype(F32)


def _dt_fwd(raw, bias, a_log):
    length = raw.shape[0]
    nc = length // SSD_CHUNK

    def body(r_ref, b_ref, a_ref, dt_ref, cum_ref, cumt_ref):
        dt = jax.nn.softplus(r_ref[...] + b_ref[...])
        cum = _dot_exact(_tri(True), dt * (-jnp.exp(a_ref[...])))
        dt_ref[...] = dt
        cum_ref[...] = cum
        cumt_ref[...] = cum.T

    blk = pl.BlockSpec((SSD_CHUNK, LANES), lambda c: (c, 0))
    row = pl.BlockSpec((1, LANES), lambda c: (0, 0))
    return pl.pallas_call(
        body, name="ssd_dt_fwd", grid=(nc,), in_specs=[blk, row, row],
        out_specs=[blk, blk, pl.BlockSpec((None, LANES, SSD_CHUNK), lambda c: (c, 0, 0))],
        out_shape=[jax.ShapeDtypeStruct((length, LANES), F32)] * 2
        + [jax.ShapeDtypeStruct((nc, LANES, SSD_CHUNK), F32)],
        compiler_params=_cparams("parallel"),
    )(raw, bias, a_log)


def _dt_bwd(raw, bias, a_log, ddt, dcum):
    length = raw.shape[0]
    nc = length // SSD_CHUNK

    def body(r_ref, b_ref, a_ref, ddt_ref, dcum_ref, dr_ref, db_ref, da_ref):
        @pl.when(pl.program_id(0) == 0)
        def _():
            db_ref[...] = jnp.zeros_like(db_ref)
            da_ref[...] = jnp.zeros_like(da_ref)

        z = r_ref[...] + b_ref[...]
        a = -jnp.exp(a_ref[...])
        dla = _dot_exact(_tri(False), dcum_ref[...])
        draw = (ddt_ref[...] + dla * a) * jax.nn.sigmoid(z)
        dr_ref[...] = draw.astype(dr_ref.dtype)
        db_ref[...] += jnp.broadcast_to(_colsum(draw), db_ref.shape)
        da_ref[...] += jnp.broadcast_to(_colsum(dla * jax.nn.softplus(z)) * a, da_ref.shape)

    blk = pl.BlockSpec((SSD_CHUNK, LANES), lambda c: (c, 0))
    row = pl.BlockSpec((1, LANES), lambda c: (0, 0))
    acc = pl.BlockSpec((8, LANES), lambda c: (0, 0))
    return pl.pallas_call(
        body, name="ssd_dt_bwd", grid=(nc,), in_specs=[blk, row, row, blk, blk],
        out_specs=[blk, acc, acc],
        out_shape=[jax.ShapeDtypeStruct((length, LANES), MXU_DT),
                   jax.ShapeDtypeStruct((8, LANES), F32), jax.ShapeDtypeStruct((8, LANES), F32)],
        compiler_params=_cparams("arbitrary"),
    )(raw, bias, a_log, ddt, dcum)


SSD_GW = SSD_HPG * SSD_HEAD_DIM
SSD_B_OFF = SSD_WIDTH // SSD_STATE
SSD_C_OFF = SSD_B_OFF + SSD_GROUPS


def _ssd_expand():
    r = lax.broadcasted_iota(jnp.int32, (LANES, SSD_GW), 0)
    c = lax.broadcasted_iota(jnp.int32, (LANES, SSD_GW), 1)
    return (c // SSD_HEAD_DIM == r).astype(F32)


def _ssd_to_channels(v16, e):
    return _dot_exact(v16, e)


def _ssd_to_heads(v, e):
    return lax.dot_general(v, e, (((1,), (1,)), ((), ())), precision=lax.Precision.HIGHEST,
                           preferred_element_type=F32)


def _ssd_decay(cum_ref, cumt_ref, r):
    q = lax.broadcasted_iota(jnp.int32, (SSD_CHUNK, SSD_CHUNK), 0)
    k = lax.broadcasted_iota(jnp.int32, (SSD_CHUNK, SSD_CHUNK), 1)
    return jnp.exp(jnp.where(q >= k, cum_ref[:, r:r + 1] - cumt_ref[r:r + 1, :], -1e30))


def _ssd_core_specs(nc, rev):
    ch = (lambda c: nc - 1 - c) if rev else (lambda c: c)
    xs = pl.BlockSpec((SSD_CHUNK, SSD_GW), lambda g, c: (ch(c), g))
    bspec = pl.BlockSpec((SSD_CHUNK, SSD_STATE), lambda g, c: (ch(c), SSD_B_OFF + g))
    cspec = pl.BlockSpec((SSD_CHUNK, SSD_STATE), lambda g, c: (ch(c), SSD_C_OFF + g))
    lane = pl.BlockSpec((None, SSD_CHUNK, LANES), lambda g, c: (g, ch(c), 0))
    rows = pl.BlockSpec((None, None, 16, SSD_CHUNK), lambda g, c: (g, ch(c), 0, 0))
    st = pl.BlockSpec((None, None, SSD_STATE, SSD_GW), lambda g, c: (g, ch(c), 0, 0))
    return xs, bspec, cspec, lane, rows, st


def _ssd_core_fwd(act, dtg, cumg, cumtg):
    length = act.shape[0]
    nc = length // SSD_CHUNK

    def body(x_ref, b_ref, c_ref, dt_ref, cum_ref, cumt_ref, y_ref, st_ref, s_scr):
        @pl.when(pl.program_id(1) == 0)
        def _():
            s_scr[...] = jnp.zeros_like(s_scr)

        e = _ssd_expand()
        bm = b_ref[...]
        cm = c_ref[...]
        cum_e = _ssd_to_channels(cum_ref[...], e)
        cl_e = cum_e[SSD_CHUNK - 1:SSD_CHUNK, :]
        xdt = x_ref[...] * _ssd_to_channels(dt_ref[...], e)
        st = s_scr[...]
        st_ref[...] = st
        g = _dg(cm, bm, "nt")
        y_off = _dg(cm, st, "nn") * jnp.exp(cum_e)
        for r in range(SSD_HPG):
            cols = slice(r * SSD_HEAD_DIM, (r + 1) * SSD_HEAD_DIM)
            y_ref[:, cols] = _dg(g * _ssd_decay(cum_ref, cumt_ref, r), xdt[:, cols], "nn") + y_off[:, cols]
        s_scr[...] = jnp.exp(cl_e) * st + _dg(bm, xdt * jnp.exp(cl_e - cum_e), "tn")

    xs, bspec, cspec, lane, rows, st = _ssd_core_specs(nc, False)
    return pl.pallas_call(
        body, name="ssd_core_fwd", grid=(SSD_GROUPS, nc),
        in_specs=[xs, bspec, cspec, lane, lane, rows], out_specs=[xs, st],
        out_shape=[jax.ShapeDtypeStruct((length, SSD_WIDTH), F32),
                   jax.ShapeDtypeStruct((SSD_GROUPS, nc, SSD_STATE, SSD_GW), F32)],
        scratch_shapes=[pltpu.VMEM((SSD_STATE, SSD_GW), F32)],
        compiler_params=_cparams("parallel", "arbitrary"),
    )(act, act, act, dtg, cumg, cumtg)


def _ssd_core_bwd(dy, dxs_add, act, dtg, cumg, cumtg, states):
    length = act.shape[0]
    nc = length // SSD_CHUNK

    def body(dy_ref, add_ref, x_ref, b_ref, c_ref, dt_ref, cum_ref, cumt_ref, st_ref,
             dx_ref, db_ref, dc_ref, ddt_ref, dcum_ref, dcumt_ref, dcl_ref, ds_scr, dxdt_scr):
        @pl.when(pl.program_id(1) == 0)
        def _():
            ds_scr[...] = jnp.zeros_like(ds_scr)

        e = _ssd_expand()
        bm = b_ref[...]
        cm = c_ref[...]
        x = x_ref[...]
        dyv = dy_ref[...]
        st = st_ref[...]
        dst_new = ds_scr[...]
        dt_e = _ssd_to_channels(dt_ref[...], e)
        cum_e = _ssd_to_channels(cum_ref[...], e)
        cl_e = cum_e[SSD_CHUNK - 1:SSD_CHUNK, :]
        xdt = x * dt_e
        exp_cum = jnp.exp(cum_e)
        exp_cl = jnp.exp(cl_e)
        z = xdt * jnp.exp(cl_e - cum_e)
        d_cs = dyv * exp_cum
        dcum_e = d_cs * _dg(cm, st, "nn")
        dcm = _dg(d_cs, st, "nt")
        ds_scr[...] = _dg(cm, d_cs, "tn") + exp_cl * dst_new
        dcl_e = _colsum(dst_new * st) * exp_cl
        dbm = _dg(z, dst_new, "nt")
        dz = _dg(bm, dst_new, "nn")
        de = dz * z
        dcl_e = dcl_e + _colsum(de)
        dcum_e = dcum_e - de
        dxdt_scr[...] = dz * jnp.exp(cl_e - cum_e)
        g = _dg(cm, bm, "nt")
        dg = jnp.zeros_like(g)
        dcum_ref[...] = _ssd_to_heads(dcum_e, e)
        dcumt_ref[...] = jnp.zeros_like(dcumt_ref)
        for r in range(SSD_HPG):
            cols = slice(r * SSD_HEAD_DIM, (r + 1) * SSD_HEAD_DIM)
            decay = _ssd_decay(cum_ref, cumt_ref, r)
            w = g * decay
            dy_r = dyv[:, cols]
            dw = _dg(dy_r, xdt[:, cols], "nt")
            dxdt_scr[:, cols] += _dg(w, dy_r, "tn")
            dg = dg + dw * decay
            dseg = dw * w
            dcum_ref[:, r:r + 1] += jnp.sum(dseg, axis=1, keepdims=True)
            dcumt_ref[r:r + 1, :] = -_colsum(dseg)
        dc_ref[...] = dcm + _dg(dg, bm, "nn")
        db_ref[...] = dbm + _dg(dg, cm, "tn")
        dxdt = dxdt_scr[...]
        dx_ref[...] = dxdt * dt_e + add_ref[...]
        ddt_ref[...] = _ssd_to_heads(dxdt * x, e)
        dcl_ref[...] = _ssd_to_heads(jnp.broadcast_to(dcl_e, (8, SSD_GW)), e)

    xs, bspec, cspec, lane, rows, st = _ssd_core_specs(nc, True)
    bc_out = pl.BlockSpec((SSD_CHUNK, SSD_STATE), lambda g, c: (nc - 1 - c, g))
    last = pl.BlockSpec((None, None, 8, LANES), lambda g, c: (g, nc - 1 - c, 0, 0))
    return pl.pallas_call(
        body, name="ssd_core_bwd", grid=(SSD_GROUPS, nc),
        in_specs=[xs, xs, xs, bspec, cspec, lane, lane, rows, st],
        out_specs=[xs, bc_out, bc_out, lane, lane, rows, last],
        out_shape=[jax.ShapeDtypeStruct((length, SSD_WIDTH), F32),
                   jax.ShapeDtypeStruct((length, SSD_GROUPS * SSD_STATE), F32),
                   jax.ShapeDtypeStruct((length, SSD_GROUPS * SSD_STATE), F32),
                   jax.ShapeDtypeStruct((SSD_GROUPS, length, LANES), F32),
                   jax.ShapeDtypeStruct((SSD_GROUPS, length, LANES), F32),
                   jax.ShapeDtypeStruct((SSD_GROUPS, nc, 16, SSD_CHUNK), F32),
                   jax.ShapeDtypeStruct((SSD_GROUPS, nc, 8, LANES), F32)],
        scratch_shapes=[pltpu.VMEM((SSD_STATE, SSD_GW), F32), pltpu.VMEM((SSD_CHUNK, SSD_GW), F32)],
        compiler_params=_cparams("parallel", "arbitrary"),
    )(dy, dxs_add, act, act, act, dtg, cumg, cumtg, states)


def _ssd_post_fn(yc, xs, z, dch, nw):
    y = (yc + dch * xs) * jax.nn.silu(z)
    return y * lax.rsqrt(jnp.mean(y * y, axis=-1, keepdims=True) + NORM_EPS) * nw


FOX_SCALE = 1.0 / math.sqrt(FOX_HEAD_DIM)
MASKED = -1e30


def _fgate_fwd(f_raw, b_f):
    length = f_raw.shape[0]
    nb = length // SSD_CHUNK

    def body(f_ref, b_ref, c_ref, carry):
        @pl.when(pl.program_id(0) == 0)
        def _():
            carry[...] = jnp.zeros_like(carry)

        c = _dot_exact(_tri(True), jax.nn.log_sigmoid(f_ref[...] + b_ref[...])) + carry[0:1, :]
        c_ref[...] = c
        carry[0:1, :] = c[SSD_CHUNK - 1:SSD_CHUNK, :]

    blk = pl.BlockSpec((SSD_CHUNK, LANES), lambda i: (i, 0))
    return pl.pallas_call(
        body, name="fox_fgate_fwd", grid=(nb,), in_specs=[blk, pl.BlockSpec((1, LANES), lambda i: (0, 0))],
        out_specs=blk, out_shape=jax.ShapeDtypeStruct((length, LANES), F32),
        scratch_shapes=[pltpu.VMEM((8, LANES), F32)], compiler_params=_cparams("arbitrary"),
    )(f_raw, b_f)


def _fgate_bwd(f_raw, b_f, dc):
    length = f_raw.shape[0]
    nb = length // SSD_CHUNK

    def body(f_ref, b_ref, dc_ref, df_ref, db_ref, carry):
        @pl.when(pl.program_id(0) == 0)
        def _():
            carry[...] = jnp.zeros_like(carry)
            db_ref[...] = jnp.zeros_like(db_ref)

        dcv = dc_ref[...]
        dlog = _dot_exact(_tri(False), dcv) + carry[0:1, :]
        carry[0:1, :] += _colsum(dcv)
        df = dlog * jax.nn.sigmoid(-(f_ref[...] + b_ref[...]))
        df_ref[...] = df.astype(df_ref.dtype)
        db_ref[...] += jnp.broadcast_to(_colsum(df), db_ref.shape)

    blk = pl.BlockSpec((SSD_CHUNK, LANES), lambda i: (nb - 1 - i, 0))
    return pl.pallas_call(
        body, name="fox_fgate_bwd", grid=(nb,),
        in_specs=[blk, pl.BlockSpec((1, LANES), lambda i: (0, 0)), blk],
        out_specs=[blk, pl.BlockSpec((8, LANES), lambda i: (0, 0))],
        out_shape=[jax.ShapeDtypeStruct((length, LANES), MXU_DT), jax.ShapeDtypeStruct((8, LANES), F32)],
        scratch_shapes=[pltpu.VMEM((8, LANES), F32)], compiler_params=_cparams("arbitrary"),
    )(f_raw, b_f, dc)


def _fox_scores(q, k, bias, diagonal):
    s = _dg(q, k, "nt") * FOX_SCALE + bias
    if diagonal:
        row = lax.broadcasted_iota(jnp.int32, s.shape, 0)
        col = lax.broadcasted_iota(jnp.int32, s.shape, 1)
        s = jnp.where(col <= row, s, MASKED)
    return s


FOX_TILE = 512


def _fox_c0(c, t):
    return jnp.repeat(jnp.repeat(c[::t, :FOX_HEADS], FOX_HEAD_DIM, axis=1), 8, axis=0)


def _fox_fwd(qkv, c0_rep, c_t, tile=FOX_TILE):
    length = qkv.shape[0]
    t = min(tile, length)
    nq = length // t

    def body(q_ref, k_ref, v_ref, c0_ref, ct_ref, o_ref, lse_ref):
        i = pl.program_id(1)
        q = q_ref[...]
        c0 = c0_ref[0:1, 0:1]

        def tile_step(k0, carry, diagonal):
            m, l, acc = carry
            s = _fox_scores(q, k_ref[pl.ds(k0, t), :], c0 - ct_ref[:, pl.ds(k0, t)], diagonal)
            m_new = jnp.maximum(m, jnp.max(s, axis=1, keepdims=True))
            p = jnp.exp(s - m_new)
            alpha = jnp.exp(m - m_new)
            return (m_new, alpha * l + jnp.sum(p, axis=1, keepdims=True),
                    alpha * acc + _dg(p, v_ref[pl.ds(k0, t), :], "nn"))

        init = (jnp.full((t, 1), MASKED, F32), jnp.zeros((t, 1), F32), jnp.zeros((t, FOX_HEAD_DIM), F32))
        carry = lax.fori_loop(0, i, lambda j, c: tile_step(pl.multiple_of(j * t, t), c, False), init)
        m, l, acc = tile_step(pl.multiple_of(i * t, t), carry, True)
        o_ref[...] = acc / l
        lse_ref[...] = jnp.broadcast_to(m + jnp.log(l), lse_ref.shape)

    qt = pl.BlockSpec((t, FOX_HEAD_DIM), lambda h, i: (i, h))
    return pl.pallas_call(
        body, name="fox_attn_fwd", grid=(FOX_HEADS, nq),
        in_specs=[qt,
                  pl.BlockSpec((length, FOX_HEAD_DIM), lambda h, i: (0, FOX_HEADS + h)),
                  pl.BlockSpec((length, FOX_HEAD_DIM), lambda h, i: (0, 2 * FOX_HEADS + h)),
                  pl.BlockSpec((8, FOX_HEAD_DIM), lambda h, i: (i, h)),
                  pl.BlockSpec((None, 1, length), lambda h, i: (h, 0, 0))],
        out_specs=[qt, qt],
        out_shape=[jax.ShapeDtypeStruct((length, FOX_WIDTH), F32)] * 2,
        compiler_params=_cparams("parallel", "arbitrary"),
    )(qkv, qkv, qkv, c0_rep, c_t)


def _fox_bwd(qkv, d_att, lse, delta, c0_rep, c_t, tile=FOX_TILE):
    length = qkv.shape[0]
    t = min(tile, length)
    nk = length // t

    def body(q_ref, k_ref, v_ref, do_ref, lse_ref, dl_ref, c0_ref, ct_ref,
             dq_ref, dk_ref, dv_ref, dcq_ref, dck_ref, dq_acc):
        j = pl.program_id(1)

        @pl.when(j == 0)
        def _():
            dq_acc[...] = jnp.zeros_like(dq_acc)
            dcq_ref[...] = jnp.zeros_like(dcq_ref)

        k = k_ref[...]
        v = v_ref[...]
        ck = ct_ref[...]

        def tile_step(i, carry, diagonal):
            dk, dv, dck = carry
            rows = pl.ds(pl.multiple_of(i * t, t), t)
            q = q_ref[rows, :]
            do = do_ref[rows, :]
            c0 = c0_ref[pl.ds(pl.multiple_of(i * 8, 8), 8), :][0:1, 0:1]
            s = _fox_scores(q, k, c0 - ck, diagonal)
            p = jnp.exp(s - lse_ref[rows, 0:1])
            dv = dv + _dg(p, do, "tn")
            ds = p * (_dg(do, v, "nt") - dl_ref[rows, 0:1])
            dk = dk + _dg(ds, q, "tn") * FOX_SCALE
            dq_acc[rows, :] += _dg(ds, k, "nn") * FOX_SCALE
            dcq_ref[rows, :] += jnp.broadcast_to(jnp.sum(ds, axis=1, keepdims=True), (t, FOX_HEAD_DIM))
            return dk, dv, dck + _colsum(ds)

        init = (jnp.zeros((t, FOX_HEAD_DIM), F32), jnp.zeros((t, FOX_HEAD_DIM), F32), jnp.zeros((1, t), F32))
        carry = tile_step(j, init, True)
        dk, dv, dck = lax.fori_loop(j + 1, nk, lambda i, c: tile_step(i, c, False), carry)
        dk_ref[...] = dk.astype(dk_ref.dtype)
        dv_ref[...] = dv.astype(dv_ref.dtype)
        dck_ref[...] = -dck

        @pl.when(j == nk - 1)
        def _():
            dq_ref[...] = dq_acc[...].astype(dq_ref.dtype)

    full = lambda off: pl.BlockSpec((length, FOX_HEAD_DIM), lambda h, j: (0, off + h))
    kt = lambda off: pl.BlockSpec((t, FOX_HEAD_DIM), lambda h, j: (j, off + h))
    ck_spec = pl.BlockSpec((None, 1, t), lambda h, j: (h, 0, j))
    return pl.pallas_call(
        body, name="fox_attn_bwd", grid=(FOX_HEADS, nk),
        in_specs=[full(0), kt(FOX_HEADS), kt(2 * FOX_HEADS), full(0), full(0), full(0),
                  pl.BlockSpec((8 * nk, FOX_HEAD_DIM), lambda h, j: (0, h)), ck_spec],
        out_specs=[full(0), kt(0), kt(0), full(0), ck_spec],
        out_shape=[jax.ShapeDtypeStruct((length, FOX_WIDTH), MXU_DT)] * 3
        + [jax.ShapeDtypeStruct((length, FOX_WIDTH), F32), jax.ShapeDtypeStruct((FOX_HEADS, 1, length), F32)],
        scratch_shapes=[pltpu.VMEM((length, FOX_HEAD_DIM), F32)],
        compiler_params=_cparams("parallel", "arbitrary"),
    )(qkv, qkv, qkv, d_att, lse, delta, c0_rep, c_t)


def _fox_gate_fn(att, gate):
    return att * jax.nn.silu(gate)


N_CHIP = 4


def _other_chips(mx, my):
    return [(1 - mx, my), (mx, 1 - my), (1 - mx, 1 - my)]


def _handshake(peers):
    barrier = pltpu.get_barrier_semaphore()
    for peer in peers:
        pl.semaphore_signal(barrier, inc=1, device_id=peer, device_id_type=pl.DeviceIdType.MESH)
    pl.semaphore_wait(barrier, len(peers))


def _exchange_call(body, x, out_struct, name, n_sems, local_sem, collective_id):
    sems = [pltpu.SemaphoreType.DMA((n_sems,)), pltpu.SemaphoreType.DMA((n_sems,))]
    sems += [pltpu.SemaphoreType.DMA] if local_sem else []
    if collective_id is None:
        return pl.pallas_call(
            body, name=name, in_specs=[ANYSPACE], out_specs=ANYSPACE, out_shape=out_struct, scratch_shapes=sems,
            compiler_params=pltpu.CompilerParams(has_side_effects=True),
        )(x)
    x_ref = jax.new_ref(x, memory_space=pltpu.MemorySpace.HBM)
    o_ref = jax.empty_ref(out_struct, memory_space=pltpu.MemorySpace.HBM)

    @pl.kernel(mesh=plsc.ScalarSubcoreMesh(axis_name="sequencer", num_cores=1), name=name,
               scratch_types=tuple(sems), compiler_params=pltpu.CompilerParams(collective_id=collective_id))
    def launch(*sem_refs):
        body(x_ref, o_ref, *sem_refs)

    launch()
    return o_ref[...]


def _gather(x, name, collective_id=None):
    def body(x_ref, o_ref, send_sems, recv_sems, local_sem):
        mx, my, mc = lax.axis_index("x"), lax.axis_index("y"), lax.axis_index("c")
        me, sibling = (mx, my, mc), (mx, my, 1 - mc)
        chips = _other_chips(mx, my)
        if collective_id is not None:
            _handshake([sibling] + [(*chip, mc) for chip in chips])

        def slot(px, py, pc):
            return o_ref.at[4 * px + 2 * py + pc]

        def copy(k, block, to, src=None):
            return pltpu.make_async_remote_copy(
                src_ref=slot(*block) if src is None else src, dst_ref=slot(*block),
                send_sem=send_sems.at[k], recv_sem=recv_sems.at[k],
                device_id=to, device_id_type=pl.DeviceIdType.MESH)

        mine = pltpu.make_async_copy(x_ref, slot(*me), local_sem)
        mine.start()
        first = [copy(0, me, sibling, src=x_ref)]
        first += [copy(1 + j, me, (*chip, mc), src=x_ref) for j, chip in enumerate(chips)]
        for cp in first:
            cp.start()
        passed = [copy(4 + j, (*chip, mc), sibling) for j, chip in enumerate(chips)]
        for j, chip in enumerate(chips):
            copy(1 + j, (*chip, mc), me).wait_recv()
            passed[j].start()
        copy(0, sibling, me).wait_recv()
        for j, chip in enumerate(chips):
            copy(4 + j, (*chip, 1 - mc), me).wait_recv()
        for cp in first + passed:
            cp.wait_send()
        mine.wait()

    return _exchange_call(body, x, jax.ShapeDtypeStruct((N_DEV,) + x.shape, x.dtype), name, N_DEV - 1, True,
                          collective_id)


def _pair_send(parts, name, collective_id=None):
    def body(p_ref, o_ref, send_sems, recv_sems):
        mx, my, mc = lax.axis_index("x"), lax.axis_index("y"), lax.axis_index("c")
        if collective_id is not None:
            _handshake([(mx, my, 1 - mc)])
        copies = [pltpu.make_async_remote_copy(
            src_ref=p_ref.at[2 * chip + (1 - mc)], dst_ref=o_ref.at[chip],
            send_sem=send_sems.at[chip], recv_sem=recv_sems.at[chip],
            device_id=(mx, my, 1 - mc), device_id_type=pl.DeviceIdType.MESH) for chip in range(N_CHIP)]
        for cp in copies:
            cp.start()
        for cp in copies:
            cp.wait_recv()
        for cp in copies:
            cp.wait_send()

    return _exchange_call(body, parts, jax.ShapeDtypeStruct((N_CHIP,) + parts.shape[1:], parts.dtype), name,
                          N_CHIP, False, collective_id)


def _pair_sum(parts, recv, out_dtype, name):
    _, rows, cols = parts.shape
    tr, tc = _tile2d(rows, cols)

    def body(c_ref, p_ref, r_ref, o_ref):
        o_ref[...] = (p_ref[...] + r_ref[...]).astype(o_ref.dtype)

    return pl.pallas_call(
        body, name=name,
        grid_spec=pltpu.PrefetchScalarGridSpec(
            num_scalar_prefetch=1, grid=(N_CHIP, rows // tr, cols // tc),
            in_specs=[pl.BlockSpec((None, tr, tc), lambda k, i, j, c: (2 * k + c[0], i, j)),
                      pl.BlockSpec((None, tr, tc), lambda k, i, j, c: (k, i, j))],
            out_specs=pl.BlockSpec((None, tr, tc), lambda k, i, j, c: (k, i, j))),
        out_shape=jax.ShapeDtypeStruct((N_CHIP, rows, cols), out_dtype),
        compiler_params=_cparams("parallel", "parallel", "parallel"),
    )(lax.axis_index("c").astype(jnp.int32).reshape(1), parts, recv)


def _chip_exchange(sums, name, collective_id=None):
    def body(s_ref, o_ref, send_sems, recv_sems, local_sem):
        mx, my, mc = lax.axis_index("x"), lax.axis_index("y"), lax.axis_index("c")
        my_chip = 2 * mx + my
        if collective_id is not None:
            _handshake([(px, py, mc) for px, py in _other_chips(mx, my)])
        local = pltpu.make_async_copy(s_ref.at[my_chip], o_ref.at[my_chip], local_sem)
        local.start()
        sends, recvs = [], []
        for k, (px, py) in enumerate(_other_chips(mx, my)):
            peer = 2 * px + py

            def copy(src_slot, dst_slot, k=k, dev=(px, py, mc)):
                return pltpu.make_async_remote_copy(
                    src_ref=s_ref.at[src_slot], dst_ref=o_ref.at[dst_slot], send_sem=send_sems.at[k],
                    recv_sem=recv_sems.at[k], device_id=dev, device_id_type=pl.DeviceIdType.MESH)

            sends.append(copy(peer, my_chip))
            recvs.append(copy(peer, peer))
        for cp in sends:
            cp.start()
        for cp in recvs:
            cp.wait_recv()
        for cp in sends:
            cp.wait_send()
        local.wait()

    return _exchange_call(body, sums, jax.ShapeDtypeStruct(sums.shape, sums.dtype), name, N_CHIP - 1, True,
                          collective_id)


ADAM_TILE_ELEMS = 128 * 1024


def _tile2d(rows, cols):
    if rows * cols <= ADAM_TILE_ELEMS:
        return rows, cols
    if rows % 8 == 0:
        return _pick(rows, max(8, ADAM_TILE_ELEMS // cols), 8), cols
    return rows, _pick(cols, max(LANES, ADAM_TILE_ELEMS // rows))


def _adamw(w, parts, m, v, name):
    rows, cols = w.shape
    n_parts = parts.shape[0]
    tr, tc = _tile2d(rows, cols)

    def body(w_ref, p_ref, m_ref, v_ref, g_ref, d_ref, nm_ref, nv_ref):
        g = p_ref[0].astype(F32)
        for p in range(1, n_parts):
            g = g + p_ref[p].astype(F32)
        mm = ADAM_B1 * m_ref[...] + (1.0 - ADAM_B1) * g
        vv = ADAM_B2 * v_ref[...] + (1.0 - ADAM_B2) * jnp.square(g)
        m_hat = mm / (1.0 - ADAM_B1 ** ADAM_STEP)
        v_hat = vv / (1.0 - ADAM_B2 ** ADAM_STEP)
        g_ref[...] = g
        d_ref[...] = -ADAM_LR * (m_hat / (jnp.sqrt(v_hat) + ADAM_EPS) + ADAM_WD * w_ref[...])
        nm_ref[...] = mm
        nv_ref[...] = vv

    blk = pl.BlockSpec((tr, tc), lambda i, j: (i, j))
    return pl.pallas_call(
        body, name=name, grid=(rows // tr, cols // tc),
        in_specs=[blk, pl.BlockSpec((n_parts, tr, tc), lambda i, j: (0, i, j)), blk, blk],
        out_specs=[blk] * 4, out_shape=[jax.ShapeDtypeStruct((rows, cols), F32)] * 4,
        compiler_params=_cparams("parallel", "parallel"),
    )(w, parts, m, v)


WEIGHTS = ("l0_norm_w", "l0_w_in", "l0_s5_lambda_re", "l0_s5_lambda_im", "l0_s5_log_step", "l0_s5_b_re",
           "l0_s5_b_im", "l0_s5_c_re", "l0_s5_c_im", "l0_s5_d", "l0_s5_w_glu", "l0_s5_b_glu", "l0_ssd_conv_w",
           "l0_ssd_conv_b", "l0_ssd_dt_bias", "l0_ssd_a_log", "l0_ssd_d", "l0_ssd_norm_w", "l0_w_out",
           "l1_norm_w", "l1_w_in", "l1_fox_b_f", "l1_w_out", "final_norm_w")
SHARDED = ("l0_w_in", "l0_s5_w_glu", "l0_ssd_conv_w", "l0_w_out", "l1_w_in", "l1_w_out")


def _pad_lanes(a, width=LANES):
    return jnp.pad(a, [(0, 0)] * (a.ndim - 1) + [(0, width - a.shape[-1])])


def _pad_rows(a, height=LANES):
    return jnp.pad(a, [(0, height - a.shape[0])] + [(0, 0)] * (a.ndim - 1))


def _to_groups(a):
    length = a.shape[0]
    return _pad_lanes(a[:, :SSD_HEADS].reshape(length, SSD_GROUPS, SSD_HPG).transpose(1, 0, 2))


def _from_groups(a):
    length = a.shape[1]
    return a[:, :, :SSD_HPG].transpose(1, 0, 2).reshape(length, SSD_HEADS)


class _Grads(dict):
    def __init__(self, on_grad):
        super().__init__()
        self.on_grad = on_grad
        self.raw = {}
        self.sums = {}

    def __setitem__(self, name, value):
        self.store(name, value)

    def store(self, name, value, between=None):
        if self.on_grad is not None and name in SHARDED:
            self.raw[name] = value
            self.sums[name], value = self.on_grad(name, value, between)
        elif between is not None:
            between()
        super().__setitem__(name, value)

    def before(self, x, stage, name):
        if self.on_grad is None:
            return x
        made = self if stage == "done" else getattr(self, stage)
        return lax.optimization_barrier((x, made[name]))[0]


def _local_step(x, target, w, on_grad=None):
    length = x.shape[0]
    nc = length // SSD_CHUNK
    g = _Grads(on_grad)

    h0 = _rmsnorm_fwd(x, w["l0_norm_w"], "l0_norm")
    w0 = w["l0_w_in"]
    r0_s5, r0_z, r0_xbc = (0, 2 * S5_WIDTH), (2 * S5_WIDTH, SSD_WIDTH), (2 * S5_WIDTH + SSD_WIDTH, SSD_XBC)
    r0_dt = 2 * S5_WIDTH + SSD_WIDTH + SSD_XBC
    w0_dt = _pad_rows(w0[r0_dt:])
    p_s5 = _mm(h0, w0, "nt", F32, "l0_in_s5", b_rows=r0_s5)
    p_z = _mm(h0, w0, "nt", F32, "l0_in_z", b_rows=r0_z)
    p_xbc = _mm(h0, w0, "nt", F32, "l0_in_xbc", b_rows=r0_xbc)
    p_dt = _mm(h0, w0_dt, "nt", F32, "l0_in_dt")

    row = lambda a: a.reshape(1, S5_NS)
    b_rows = lambda a: a.transpose(2, 0, 1).reshape(S5_GROUP, S5_NS)
    prep_in = (row(w["l0_s5_lambda_re"]), row(w["l0_s5_lambda_im"]),
               row(jnp.repeat(w["l0_s5_log_step"], S5_STATE)), b_rows(w["l0_s5_b_re"]), b_rows(w["l0_s5_b_im"]))
    ab_re, ab_im, bbr, bbi = _s5_prep(*prep_in)
    to_bb = lambda a: _blockdiag(a.reshape(S5_GROUP, S5_GROUPS, S5_STATE).transpose(1, 0, 2),
                                 S5_GROUP, S5_STATE).astype(MXU_DT)
    to_ct = lambda a: _blockdiag(a.transpose(0, 2, 1), S5_STATE, S5_GROUP).astype(MXU_DT)
    bb_re, bb_im = to_bb(bbr), to_bb(bbi)
    ct_re, ct_im = to_ct(w["l0_s5_c_re"]), to_ct(w["l0_s5_c_im"])
    a_re3, a_im3 = ab_re.reshape(S5_BLK, 1, S5_BS), ab_im.reshape(S5_BLK, 1, S5_BS)
    d3 = w["l0_s5_d"].reshape(S5_BLK, 1, S5_BC)
    y5, s_re, s_im = _s5_fwd(p_s5, bb_re, bb_im, ct_re, ct_im, a_re3, a_im3, d3)
    g_bf = _tiles("s5_gelu", lambda yb: (_gelu(yb),), [y5], [], [(S5_WIDTH, MXU_DT)], 0, tr=256)[0]
    zg = _mm(g_bf, w["l0_s5_w_glu"], "nn", F32, "s5_glu")
    b_glu = w["l0_s5_b_glu"].reshape(1, -1)
    s5_out = _tiles("s5_out", lambda yb, zb, gb, bb: (_s5_out_fn(yb, zb, gb, bb),),
                    [y5, zg, (p_s5, 1)], [b_glu], [(S5_WIDTH, MXU_DT)], 0, tr=256)[0]

    conv_w = w["l0_ssd_conv_w"]
    cv, act = _conv_fwd(p_xbc, conv_w, w["l0_ssd_conv_b"])
    bias_row = _pad_lanes(w["l0_ssd_dt_bias"].reshape(1, -1))
    alog_row = _pad_lanes(w["l0_ssd_a_log"].reshape(1, -1))
    dt, cum, cum_t = _dt_fwd(p_dt, bias_row, alog_row)
    dtg, cumg = _to_groups(dt), _to_groups(cum)
    cumtg = cum_t[:, :SSD_HEADS].reshape(nc, SSD_GROUPS, SSD_HPG, SSD_CHUNK).transpose(1, 0, 2, 3)
    cumtg = jnp.pad(cumtg, ((0, 0), (0, 0), (0, 16 - SSD_HPG), (0, 0)))
    ycore, states = _ssd_core_fwd(act, dtg, cumg, cumtg)
    dchan = jnp.repeat(w["l0_ssd_d"], SSD_HEAD_DIM).reshape(1, -1)
    nw_row = w["l0_ssd_norm_w"].reshape(1, -1)
    ssd_out = _tiles("ssd_post", lambda a, b, c, d, e: (_ssd_post_fn(a, b, c, d, e),),
                     [ycore, act, p_z], [dchan, nw_row], [(SSD_WIDTH, MXU_DT)], 0, tr=256, tc=SSD_GW)[0]
    mixed = jnp.concatenate([s5_out, ssd_out], axis=1)
    x1 = _mm(mixed, w["l0_w_out"], "nn", F32, "l0_out", res=x)

    h1 = _rmsnorm_fwd(x1, w["l1_norm_w"], "l1_norm")
    w1 = w["l1_w_in"]
    r1_qkv, r1_gate = (0, 3 * FOX_WIDTH), (3 * FOX_WIDTH, FOX_WIDTH)
    w1_f = _pad_rows(w1[4 * FOX_WIDTH:])
    qkv = _mm(h1, w1, "nt", MXU_DT, "l1_in_qkv", b_rows=r1_qkv)
    gate1 = _mm(h1, w1, "nt", F32, "l1_in_gate", b_rows=r1_gate)
    f_raw = _mm(h1, w1_f, "nt", F32, "l1_in_f")
    bf_row = _pad_lanes(w["l1_fox_b_f"].reshape(1, -1))
    c = _fgate_fwd(f_raw, bf_row)
    c0_rep = _fox_c0(c, min(FOX_TILE, length))
    c_t = c[:, :FOX_HEADS].T.reshape(FOX_HEADS, 1, length)
    att, lse = _fox_fwd(qkv, c0_rep, c_t)
    out1 = _tiles("fox_gate", lambda a, b: (_fox_gate_fn(a, b),), [att, gate1], [],
                  [(FOX_WIDTH, MXU_DT)], 0, tr=256)[0]
    x2 = _mm(out1, w["l1_w_out"], "nn", F32, "l1_out", res=x1)

    loss_part, dx2, dx2b, g["final_norm_w"] = _final_loss(x2, w["final_norm_w"], target, "final_loss")

    d_out1 = _mm(dx2b, w["l1_w_out"], "nt", F32, "l1_out_dx")
    g["l1_w_out"] = _mm(out1, dx2b, "tn", F32, "l1_out_dw")

    def gate_bwd(a, gt, d):
        _, vjp = jax.vjp(_fox_gate_fn, a, gt)
        da, dgt = vjp(d)
        return da, dgt, jnp.broadcast_to(jnp.sum(da * a, axis=1, keepdims=True), a.shape)

    d_att, d_gate1, delta = _tiles("fox_gate_bwd", gate_bwd, [att, gate1, d_out1], [],
                                   [(FOX_WIDTH, MXU_DT), (FOX_WIDTH, MXU_DT), (FOX_WIDTH, F32)], 0,
                                   tr=512, tc=FOX_HEAD_DIM)
    dq, dk, dv, dcq, dck = _fox_bwd(qkv, g.before(d_att, "raw", "l1_w_out"), lse, delta, c0_rep, c_t)
    dc = dcq.reshape(length, FOX_HEADS, FOX_HEAD_DIM)[:, :, 0] + dck.reshape(FOX_HEADS, length).T
    df, dbf = _fgate_bwd(f_raw, bf_row, _pad_lanes(dc))
    g["l1_fox_b_f"] = dbf[0, :FOX_HEADS]
    pieces = list(zip(("q", "k", "v", "gate"), (dq, dk, dv, d_gate1)))
    dh1 = None
    for i, (tag, d_piece) in enumerate(pieces):
        dh1 = _mm(d_piece, w1, "nn", F32, "l1_in_dx_" + tag, res=dh1, b_rows=(i * FOX_WIDTH, FOX_WIDTH))
    dh1 = _mm(df, w1_f, "nn", F32, "l1_in_dx_f", res=dh1)
    dw1 = lax.empty((ODD_IN, D_MODEL), F32)
    for i, (tag, d_piece) in enumerate(pieces):
        dw1 = _mm(d_piece, h1, "tn", F32, "l1_in_dw_" + tag, into=(dw1, i * FOX_WIDTH))
    g["l1_w_in"] = dw1.at[4 * FOX_WIDTH:].set(_mm(df, h1, "tn", F32, "l1_in_dw_f")[:FOX_HEADS])
    dh1 = g.before(g.before(dh1, "raw", "l1_w_in"), "done", "l1_w_out")
    dx1, dx1b, g["l1_norm_w"] = _rmsnorm_bwd(x1, w["l1_norm_w"], dh1, dx2, "l1_norm_bwd")

    wout0 = w["l0_w_out"]
    d_s5 = _mm(dx1b, wout0, "nt", F32, "l0_out_dx_s5", b_rows=(0, S5_WIDTH))
    d_ssd = _mm(dx1b, wout0, "nt", F32, "l0_out_dx_ssd", b_rows=(S5_WIDTH, SSD_WIDTH))
    g["l0_w_out"] = _mm(mixed, dx1b, "tn", F32, "l0_out_dw")
    d_ssd = g.before(d_ssd, "sums", "l1_w_in")

    def post_bwd(a, b, c_, d, dch, nw):
        _, vjp = jax.vjp(_ssd_post_fn, a, b, c_, dch, nw)
        return vjp(d)

    dycore, dxs_post, dz, ddch, dnw = _tiles(
        "ssd_post_bwd", post_bwd, [ycore, act, p_z, d_ssd], [dchan, nw_row],
        [(SSD_WIDTH, F32), (SSD_WIDTH, F32), (SSD_WIDTH, MXU_DT)], 2, tr=256, tc=SSD_GW)
    g["l0_ssd_d"] = ddch[0].reshape(SSD_HEADS, SSD_HEAD_DIM).sum(axis=1)
    g["l0_ssd_norm_w"] = dnw[0]
    dycore = g.before(dycore, "raw", "l0_w_out")
    dxs, d_b, d_c, ddtg, dcumg, dcumtg, dclg = _ssd_core_bwd(dycore, dxs_post, act, dtg, cumg, cumtg, states)
    dact = g.before(jnp.concatenate([dxs, d_b, d_c], axis=1), "sums", "l0_w_out")
    dxbc, dconvw, dconvb = _conv_bwd(dact, cv, p_xbc, conv_w)
    g["l0_ssd_conv_w"] = dconvw[:SSD_CONV]
    g["l0_ssd_conv_b"] = dconvb[0]
    dcum = _from_groups(dcumg)
    dcum = dcum + dcumtg[:, :, :SSD_HPG].transpose(1, 3, 0, 2).reshape(length, SSD_HEADS)
    dcl = dclg[:, :, 0, :SSD_HPG].transpose(1, 0, 2).reshape(nc, SSD_HEADS)
    dcum = dcum.reshape(nc, SSD_CHUNK, SSD_HEADS).at[:, SSD_CHUNK - 1, :].add(dcl).reshape(length, SSD_HEADS)
    ddt_raw, dbias, dalog = _dt_bwd(p_dt, bias_row, alog_row, _pad_lanes(_from_groups(ddtg)), _pad_lanes(dcum))
    g["l0_ssd_dt_bias"] = dbias[0, :SSD_HEADS]
    g["l0_ssd_a_log"] = dalog[0, :SSD_HEADS]

    def s5_out_bwd(yb, zb, gb, d, bb):
        _, vjp = jax.vjp(_s5_out_fn, yb, zb, gb, bb)
        return vjp(d)

    dy_direct, dzg, dgate0, dbglu = _tiles(
        "s5_out_bwd", s5_out_bwd, [y5, zg, (p_s5, 1), d_s5], [b_glu],
        [(S5_WIDTH, F32), (S5_WIDTH, MXU_DT), (S5_WIDTH, MXU_DT)], 1, tr=256)
    g["l0_s5_b_glu"] = dbglu[0]
    g["l0_s5_w_glu"] = _mm(g_bf, dzg, "tn", F32, "s5_glu_dw")
    dg2 = _mm(dzg, w["l0_s5_w_glu"], "nt", F32, "s5_glu_dx")

    def gelu_bwd(yb, d, direct):
        _, vjp = jax.vjp(_gelu, yb)
        return (vjp(d)[0] + direct,)

    dy5 = _tiles("s5_gelu_bwd", gelu_bwd, [y5, dg2, dy_direct], [], [(S5_WIDTH, F32)], 0, tr=256)[0]
    dy5 = g.before(dy5, "done", "l1_w_in")
    du, dbbr3, dbbi3, dctr3, dcti3, dar, dai, dd5 = _s5_bwd(dy5, p_s5, s_re, s_im, bb_re, bb_im, ct_re, ct_im,
                                                           a_re3, a_im3, d3)
    from_bb = lambda a: _blockdiag_t(a, S5_GROUP, S5_STATE).transpose(1, 0, 2).reshape(S5_GROUP, S5_NS)
    from_ct = lambda a: _blockdiag_t(a, S5_STATE, S5_GROUP).transpose(0, 2, 1)
    g["l0_s5_c_re"], g["l0_s5_c_im"] = from_ct(dctr3), from_ct(dcti3)
    g["l0_s5_d"] = dd5[:, 0, :].reshape(S5_GROUPS, S5_GROUP)
    dlr, dli, dls, dbr, dbi = _s5_prep_bwd(*prep_in, dar[:, 0, :].reshape(1, S5_NS), dai[:, 0, :].reshape(1, S5_NS),
                                           from_bb(dbbr3), from_bb(dbbi3))
    g["l0_s5_lambda_re"] = dlr.reshape(S5_GROUPS, S5_STATE)
    g["l0_s5_lambda_im"] = dli.reshape(S5_GROUPS, S5_STATE)
    g["l0_s5_log_step"] = dls.reshape(S5_GROUPS, S5_STATE).sum(axis=1)
    from_rows = lambda a: a.reshape(S5_GROUP, S5_GROUPS, S5_STATE).transpose(1, 2, 0)
    g["l0_s5_b_re"], g["l0_s5_b_im"] = from_rows(dbr), from_rows(dbi)

    dus = g.before(jnp.concatenate([du, dgate0], axis=1), "done", "l0_w_out")
    dw0 = _mm(dus, h0, "tn", F32, "l0_in_dw_s5", into=(lax.empty((EVEN_IN, D_MODEL), F32), r0_s5[0]))
    dw0 = _mm(dz, h0, "tn", F32, "l0_in_dw_z", into=(dw0, r0_z[0]))
    dw0 = _mm(dxbc, h0, "tn", F32, "l0_in_dw_xbc", into=(dw0, r0_xbc[0]))
    dw0 = dw0.at[r0_dt:].set(_mm(ddt_raw, h0, "tn", F32, "l0_in_dw_dt")[:SSD_HEADS])
    first = {}

    def first_half():
        dus_late = lax.optimization_barrier((dus, dw0))[0] if on_grad is not None else dus
        dh = _mm(dus_late, w0, "nn", F32, "l0_in_dx_s5", b_rows=r0_s5)
        first["dh0"] = _mm(dz, w0, "nn", F32, "l0_in_dx_z", res=dh, b_rows=r0_z)
        return first["dh0"]

    g.store("l0_w_in", dw0, between=first_half)
    dh0 = _mm(dxbc, w0, "nn", F32, "l0_in_dx_xbc", res=g.before(first["dh0"], "sums", "l0_w_in"), b_rows=r0_xbc)
    dh0 = _mm(ddt_raw, w0_dt, "nn", F32, "l0_in_dx_dt", res=dh0)
    grad_x, _, g["l0_norm_w"] = _rmsnorm_bwd(x, w["l0_norm_w"], dh0, dx1, "l0_norm_bwd")
    return loss_part, grad_x, g


TRANSPOSED = ("l0_w_in", "l1_w_in")


SEQUENCER_IDS = {"l0_w_out": (0, 1, 2), "l1_w_in": (3, 4, 5), "l1_w_out": (6, 7, 8), "l0_w_in": (None, 9, 10),
                 "l0_s5_w_glu": (11, 12, 13), "l0_ssd_conv_w": (14, 15, 16)}
NO_IDS = (None, None, None)


def _gather_weight(name, shard):
    cid = SEQUENCER_IDS.get(name, NO_IDS)[0]
    if name == "l0_ssd_conv_w":
        full = _gather(shard, "gather_" + name, cid)
        return full.transpose(1, 0, 2).reshape(shard.shape[0], N_DEV * shard.shape[1])
    if name in TRANSPOSED:
        full = _gather(shard.T.astype(MXU_DT), "gather_" + name, cid)
        return full.reshape(N_DEV * shard.shape[1], shard.shape[0])
    full = _gather(shard.astype(MXU_DT), "gather_" + name, cid)
    return full.reshape(N_DEV * shard.shape[0], shard.shape[1])


def _reduce_grad(name, grad, shard_shape, between=None):
    rows, cols = shard_shape
    if name == "l0_ssd_conv_w":
        parts = grad.reshape(rows, N_DEV, cols).transpose(1, 0, 2)
    elif name in TRANSPOSED:
        parts = grad.reshape(N_DEV, cols, rows)
    else:
        parts = grad.reshape(N_DEV, rows, cols)
    _, pair_id, chip_id = SEQUENCER_IDS.get(name, NO_IDS)
    recv = _pair_send(parts, "pair_" + name, pair_id)
    if between is not None:
        parts = lax.optimization_barrier((parts, between()))[0]
    sums = _pair_sum(parts, recv, F32 if name == "l0_ssd_conv_w" else MXU_DT, "pairsum_" + name)
    return sums, _chip_exchange(sums, "scatter_" + name, chip_id)


SMALL_ROWS_QUANTUM = 8 * LANES


def _step(args):
    x = args["x"][0]
    target = args["loss_target"][0]
    full = {n: args[n] for n in WEIGHTS if n not in SHARDED}
    for n in SHARDED:
        shard = args[n]
        if n != SHARDED[0]:
            shard, full[SHARDED[0]] = lax.optimization_barrier((shard, full[SHARDED[0]]))
        full[n] = _gather_weight(n, shard)
    out_g, out_d, out_m, out_v = {}, {}, {}, {}

    loss_part, grad_x, g = _local_step(
        x, target, full, lambda n, grad, between: _reduce_grad(n, grad, args[n].shape, between))
    for n in reversed(SHARDED):
        view = (lambda a: a.T) if n in TRANSPOSED else (lambda a: a)
        outs = _adamw(view(args[n]), g[n], view(args["m_" + n]), view(args["v_" + n]), "adamw_" + n)
        out_g[n], out_d[n], out_m[n], out_v[n] = [view(o) for o in outs]

    small = [n for n in WEIGHTS if n not in SHARDED]
    sizes = [int(math.prod(args[n].shape)) for n in small]
    total = sum(sizes) + 1
    padded = -(-total // SMALL_ROWS_QUANTUM) * SMALL_ROWS_QUANTUM

    def pack(pieces, extra):
        flat = jnp.concatenate([p.reshape(-1).astype(F32) for p in pieces] + [extra.reshape(1)])
        return jnp.pad(flat, (0, padded - total)).reshape(padded // LANES, LANES)

    zero = jnp.zeros((), F32)
    parts = _gather(pack([g[n] for n in small], loss_part), "gather_small_grads")
    sg, sd, sm, sv = _adamw(pack([args[n] for n in small], zero), parts,
                            pack([args["m_" + n] for n in small], zero),
                            pack([args["v_" + n] for n in small], zero), "adamw_small")
    off = 0
    for n, sz in zip(small, sizes):
        cut = lambda a: a.reshape(-1)[off:off + sz].reshape(args[n].shape)
        out_g[n], out_d[n], out_m[n], out_v[n] = cut(sg), cut(sd), cut(sm), cut(sv)
        off += sz
    loss = sg.reshape(-1)[total - 1]
    return (loss, grad_x[None], *[out_g[n] for n in WEIGHTS], *[out_d[n] for n in WEIGHTS],
            *[out_m[n] for n in WEIGHTS], *[out_v[n] for n in WEIGHTS])


def kernel(x, l0_norm_w, l0_w_in, l0_s5_lambda_re, l0_s5_lambda_im, l0_s5_log_step, l0_s5_b_re, l0_s5_b_im, l0_s5_c_re, l0_s5_c_im, l0_s5_d, l0_s5_w_glu, l0_s5_b_glu, l0_ssd_conv_w, l0_ssd_conv_b, l0_ssd_dt_bias, l0_ssd_a_log, l0_ssd_d, l0_ssd_norm_w, l0_w_out, l1_norm_w, l1_w_in, l1_fox_b_f, l1_w_out, final_norm_w, loss_target, m_l0_norm_w, m_l0_w_in, m_l0_s5_lambda_re, m_l0_s5_lambda_im, m_l0_s5_log_step, m_l0_s5_b_re, m_l0_s5_b_im, m_l0_s5_c_re, m_l0_s5_c_im, m_l0_s5_d, m_l0_s5_w_glu, m_l0_s5_b_glu, m_l0_ssd_conv_w, m_l0_ssd_conv_b, m_l0_ssd_dt_bias, m_l0_ssd_a_log, m_l0_ssd_d, m_l0_ssd_norm_w, m_l0_w_out, m_l1_norm_w, m_l1_w_in, m_l1_fox_b_f, m_l1_w_out, m_final_norm_w, v_l0_norm_w, v_l0_w_in, v_l0_s5_lambda_re, v_l0_s5_lambda_im, v_l0_s5_log_step, v_l0_s5_b_re, v_l0_s5_b_im, v_l0_s5_c_re, v_l0_s5_c_im, v_l0_s5_d, v_l0_s5_w_glu, v_l0_s5_b_glu, v_l0_ssd_conv_w, v_l0_ssd_conv_b, v_l0_ssd_dt_bias, v_l0_ssd_a_log, v_l0_ssd_d, v_l0_ssd_norm_w, v_l0_w_out, v_l1_norm_w, v_l1_w_in, v_l1_fox_b_f, v_l1_w_out, v_final_norm_w):
    return _step(dict(locals()))
```

```python
import functools
import math

import jax
import jax.numpy as jnp
from jax import lax
from jax.experimental import pallas as pl
from jax.experimental.pallas import tpu as pltpu
from jax.experimental.pallas import tpu_sc as plsc

F32 = jnp.float32
BF16 = jnp.bfloat16
MXU_DT = BF16

D_MODEL = 4096
S5_WIDTH = 2048
S5_GROUP = 16
S5_GROUPS = 128
S5_STATE = 64
S5_EIG_CLIP = -1e-4
S5_BLK = 16
SSD_WIDTH = 6144
SSD_HEAD_DIM = 64
SSD_HEADS = 96
SSD_GROUPS = 8
SSD_STATE = 128
SSD_CONV = 4
SSD_CHUNK = 128
SSD_XBC = 8192
SSD_HPG = SSD_HEADS // SSD_GROUPS
FOX_HEAD_DIM = 128
FOX_HEADS = 32
FOX_WIDTH = 4096
NORM_EPS = 1e-5
EVEN_IN = 18528
ODD_IN = 16416
EVEN_PAD = 18560
ODD_PAD = 16512
LANES = 128
N_DEV = 8

ADAM_LR = 0.001
ADAM_B1 = 0.9
ADAM_B2 = 0.999
ADAM_EPS = 1e-08
ADAM_WD = 0.01
ADAM_STEP = 10

VMEM_LIMIT_BYTES = 48 * 1024 * 1024


ANYSPACE = pl.BlockSpec(memory_space=pl.ANY)


def _cparams(*sem):
    return pltpu.CompilerParams(dimension_semantics=sem, vmem_limit_bytes=VMEM_LIMIT_BYTES)


def _pick(n, target, quantum=LANES):
    if n <= target:
        return n
    t = (target // quantum) * quantum
    while t >= quantum:
        if n % t == 0:
            return t
        t -= quantum
    raise ValueError((n, target, quantum))


_MM_DIMS = {"nn": ((1,), (0,)), "nt": ((1,), (1,)), "tn": ((0,), (0,))}


MM_VMEM_BUDGET = 38 * 1024 * 1024


def _mm_tk(k, tm, tn, out_bytes, has_res, tk_t, start=0):
    fixed = 2 * tm * tn * out_bytes + (2 * tm * tn * 4 if has_res else 0)
    tk = min(k, tk_t)
    while True:
        if k % tk == 0 and start % tk == 0 and (tk == k or tk % LANES == 0):
            need = fixed + 2 * (tm + tn) * tk * 2 + (tm * tn * 4 if tk < k else 0)
            if need <= MM_VMEM_BUDGET or tk <= LANES:
                return tk
        tk -= LANES if tk % LANES == 0 else tk % LANES


def _mm(a, b, mode, out_dtype, name, res=None, b_rows=None, into=None, tm_t=1024, tn_t=512, tk_t=8192):
    b_start, b_size = b_rows if b_rows is not None else (0, b.shape[0])
    if mode == "nn":
        (m, k), (k2, n) = a.shape, (b_size, b.shape[1])
    elif mode == "nt":
        (m, k), (n, k2) = a.shape, (b_size, b.shape[1])
    else:
        (k, m), (k2, n) = a.shape, (b_size, b.shape[1])
    assert k == k2, (a.shape, b.shape, mode)
    tm, tn = _pick(m, tm_t), _pick(n, tn_t)
    has_res = res is not None
    has_into = into is not None
    tk = _mm_tk(k, tm, tn, jnp.dtype(out_dtype).itemsize, has_res, tk_t, b_start if mode == "nn" else 0)
    nk = k // tk
    dims = (_MM_DIMS[mode], ((), ()))
    o_row = 0
    if has_into:
        assert into[1] % tm == 0 and into[0].shape[1] == n and into[0].dtype == out_dtype, (into[1], tm)
        o_row = into[1] // tm
    if mode == "nt":
        assert b_start % tn == 0, (b_start, tn)
    b_blk = b_start // (tn if mode == "nt" else tk)

    def body(*refs):
        a_ref, b_ref = refs[:2]
        r_ref = refs[2] if has_res else None
        o_ref = refs[2 + has_res + has_into]
        part = lax.dot_general(a_ref[...].astype(MXU_DT), b_ref[...].astype(MXU_DT), dims,
                               preferred_element_type=F32)

        def finish(r):
            if has_res:
                r = r + r_ref[...]
            o_ref[...] = r.astype(out_dtype)

        if nk == 1:
            finish(part)
            return
        acc = refs[-1]
        kk = pl.program_id(2)

        @pl.when(kk == 0)
        def _():
            acc[...] = part

        @pl.when(jnp.logical_and(kk > 0, kk < nk - 1))
        def _():
            acc[...] += part

        @pl.when(kk == nk - 1)
        def _():
            finish(acc[...] + part)

    a_spec = (pl.BlockSpec((tk, tm), lambda i, j, kk: (kk, i)) if mode == "tn"
              else pl.BlockSpec((tm, tk), lambda i, j, kk: (i, kk)))
    b_spec = (pl.BlockSpec((tn, tk), lambda i, j, kk: (j + b_blk, kk)) if mode == "nt"
              else pl.BlockSpec((tk, tn), lambda i, j, kk: (kk + b_blk, j)))
    r_spec = pl.BlockSpec((tm, tn), lambda i, j, kk: (i, j))
    o_spec = pl.BlockSpec((tm, tn), lambda i, j, kk: (i + o_row, j))
    in_specs = [a_spec, b_spec] + ([r_spec] if has_res else []) + ([ANYSPACE] if has_into else [])
    args = (a, b) + ((res,) if has_res else ()) + ((into[0],) if has_into else ())
    return pl.pallas_call(
        body, name=name, grid=(m // tm, n // tn, nk), in_specs=in_specs, out_specs=o_spec,
        out_shape=jax.ShapeDtypeStruct(into[0].shape if has_into else (m, n), out_dtype),
        scratch_shapes=[pltpu.VMEM((tm, tn), F32)] if nk > 1 else [],
        input_output_aliases={len(args) - 1: 0} if has_into else {},
        compiler_params=_cparams("parallel", "parallel", "arbitrary"),
    )(*args)


def _tiles(name, fn, tiled, rows, out_tiled, out_acc, tr, tc=None):
    tiled = [t if isinstance(t, tuple) else (t, 0) for t in tiled]
    length = tiled[0][0].shape[0]
    width = out_tiled[0][0] if out_tiled else rows[0].shape[1]
    tc = width if tc is None else tc
    tr = min(tr, length)
    n_in = len(tiled) + len(rows)
    n_ot = len(out_tiled)

    def body(*refs):
        outs = fn(*[r[...] for r in refs[:n_in]])
        outs_t, outs_a = outs[:n_ot], outs[n_ot:]
        for r, v in zip(refs[n_in:n_in + n_ot], outs_t):
            r[...] = v.astype(r.dtype)
        i = pl.program_id(1)
        for r, v in zip(refs[n_in + n_ot:], outs_a):
            @pl.when(i == 0)
            def _(r=r):
                r[...] = jnp.zeros_like(r)

            r[...] += jnp.broadcast_to(v, r.shape)

    def tspec(off):
        return pl.BlockSpec((tr, tc), lambda j, i: (i, j + off))

    in_specs = [tspec(off) for _, off in tiled] + [pl.BlockSpec((1, tc), lambda j, i: (0, j)) for _ in rows]
    out_specs = [tspec(0) for _ in out_tiled] + [pl.BlockSpec((8, tc), lambda j, i: (0, j)) for _ in range(out_acc)]
    out_shape = ([jax.ShapeDtypeStruct((length, w), dt) for w, dt in out_tiled]
                 + [jax.ShapeDtypeStruct((8, width), F32) for _ in range(out_acc)])
    return pl.pallas_call(
        body, name=name, grid=(width // tc, length // tr), in_specs=in_specs, out_specs=out_specs,
        out_shape=out_shape, compiler_params=_cparams("parallel", "arbitrary"),
    )(*[t for t, _ in tiled], *rows)


def _rms(x, w):
    return x * lax.rsqrt(jnp.mean(x * x, axis=-1, keepdims=True) + NORM_EPS) * w


def _colsum(v):
    return jnp.sum(v, axis=0, keepdims=True)


def _rmsnorm_fwd(x, w, name):
    def fn(xb, wb):
        return (_rms(xb, wb),)
    return _tiles(name, fn, [x], [w.reshape(1, -1)], [(x.shape[1], MXU_DT)], 0, tr=256)[0]


def _rmsnorm_bwd(x, w, dh, dres, name):
    def fn(xb, db, rb, wb):
        _, vjp = jax.vjp(_rms, xb, wb)
        dx, dw = vjp(db)
        return dx + rb, dx + rb, dw
    dx, dxb, dw = _tiles(name, fn, [x, dh, dres], [w.reshape(1, -1)],
                         [(x.shape[1], F32), (x.shape[1], MXU_DT)], 1, tr=256)
    return dx, dxb, dw[0]


def _final_loss(x, w, target, name):
    def fn(xb, tb, wb):
        def f(xv, wv):
            e = _rms(xv, wv) - tb
            return 0.5 * jnp.sum(jnp.mean(e * e, axis=-1, keepdims=True), axis=0, keepdims=True)
        lv, vjp = jax.vjp(f, xb, wb)
        dx, dw = vjp(jnp.ones_like(lv))
        return dx, dx, dw, jnp.broadcast_to(lv, (1, xb.shape[1]))
    dx, dxb, dw, lv = _tiles(name, fn, [x, target], [w.reshape(1, -1)],
                             [(x.shape[1], F32), (x.shape[1], MXU_DT)], 2, tr=256)
    return lv[0, 0], dx, dxb, dw[0]


def _dg(a, b, mode):
    return lax.dot_general(a.astype(MXU_DT), b.astype(MXU_DT), (_MM_DIMS[mode], ((), ())),
                           preferred_element_type=F32)


@jax.custom_vjp
def _dot_nn(a, b):
    return _dg(a, b, "nn")


@jax.custom_vjp
def _dot_nt(a, b):
    return _dg(a, b, "nt")


@jax.custom_vjp
def _dot_tn(a, b):
    return _dg(a, b, "tn")


_dot_nn.defvjp(lambda a, b: (_dg(a, b, "nn"), (a, b)),
               lambda r, g: (_dg(g, r[1], "nt"), _dg(r[0], g, "tn")))
_dot_nt.defvjp(lambda a, b: (_dg(a, b, "nt"), (a, b)),
               lambda r, g: (_dg(g, r[1], "nn"), _dg(g, r[0], "tn")))
_dot_tn.defvjp(lambda a, b: (_dg(a, b, "tn"), (a, b)),
               lambda r, g: (_dg(r[1], g, "nt"), _dg(r[0], g, "nn")))


def _dot_exact(a, b):
    return jnp.dot(a, b, precision=lax.Precision.HIGHEST, preferred_element_type=F32)


S5_NS = S5_GROUPS * S5_STATE
S5_BS = S5_NS // S5_BLK
S5_BC = S5_WIDTH // S5_BLK


def _s5_disc(lr_raw, li, ls, br, bi):
    lr = jnp.minimum(lr_raw, S5_EIG_CLIP)
    step = jnp.exp(ls)
    mag = jnp.exp(lr * step)
    ab_re = mag * jnp.cos(li * step)
    ab_im = mag * jnp.sin(li * step)
    denom = lr * lr + li * li
    nr = ab_re - 1.0
    ni = ab_im
    coef_re = (nr * lr + ni * li) / denom
    coef_im = (ni * lr - nr * li) / denom
    return ab_re, ab_im, coef_re * br - coef_im * bi, coef_re * bi + coef_im * br


def _s5_prep(lr_raw, li, ls, br, bi):
    shapes = [jax.ShapeDtypeStruct((1, S5_NS), F32)] * 2 + [jax.ShapeDtypeStruct((S5_GROUP, S5_NS), F32)] * 2

    def body(a, b, c, d, e, o1, o2, o3, o4):
        for r, v in zip((o1, o2, o3, o4), _s5_disc(a[...], b[...], c[...], d[...], e[...])):
            r[...] = v

    return pl.pallas_call(body, name="s5_prep", out_shape=shapes)(lr_raw, li, ls, br, bi)


def _s5_prep_bwd(lr_raw, li, ls, br, bi, d_are, d_aim, d_bbre, d_bbim):
    shapes = [jax.ShapeDtypeStruct((1, S5_NS), F32)] * 3 + [jax.ShapeDtypeStruct((S5_GROUP, S5_NS), F32)] * 2

    def body(a, b, c, d, e, g1, g2, g3, g4, o1, o2, o3, o4, o5):
        _, vjp = jax.vjp(_s5_disc, a[...], b[...], c[...], d[...], e[...])
        for r, v in zip((o1, o2, o3, o4, o5), vjp((g1[...], g2[...], g3[...], g4[...]))):
            r[...] = v

    return pl.pallas_call(body, name="s5_prep_bwd", out_shape=shapes)(
        lr_raw, li, ls, br, bi, d_are, d_aim, d_bbre, d_bbim)


def _cmul(ar, ai, br, bi):
    return ar * br - ai * bi, ar * bi + ai * br


def _s5_powers(ar, ai, n):
    pr, pi_ = [ar], [ai]
    for _ in range(n - 1):
        r, i = _cmul(pr[-1], pi_[-1], pr[-1], pi_[-1])
        pr.append(r)
        pi_.append(i)
    return pr, pi_


S5_SUB = 8
S5_NPOW = 3


def _s5_in_groups(xr, xi, pr, pi_, reverse):
    t = xr.shape[0]
    sub = lax.broadcasted_iota(jnp.int32, xr.shape, 0) & (S5_SUB - 1)
    sr, si = xr, xi
    for k in range(S5_NPOW):
        d = 1 << k
        shift = (t - d) if reverse else d
        keep = (sub < S5_SUB - d) if reverse else (sub >= d)
        qr = jnp.where(keep, pltpu.roll(sr, shift, 0), 0.0)
        qi = jnp.where(keep, pltpu.roll(si, shift, 0), 0.0)
        mr, mi = _cmul(pr[k], pi_[k], qr, qi)
        sr, si = sr + mr, si + mi
    return sr, si


def _s5_scan_tile(xr, xi, apr, api, tabr, tabi, cr, ci, out_r, out_i, reverse):
    t = xr.shape[0]
    pr = [apr[k:k + 1, :] for k in range(S5_NPOW)]
    pi_ = [api[k:k + 1, :] for k in range(S5_NPOW)]
    sr, si = _s5_in_groups(xr, xi, pr, pi_, reverse)
    tr, ti = tabr[...], tabi[...]
    car_r, car_i = cr[0:1, :], ci[0:1, :]
    groups = range(t // S5_SUB)
    for g in (reversed(groups) if reverse else groups):
        rows = slice(g * S5_SUB, (g + 1) * S5_SUB)
        mr, mi = _cmul(tr, ti, car_r, car_i)
        br, bi = sr[rows] + mr, si[rows] + mi
        out_r[rows, :] = br
        out_i[rows, :] = bi
        edge = slice(0, 1) if reverse else slice(S5_SUB - 1, S5_SUB)
        car_r, car_i = br[edge], bi[edge]
    cr[0:1, :] = car_r
    ci[0:1, :] = car_i


def _s5_setup(ar, ai, reverse, apr, api, tabr, tabi):
    pr, pi_ = _s5_powers(ar, ai, S5_NPOW)
    for k in range(S5_NPOW):
        apr[k:k + 1, :] = pr[k]
        api[k:k + 1, :] = pi_[k]
    row = lax.broadcasted_iota(jnp.int32, (S5_SUB, ar.shape[1]), 0)
    first = (row == S5_SUB - 1) if reverse else (row == 0)
    sr, si = _s5_in_groups(jnp.where(first, ar, 0.0), jnp.where(first, ai, 0.0), pr, pi_, reverse)
    tabr[...] = sr
    tabi[...] = si


def _s5_fwd(p_s5, bb_re, bb_im, ct_re, ct_im, a_re, a_im, d_row, t_tile=256):
    length = p_s5.shape[0]
    t = min(t_tile, length)
    nt = length // t

    def body(u_ref, bbr, bbi, ctr, cti, ar_ref, ai_ref, d_ref, y_ref, sr_ref, si_ref,
             apr, api, tabr, tabi, cr, ci):
        i = pl.program_id(1)

        @pl.when(i == 0)
        def _():
            _s5_setup(ar_ref[...], ai_ref[...], False, apr, api, tabr, tabi)
            cr[...] = jnp.zeros_like(cr)
            ci[...] = jnp.zeros_like(ci)

        u = u_ref[...]
        _s5_scan_tile(_dg(u, bbr[...], "nn"), _dg(u, bbi[...], "nn"), apr, api, tabr, tabi, cr, ci,
                      sr_ref, si_ref, False)
        y_ref[...] = _dg(sr_ref[...], ctr[...], "nn") - _dg(si_ref[...], cti[...], "nn") + d_ref[...] * u

    blk3 = lambda a, b: pl.BlockSpec((None, a, b), lambda j, i: (j, 0, 0))
    return pl.pallas_call(
        body, name="s5_fwd", grid=(S5_BLK, nt),
        in_specs=[pl.BlockSpec((t, S5_BC), lambda j, i: (i, j)),
                  blk3(S5_BC, S5_BS), blk3(S5_BC, S5_BS), blk3(S5_BS, S5_BC), blk3(S5_BS, S5_BC),
                  blk3(1, S5_BS), blk3(1, S5_BS), blk3(1, S5_BC)],
        out_specs=[pl.BlockSpec((t, S5_BC), lambda j, i: (i, j)),
                   pl.BlockSpec((t, S5_BS), lambda j, i: (i, j)),
                   pl.BlockSpec((t, S5_BS), lambda j, i: (i, j))],
        out_shape=[jax.ShapeDtypeStruct((length, S5_WIDTH), F32),
                   jax.ShapeDtypeStruct((length, S5_NS), F32),
                   jax.ShapeDtypeStruct((length, S5_NS), F32)],
        scratch_shapes=[pltpu.VMEM((S5_SUB, S5_BS), F32)] * 6,
        compiler_params=_cparams("parallel", "arbitrary"),
    )(p_s5, bb_re, bb_im, ct_re, ct_im, a_re, a_im, d_row)


def _s5_bwd(dy, p_s5, s_re, s_im, bb_re, bb_im, ct_re, ct_im, a_re, a_im, d_row, t_tile=256):
    length = p_s5.shape[0]
    t = min(t_tile, length)
    nt = length // t

    def body(dy_ref, u_ref, sr_ref, si_ref, pr_ref, pi_ref, bbr, bbi, ctr, cti, ar_ref, ai_ref, d_ref,
             du_ref, dbbr, dbbi, dctr, dcti, dar, dai, dd_ref, apr, api, tabr, tabi, cr, ci, lam_r, lam_i):
        i = pl.program_id(1)

        @pl.when(i == 0)
        def _():
            _s5_setup(ar_ref[...], -ai_ref[...], True, apr, api, tabr, tabi)
            for r in (cr, ci, dbbr, dbbi, dctr, dcti, dar, dai, dd_ref):
                r[...] = jnp.zeros_like(r)

        dyv = dy_ref[...]
        u = u_ref[...]
        _s5_scan_tile(_dg(dyv, ctr[...], "nt"), -_dg(dyv, cti[...], "nt"), apr, api, tabr, tabi, cr, ci,
                      lam_r, lam_i, True)
        lr, li = lam_r[...], lam_i[...]
        du_ref[...] = (_dg(lr, bbr[...], "nt") + _dg(li, bbi[...], "nt") + d_ref[...] * dyv).astype(du_ref.dtype)
        dbbr[...] += _dg(u, lr, "tn")
        dbbi[...] += _dg(u, li, "tn")
        sr = sr_ref[...]
        si = si_ref[...]
        dctr[...] += _dg(sr, dyv, "tn")
        dcti[...] -= _dg(si, dyv, "tn")
        dd_ref[...] += jnp.broadcast_to(_colsum(dyv * u), dd_ref.shape)
        row = lax.broadcasted_iota(jnp.int32, sr.shape, 0)
        has_prev = (i < nt - 1).astype(F32)
        ssr = jnp.where(row == 0, pr_ref[7:8, :] * has_prev, pltpu.roll(sr, 1, 0))
        ssi = jnp.where(row == 0, pi_ref[7:8, :] * has_prev, pltpu.roll(si, 1, 0))
        dar[...] += jnp.broadcast_to(_colsum(lr * ssr + li * ssi), dar.shape)
        dai[...] += jnp.broadcast_to(_colsum(li * ssr - lr * ssi), dai.shape)

    rev = lambda j, i: (nt - 1 - i, j)
    prev = lambda j, i: (jnp.maximum((nt - 1 - i) * (t // 8) - 1, 0), j)
    blk3 = lambda a, b: pl.BlockSpec((None, a, b), lambda j, i: (j, 0, 0))
    return pl.pallas_call(
        body, name="s5_bwd", grid=(S5_BLK, nt),
        in_specs=[pl.BlockSpec((t, S5_BC), rev), pl.BlockSpec((t, S5_BC), rev),
                  pl.BlockSpec((t, S5_BS), rev), pl.BlockSpec((t, S5_BS), rev),
                  pl.BlockSpec((8, S5_BS), prev), pl.BlockSpec((8, S5_BS), prev),
                  blk3(S5_BC, S5_BS), blk3(S5_BC, S5_BS), blk3(S5_BS, S5_BC), blk3(S5_BS, S5_BC),
                  blk3(1, S5_BS), blk3(1, S5_BS), blk3(1, S5_BC)],
        out_specs=[pl.BlockSpec((t, S5_BC), rev),
                   blk3(S5_BC, S5_BS), blk3(S5_BC, S5_BS), blk3(S5_BS, S5_BC), blk3(S5_BS, S5_BC),
                   blk3(8, S5_BS), blk3(8, S5_BS), blk3(8, S5_BC)],
        out_shape=[jax.ShapeDtypeStruct((length, S5_WIDTH), MXU_DT),
                   jax.ShapeDtypeStruct((S5_BLK, S5_BC, S5_BS), F32), jax.ShapeDtypeStruct((S5_BLK, S5_BC, S5_BS), F32),
                   jax.ShapeDtypeStruct((S5_BLK, S5_BS, S5_BC), F32), jax.ShapeDtypeStruct((S5_BLK, S5_BS, S5_BC), F32),
                   jax.ShapeDtypeStruct((S5_BLK, 8, S5_BS), F32), jax.ShapeDtypeStruct((S5_BLK, 8, S5_BS), F32),
                   jax.ShapeDtypeStruct((S5_BLK, 8, S5_BC), F32)],
        scratch_shapes=[pltpu.VMEM((S5_SUB, S5_BS), F32)] * 6 + [pltpu.VMEM((t, S5_BS), F32)] * 2,
        compiler_params=_cparams("parallel", "arbitrary"),
    )(dy, p_s5, s_re, s_im, s_re, s_im, bb_re, bb_im, ct_re, ct_im, a_re, a_im, d_row)


def _blockdiag(m, rows, cols):
    m = m.reshape(S5_BLK, 8, rows, 1, cols)
    on_diag = jnp.eye(8, dtype=bool)[None, :, None, :, None]
    return jnp.where(on_diag, m, 0).reshape(S5_BLK, 8 * rows, 8 * cols)


def _blockdiag_t(m, rows, cols):
    m = m.reshape(S5_BLK, 8, rows, 8, cols)
    on_diag = jnp.eye(8, dtype=bool)[None, :, None, :, None]
    return jnp.sum(jnp.where(on_diag, m, 0), axis=3).reshape(S5_GROUPS, rows, cols)


def _gelu(y):
    return jax.nn.gelu(y)


def _s5_out_fn(y, zg, gate, b):
    return _gelu(y) * jax.nn.sigmoid(zg + b) * jax.nn.silu(gate)


HALO = 8


def _conv_fwd(xbc, w, b, tr=256, tc=1024):
    length, width = xbc.shape
    tr = min(tr, length)

    def body(x_ref, h_ref, w_ref, b_ref, cv_ref, act_ref):
        i = pl.program_id(1)
        x = x_ref[...]
        xx = jnp.concatenate([h_ref[...] * (i > 0).astype(F32), x], axis=0)
        acc = b_ref[...] + w_ref[3:4, :] * x
        for k in range(SSD_CONV - 1):
            acc = acc + w_ref[k:k + 1, :] * pltpu.roll(xx, SSD_CONV - 1 - k, 0)[HALO:, :]
        cv_ref[...] = acc
        act_ref[...] = jax.nn.silu(acc)

    main = pl.BlockSpec((tr, tc), lambda j, i: (i, j))
    before = pl.BlockSpec((HALO, tc), lambda j, i: (jnp.maximum(i * (tr // HALO) - 1, 0), j))
    return pl.pallas_call(
        body, name="ssd_conv_fwd", grid=(width // tc, length // tr),
        in_specs=[main, before, pl.BlockSpec((SSD_CONV, tc), lambda j, i: (0, j)),
                  pl.BlockSpec((1, tc), lambda j, i: (0, j))],
        out_specs=[main, main],
        out_shape=[jax.ShapeDtypeStruct((length, width), F32)] * 2,
        compiler_params=_cparams("parallel", "arbitrary"),
    )(xbc, xbc, w, b.reshape(1, -1))


def _conv_bwd(dact, cv, xbc, w, tr=256, tc=1024):
    length, width = xbc.shape
    tr = min(tr, length)
    nr = length // tr
    n = tr + HALO

    def dsilu(d, c):
        sg = jax.nn.sigmoid(c)
        return d * (sg * (1.0 + c * (1.0 - sg)))

    def body(da_ref, dan_ref, cv_ref, cvn_ref, x_ref, xp_ref, w_ref, dx_ref, dw_ref, db_ref):
        i = pl.program_id(1)

        @pl.when(i == 0)
        def _():
            dw_ref[...] = jnp.zeros_like(dw_ref)
            db_ref[...] = jnp.zeros_like(db_ref)

        dc = dsilu(da_ref[...], cv_ref[...])
        dcn = dsilu(dan_ref[...], cvn_ref[...]) * (i < nr - 1).astype(F32)
        dd = jnp.concatenate([dc, dcn], axis=0)
        x = x_ref[...]
        xx = jnp.concatenate([xp_ref[...] * (i > 0).astype(F32), x], axis=0)
        dx = w_ref[3:4, :] * dc
        dw_ref[3:4, :] += _colsum(dc * x)
        for k in range(SSD_CONV - 1):
            j = SSD_CONV - 1 - k
            dx = dx + w_ref[k:k + 1, :] * pltpu.roll(dd, n - j, 0)[:tr, :]
            dw_ref[k:k + 1, :] += _colsum(dc * pltpu.roll(xx, j, 0)[HALO:, :])
        dx_ref[...] = dx.astype(dx_ref.dtype)
        db_ref[...] += jnp.broadcast_to(_colsum(dc), db_ref.shape)

    main = pl.BlockSpec((tr, tc), lambda j, i: (i, j))
    before = pl.BlockSpec((HALO, tc), lambda j, i: (jnp.maximum(i * (tr // HALO) - 1, 0), j))
    after = pl.BlockSpec((HALO, tc), lambda j, i: (jnp.minimum((i + 1) * (tr // HALO), length // HALO - 1), j))
    acc = pl.BlockSpec((8, tc), lambda j, i: (0, j))
    return pl.pallas_call(
        body, name="ssd_conv_bwd", grid=(width // tc, nr),
        in_specs=[main, after, main, after, main, before, pl.BlockSpec((SSD_CONV, tc), lambda j, i: (0, j))],
        out_specs=[main, acc, acc],
        out_shape=[jax.ShapeDtypeStruct((length, width), MXU_DT),
                   jax.ShapeDtypeStruct((8, width), F32), jax.ShapeDtypeStruct((8, width), F32)],
        compiler_params=_cparams("parallel", "arbitrary"),
    )(dact, dact, cv, cv, xbc, xbc, w)


def _tri(lower):
    r = lax.broadcasted_iota(jnp.int32, (SSD_CHUNK, SSD_CHUNK), 0)
    c = lax.broadcasted_iota(jnp.int32, (SSD_CHUNK, SSD_CHUNK), 1)
    return ((r >= c) if lower else (r <= c)).astype(F32)


def _dt_fwd(raw, bias, a_log):
    length = raw.shape[0]
    nc = length // SSD_CHUNK

    def body(r_ref, b_ref, a_ref, dt_ref, cum_ref, cumt_ref):
        dt = jax.nn.softplus(r_ref[...] + b_ref[...])
        cum = _dot_exact(_tri(True), dt * (-jnp.exp(a_ref[...])))
        dt_ref[...] = dt
        cum_ref[...] = cum
        cumt_ref[...] = cum.T

    blk = pl.BlockSpec((SSD_CHUNK, LANES), lambda c: (c, 0))
    row = pl.BlockSpec((1, LANES), lambda c: (0, 0))
    return pl.pallas_call(
        body, name="ssd_dt_fwd", grid=(nc,), in_specs=[blk, row, row],
        out_specs=[blk, blk, pl.BlockSpec((None, LANES, SSD_CHUNK), lambda c: (c, 0, 0))],
        out_shape=[jax.ShapeDtypeStruct((length, LANES), F32)] * 2
        + [jax.ShapeDtypeStruct((nc, LANES, SSD_CHUNK), F32)],
        compiler_params=_cparams("parallel"),
    )(raw, bias, a_log)


def _dt_bwd(raw, bias, a_log, ddt, dcum):
    length = raw.shape[0]
    nc = length // SSD_CHUNK

    def body(r_ref, b_ref, a_ref, ddt_ref, dcum_ref, dr_ref, db_ref, da_ref):
        @pl.when(pl.program_id(0) == 0)
        def _():
            db_ref[...] = jnp.zeros_like(db_ref)
            da_ref[...] = jnp.zeros_like(da_ref)

        z = r_ref[...] + b_ref[...]
        a = -jnp.exp(a_ref[...])
        dla = _dot_exact(_tri(False), dcum_ref[...])
        draw = (ddt_ref[...] + dla * a) * jax.nn.sigmoid(z)
        dr_ref[...] = draw.astype(dr_ref.dtype)
        db_ref[...] += jnp.broadcast_to(_colsum(draw), db_ref.shape)
        da_ref[...] += jnp.broadcast_to(_colsum(dla * jax.nn.softplus(z)) * a, da_ref.shape)

    blk = pl.BlockSpec((SSD_CHUNK, LANES), lambda c: (c, 0))
    row = pl.BlockSpec((1, LANES), lambda c: (0, 0))
    acc = pl.BlockSpec((8, LANES), lambda c: (0, 0))
    return pl.pallas_call(
        body, name="ssd_dt_bwd", grid=(nc,), in_specs=[blk, row, row, blk, blk],
        out_specs=[blk, acc, acc],
        out_shape=[jax.ShapeDtypeStruct((length, LANES), MXU_DT),
                   jax.ShapeDtypeStruct((8, LANES), F32), jax.ShapeDtypeStruct((8, LANES), F32)],
        compiler_params=_cparams("arbitrary"),
    )(raw, bias, a_log, ddt, dcum)


SSD_GW = SSD_HPG * SSD_HEAD_DIM
SSD_B_OFF = SSD_WIDTH // SSD_STATE
SSD_C_OFF = SSD_B_OFF + SSD_GROUPS


def _ssd_expand():
    r = lax.broadcasted_iota(jnp.int32, (LANES, SSD_GW), 0)
    c = lax.broadcasted_iota(jnp.int32, (LANES, SSD_GW), 1)
    return (c // SSD_HEAD_DIM == r).astype(F32)


def _ssd_to_channels(v16, e):
    return _dot_exact(v16, e)


def _ssd_to_heads(v, e):
    return lax.dot_general(v, e, (((1,), (1,)), ((), ())), precision=lax.Precision.HIGHEST,
                           preferred_element_type=F32)


def _ssd_decay(cum_ref, cumt_ref, r):
    q = lax.broadcasted_iota(jnp.int32, (SSD_CHUNK, SSD_CHUNK), 0)
    k = lax.broadcasted_iota(jnp.int32, (SSD_CHUNK, SSD_CHUNK), 1)
    return jnp.exp(jnp.where(q >= k, cum_ref[:, r:r + 1] - cumt_ref[r:r + 1, :], -1e30))


def _ssd_core_specs(nc, rev):
    ch = (lambda c: nc - 1 - c) if rev else (lambda c: c)
    xs = pl.BlockSpec((SSD_CHUNK, SSD_GW), lambda g, c: (ch(c), g))
    bspec = pl.BlockSpec((SSD_CHUNK, SSD_STATE), lambda g, c: (ch(c), SSD_B_OFF + g))
    cspec = pl.BlockSpec((SSD_CHUNK, SSD_STATE), lambda g, c: (ch(c), SSD_C_OFF + g))
    lane = pl.BlockSpec((None, SSD_CHUNK, LANES), lambda g, c: (g, ch(c), 0))
    rows = pl.BlockSpec((None, None, 16, SSD_CHUNK), lambda g, c: (g, ch(c), 0, 0))
    st = pl.BlockSpec((None, None, SSD_STATE, SSD_GW), lambda g, c: (g, ch(c), 0, 0))
    return xs, bspec, cspec, lane, rows, st


def _ssd_core_fwd(act, dtg, cumg, cumtg):
    length = act.shape[0]
    nc = length // SSD_CHUNK

    def body(x_ref, b_ref, c_ref, dt_ref, cum_ref, cumt_ref, y_ref, st_ref, s_scr):
        @pl.when(pl.program_id(1) == 0)
        def _():
            s_scr[...] = jnp.zeros_like(s_scr)

        e = _ssd_expand()
        bm = b_ref[...]
        cm = c_ref[...]
        cum_e = _ssd_to_channels(cum_ref[...], e)
        cl_e = cum_e[SSD_CHUNK - 1:SSD_CHUNK, :]
        xdt = x_ref[...] * _ssd_to_channels(dt_ref[...], e)
        st = s_scr[...]
        st_ref[...] = st
        g = _dg(cm, bm, "nt")
        y_off = _dg(cm, st, "nn") * jnp.exp(cum_e)
        for r in range(SSD_HPG):
            cols = slice(r * SSD_HEAD_DIM, (r + 1) * SSD_HEAD_DIM)
            y_ref[:, cols] = _dg(g * _ssd_decay(cum_ref, cumt_ref, r), xdt[:, cols], "nn") + y_off[:, cols]
        s_scr[...] = jnp.exp(cl_e) * st + _dg(bm, xdt * jnp.exp(cl_e - cum_e), "tn")

    xs, bspec, cspec, lane, rows, st = _ssd_core_specs(nc, False)
    return pl.pallas_call(
        body, name="ssd_core_fwd", grid=(SSD_GROUPS, nc),
        in_specs=[xs, bspec, cspec, lane, lane, rows], out_specs=[xs, st],
        out_shape=[jax.ShapeDtypeStruct((length, SSD_WIDTH), F32),
                   jax.ShapeDtypeStruct((SSD_GROUPS, nc, SSD_STATE, SSD_GW), F32)],
        scratch_shapes=[pltpu.VMEM((SSD_STATE, SSD_GW), F32)],
        compiler_params=_cparams("parallel", "arbitrary"),
    )(act, act, act, dtg, cumg, cumtg)


def _ssd_core_bwd(dy, dxs_add, act, dtg, cumg, cumtg, states):
    length = act.shape[0]
    nc = length // SSD_CHUNK

    def body(dy_ref, add_ref, x_ref, b_ref, c_ref, dt_ref, cum_ref, cumt_ref, st_ref,
             dx_ref, db_ref, dc_ref, ddt_ref, dcum_ref, dcumt_ref, dcl_ref, ds_scr, dxdt_scr):
        @pl.when(pl.program_id(1) == 0)
        def _():
            ds_scr[...] = jnp.zeros_like(ds_scr)

        e = _ssd_expand()
        bm = b_ref[...]
        cm = c_ref[...]
        x = x_ref[...]
        dyv = dy_ref[...]
        st = st_ref[...]
        dst_new = ds_scr[...]
        dt_e = _ssd_to_channels(dt_ref[...], e)
        cum_e = _ssd_to_channels(cum_ref[...], e)
        cl_e = cum_e[SSD_CHUNK - 1:SSD_CHUNK, :]
        xdt = x * dt_e
        exp_cum = jnp.exp(cum_e)
        exp_cl = jnp.exp(cl_e)
        z = xdt * jnp.exp(cl_e - cum_e)
        d_cs = dyv * exp_cum
        dcum_e = d_cs * _dg(cm, st, "nn")
        dcm = _dg(d_cs, st, "nt")
        ds_scr[...] = _dg(cm, d_cs, "tn") + exp_cl * dst_new
        dcl_e = _colsum(dst_new * st) * exp_cl
        dbm = _dg(z, dst_new, "nt")
        dz = _dg(bm, dst_new, "nn")
        de = dz * z
        dcl_e = dcl_e + _colsum(de)
        dcum_e = dcum_e - de
        dxdt_scr[...] = dz * jnp.exp(cl_e - cum_e)
        g = _dg(cm, bm, "nt")
        dg = jnp.zeros_like(g)
        dcum_ref[...] = _ssd_to_heads(dcum_e, e)
        dcumt_ref[...] = jnp.zeros_like(dcumt_ref)
        for r in range(SSD_HPG):
            cols = slice(r * SSD_HEAD_DIM, (r + 1) * SSD_HEAD_DIM)
            decay = _ssd_decay(cum_ref, cumt_ref, r)
            w = g * decay
            dy_r = dyv[:, cols]
            dw = _dg(dy_r, xdt[:, cols], "nt")
            dxdt_scr[:, cols] += _dg(w, dy_r, "tn")
            dg = dg + dw * decay
            dseg = dw * w
            dcum_ref[:, r:r + 1] += jnp.sum(dseg, axis=1, keepdims=True)
            dcumt_ref[r:r + 1, :] = -_colsum(dseg)
        dc_ref[...] = dcm + _dg(dg, bm, "nn")
        db_ref[...] = dbm + _dg(dg, cm, "tn")
        dxdt = dxdt_scr[...]
        dx_ref[...] = dxdt * dt_e + add_ref[...]
        ddt_ref[...] = _ssd_to_heads(dxdt * x, e)
        dcl_ref[...] = _ssd_to_heads(jnp.broadcast_to(dcl_e, (8, SSD_GW)), e)

    xs, bspec, cspec, lane, rows, st = _ssd_core_specs(nc, True)
    bc_out = pl.BlockSpec((SSD_CHUNK, SSD_STATE), lambda g, c: (nc - 1 - c, g))
    last = pl.BlockSpec((None, None, 8, LANES), lambda g, c: (g, nc - 1 - c, 0, 0))
    return pl.pallas_call(
        body, name="ssd_core_bwd", grid=(SSD_GROUPS, nc),
        in_specs=[xs, xs, xs, bspec, cspec, lane, lane, rows, st],
        out_specs=[xs, bc_out, bc_out, lane, lane, rows, last],
        out_shape=[jax.ShapeDtypeStruct((length, SSD_WIDTH), F32),
                   jax.ShapeDtypeStruct((length, SSD_GROUPS * SSD_STATE), F32),
                   jax.ShapeDtypeStruct((length, SSD_GROUPS * SSD_STATE), F32),
                   jax.ShapeDtypeStruct((SSD_GROUPS, length, LANES), F32),
                   jax.ShapeDtypeStruct((SSD_GROUPS, length, LANES), F32),
                   jax.ShapeDtypeStruct((SSD_GROUPS, nc, 16, SSD_CHUNK), F32),
                   jax.ShapeDtypeStruct((SSD_GROUPS, nc, 8, LANES), F32)],
        scratch_shapes=[pltpu.VMEM((SSD_STATE, SSD_GW), F32), pltpu.VMEM((SSD_CHUNK, SSD_GW), F32)],
        compiler_params=_cparams("parallel", "arbitrary"),
    )(dy, dxs_add, act, act, act, dtg, cumg, cumtg, states)


def _ssd_post_fn(yc, xs, z, dch, nw):
    y = (yc + dch * xs) * jax.nn.silu(z)
    return y * lax.rsqrt(jnp.mean(y * y, axis=-1, keepdims=True) + NORM_EPS) * nw


FOX_SCALE = 1.0 / math.sqrt(FOX_HEAD_DIM)
MASKED = -1e30


def _fgate_fwd(f_raw, b_f):
    length = f_raw.shape[0]
    nb = length // SSD_CHUNK

    def body(f_ref, b_ref, c_ref, carry):
        @pl.when(pl.program_id(0) == 0)
        def _():
            carry[...] = jnp.zeros_like(carry)

        c = _dot_exact(_tri(True), jax.nn.log_sigmoid(f_ref[...] + b_ref[...])) + carry[0:1, :]
        c_ref[...] = c
        carry[0:1, :] = c[SSD_CHUNK - 1:SSD_CHUNK, :]

    blk = pl.BlockSpec((SSD_CHUNK, LANES), lambda i: (i, 0))
    return pl.pallas_call(
        body, name="fox_fgate_fwd", grid=(nb,), in_specs=[blk, pl.BlockSpec((1, LANES), lambda i: (0, 0))],
        out_specs=blk, out_shape=jax.ShapeDtypeStruct((length, LANES), F32),
        scratch_shapes=[pltpu.VMEM((8, LANES), F32)], compiler_params=_cparams("arbitrary"),
    )(f_raw, b_f)


def _fgate_bwd(f_raw, b_f, dc):
    length = f_raw.shape[0]
    nb = length // SSD_CHUNK

    def body(f_ref, b_ref, dc_ref, df_ref, db_ref, carry):
        @pl.when(pl.program_id(0) == 0)
        def _():
            carry[...] = jnp.zeros_like(carry)
            db_ref[...] = jnp.zeros_like(db_ref)

        dcv = dc_ref[...]
        dlog = _dot_exact(_tri(False), dcv) + carry[0:1, :]
        carry[0:1, :] += _colsum(dcv)
        df = dlog * jax.nn.sigmoid(-(f_ref[...] + b_ref[...]))
        df_ref[...] = df.astype(df_ref.dtype)
        db_ref[...] += jnp.broadcast_to(_colsum(df), db_ref.shape)

    blk = pl.BlockSpec((SSD_CHUNK, LANES), lambda i: (nb - 1 - i, 0))
    return pl.pallas_call(
        body, name="fox_fgate_bwd", grid=(nb,),
        in_specs=[blk, pl.BlockSpec((1, LANES), lambda i: (0, 0)), blk],
        out_specs=[blk, pl.BlockSpec((8, LANES), lambda i: (0, 0))],
        out_shape=[jax.ShapeDtypeStruct((length, LANES), MXU_DT), jax.ShapeDtypeStruct((8, LANES), F32)],
        scratch_shapes=[pltpu.VMEM((8, LANES), F32)], compiler_params=_cparams("arbitrary"),
    )(f_raw, b_f, dc)


def _fox_scores(q, k, bias, diagonal):
    s = _dg(q, k, "nt") * FOX_SCALE + bias
    if diagonal:
        row = lax.broadcasted_iota(jnp.int32, s.shape, 0)
        col = lax.broadcasted_iota(jnp.int32, s.shape, 1)
        s = jnp.where(col <= row, s, MASKED)
    return s


FOX_TILE = 512


def _fox_c0(c, t):
    return jnp.repeat(jnp.repeat(c[::t, :FOX_HEADS], FOX_HEAD_DIM, axis=1), 8, axis=0)


def _fox_fwd(qkv, c0_rep, c_t, tile=FOX_TILE):
    length = qkv.shape[0]
    t = min(tile, length)
    nq = length // t

    def body(q_ref, k_ref, v_ref, c0_ref, ct_ref, o_ref, lse_ref):
        i = pl.program_id(1)
        q = q_ref[...]
        c0 = c0_ref[0:1, 0:1]

        half = t // 2

        def tile_step(k0, carry, diagonal):
            kt = k_ref[pl.ds(k0, t), :]
            vt = v_ref[pl.ds(k0, t), :]
            bias = c0 - ct_ref[:, pl.ds(k0, t)]
            out = []
            for h in range(2):
                m, l, acc = carry[h]
                s = _dg(q[h * half:(h + 1) * half], kt, "nt") * FOX_SCALE + bias
                if diagonal:
                    row = h * half + lax.broadcasted_iota(jnp.int32, s.shape, 0)
                    s = jnp.where(lax.broadcasted_iota(jnp.int32, s.shape, 1) <= row, s, MASKED)
                m_new = jnp.maximum(m, jnp.max(s, axis=1, keepdims=True))
                p = jnp.exp(s - m_new)
                alpha = jnp.exp(m - m_new)
                out.append((m_new, alpha * l + jnp.sum(p, axis=1, keepdims=True), alpha * acc + _dg(p, vt, "nn")))
            return tuple(out)

        strip = (jnp.full((half, 1), MASKED, F32), jnp.zeros((half, 1), F32), jnp.zeros((half, FOX_HEAD_DIM), F32))
        carry = lax.fori_loop(0, i, lambda j, c: tile_step(pl.multiple_of(j * t, t), c, False), (strip, strip))
        for h, (m, l, acc) in enumerate(tile_step(pl.multiple_of(i * t, t), carry, True)):
            o_ref[h * half:(h + 1) * half, :] = acc / l
            lse_ref[h * half:(h + 1) * half, :] = jnp.broadcast_to(m + jnp.log(l), (half, FOX_HEAD_DIM))

    qt = pl.BlockSpec((t, FOX_HEAD_DIM), lambda h, i: (i, h))
    return pl.pallas_call(
        body, name="fox_attn_fwd", grid=(FOX_HEADS, nq),
        in_specs=[qt,
                  pl.BlockSpec((length, FOX_HEAD_DIM), lambda h, i: (0, FOX_HEADS + h)),
                  pl.BlockSpec((length, FOX_HEAD_DIM), lambda h, i: (0, 2 * FOX_HEADS + h)),
                  pl.BlockSpec((8, FOX_HEAD_DIM), lambda h, i: (i, h)),
                  pl.BlockSpec((None, 1, length), lambda h, i: (h, 0, 0))],
        out_specs=[qt, qt],
        out_shape=[jax.ShapeDtypeStruct((length, FOX_WIDTH), F32)] * 2,
        compiler_params=_cparams("parallel", "arbitrary"),
    )(qkv, qkv, qkv, c0_rep, c_t)


def _fox_bwd(qkv, d_att, lse, delta, c0_rep, c_t, tile=FOX_TILE):
    length = qkv.shape[0]
    t = min(tile, length)
    nk = length // t

    def body(q_ref, k_ref, v_ref, do_ref, lse_ref, dl_ref, c0_ref, ct_ref,
             dq_ref, dk_ref, dv_ref, dcq_ref, dck_ref, dq_acc):
        j = pl.program_id(1)

        @pl.when(j == 0)
        def _():
            dq_acc[...] = jnp.zeros_like(dq_acc)
            dcq_ref[...] = jnp.zeros_like(dcq_ref)

        k = k_ref[...]
        v = v_ref[...]
        ck = ct_ref[...]

        def tile_step(i, carry, diagonal):
            dk, dv, dck = carry
            rows = pl.ds(pl.multiple_of(i * t, t), t)
            q = q_ref[rows, :]
            do = do_ref[rows, :]
            c0 = c0_ref[pl.ds(pl.multiple_of(i * 8, 8), 8), :][0:1, 0:1]
            s = _fox_scores(q, k, c0 - ck, diagonal)
            p = jnp.exp(s - lse_ref[rows, 0:1])
            dv = dv + _dg(p, do, "tn")
            ds = p * (_dg(do, v, "nt") - dl_ref[rows, 0:1])
            dk = dk + _dg(ds, q, "tn") * FOX_SCALE
            dq_acc[rows, :] += _dg(ds, k, "nn") * FOX_SCALE
            dcq_ref[rows, :] += jnp.broadcast_to(jnp.sum(ds, axis=1, keepdims=True), (t, FOX_HEAD_DIM))
            return dk, dv, dck + _colsum(ds)

        init = (jnp.zeros((t, FOX_HEAD_DIM), F32), jnp.zeros((t, FOX_HEAD_DIM), F32), jnp.zeros((1, t), F32))
        carry = tile_step(j, init, True)
        dk, dv, dck = lax.fori_loop(j + 1, nk, lambda i, c: tile_step(i, c, False), carry)
        dk_ref[...] = dk.astype(dk_ref.dtype)
        dv_ref[...] = dv.astype(dv_ref.dtype)
        dck_ref[...] = -dck

        @pl.when(j == nk - 1)
        def _():
            dq_ref[...] = dq_acc[...].astype(dq_ref.dtype)

    full = lambda off: pl.BlockSpec((length, FOX_HEAD_DIM), lambda h, j: (0, off + h))
    kt = lambda off: pl.BlockSpec((t, FOX_HEAD_DIM), lambda h, j: (j, off + h))
    ck_spec = pl.BlockSpec((None, 1, t), lambda h, j: (h, 0, j))
    return pl.pallas_call(
        body, name="fox_attn_bwd", grid=(FOX_HEADS, nk),
        in_specs=[full(0), kt(FOX_HEADS), kt(2 * FOX_HEADS), full(0), full(0), full(0),
                  pl.BlockSpec((8 * nk, FOX_HEAD_DIM), lambda h, j: (0, h)), ck_spec],
        out_specs=[full(0), kt(0), kt(0), full(0), ck_spec],
        out_shape=[jax.ShapeDtypeStruct((length, FOX_WIDTH), MXU_DT)] * 3
        + [jax.ShapeDtypeStruct((length, FOX_WIDTH), F32), jax.ShapeDtypeStruct((FOX_HEADS, 1, length), F32)],
        scratch_shapes=[pltpu.VMEM((length, FOX_HEAD_DIM), F32)],
        compiler_params=_cparams("parallel", "arbitrary"),
    )(qkv, qkv, qkv, d_att, lse, delta, c0_rep, c_t)


def _fox_gate_fn(att, gate):
    return att * jax.nn.silu(gate)


N_CHIP = 4


def _other_chips(mx, my):
    return [(1 - mx, my), (mx, 1 - my), (1 - mx, 1 - my)]


def _handshake(peers):
    barrier = pltpu.get_barrier_semaphore()
    for peer in peers:
        pl.semaphore_signal(barrier, inc=1, device_id=peer, device_id_type=pl.DeviceIdType.MESH)
    pl.semaphore_wait(barrier, len(peers))


def _exchange_call(body, x, out_struct, name, n_sems, local_sem, collective_id):
    sems = [pltpu.SemaphoreType.DMA((n_sems,)), pltpu.SemaphoreType.DMA((n_sems,))]
    sems += [pltpu.SemaphoreType.DMA] if local_sem else []
    if collective_id is None:
        return pl.pallas_call(
            body, name=name, in_specs=[ANYSPACE], out_specs=ANYSPACE, out_shape=out_struct, scratch_shapes=sems,
            compiler_params=pltpu.CompilerParams(has_side_effects=True),
        )(x)
    x_ref = jax.new_ref(x, memory_space=pltpu.MemorySpace.HBM)
    o_ref = jax.empty_ref(out_struct, memory_space=pltpu.MemorySpace.HBM)

    @pl.kernel(mesh=plsc.ScalarSubcoreMesh(axis_name="sequencer", num_cores=1), name=name,
               scratch_types=tuple(sems), compiler_params=pltpu.CompilerParams(collective_id=collective_id))
    def launch(*sem_refs):
        body(x_ref, o_ref, *sem_refs)

    launch()
    return o_ref[...]


def _gather(x, name, collective_id=None):
    def body(x_ref, o_ref, send_sems, recv_sems, local_sem):
        mx, my, mc = lax.axis_index("x"), lax.axis_index("y"), lax.axis_index("c")
        me, sibling = (mx, my, mc), (mx, my, 1 - mc)
        chips = _other_chips(mx, my)
        if collective_id is not None:
            _handshake([sibling] + [(*chip, mc) for chip in chips])

        def slot(px, py, pc):
            return o_ref.at[4 * px + 2 * py + pc]

        def copy(k, block, to, src=None):
            return pltpu.make_async_remote_copy(
                src_ref=slot(*block) if src is None else src, dst_ref=slot(*block),
                send_sem=send_sems.at[k], recv_sem=recv_sems.at[k],
                device_id=to, device_id_type=pl.DeviceIdType.MESH)

        mine = pltpu.make_async_copy(x_ref, slot(*me), local_sem)
        mine.start()
        first = [copy(0, me, sibling, src=x_ref)]
        first += [copy(1 + j, me, (*chip, mc), src=x_ref) for j, chip in enumerate(chips)]
        for cp in first:
            cp.start()
        passed = [copy(4 + j, (*chip, mc), sibling) for j, chip in enumerate(chips)]
        for j, chip in enumerate(chips):
            copy(1 + j, (*chip, mc), me).wait_recv()
            passed[j].start()
        copy(0, sibling, me).wait_recv()
        for j, chip in enumerate(chips):
            copy(4 + j, (*chip, 1 - mc), me).wait_recv()
        for cp in first + passed:
            cp.wait_send()
        mine.wait()

    return _exchange_call(body, x, jax.ShapeDtypeStruct((N_DEV,) + x.shape, x.dtype), name, N_DEV - 1, True,
                          collective_id)


def _pair_send(parts, name, collective_id=None):
    def body(p_ref, o_ref, send_sems, recv_sems):
        mx, my, mc = lax.axis_index("x"), lax.axis_index("y"), lax.axis_index("c")
        if collective_id is not None:
            _handshake([(mx, my, 1 - mc)])
        copies = [pltpu.make_async_remote_copy(
            src_ref=p_ref.at[2 * chip + (1 - mc)], dst_ref=o_ref.at[chip],
            send_sem=send_sems.at[chip], recv_sem=recv_sems.at[chip],
            device_id=(mx, my, 1 - mc), device_id_type=pl.DeviceIdType.MESH) for chip in range(N_CHIP)]
        for cp in copies:
            cp.start()
        for cp in copies:
            cp.wait_recv()
        for cp in copies:
            cp.wait_send()

    return _exchange_call(body, parts, jax.ShapeDtypeStruct((N_CHIP,) + parts.shape[1:], parts.dtype), name,
                          N_CHIP, False, collective_id)


def _pair_sum(parts, recv, out_dtype, name):
    _, rows, cols = parts.shape
    tr, tc = _tile2d(rows, cols)

    def body(c_ref, p_ref, r_ref, o_ref):
        o_ref[...] = (p_ref[...] + r_ref[...]).astype(o_ref.dtype)

    return pl.pallas_call(
        body, name=name,
        grid_spec=pltpu.PrefetchScalarGridSpec(
            num_scalar_prefetch=1, grid=(N_CHIP, rows // tr, cols // tc),
            in_specs=[pl.BlockSpec((None, tr, tc), lambda k, i, j, c: (2 * k + c[0], i, j)),
                      pl.BlockSpec((None, tr, tc), lambda k, i, j, c: (k, i, j))],
            out_specs=pl.BlockSpec((None, tr, tc), lambda k, i, j, c: (k, i, j))),
        out_shape=jax.ShapeDtypeStruct((N_CHIP, rows, cols), out_dtype),
        compiler_params=_cparams("parallel", "parallel", "parallel"),
    )(lax.axis_index("c").astype(jnp.int32).reshape(1), parts, recv)


def _chip_exchange(sums, name, collective_id=None):
    def body(s_ref, o_ref, send_sems, recv_sems, local_sem):
        mx, my, mc = lax.axis_index("x"), lax.axis_index("y"), lax.axis_index("c")
        my_chip = 2 * mx + my
        if collective_id is not None:
            _handshake([(px, py, mc) for px, py in _other_chips(mx, my)])
        local = pltpu.make_async_copy(s_ref.at[my_chip], o_ref.at[my_chip], local_sem)
        local.start()
        sends, recvs = [], []
        for k, (px, py) in enumerate(_other_chips(mx, my)):
            peer = 2 * px + py

            def copy(src_slot, dst_slot, k=k, dev=(px, py, mc)):
                return pltpu.make_async_remote_copy(
                    src_ref=s_ref.at[src_slot], dst_ref=o_ref.at[dst_slot], send_sem=send_sems.at[k],
                    recv_sem=recv_sems.at[k], device_id=dev, device_id_type=pl.DeviceIdType.MESH)

            sends.append(copy(peer, my_chip))
            recvs.append(copy(peer, peer))
        for cp in sends:
            cp.start()
        for cp in recvs:
            cp.wait_recv()
        for cp in sends:
            cp.wait_send()
        local.wait()

    return _exchange_call(body, sums, jax.ShapeDtypeStruct(sums.shape, sums.dtype), name, N_CHIP - 1, True,
                          collective_id)


ADAM_TILE_ELEMS = 128 * 1024


def _tile2d(rows, cols):
    if rows * cols <= ADAM_TILE_ELEMS:
        return rows, cols
    if rows % 8 == 0:
        return _pick(rows, max(8, ADAM_TILE_ELEMS // cols), 8), cols
    return rows, _pick(cols, max(LANES, ADAM_TILE_ELEMS // rows))


def _adamw(w, parts, m, v, name):
    rows, cols = w.shape
    n_parts = parts.shape[0]
    tr, tc = _tile2d(rows, cols)

    def body(w_ref, p_ref, m_ref, v_ref, g_ref, d_ref, nm_ref, nv_ref):
        g = p_ref[0].astype(F32)
        for p in range(1, n_parts):
            g = g + p_ref[p].astype(F32)
        mm = ADAM_B1 * m_ref[...] + (1.0 - ADAM_B1) * g
        vv = ADAM_B2 * v_ref[...] + (1.0 - ADAM_B2) * jnp.square(g)
        m_hat = mm / (1.0 - ADAM_B1 ** ADAM_STEP)
        v_hat = vv / (1.0 - ADAM_B2 ** ADAM_STEP)
        g_ref[...] = g
        d_ref[...] = -ADAM_LR * (m_hat / (jnp.sqrt(v_hat) + ADAM_EPS) + ADAM_WD * w_ref[...])
        nm_ref[...] = mm
        nv_ref[...] = vv

    blk = pl.BlockSpec((tr, tc), lambda i, j: (i, j))
    return pl.pallas_call(
        body, name=name, grid=(rows // tr, cols // tc),
        in_specs=[blk, pl.BlockSpec((n_parts, tr, tc), lambda i, j: (0, i, j)), blk, blk],
        out_specs=[blk] * 4, out_shape=[jax.ShapeDtypeStruct((rows, cols), F32)] * 4,
        compiler_params=_cparams("parallel", "parallel"),
    )(w, parts, m, v)


WEIGHTS = ("l0_norm_w", "l0_w_in", "l0_s5_lambda_re", "l0_s5_lambda_im", "l0_s5_log_step", "l0_s5_b_re",
           "l0_s5_b_im", "l0_s5_c_re", "l0_s5_c_im", "l0_s5_d", "l0_s5_w_glu", "l0_s5_b_glu", "l0_ssd_conv_w",
           "l0_ssd_conv_b", "l0_ssd_dt_bias", "l0_ssd_a_log", "l0_ssd_d", "l0_ssd_norm_w", "l0_w_out",
           "l1_norm_w", "l1_w_in", "l1_fox_b_f", "l1_w_out", "final_norm_w")
SHARDED = ("l0_w_in", "l0_s5_w_glu", "l0_ssd_conv_w", "l0_w_out", "l1_w_in", "l1_w_out")


def _pad_lanes(a, width=LANES):
    return jnp.pad(a, [(0, 0)] * (a.ndim - 1) + [(0, width - a.shape[-1])])


def _pad_rows(a, height=LANES):
    return jnp.pad(a, [(0, height - a.shape[0])] + [(0, 0)] * (a.ndim - 1))


def _to_groups(a):
    length = a.shape[0]
    return _pad_lanes(a[:, :SSD_HEADS].reshape(length, SSD_GROUPS, SSD_HPG).transpose(1, 0, 2))


def _from_groups(a):
    length = a.shape[1]
    return a[:, :, :SSD_HPG].transpose(1, 0, 2).reshape(length, SSD_HEADS)


class _Grads(dict):
    def __init__(self, on_grad):
        super().__init__()
        self.on_grad = on_grad
        self.raw = {}
        self.sums = {}

    def __setitem__(self, name, value):
        self.store(name, value)

    def store(self, name, value, between=None):
        if self.on_grad is not None and name in SHARDED:
            self.raw[name] = value
            self.sums[name], value = self.on_grad(name, value, between)
        elif between is not None:
            between()
        super().__setitem__(name, value)

    def before(self, x, stage, name):
        if self.on_grad is None:
            return x
        made = self if stage == "done" else getattr(self, stage)
        return lax.optimization_barrier((x, made[name]))[0]


def _local_step(x, target, w, on_grad=None):
    length = x.shape[0]
    nc = length // SSD_CHUNK
    g = _Grads(on_grad)

    h0 = _rmsnorm_fwd(x, w["l0_norm_w"], "l0_norm")
    w0 = w["l0_w_in"]
    r0_s5, r0_z, r0_xbc = (0, 2 * S5_WIDTH), (2 * S5_WIDTH, SSD_WIDTH), (2 * S5_WIDTH + SSD_WIDTH, SSD_XBC)
    r0_dt = 2 * S5_WIDTH + SSD_WIDTH + SSD_XBC
    w0_dt = _pad_rows(w0[r0_dt:])
    p_s5 = _mm(h0, w0, "nt", F32, "l0_in_s5", b_rows=r0_s5)
    p_z = _mm(h0, w0, "nt", F32, "l0_in_z", b_rows=r0_z)
    p_xbc = _mm(h0, w0, "nt", F32, "l0_in_xbc", b_rows=r0_xbc)
    p_dt = _mm(h0, w0_dt, "nt", F32, "l0_in_dt")

    row = lambda a: a.reshape(1, S5_NS)
    b_rows = lambda a: a.transpose(2, 0, 1).reshape(S5_GROUP, S5_NS)
    prep_in = (row(w["l0_s5_lambda_re"]), row(w["l0_s5_lambda_im"]),
               row(jnp.repeat(w["l0_s5_log_step"], S5_STATE)), b_rows(w["l0_s5_b_re"]), b_rows(w["l0_s5_b_im"]))
    ab_re, ab_im, bbr, bbi = _s5_prep(*prep_in)
    to_bb = lambda a: _blockdiag(a.reshape(S5_GROUP, S5_GROUPS, S5_STATE).transpose(1, 0, 2),
                                 S5_GROUP, S5_STATE).astype(MXU_DT)
    to_ct = lambda a: _blockdiag(a.transpose(0, 2, 1), S5_STATE, S5_GROUP).astype(MXU_DT)
    bb_re, bb_im = to_bb(bbr), to_bb(bbi)
    ct_re, ct_im = to_ct(w["l0_s5_c_re"]), to_ct(w["l0_s5_c_im"])
    a_re3, a_im3 = ab_re.reshape(S5_BLK, 1, S5_BS), ab_im.reshape(S5_BLK, 1, S5_BS)
    d3 = w["l0_s5_d"].reshape(S5_BLK, 1, S5_BC)
    y5, s_re, s_im = _s5_fwd(p_s5, bb_re, bb_im, ct_re, ct_im, a_re3, a_im3, d3)
    g_bf = _tiles("s5_gelu", lambda yb: (_gelu(yb),), [y5], [], [(S5_WIDTH, MXU_DT)], 0, tr=256)[0]
    zg = _mm(g_bf, w["l0_s5_w_glu"], "nn", F32, "s5_glu")
    b_glu = w["l0_s5_b_glu"].reshape(1, -1)
    s5_out = _tiles("s5_out", lambda yb, zb, gb, bb: (_s5_out_fn(yb, zb, gb, bb),),
                    [y5, zg, (p_s5, 1)], [b_glu], [(S5_WIDTH, MXU_DT)], 0, tr=256)[0]

    conv_w = w["l0_ssd_conv_w"]
    cv, act = _conv_fwd(p_xbc, conv_w, w["l0_ssd_conv_b"])
    bias_row = _pad_lanes(w["l0_ssd_dt_bias"].reshape(1, -1))
    alog_row = _pad_lanes(w["l0_ssd_a_log"].reshape(1, -1))
    dt, cum, cum_t = _dt_fwd(p_dt, bias_row, alog_row)
    dtg, cumg = _to_groups(dt), _to_groups(cum)
    cumtg = cum_t[:, :SSD_HEADS].reshape(nc, SSD_GROUPS, SSD_HPG, SSD_CHUNK).transpose(1, 0, 2, 3)
    cumtg = jnp.pad(cumtg, ((0, 0), (0, 0), (0, 16 - SSD_HPG), (0, 0)))
    ycore, states = _ssd_core_fwd(act, dtg, cumg, cumtg)
    dchan = jnp.repeat(w["l0_ssd_d"], SSD_HEAD_DIM).reshape(1, -1)
    nw_row = w["l0_ssd_norm_w"].reshape(1, -1)
    ssd_out = _tiles("ssd_post", lambda a, b, c, d, e: (_ssd_post_fn(a, b, c, d, e),),
                     [ycore, act, p_z], [dchan, nw_row], [(SSD_WIDTH, MXU_DT)], 0, tr=256, tc=SSD_GW)[0]
    mixed = jnp.concatenate([s5_out, ssd_out], axis=1)
    x1 = _mm(mixed, w["l0_w_out"], "nn", F32, "l0_out", res=x)

    h1 = _rmsnorm_fwd(x1, w["l1_norm_w"], "l1_norm")
    w1 = w["l1_w_in"]
    r1_qkv, r1_gate = (0, 3 * FOX_WIDTH), (3 * FOX_WIDTH, FOX_WIDTH)
    w1_f = _pad_rows(w1[4 * FOX_WIDTH:])
    qkv = _mm(h1, w1, "nt", MXU_DT, "l1_in_qkv", b_rows=r1_qkv)
    gate1 = _mm(h1, w1, "nt", F32, "l1_in_gate", b_rows=r1_gate)
    f_raw = _mm(h1, w1_f, "nt", F32, "l1_in_f")
    bf_row = _pad_lanes(w["l1_fox_b_f"].reshape(1, -1))
    c = _fgate_fwd(f_raw, bf_row)
    c0_rep = _fox_c0(c, min(FOX_TILE, length))
    c_t = c[:, :FOX_HEADS].T.reshape(FOX_HEADS, 1, length)
    att, lse = _fox_fwd(qkv, c0_rep, c_t)
    out1 = _tiles("fox_gate", lambda a, b: (_fox_gate_fn(a, b),), [att, gate1], [],
                  [(FOX_WIDTH, MXU_DT)], 0, tr=256)[0]
    x2 = _mm(out1, w["l1_w_out"], "nn", F32, "l1_out", res=x1)

    loss_part, dx2, dx2b, g["final_norm_w"] = _final_loss(x2, w["final_norm_w"], target, "final_loss")

    d_out1 = _mm(dx2b, w["l1_w_out"], "nt", F32, "l1_out_dx")
    g["l1_w_out"] = _mm(out1, dx2b, "tn", F32, "l1_out_dw")

    def gate_bwd(a, gt, d):
        _, vjp = jax.vjp(_fox_gate_fn, a, gt)
        da, dgt = vjp(d)
        return da, dgt, jnp.broadcast_to(jnp.sum(da * a, axis=1, keepdims=True), a.shape)

    d_att, d_gate1, delta = _tiles("fox_gate_bwd", gate_bwd, [att, gate1, d_out1], [],
                                   [(FOX_WIDTH, MXU_DT), (FOX_WIDTH, MXU_DT), (FOX_WIDTH, F32)], 0,
                                   tr=512, tc=FOX_HEAD_DIM)
    dq, dk, dv, dcq, dck = _fox_bwd(qkv, g.before(d_att, "raw", "l1_w_out"), lse, delta, c0_rep, c_t)
    dc = dcq.reshape(length, FOX_HEADS, FOX_HEAD_DIM)[:, :, 0] + dck.reshape(FOX_HEADS, length).T
    df, dbf = _fgate_bwd(f_raw, bf_row, _pad_lanes(dc))
    g["l1_fox_b_f"] = dbf[0, :FOX_HEADS]
    pieces = list(zip(("q", "k", "v", "gate"), (dq, dk, dv, d_gate1)))
    dh1 = None
    for i, (tag, d_piece) in enumerate(pieces):
        dh1 = _mm(d_piece, w1, "nn", F32, "l1_in_dx_" + tag, res=dh1, b_rows=(i * FOX_WIDTH, FOX_WIDTH))
    dh1 = _mm(df, w1_f, "nn", F32, "l1_in_dx_f", res=dh1)
    dw1 = lax.empty((ODD_IN, D_MODEL), F32)
    for i, (tag, d_piece) in enumerate(pieces):
        dw1 = _mm(d_piece, h1, "tn", F32, "l1_in_dw_" + tag, into=(dw1, i * FOX_WIDTH))
    g["l1_w_in"] = dw1.at[4 * FOX_WIDTH:].set(_mm(df, h1, "tn", F32, "l1_in_dw_f")[:FOX_HEADS])
    dh1 = g.before(g.before(dh1, "raw", "l1_w_in"), "done", "l1_w_out")
    dx1, dx1b, g["l1_norm_w"] = _rmsnorm_bwd(x1, w["l1_norm_w"], dh1, dx2, "l1_norm_bwd")

    wout0 = w["l0_w_out"]
    d_s5 = _mm(dx1b, wout0, "nt", F32, "l0_out_dx_s5", b_rows=(0, S5_WIDTH))
    d_ssd = _mm(dx1b, wout0, "nt", F32, "l0_out_dx_ssd", b_rows=(S5_WIDTH, SSD_WIDTH))
    g["l0_w_out"] = _mm(mixed, dx1b, "tn", F32, "l0_out_dw")
    d_ssd = g.before(d_ssd, "sums", "l1_w_in")

    def post_bwd(a, b, c_, d, dch, nw):
        _, vjp = jax.vjp(_ssd_post_fn, a, b, c_, dch, nw)
        return vjp(d)

    dycore, dxs_post, dz, ddch, dnw = _tiles(
        "ssd_post_bwd", post_bwd, [ycore, act, p_z, d_ssd], [dchan, nw_row],
        [(SSD_WIDTH, F32), (SSD_WIDTH, F32), (SSD_WIDTH, MXU_DT)], 2, tr=256, tc=SSD_GW)
    g["l0_ssd_d"] = ddch[0].reshape(SSD_HEADS, SSD_HEAD_DIM).sum(axis=1)
    g["l0_ssd_norm_w"] = dnw[0]
    dycore = g.before(dycore, "raw", "l0_w_out")
    dxs, d_b, d_c, ddtg, dcumg, dcumtg, dclg = _ssd_core_bwd(dycore, dxs_post, act, dtg, cumg, cumtg, states)
    dact = g.before(jnp.concatenate([dxs, d_b, d_c], axis=1), "sums", "l0_w_out")
    dxbc, dconvw, dconvb = _conv_bwd(dact, cv, p_xbc, conv_w)
    g["l0_ssd_conv_w"] = dconvw[:SSD_CONV]
    g["l0_ssd_conv_b"] = dconvb[0]
    dcum = _from_groups(dcumg)
    dcum = dcum + dcumtg[:, :, :SSD_HPG].transpose(1, 3, 0, 2).reshape(length, SSD_HEADS)
    dcl = dclg[:, :, 0, :SSD_HPG].transpose(1, 0, 2).reshape(nc, SSD_HEADS)
    dcum = dcum.reshape(nc, SSD_CHUNK, SSD_HEADS).at[:, SSD_CHUNK - 1, :].add(dcl).reshape(length, SSD_HEADS)
    ddt_raw, dbias, dalog = _dt_bwd(p_dt, bias_row, alog_row, _pad_lanes(_from_groups(ddtg)), _pad_lanes(dcum))
    g["l0_ssd_dt_bias"] = dbias[0, :SSD_HEADS]
    g["l0_ssd_a_log"] = dalog[0, :SSD_HEADS]

    def s5_out_bwd(yb, zb, gb, d, bb):
        _, vjp = jax.vjp(_s5_out_fn, yb, zb, gb, bb)
        return vjp(d)

    dy_direct, dzg, dgate0, dbglu = _tiles(
        "s5_out_bwd", s5_out_bwd, [y5, zg, (p_s5, 1), d_s5], [b_glu],
        [(S5_WIDTH, F32), (S5_WIDTH, MXU_DT), (S5_WIDTH, MXU_DT)], 1, tr=256)
    g["l0_s5_b_glu"] = dbglu[0]
    g["l0_s5_w_glu"] = _mm(g_bf, dzg, "tn", F32, "s5_glu_dw")
    dg2 = _mm(dzg, w["l0_s5_w_glu"], "nt", F32, "s5_glu_dx")

    def gelu_bwd(yb, d, direct):
        _, vjp = jax.vjp(_gelu, yb)
        return (vjp(d)[0] + direct,)

    dy5 = _tiles("s5_gelu_bwd", gelu_bwd, [y5, dg2, dy_direct], [], [(S5_WIDTH, F32)], 0, tr=256)[0]
    dy5 = g.before(dy5, "done", "l1_w_in")
    du, dbbr3, dbbi3, dctr3, dcti3, dar, dai, dd5 = _s5_bwd(dy5, p_s5, s_re, s_im, bb_re, bb_im, ct_re, ct_im,
                                                           a_re3, a_im3, d3)
    from_bb = lambda a: _blockdiag_t(a, S5_GROUP, S5_STATE).transpose(1, 0, 2).reshape(S5_GROUP, S5_NS)
    from_ct = lambda a: _blockdiag_t(a, S5_STATE, S5_GROUP).transpose(0, 2, 1)
    g["l0_s5_c_re"], g["l0_s5_c_im"] = from_ct(dctr3), from_ct(dcti3)
    g["l0_s5_d"] = dd5[:, 0, :].reshape(S5_GROUPS, S5_GROUP)
    dlr, dli, dls, dbr, dbi = _s5_prep_bwd(*prep_in, dar[:, 0, :].reshape(1, S5_NS), dai[:, 0, :].reshape(1, S5_NS),
                                           from_bb(dbbr3), from_bb(dbbi3))
    g["l0_s5_lambda_re"] = dlr.reshape(S5_GROUPS, S5_STATE)
    g["l0_s5_lambda_im"] = dli.reshape(S5_GROUPS, S5_STATE)
    g["l0_s5_log_step"] = dls.reshape(S5_GROUPS, S5_STATE).sum(axis=1)
    from_rows = lambda a: a.reshape(S5_GROUP, S5_GROUPS, S5_STATE).transpose(1, 2, 0)
    g["l0_s5_b_re"], g["l0_s5_b_im"] = from_rows(dbr), from_rows(dbi)

    dus = g.before(jnp.concatenate([du, dgate0], axis=1), "done", "l0_w_out")
    dw0 = _mm(dus, h0, "tn", F32, "l0_in_dw_s5", into=(lax.empty((EVEN_IN, D_MODEL), F32), r0_s5[0]))
    dw0 = _mm(dz, h0, "tn", F32, "l0_in_dw_z", into=(dw0, r0_z[0]))
    dw0 = _mm(dxbc, h0, "tn", F32, "l0_in_dw_xbc", into=(dw0, r0_xbc[0]))
    dw0 = dw0.at[r0_dt:].set(_mm(ddt_raw, h0, "tn", F32, "l0_in_dw_dt")[:SSD_HEADS])
    first = {}

    def first_half():
        dus_late = lax.optimization_barrier((dus, dw0))[0] if on_grad is not None else dus
        dh = _mm(dus_late, w0, "nn", F32, "l0_in_dx_s5", b_rows=r0_s5)
        first["dh0"] = _mm(dz, w0, "nn", F32, "l0_in_dx_z", res=dh, b_rows=r0_z)
        return first["dh0"]

    g.store("l0_w_in", dw0, between=first_half)
    dh0 = _mm(dxbc, w0, "nn", F32, "l0_in_dx_xbc", res=g.before(first["dh0"], "sums", "l0_w_in"), b_rows=r0_xbc)
    dh0 = _mm(ddt_raw, w0_dt, "nn", F32, "l0_in_dx_dt", res=dh0)
    grad_x, _, g["l0_norm_w"] = _rmsnorm_bwd(x, w["l0_norm_w"], dh0, dx1, "l0_norm_bwd")
    return loss_part, grad_x, g


TRANSPOSED = ("l0_w_in", "l1_w_in")


SEQUENCER_IDS = {"l0_w_out": (0, 1, 2), "l1_w_in": (3, 4, 5), "l1_w_out": (6, 7, 8), "l0_w_in": (None, 9, 10),
                 "l0_s5_w_glu": (11, 12, 13), "l0_ssd_conv_w": (14, 15, 16)}
NO_IDS = (None, None, None)


def _gather_weight(name, shard):
    cid = SEQUENCER_IDS.get(name, NO_IDS)[0]
    if name == "l0_ssd_conv_w":
        full = _gather(shard, "gather_" + name, cid)
        return full.transpose(1, 0, 2).reshape(shard.shape[0], N_DEV * shard.shape[1])
    if name in TRANSPOSED:
        full = _gather(shard.T.astype(MXU_DT), "gather_" + name, cid)
        return full.reshape(N_DEV * shard.shape[1], shard.shape[0])
    full = _gather(shard.astype(MXU_DT), "gather_" + name, cid)
    return full.reshape(N_DEV * shard.shape[0], shard.shape[1])


def _reduce_grad(name, grad, shard_shape, between=None):
    rows, cols = shard_shape
    if name == "l0_ssd_conv_w":
        parts = grad.reshape(rows, N_DEV, cols).transpose(1, 0, 2)
    elif name in TRANSPOSED:
        parts = grad.reshape(N_DEV, cols, rows)
    else:
        parts = grad.reshape(N_DEV, rows, cols)
    _, pair_id, chip_id = SEQUENCER_IDS.get(name, NO_IDS)
    recv = _pair_send(parts, "pair_" + name, pair_id)
    if between is not None:
        parts = lax.optimization_barrier((parts, between()))[0]
    sums = _pair_sum(parts, recv, F32 if name == "l0_ssd_conv_w" else MXU_DT, "pairsum_" + name)
    return sums, _chip_exchange(sums, "scatter_" + name, chip_id)


SMALL_ROWS_QUANTUM = 8 * LANES


def _step(args):
    x = args["x"][0]
    target = args["loss_target"][0]
    full = {n: args[n] for n in WEIGHTS if n not in SHARDED}
    for n in SHARDED:
        shard = args[n]
        if n != SHARDED[0]:
            shard, full[SHARDED[0]] = lax.optimization_barrier((shard, full[SHARDED[0]]))
        full[n] = _gather_weight(n, shard)
    out_g, out_d, out_m, out_v = {}, {}, {}, {}

    loss_part, grad_x, g = _local_step(
        x, target, full, lambda n, grad, between: _reduce_grad(n, grad, args[n].shape, between))
    for n in reversed(SHARDED):
        view = (lambda a: a.T) if n in TRANSPOSED else (lambda a: a)
        outs = _adamw(view(args[n]), g[n], view(args["m_" + n]), view(args["v_" + n]), "adamw_" + n)
        out_g[n], out_d[n], out_m[n], out_v[n] = [view(o) for o in outs]

    small = [n for n in WEIGHTS if n not in SHARDED]
    sizes = [int(math.prod(args[n].shape)) for n in small]
    total = sum(sizes) + 1
    padded = -(-total // SMALL_ROWS_QUANTUM) * SMALL_ROWS_QUANTUM

    def pack(pieces, extra):
        flat = jnp.concatenate([p.reshape(-1).astype(F32) for p in pieces] + [extra.reshape(1)])
        return jnp.pad(flat, (0, padded - total)).reshape(padded // LANES, LANES)

    zero = jnp.zeros((), F32)
    parts = _gather(pack([g[n] for n in small], loss_part), "gather_small_grads")
    sg, sd, sm, sv = _adamw(pack([args[n] for n in small], zero), parts,
                            pack([args["m_" + n] for n in small], zero),
                            pack([args["v_" + n] for n in small], zero), "adamw_small")
    off = 0
    for n, sz in zip(small, sizes):
        cut = lambda a: a.reshape(-1)[off:off + sz].reshape(args[n].shape)
        out_g[n], out_d[n], out_m[n], out_v[n] = cut(sg), cut(sd), cut(sm), cut(sv)
        off += sz
    loss = sg.reshape(-1)[total - 1]
    return (loss, grad_x[None], *[out_g[n] for n in WEIGHTS], *[out_d[n] for n in WEIGHTS],
            *[out_m[n] for n in WEIGHTS], *[out_v[n] for n in WEIGHTS])


def kernel(x, l0_norm_w, l0_w_in, l0_s5_lambda_re, l0_s5_lambda_im, l0_s5_log_step, l0_s5_b_re, l0_s5_b_im, l0_s5_c_re, l0_s5_c_im, l0_s5_d, l0_s5_w_glu, l0_s5_b_glu, l0_ssd_conv_w, l0_ssd_conv_b, l0_ssd_dt_bias, l0_ssd_a_log, l0_ssd_d, l0_ssd_norm_w, l0_w_out, l1_norm_w, l1_w_in, l1_fox_b_f, l1_w_out, final_norm_w, loss_target, m_l0_norm_w, m_l0_w_in, m_l0_s5_lambda_re, m_l0_s5_lambda_im, m_l0_s5_log_step, m_l0_s5_b_re, m_l0_s5_b_im, m_l0_s5_c_re, m_l0_s5_c_im, m_l0_s5_d, m_l0_s5_w_glu, m_l0_s5_b_glu, m_l0_ssd_conv_w, m_l0_ssd_conv_b, m_l0_ssd_dt_bias, m_l0_ssd_a_log, m_l0_ssd_d, m_l0_ssd_norm_w, m_l0_w_out, m_l1_norm_w, m_l1_w_in, m_l1_fox_b_f, m_l1_w_out, m_final_norm_w, v_l0_norm_w, v_l0_w_in, v_l0_s5_lambda_re, v_l0_s5_lambda_im, v_l0_s5_log_step, v_l0_s5_b_re, v_l0_s5_b_im, v_l0_s5_c_re, v_l0_s5_c_im, v_l0_s5_d, v_l0_s5_w_glu, v_l0_s5_b_glu, v_l0_ssd_conv_w, v_l0_ssd_conv_b, v_l0_ssd_dt_bias, v_l0_ssd_a_log, v_l0_ssd_d, v_l0_ssd_norm_w, v_l0_w_out, v_l1_norm_w, v_l1_w_in, v_l1_fox_b_f, v_l1_w_out, v_final_norm_w):
    return _step(dict(locals()))
```
